```python
import math
import jax
import jax.numpy as jnp
from jax import lax
import numpy as np

D_MODEL = 1024
BATCH = 8
SEQ = 8192
DEPTH = 1
DEC_BATCH = 128
DEC_SEQ = 8
PAST_LEN = 8192
PAGE_SIZE = 128

A_HEADS = 8
A_HEAD_DIM = 64
A_WIDTH = A_HEADS * A_HEAD_DIM
A_SCALE = A_HEAD_DIM ** -0.5
DILATED_GROUPS = ((128, 1), (512, 4), (2048, 16))
WINDOW_MAX = 2048
SWA_BLOCK = 128
B_HEADS = 4
B_KEY_DIM = 64
B_VAL_DIM = 128
B_QK_WIDTH = B_HEADS * B_KEY_DIM
B_V_WIDTH = B_HEADS * B_VAL_DIM
GATE_RANK = 16
GATE_TEMP = 16.0
GLA_CHUNK = 64
MIX_WIDTH = A_WIDTH + B_V_WIDTH
IN_SIZES = (A_WIDTH, A_WIDTH, A_WIDTH, B_QK_WIDTH, B_QK_WIDTH, B_V_WIDTH, B_V_WIDTH, GATE_RANK)
IN_WIDTH = 3 * A_WIDTH + 2 * B_QK_WIDTH + 2 * B_V_WIDTH + GATE_RANK
N_MEM = 256
MEM_HEADS = 4
MEM_HEAD_DIM = 128
MEM_WIDTH = MEM_HEADS * MEM_HEAD_DIM
N_EXPERTS = 32
TOP_K = 4
D_FF = 1024
SWIGLU_ALPHA = 1.702
SWIGLU_LIMIT = 7.0
MOE_TOKEN_BLOCK = 8192
EPS = 1e-6

kernel_name = 'hybrid_dilated_swa_gla_mem_moe_step'


def rmsnorm(x, g):
    xf = x.astype(jnp.float32)
    y = xf * lax.rsqrt(jnp.mean(xf * xf, axis=-1, keepdims=True) + EPS)
    return (y * g.astype(jnp.float32)).astype(x.dtype)


def masked_softmax(s, valid):
    s = jnp.where(valid, s, -jnp.inf)
    m = jnp.max(s, axis=-1, keepdims=True)
    p = jnp.exp(s - m)
    l = jnp.sum(p, axis=-1, keepdims=True)
    return p / l, (m + jnp.log(l))[..., 0]


def combine_groups(outs, lses):
    w = jax.nn.softmax(jnp.stack(lses), axis=0)
    return jnp.sum(w[..., None] * jnp.stack(outs), axis=0)


def to_strided(t, dil, padded):
    bsz, seq, heads, hd = t.shape
    t = jnp.pad(t, ((0, 0), (0, padded - seq), (0, 0), (0, 0)))
    t = t.reshape(bsz, padded // dil, dil, heads, hd).transpose(0, 2, 1, 3, 4)
    return t.reshape(bsz, dil, padded // (dil * SWA_BLOCK), SWA_BLOCK, heads, hd)


def with_prev_block(t):
    prev = jnp.pad(t, ((0, 0), (0, 0), (1, 0), (0, 0), (0, 0), (0, 0)))[:, :, :-1]
    return jnp.concatenate([prev, t], axis=3)


def dilated_prompt(q, k, v):
    bsz, seq, heads, hd = q.shape
    outs, lses = [], []
    for window, dil in DILATED_GROUPS:
        span = window // dil
        unit = dil * SWA_BLOCK
        padded = -(-seq // unit) * unit
        nb = padded // unit
        qs = to_strided(q, dil, padded)
        ks = with_prev_block(to_strided(k, dil, padded))
        vs = with_prev_block(to_strided(v, dil, padded))
        s = jnp.einsum('brnqhd,brnkhd->brnhqk', qs, ks, preferred_element_type=jnp.float32) * A_SCALE
        qi = jnp.arange(SWA_BLOCK)[:, None]
        ki = jnp.arange(2 * SWA_BLOCK)[None, :]
        dist = qi + SWA_BLOCK - ki
        band = (dist >= 0) & (dist <= span)
        has_key = (jnp.arange(nb) > 0)[:, None, None] | (ki >= SWA_BLOCK)[None]
        valid = (band[None] & has_key)[None, None, :, None]
        pr, lse = masked_softmax(s, valid)
        o = jnp.einsum('brnhqk,brnkhd->brnqhd', pr, vs.astype(jnp.float32))
        o = o.reshape(bsz, dil, padded // dil, heads, hd).transpose(0, 2, 1, 3, 4)
        o = o.reshape(bsz, padded, heads, hd)[:, :seq]
        lse = lse.transpose(0, 1, 2, 4, 3).reshape(bsz, dil, padded // dil, heads).transpose(0, 2, 1, 3)
        lse = lse.reshape(bsz, padded, heads)[:, :seq]
        outs.append(o)
        lses.append(lse)
    return combine_groups(outs, lses).astype(q.dtype)


def dilated_sample(q, k_ext, v_ext, w_buf):
    n_new = q.shape[1]
    outs, lses = [], []
    for window, dil in DILATED_GROUPS:
        j = jnp.arange(window // dil + 1)
        idx = w_buf + jnp.arange(n_new)[:, None] - dil * j[None, :]
        valid = idx >= 0
        idx = jnp.maximum(idx, 0)
        kg = k_ext[:, idx]
        vg = v_ext[:, idx]
        s = jnp.einsum('bthd,btjhd->bthj', q, kg, preferred_element_type=jnp.float32) * A_SCALE
        pr, lse = masked_softmax(s, valid[None, :, None, :])
        outs.append(jnp.einsum('bthj,btjhd->bthd', pr, vg.astype(jnp.float32)))
        lses.append(lse)
    return combine_groups(outs, lses).astype(q.dtype)


def gla(q, k, v, log_a, s0):
    bsz, length, heads, dk = q.shape
    chunk = math.gcd(length, GLA_CHUNK)
    nc = length // chunk

    def chunks(t):
        t = t.astype(jnp.float32)
        return t.reshape(bsz, nc, chunk, heads, t.shape[-1]).transpose(1, 0, 3, 2, 4)

    causal = jnp.tril(jnp.ones((chunk, chunk), dtype=bool))

    def step(state, inp):
        qc, kc, vc, gc = inp
        b = jnp.cumsum(gc, axis=2)
        o_inter = jnp.einsum('bhck,bhkv->bhcv', qc * jnp.exp(b), state)
        rel = jnp.where(causal[:, :, None], b[:, :, :, None, :] - b[:, :, None, :, :], -jnp.inf)
        att = jnp.einsum('bhtk,bhsk,bhtsk->bhts', qc, kc, jnp.exp(rel))
        o = o_inter + jnp.einsum('bhts,bhsv->bhtv', att, vc)
        b_last = b[:, :, -1:, :]
        new_state = jnp.exp(b_last[:, :, 0, :])[..., None] * state + jnp.einsum(
            'bhsk,bhsv->bhkv', kc * jnp.exp(b_last - b), vc)
        return new_state, o

    s_fin, o = lax.scan(step, s0.astype(jnp.float32), (chunks(q), chunks(k), chunks(v), chunks(log_a)))
    o = o.transpose(1, 0, 3, 2, 4).reshape(bsz, length, heads, v.shape[-1])
    return o.astype(v.dtype), s_fin.astype(s0.dtype)


def project_in(x, p):
    lead = x.shape[:-1]
    z = rmsnorm(x, p['norm1']) @ p['w_in']
    offs = np.cumsum(IN_SIZES)[:-1].tolist()
    aq, ak, av, bq, bk, bv, br, blr = jnp.split(z, offs, axis=-1)
    aq = rmsnorm(aq.reshape(*lead, A_HEADS, A_HEAD_DIM), p['a_q_g'])
    ak = rmsnorm(ak.reshape(*lead, A_HEADS, A_HEAD_DIM), p['a_k_g'])
    av = av.reshape(*lead, A_HEADS, A_HEAD_DIM)
    bq = bq.reshape(*lead, B_HEADS, B_KEY_DIM) * (B_KEY_DIM ** -0.5)
    bk = bk.reshape(*lead, B_HEADS, B_KEY_DIM)
    bv = bv.reshape(*lead, B_HEADS, B_VAL_DIM)
    log_a = jax.nn.log_sigmoid((blr @ p['gla_w_a'] + p['gla_b_a']).astype(jnp.float32)) / GATE_TEMP
    log_a = log_a.reshape(*lead, B_HEADS, B_KEY_DIM)
    return aq, ak, av, bq, bk, bv, br, log_a


def mix_out(x, o_a, o_b, br, p):
    lead = x.shape[:-1]
    o_b = rmsnorm(o_b, p['gla_out_g']).reshape(*lead, B_V_WIDTH) * jax.nn.silu(br)
    o = jnp.concatenate([o_a.reshape(*lead, A_WIDTH), o_b], axis=-1)
    return x + o @ p['w_out']


def memory_kv(mem, p):
    bsz = mem.shape[0]
    mn = rmsnorm(mem, p['mem_norm'])
    mk = rmsnorm((mn @ p['mem_w_k']).reshape(bsz, N_MEM, MEM_HEADS, MEM_HEAD_DIM), p['mem_k_g'])
    mv = (mn @ p['mem_w_v']).reshape(bsz, N_MEM, MEM_HEADS, MEM_HEAD_DIM)
    return mk, mv


def moe_block(xb, p):
    logits = (xb @ p['router_w'] + p['router_b']).astype(jnp.float32)
    top_v, top_i = lax.top_k(logits, TOP_K)
    gates = jax.nn.softmax(top_v, axis=-1)
    eid = top_i.reshape(-1)
    order = jnp.argsort(eid)
    eid_s = eid[order]
    tok = order // TOP_K
    sizes = jnp.bincount(eid, length=N_EXPERTS).astype(jnp.int32)
    h = lax.ragged_dot(xb[tok], p['w1'], sizes) + p['b1'][eid_s]
    glu = jnp.minimum(h[:, 0::2], SWIGLU_LIMIT)
    lin = jnp.clip(h[:, 1::2], -SWIGLU_LIMIT, SWIGLU_LIMIT)
    act = glu * jax.nn.sigmoid(SWIGLU_ALPHA * glu) * (lin + 1.0)
    out = lax.ragged_dot(act, p['w2'], sizes) + p['b2'][eid_s]
    out = out * gates.reshape(-1)[order][:, None].astype(out.dtype)
    return jax.ops.segment_sum(out, tok, num_segments=xb.shape[0])


def moe(xn, p):
    lead = xn.shape[:-1]
    xf = xn.reshape(-1, D_MODEL)
    n_tok = xf.shape[0]
    blk = math.gcd(n_tok, MOE_TOKEN_BLOCK)
    y = lax.map(lambda xb: moe_block(xb, p), xf.reshape(n_tok // blk, blk, D_MODEL))
    return y.reshape(*lead, D_MODEL)


def memory_and_moe(h, mk, mv, p):
    lead = h.shape[:-1]
    hn = rmsnorm(h, p['norm2'])
    q = rmsnorm((hn @ p['mem_w_q']).reshape(*lead, MEM_HEADS, MEM_HEAD_DIM), p['mem_q_g'])
    s = jnp.einsum('blhd,bmhd->bhlm', q, mk, preferred_element_type=jnp.float32) * (MEM_HEAD_DIM ** -0.5)
    pr = jax.nn.softmax(s, axis=-1)
    o = jnp.einsum('bhlm,bmhd->blhd', pr, mv.astype(jnp.float32)).astype(h.dtype)
    h = h + o.reshape(*lead, MEM_WIDTH) @ p['mem_w_o']
    return h + moe(rmsnorm(h, p['norm3']), p)


def setup_inputs(seed: int = 0) -> dict:
    key = jax.random.key(seed)
    ks = iter(jax.random.split(key, 64))

    def nrm(shape, scale):
        return jax.random.normal(next(ks), shape, jnp.float32) * scale

    def gain(shape):
        return 1.0 + nrm(shape, 0.02)

    w_buf = min(WINDOW_MAX, PAST_LEN)
    return {
        'x_prompt': nrm((BATCH, SEQ, D_MODEL), 1.0),
        'x_sample': nrm((DEC_BATCH, DEC_SEQ, D_MODEL), 1.0),
        'mem_prompt': nrm((BATCH, N_MEM, D_MODEL), 1.0),
        'cache_swa_k': nrm((DEPTH, DEC_BATCH, w_buf, A_HEADS, A_HEAD_DIM), 1.0),
        'cache_swa_v': nrm((DEPTH, DEC_BATCH, w_buf, A_HEADS, A_HEAD_DIM), 1.0),
        'state_gla': nrm((DEPTH, DEC_BATCH, B_HEADS, B_KEY_DIM, B_VAL_DIM), 0.5),
        'cache_mem_k': nrm((DEPTH, DEC_BATCH, N_MEM, MEM_HEADS, MEM_HEAD_DIM), 1.0),
        'cache_mem_v': nrm((DEPTH, DEC_BATCH, N_MEM, MEM_HEADS, MEM_HEAD_DIM), 1.0),
        'norm1_g': gain((DEPTH, D_MODEL)),
        'w_in': nrm((DEPTH, D_MODEL, IN_WIDTH), D_MODEL ** -0.5),
        'a_q_norm_g': gain((DEPTH, A_HEAD_DIM)),
        'a_k_norm_g': gain((DEPTH, A_HEAD_DIM)),
        'gla_w_alpha': nrm((DEPTH, GATE_RANK, B_QK_WIDTH), GATE_RANK ** -0.5),
        'gla_b_alpha': nrm((DEPTH, B_QK_WIDTH), 0.1),
        'gla_out_norm_g': gain((DEPTH, B_VAL_DIM)),
        'w_out': nrm((DEPTH, MIX_WIDTH, D_MODEL), MIX_WIDTH ** -0.5),
        'norm2_g': gain((DEPTH, D_MODEL)),
        'mem_norm_g': gain((DEPTH, D_MODEL)),
        'mem_w_q': nrm((DEPTH, D_MODEL, MEM_WIDTH), D_MODEL ** -0.5),
        'mem_w_k': nrm((DEPTH, D_MODEL, MEM_WIDTH), D_MODEL ** -0.5),
        'mem_w_v': nrm((DEPTH, D_MODEL, MEM_WIDTH), D_MODEL ** -0.5),
        'mem_q_norm_g': gain((DEPTH, MEM_HEAD_DIM)),
        'mem_k_norm_g': gain((DEPTH, MEM_HEAD_DIM)),
        'mem_w_o': nrm((DEPTH, MEM_WIDTH, D_MODEL), MEM_WIDTH ** -0.5),
        'norm3_g': gain((DEPTH, D_MODEL)),
        'router_w': nrm((DEPTH, D_MODEL, N_EXPERTS), D_MODEL ** -0.5),
        'router_b': nrm((DEPTH, N_EXPERTS), 0.01),
        'exp_w1': nrm((DEPTH, N_EXPERTS, D_MODEL, 2 * D_FF), D_MODEL ** -0.5),
        'exp_b1': nrm((DEPTH, N_EXPERTS, 2 * D_FF), 0.01),
        'exp_w2': nrm((DEPTH, N_EXPERTS, D_FF, D_MODEL), D_FF ** -0.5),
        'exp_b2': nrm((DEPTH, N_EXPERTS, D_MODEL), 0.01),
    }


def reference(x_prompt, x_sample, mem_prompt, cache_swa_k, cache_swa_v, state_gla, cache_mem_k, cache_mem_v,
              norm1_g, w_in, a_q_norm_g, a_k_norm_g, gla_w_alpha, gla_b_alpha, gla_out_norm_g, w_out,
              norm2_g, mem_norm_g, mem_w_q, mem_w_k, mem_w_v, mem_q_norm_g, mem_k_norm_g, mem_w_o,
              norm3_g, router_w, router_b, exp_w1, exp_b1, exp_w2, exp_b2):
    xp, xs = x_prompt, x_sample
    w_buf = cache_swa_k.shape[2]
    swa_k_p, swa_v_p, gla_p, mem_k_p, mem_v_p = [], [], [], [], []
    swa_k_s, swa_v_s, gla_s = [], [], []
    for l in range(DEPTH):
        p = {'norm1': norm1_g[l], 'w_in': w_in[l], 'a_q_g': a_q_norm_g[l], 'a_k_g': a_k_norm_g[l],
             'gla_w_a': gla_w_alpha[l], 'gla_b_a': gla_b_alpha[l], 'gla_out_g': gla_out_norm_g[l],
             'w_out': w_out[l], 'norm2': norm2_g[l], 'mem_norm': mem_norm_g[l], 'mem_w_q': mem_w_q[l],
             'mem_w_k': mem_w_k[l], 'mem_w_v': mem_w_v[l], 'mem_q_g': mem_q_norm_g[l],
             'mem_k_g': mem_k_norm_g[l], 'mem_w_o': mem_w_o[l], 'norm3': norm3_g[l],
             'router_w': router_w[l], 'router_b': router_b[l], 'w1': exp_w1[l], 'b1': exp_b1[l],
             'w2': exp_w2[l], 'b2': exp_b2[l]}

        aq, ak, av, bq, bk, bv, br, la = project_in(xp, p)
        o_a = dilated_prompt(aq, ak, av)
        w_p = min(WINDOW_MAX, xp.shape[1])
        s0 = jnp.zeros((xp.shape[0], B_HEADS, B_KEY_DIM, B_VAL_DIM), xp.dtype)
        o_b, s_p = gla(bq, bk, bv, la, s0)
        h = mix_out(xp, o_a, o_b, br, p)
        mk, mv = memory_kv(mem_prompt, p)
        xp = memory_and_moe(h, mk, mv, p)
        swa_k_p.append(ak[:, -w_p:])
        swa_v_p.append(av[:, -w_p:])
        gla_p.append(s_p)
        mem_k_p.append(mk)
        mem_v_p.append(mv)

        aq, ak, av, bq, bk, bv, br, la = project_in(xs, p)
        k_ext = jnp.concatenate([cache_swa_k[l].astype(ak.dtype), ak], axis=1)
        v_ext = jnp.concatenate([cache_swa_v[l].astype(av.dtype), av], axis=1)
        o_a = dilated_sample(aq, k_ext, v_ext, w_buf)
        o_b, s_s = gla(bq, bk, bv, la, state_gla[l])
        h = mix_out(xs, o_a, o_b, br, p)
        xs = memory_and_moe(h, cache_mem_k[l], cache_mem_v[l], p)
        swa_k_s.append(k_ext[:, -w_buf:])
        swa_v_s.append(v_ext[:, -w_buf:])
        gla_s.append(s_s)

    return (xp, xs, jnp.stack(swa_k_p), jnp.stack(swa_v_p), jnp.stack(gla_p), jnp.stack(mem_k_p),
            jnp.stack(mem_v_p), jnp.stack(swa_k_s), jnp.stack(swa_v_s), jnp.stack(gla_s))
```

```python
import functools

import jax
import jax.numpy as jnp
import numpy as np
from jax import lax
from jax.experimental import pallas as pl
from jax.experimental.pallas import tpu as pltpu

F32 = jnp.float32
BF16 = jnp.bfloat16
I32 = jnp.int32

EPS = 1e-6
D_MODEL = 1024
A_HEADS, A_HEAD_DIM, A_WIDTH = 8, 64, 512
A_SCALE = A_HEAD_DIM ** -0.5
DILATED_GROUPS = ((128, 1), (512, 4), (2048, 16))
SPAN = 128
B_HEADS, B_KEY_DIM, B_VAL_DIM = 4, 64, 128
B_QK_WIDTH, B_V_WIDTH = 256, 512
GATE_RANK = 16
GATE_TEMP = 16.0
N_MEM, MEM_HEADS, MEM_HEAD_DIM, MEM_WIDTH = 256, 4, 128, 512
N_EXPERTS, TOP_K, D_FF = 32, 4, 1024
SWIGLU_ALPHA, SWIGLU_LIMIT = 1.702, 7.0

LANES = 128
VMEM_LIMIT = 56 * 1024 * 1024

PROJ_TILE = 512
SWA_BLOCK = 2048
GLA_CHUNK, GLA_SUB, GLA_STEP = 64, 16, 512
POST_TILE = 256
SAMPLE_SEQS = 8
EXPERT_TILE = 512
ROW_TILE = 256


def _cparams(sem, vmem=VMEM_LIMIT):
    return pltpu.CompilerParams(dimension_semantics=sem, vmem_limit_bytes=vmem)


def _full(shape):
    n = len(shape)
    return pl.BlockSpec(shape, lambda *_: (0,) * n)


def _rms(x, g):
    ms = jnp.mean(x * x, axis=-1, keepdims=True)
    return x * lax.rsqrt(ms + EPS) * g


def _dot(a, b):
    return jnp.dot(a, b, preferred_element_type=F32)


def _dot_nt(a, b):
    return lax.dot_general(a, b, (((1,), (1,)), ((), ())), preferred_element_type=F32)


def _block_diag(n, blk, val, dtype):
    i = np.arange(n)
    return jnp.asarray(np.where((i[:, None] // blk) == (i[None, :] // blk), val, 0.0), dtype)


def _in_proj_body(x_ref, g1_ref, w_ref, gq_ref, gk_ref, bd_ref, wa_ref, ba_ref,
                  q_ref, k_ref, v_ref, bq_ref, bk_ref, bv_ref, gate_ref, la_ref):
    xn = _rms(x_ref[...], g1_ref[...]).astype(BF16)
    bd = bd_ref[...]

    def proj(lo, hi):
        return _dot(xn, w_ref[:, lo:hi])

    def headnorm(z, g):
        ms = _dot((z * z).astype(BF16), bd)
        return z * lax.rsqrt(ms + EPS) * g

    def put_groups(ref, z):
        for p in range(A_WIDTH // LANES):
            ref[p] = z[:, p * LANES:(p + 1) * LANES]

    put_groups(q_ref, headnorm(proj(0, 512), gq_ref[...]) * A_SCALE)
    put_groups(k_ref, headnorm(proj(512, 1024), gk_ref[...]))
    put_groups(v_ref, proj(1024, 1536))
    bq_ref[...] = proj(1536, 1792) * (B_KEY_DIM ** -0.5)
    bk_ref[...] = proj(1792, 2048)
    bv_ref[...] = proj(2048, 2560).astype(bv_ref.dtype)
    br = proj(2560, 3072)
    gate_ref[...] = (br * jax.nn.sigmoid(br)).astype(gate_ref.dtype)
    lr = proj(3072, 3200).astype(BF16)
    pre = _dot(lr, wa_ref[...]) + ba_ref[...]
    log_sig = jnp.minimum(pre, 0.0) - jnp.log1p(jnp.exp(-jnp.abs(pre)))
    la_ref[...] = log_sig * (1.0 / GATE_TEMP)


def _in_proj(x, wp, wide_dtype):
    n = x.shape[0]
    tm = min(PROJ_TILE, n)
    row = lambda w: pl.BlockSpec((tm, w), lambda i: (i, 0))
    ngrp = A_WIDTH // LANES
    grp = pl.BlockSpec((ngrp, tm, LANES), lambda i: (0, i, 0))
    outs = [(256, F32), (256, F32), (512, wide_dtype), (512, wide_dtype), (256, F32)]
    return pl.pallas_call(
        _in_proj_body,
        grid=(n // tm,),
        in_specs=[row(D_MODEL), _full((1, D_MODEL)), _full((D_MODEL, 3200)), _full((1, 512)),
                  _full((1, 512)), _full((512, 512)), _full((LANES, 256)), _full((1, 256))],
        out_specs=[grp] * 3 + [row(w) for w, _ in outs],
        out_shape=[jax.ShapeDtypeStruct((ngrp, n, LANES), F32)] * 3
        + [jax.ShapeDtypeStruct((n, w), dt) for w, dt in outs],
        compiler_params=_cparams(("parallel",)),
        name="in_proj",
    )(x, wp["g1"], wp["w_in"], wp["gq"], wp["gk"], wp["bd64"], wp["wa"], wp["ba"])


def _ds(start, size, stride):
    return pl.ds(start, size) if stride == 1 else pl.ds(start, size, stride=stride)


def _swa_prompt_body(q_ref, kp_ref, kc_ref, vp_ref, vc_ref, o_ref, m_s, l_s, acc_s):
    i = pl.program_id(2)
    qb = SWA_BLOCK
    lane = lax.broadcasted_iota(I32, (SPAN, LANES), 1)
    lo_mask = lane < A_HEAD_DIM
    jq = lax.broadcasted_iota(I32, (SPAN, 2 * SPAN), 0)
    jk = lax.broadcasted_iota(I32, (SPAN, 2 * SPAN), 1)
    dist = jq + SPAN - jk
    band = (dist >= 0) & (dist <= SPAN)
    cur_half = jk >= SPAN

    def attend(qp, kp, vp, valid, rows, first_group):
        res = []
        for hh in range(2):
            msk = lo_mask if hh == 0 else jnp.logical_not(lo_mask)
            qm = jnp.where(msk, qp, 0.0).astype(BF16)
            s = jnp.where(valid, _dot_nt(qm, kp), -jnp.inf)
            m = jnp.max(s, axis=1, keepdims=True)
            pr = jnp.exp(s - m)
            l = jnp.sum(pr, axis=1, keepdims=True)
            res.append((m, l, _dot(pr.astype(BF16), vp)))
        m_new = jnp.where(lo_mask, res[0][0], res[1][0])
        l_new = jnp.where(lo_mask, res[0][1], res[1][1])
        o_new = jnp.where(lo_mask, res[0][2], res[1][2])
        if first_group:
            m_s[rows, :] = m_new
            l_s[rows, :] = l_new
            acc_s[rows, :] = o_new
        else:
            m_old = m_s[rows, :]
            m = jnp.maximum(m_old, m_new)
            a_old = jnp.exp(m_old - m)
            a_new = jnp.exp(m_new - m)
            m_s[rows, :] = m
            l_s[rows, :] = l_s[rows, :] * a_old + l_new * a_new
            acc_s[rows, :] = acc_s[rows, :] * a_old + o_new * a_new

    for gi, (_, dil) in enumerate(DILATED_GROUPS):
        unit = dil * SPAN
        nblk = qb // unit
        first = gi == 0

        def head_block(r, carry, dil=dil, unit=unit, first=first):
            rows = _ds(r, SPAN, dil)
            prev = _ds(qb - unit + r, SPAN, dil)
            ks = jnp.concatenate([kp_ref[prev, :], kc_ref[rows, :]], axis=0).astype(BF16)
            vs = jnp.concatenate([vp_ref[prev, :], vc_ref[rows, :]], axis=0).astype(BF16)
            valid = band & (cur_half | (i > 0))
            attend(q_ref[rows, :], ks, vs, valid, rows, first)
            return carry

        lax.fori_loop(0, dil, head_block, 0)

        if nblk > 1:
            def tail_block(idx, carry, dil=dil, unit=unit, first=first):
                n = idx // dil + 1
                r = idx % dil
                start = unit * n + r
                rows = _ds(start, SPAN, dil)
                keys = _ds(start - unit, 2 * SPAN, dil)
                attend(q_ref[rows, :], kc_ref[keys, :].astype(BF16), vc_ref[keys, :].astype(BF16),
                       band, rows, first)
                return carry

            lax.fori_loop(0, (nblk - 1) * dil, tail_block, 0)

    o_ref[...] = (acc_s[...] / l_s[...]).astype(o_ref.dtype)


def _swa_prompt(q, k, v, bsz, seq):
    qb = SWA_BLOCK
    nb = seq // qb
    cur = pl.BlockSpec((None, qb, LANES), lambda p, b, i: (p, b * nb + i, 0))
    prev = pl.BlockSpec((None, qb, LANES), lambda p, b, i: (p, b * nb + jnp.maximum(i - 1, 0), 0))
    return pl.pallas_call(
        _swa_prompt_body,
        grid=(A_WIDTH // LANES, bsz, nb),
        in_specs=[cur, prev, cur, prev, cur],
        out_specs=pl.BlockSpec((qb, LANES), lambda p, b, i: (b * nb + i, p)),
        out_shape=jax.ShapeDtypeStruct((bsz * seq, A_WIDTH), BF16),
        scratch_shapes=[pltpu.VMEM((qb, LANES), F32)] * 3,
        compiler_params=_cparams(("parallel", "parallel", "arbitrary")),
        name="swa_prompt",
    )(q, k, k, v, v)


def _swa_sample_body(q_ref, kn_ref, vn_ref, ck_ref, cv_ref, c1_ref, c2_ref,
                     o_ref, nk_ref, nv_ref):
    t_new = q_ref.shape[1]
    w_buf = ck_ref.shape[1]
    ck, cv = ck_ref[0], cv_ref[0]
    kn, vn = kn_ref[0], vn_ref[0]
    nk_ref[0, 0:w_buf - t_new, :] = ck[t_new:, :]
    nk_ref[0, w_buf - t_new:, :] = kn
    nv_ref[0, 0:w_buf - t_new, :] = cv[t_new:, :]
    nv_ref[0, w_buf - t_new:, :] = vn

    q = q_ref[0]
    lane = lax.broadcasted_iota(I32, (t_new, LANES), 1)
    lo_mask = lane < A_HEAD_DIM
    c1, c2 = c1_ref[...], c2_ref[...]
    outs = []
    for p in range(A_WIDTH // LANES):
        sl = slice(p * LANES, (p + 1) * LANES)
        qp = q[:, sl]
        qblk = jnp.concatenate([jnp.where(lo_mask, qp, 0.0), jnp.where(lo_mask, 0.0, qp)],
                               axis=0).astype(BF16)
        s1 = jnp.where(c1 > 0, _dot_nt(qblk, ck[:, sl].astype(BF16)), -jnp.inf)
        s2 = jnp.where(c2 > 0, _dot_nt(qblk, kn[:, sl].astype(BF16)), -jnp.inf)
        m = jnp.maximum(jnp.max(s1, axis=1, keepdims=True), jnp.max(s2, axis=1, keepdims=True))
        p1 = c1 * jnp.exp(s1 - m)
        p2 = c2 * jnp.exp(s2 - m)
        l = jnp.sum(p1, axis=1, keepdims=True) + jnp.sum(p2, axis=1, keepdims=True)
        o = (_dot(p1.astype(BF16), cv[:, sl].astype(BF16))
             + _dot(p2.astype(BF16), vn[:, sl].astype(BF16))) / l
        outs.append(jnp.where(lo_mask, o[:t_new], o[t_new:]))
    o_ref[0] = jnp.concatenate(outs, axis=1).astype(o_ref.dtype)


def _sample_multiplicity(t_new, w_buf):
    t = np.arange(t_new)[:, None]
    e = np.arange(w_buf + t_new)[None, :]
    d = w_buf + t - e
    c = np.zeros(d.shape, np.float32)
    for window, dil in DILATED_GROUPS:
        c += ((d >= 0) & (d % dil == 0) & (d <= window)).astype(np.float32)
    c = np.concatenate([c, c], axis=0)
    return jnp.asarray(c[:, :w_buf]), jnp.asarray(c[:, w_buf:])


def _swa_sample(q, kn, vn, cache_k, cache_v):
    db, t_new, w = q.shape
    w_buf = cache_k.shape[1]
    assert w_buf >= max(win for win, _ in DILATED_GROUPS) and t_new % 8 == 0
    c1, c2 = _sample_multiplicity(t_new, w_buf)
    new = pl.BlockSpec((1, t_new, w), lambda b: (b, 0, 0))
    cache = pl.BlockSpec((1, w_buf, w), lambda b: (b, 0, 0))
    return pl.pallas_call(
        _swa_sample_body,
        grid=(db,),
        in_specs=[new, new, new, cache, cache, _full(c1.shape), _full(c2.shape)],
        out_specs=[new, cache, cache],
        out_shape=[jax.ShapeDtypeStruct((db, t_new, w), BF16),
                   jax.ShapeDtypeStruct(cache_k.shape, cache_k.dtype),
                   jax.ShapeDtypeStruct(cache_v.shape, cache_v.dtype)],
        compiler_params=_cparams(("parallel",)),
        name="swa_sample",
    )(q, kn, vn, cache_k, cache_v, c1, c2)


def _gla_body(chunk, sub, nch, q_ref, k_ref, g_ref, v_ref, gate_ref, s0_ref, tril_ref, dmask_ref,
              bones_ref, sbm_ref, bd_ref, gout_ref, o_ref, sfin_ref, sbd):
    j = pl.program_id(1)
    sbm = sbm_ref[...]
    nsub = chunk // sub
    pad = B_KEY_DIM - chunk

    @pl.when(j == 0)
    def _():
        s0 = s0_ref[0]
        sbd[...] = jnp.concatenate([s0] * B_HEADS, axis=1) * sbm

    row = lax.broadcasted_iota(I32, (chunk, 1), 0)
    sub_id = row // sub
    lane_w = lax.broadcasted_iota(I32, (chunk, LANES * max(nsub - 1, 1)), 1)
    lo_w = (lane_w % LANES) < B_KEY_DIM

    def one_chunk(c, carry):
        off = pl.multiple_of(c * chunk, chunk)
        rows = pl.ds(off, chunk)
        q, k, g = q_ref[rows, :], k_ref[rows, :], g_ref[rows, :]
        v = v_ref[rows, :].astype(F32)
        g1 = g.astype(BF16)
        r1 = g - g1.astype(F32)
        g2 = r1.astype(BF16)
        g3 = (r1 - g2.astype(F32)).astype(BF16)
        tril = tril_ref[...]
        b = _dot(tril, g1) + _dot(tril, g2) + _dot(tril, g3)
        b_last = b[chunk - 1:chunk, :]
        state = sbd[...]

        o = _dot((q * jnp.exp(b)).astype(BF16), state.astype(BF16))

        bones = bones_ref[...]
        att = _dot((q * k).astype(BF16), bones) * dmask_ref[0]
        for d in range(1, sub):
            kd = pltpu.roll(k, d, 0)
            bd_ = pltpu.roll(b, d, 0)
            w = q * kd * jnp.exp(jnp.minimum(b - bd_, 0.0))
            att = att + _dot(w.astype(BF16), bones) * dmask_ref[d]

        if nsub > 1:
            qx, kx = [], []
            for i in range(1, nsub):
                r_i = b[sub * i - 1:sub * i, :]
                qx.append(jnp.where(sub_id == i, q * jnp.exp(jnp.minimum(b - r_i, 0.0)), 0.0))
                kx.append(jnp.where(sub_id < i, k * jnp.exp(jnp.minimum(r_i - b, 0.0)), 0.0))
            parts = []
            for p in range(B_QK_WIDTH // LANES):
                sl = slice(p * LANES, (p + 1) * LANES)
                qp = jnp.concatenate([x[:, sl] for x in qx], axis=1)
                kp = jnp.concatenate([x[:, sl] for x in kx], axis=1).astype(BF16)
                zero = jnp.zeros_like(kp)
                lhs = jnp.concatenate([jnp.where(lo_w, qp, 0.0), jnp.where(lo_w, 0.0, qp)],
                                      axis=1).astype(BF16)
                rhs = jnp.concatenate([jnp.concatenate([kp, zero], axis=1),
                                       jnp.concatenate([zero, kp], axis=1)], axis=0)
                parts.append(_dot_nt(lhs, rhs))
            att = att + jnp.concatenate(parts, axis=1)

        if pad:
            vrow = jnp.concatenate([v, jnp.zeros((pad, B_V_WIDTH), F32)], axis=0)
        else:
            vrow = v
        vbd = (jnp.concatenate([vrow] * B_HEADS, axis=0) * sbm).astype(BF16)
        o = o + _dot(att.astype(BF16), vbd)

        ke = (k * jnp.exp(b_last - b)).astype(BF16)
        upd = lax.dot_general(ke, v.astype(BF16), (((0,), (0,)), ((), ())),
                              preferred_element_type=F32)
        dec = jnp.transpose(jnp.broadcast_to(jnp.exp(b_last), (8, B_QK_WIDTH)))[:, 0:1]
        sbd[...] = (state * dec + upd) * sbm

        ms = _dot((o * o).astype(BF16), bd_ref[...])
        on = o * lax.rsqrt(ms + EPS) * gout_ref[...] * gate_ref[rows, :].astype(F32)
        o_ref[rows, :] = on.astype(o_ref.dtype)
        return carry

    lax.fori_loop(0, nch, one_chunk, 0)

    @pl.when(j == pl.num_programs(1) - 1)
    def _():
        s = sbd[...]
        sfin_ref[0] = jnp.concatenate(
            [s[h * B_KEY_DIM:(h + 1) * B_KEY_DIM, h * B_VAL_DIM:(h + 1) * B_VAL_DIM]
             for h in range(B_HEADS)], axis=0)


def _gla_consts(chunk, sub):
    t = np.arange(chunk)
    tril = (t[:, None] >= t[None, :]).astype(np.float32)
    lane = np.arange(B_QK_WIDTH)
    dmask = np.zeros((sub, chunk, B_QK_WIDTH), np.float32)
    for d in range(sub):
        ok = (t % sub) >= d
        dmask[d] = ((lane[None, :] % B_KEY_DIM) == (t[:, None] - d)) & ok[:, None]
    r = np.arange(B_QK_WIDTH)[:, None] // B_KEY_DIM
    c = np.arange(B_V_WIDTH)[None, :] // B_VAL_DIM
    sbm = (r == c).astype(np.float32)
    return jnp.asarray(tril, BF16), jnp.asarray(dmask), jnp.asarray(sbm)


def _gla(q, k, g, v, gate, s0, wp, length, chunk, sub, step):
    n = q.shape[0]
    bsz = n // length
    assert chunk == sub or chunk == B_KEY_DIM
    tril, dmask, sbm = _gla_consts(chunk, sub)
    nstep = length // step
    row = lambda w: pl.BlockSpec((step, w), lambda b, j: (b * nstep + j, 0))
    st = pl.BlockSpec((1, B_QK_WIDTH, B_VAL_DIM), lambda b, j: (b, 0, 0))
    return pl.pallas_call(
        functools.partial(_gla_body, chunk, sub, step // chunk),
        grid=(bsz, nstep),
        in_specs=[row(256), row(256), row(256), row(512), row(512), st, _full(tril.shape),
                  _full(dmask.shape), _full((256, 256)), _full(sbm.shape), _full((512, 512)),
                  _full((1, 512))],
        out_specs=[row(512), st],
        out_shape=[jax.ShapeDtypeStruct((n, B_V_WIDTH), BF16),
                   jax.ShapeDtypeStruct(s0.shape, F32)],
        scratch_shapes=[pltpu.VMEM((B_QK_WIDTH, B_V_WIDTH), F32)],
        compiler_params=_cparams(("parallel", "arbitrary")),
        name="gla",
    )(q, k, g, v, gate, s0, tril, dmask, wp["bones64"], sbm, wp["bd128"], wp["gout"])


def _head_rms(z, g, scale=1.0):
    parts = []
    for h in range(MEM_HEADS):
        zh = z[:, h * LANES:(h + 1) * LANES]
        parts.append(zh * lax.rsqrt(jnp.mean(zh * zh, axis=-1, keepdims=True) + EPS))
    return jnp.concatenate(parts, axis=1) * (g * scale)


def _mem_kv_body(m_ref, gn_ref, wk_ref, wv_ref, gk_ref, mk_ref, mv_ref):
    mn = _rms(m_ref[...], gn_ref[...]).astype(BF16)
    mk_ref[...] = _head_rms(_dot(mn, wk_ref[...]), gk_ref[...])
    mv_ref[...] = _dot(mn, wv_ref[...])


def _mem_kv(mem, wp):
    n = mem.shape[0]
    tm = 256
    row = lambda w: pl.BlockSpec((tm, w), lambda i: (i, 0))
    return pl.pallas_call(
        _mem_kv_body,
        grid=(n // tm,),
        in_specs=[row(D_MODEL), _full((1, D_MODEL)), _full((D_MODEL, 512)), _full((D_MODEL, 512)),
                  _full((1, 512))],
        out_specs=[row(512), row(512)],
        out_shape=[jax.ShapeDtypeStruct((n, 512), F32)] * 2,
        compiler_params=_cparams(("parallel",)),
        name="mem_kv",
    )(mem, wp["gmem"], wp["wmk"], wp["wmv"], wp["gmk"])


def _post_body(nseq, x_ref, oa_ref, ob_ref, woa_ref, wob_ref, g2_ref, wq_ref, gmq_ref, mk_ref,
               mv_ref, wo_ref, g3_ref, rwh_ref, rwl_ref, rb_ref, cnt0_ref, tri_ref,
               h2_ref, xn_ref, meta_ref, cnt_ref, carry):
    tm = x_ref.shape[0]

    @pl.when(pl.program_id(0) == 0)
    def _():
        carry[...] = cnt0_ref[...]

    h = x_ref[...] + _dot(oa_ref[...], woa_ref[...]) + _dot(ob_ref[...], wob_ref[...])
    hn = _rms(h, g2_ref[...]).astype(BF16)
    qm = _head_rms(_dot(hn, wq_ref[...]), gmq_ref[...], MEM_HEAD_DIM ** -0.5).astype(BF16)
    nk = nseq * N_MEM
    mk = mk_ref[...].reshape(nk, MEM_WIDTH)
    mv = mv_ref[...].reshape(nk, MEM_WIDTH)
    if nseq > 1:
        rt = lax.broadcasted_iota(I32, (tm, nk), 0) // (tm // nseq)
        ct = lax.broadcasted_iota(I32, (tm, nk), 1) // N_MEM
        same = rt == ct
    outs = []
    for hd in range(MEM_HEADS):
        sl = slice(hd * LANES, (hd + 1) * LANES)
        s = _dot_nt(qm[:, sl], mk[:, sl].astype(BF16))
        if nseq > 1:
            s = jnp.where(same, s, -jnp.inf)
        m = jnp.max(s, axis=1, keepdims=True)
        pr = jnp.exp(s - m)
        l = jnp.sum(pr, axis=1, keepdims=True)
        outs.append(_dot(pr.astype(BF16), mv[:, sl].astype(BF16)) / l)
    h2 = h + _dot(jnp.concatenate(outs, axis=1).astype(BF16), wo_ref[...])
    h2_ref[...] = h2
    xn = _rms(h2, g3_ref[...])
    xn_ref[...] = xn

    x1 = xn.astype(BF16)
    x2 = (xn - x1.astype(F32)).astype(BF16)
    logits = (_dot(x1, rwh_ref[...]) + _dot(x1, rwl_ref[...]) + _dot(x2, rwh_ref[...])
              + rb_ref[...])
    lane = lax.broadcasted_iota(I32, (tm, LANES), 1)
    vals, idxs, hots = [], [], []
    work = logits
    for _ in range(TOP_K):
        m = jnp.max(work, axis=1, keepdims=True)
        idx = jnp.min(jnp.where(work == m, lane, LANES), axis=1, keepdims=True)
        hot = lane == idx
        vals.append(m)
        idxs.append(idx)
        hots.append(hot)
        work = jnp.where(hot, -jnp.inf, work)
    exps = [jnp.exp(v - vals[0]) for v in vals]
    den = exps[0] + exps[1] + exps[2] + exps[3]

    sel = (hots[0] | hots[1] | hots[2] | hots[3]).astype(F32)
    before = _dot(tri_ref[...], sel.astype(BF16)) + carry[...]
    carry[...] = carry[...] + jnp.sum(sel, axis=0, keepdims=True)
    cnt_ref[...] = carry[...]

    meta = jnp.zeros((tm, LANES), F32)
    for kk in range(TOP_K):
        rank = jnp.sum(jnp.where(hots[kk], before, 0.0), axis=1, keepdims=True)
        meta = jnp.where(lane == kk, idxs[kk].astype(F32), meta)
        meta = jnp.where(lane == TOP_K + kk, rank, meta)
        meta = jnp.where(lane == 2 * TOP_K + kk, exps[kk] / den, meta)
    meta_ref[...] = meta


def _post(x, oa, ob, mk, mv, cnt0, wp, tm, nseq, tiles_per_mem):
    n = x.shape[0]
    row = lambda w: pl.BlockSpec((tm, w), lambda i: (i, 0))
    mem = pl.BlockSpec((nseq, N_MEM, MEM_WIDTH), lambda i: (i // tiles_per_mem, 0, 0))
    tri = jnp.asarray(np.tril(np.ones((tm, tm), np.float32), -1), BF16)
    return pl.pallas_call(
        functools.partial(_post_body, nseq),
        grid=(n // tm,),
        in_specs=[row(D_MODEL), row(512), row(512), _full((512, D_MODEL)), _full((512, D_MODEL)),
                  _full((1, D_MODEL)), _full((D_MODEL, 512)), _full((1, 512)), mem, mem,
                  _full((512, D_MODEL)), _full((1, D_MODEL)), _full((D_MODEL, LANES)),
                  _full((D_MODEL, LANES)), _full((1, LANES)), _full((1, LANES)), _full((tm, tm))],
        out_specs=[row(D_MODEL), row(D_MODEL), row(LANES), _full((1, LANES))],
        out_shape=[jax.ShapeDtypeStruct((n, D_MODEL), F32), jax.ShapeDtypeStruct((n, D_MODEL), F32),
                   jax.ShapeDtypeStruct((n, LANES), F32), jax.ShapeDtypeStruct((1, LANES), F32)],
        scratch_shapes=[pltpu.VMEM((1, LANES), F32)],
        compiler_params=_cparams(("arbitrary",)),
        name="post",
    )(x, oa, ob, wp["woa"], wp["wob"], wp["g2"], wp["wmq"], wp["gmq"], mk, mv, wp["wmo"],
      wp["g3"], wp["rwh"], wp["rwl"], wp["rb"], cnt0, tri)


def _row_copy(src, src_row, dst, dst_row, sem):
    return pltpu.make_async_copy(src.at[pl.ds(src_row, 1)], dst.at[pl.ds(dst_row, 1)], sem)


def _dispatch_body(dest_ref, x_ref, xs_in_ref, xs_ref, sem):
    del xs_in_ref
    n = dest_ref.shape[2]

    def issue(j, carry):
        _row_copy(x_ref, j // TOP_K, xs_ref, dest_ref[0, 0, j], sem).start()
        return carry

    def drain(j, carry):
        _row_copy(x_ref, 0, xs_ref, 0, sem).wait()
        return carry

    lax.fori_loop(0, n, issue, 0)
    lax.fori_loop(0, n, drain, 0)


def _dispatch(dest, xn, xs):
    n = xn.shape[0]
    tt = min(ROW_TILE, n)
    dest3 = dest.reshape(n // tt, 1, tt * TOP_K)
    return pl.pallas_call(
        _dispatch_body,
        grid=(n // tt,),
        in_specs=[pl.BlockSpec((1, 1, tt * TOP_K), lambda i: (i, 0, 0), memory_space=pltpu.SMEM),
                  pl.BlockSpec((tt, D_MODEL), lambda i: (i, 0)),
                  pl.BlockSpec(memory_space=pl.ANY)],
        out_specs=pl.BlockSpec(memory_space=pl.ANY),
        out_shape=jax.ShapeDtypeStruct(xs.shape, xs.dtype),
        scratch_shapes=[pltpu.SemaphoreType.DMA],
        input_output_aliases={2: 0},
        compiler_params=_cparams(("arbitrary",)),
        name="moe_dispatch",
    )(dest3, xn, xs)


def _expert_body(te_ref, nt_ref, x_ref, w1g_ref, w1l_ref, b1g_ref, b1l_ref, w2_ref, b2_ref, y_ref):
    @pl.when(pl.program_id(0) < nt_ref[0])
    def _():
        x = x_ref[...].astype(BF16)
        glu = jnp.minimum(_dot(x, w1g_ref[0]) + b1g_ref[0], SWIGLU_LIMIT)
        lin = jnp.clip(_dot(x, w1l_ref[0]) + b1l_ref[0], -SWIGLU_LIMIT, SWIGLU_LIMIT)
        act = glu * jax.nn.sigmoid(SWIGLU_ALPHA * glu) * (lin + 1.0)
        y_ref[...] = _dot(act.astype(BF16), w2_ref[0]) + b2_ref[0]

    @pl.when(pl.program_id(0) >= nt_ref[0])
    def _():
        y_ref[...] = jnp.zeros_like(y_ref)


def _experts(tile_expert, n_tiles, xs, wp):
    rows = xs.shape[0]
    tile = lambda i, te, nt: (jnp.minimum(i, nt[0] - 1), 0)
    out_tile = lambda i, te, nt: (i, 0)
    wsel = lambda i, te, nt: (te[jnp.minimum(i, nt[0] - 1)], 0, 0)
    wspec = lambda r, c: pl.BlockSpec((1, r, c), wsel)
    return pl.pallas_call(
        _expert_body,
        grid_spec=pltpu.PrefetchScalarGridSpec(
            num_scalar_prefetch=2, grid=(rows // EXPERT_TILE,),
            in_specs=[pl.BlockSpec((EXPERT_TILE, D_MODEL), tile),
                      wspec(D_MODEL, D_FF), wspec(D_MODEL, D_FF), wspec(1, D_FF), wspec(1, D_FF),
                      wspec(D_FF, D_MODEL), wspec(1, D_MODEL)],
            out_specs=pl.BlockSpec((EXPERT_TILE, D_MODEL), out_tile)),
        out_shape=jax.ShapeDtypeStruct((rows, D_MODEL), F32),
        compiler_params=_cparams(("arbitrary",)),
        name="moe_experts",
    )(tile_expert, n_tiles, xs, wp["w1g"], wp["w1l"], wp["b1g"], wp["b1l"], wp["w2"], wp["b2"])


def _combine_body(pos_ref, meta_ref, h_ref, ys_ref, o_ref, buf, sem):
    n = pos_ref.shape[2]

    def issue(j, carry):
        _row_copy(ys_ref, pos_ref[0, 0, j], buf.at[j % TOP_K], j // TOP_K, sem).start()
        return carry

    def drain(j, carry):
        _row_copy(ys_ref, 0, buf.at[0], 0, sem).wait()
        return carry

    lax.fori_loop(0, n, issue, 0)
    lax.fori_loop(0, n, drain, 0)
    meta = meta_ref[...]
    out = h_ref[...]
    for kk in range(TOP_K):
        out = out + meta[:, 2 * TOP_K + kk:2 * TOP_K + kk + 1] * buf[kk]
    o_ref[...] = out


def _combine(pos, meta, h2, ys):
    n = h2.shape[0]
    tt = min(ROW_TILE, n)
    pos3 = pos.reshape(n // tt, 1, tt * TOP_K)
    row = lambda w: pl.BlockSpec((tt, w), lambda i: (i, 0))
    return pl.pallas_call(
        _combine_body,
        grid=(n // tt,),
        in_specs=[pl.BlockSpec((1, 1, tt * TOP_K), lambda i: (i, 0, 0), memory_space=pltpu.SMEM),
                  row(LANES), row(D_MODEL), pl.BlockSpec(memory_space=pl.ANY)],
        out_specs=row(D_MODEL),
        out_shape=jax.ShapeDtypeStruct((n, D_MODEL), F32),
        scratch_shapes=[pltpu.VMEM((TOP_K, tt, D_MODEL), F32), pltpu.SemaphoreType.DMA],
        compiler_params=_cparams(("arbitrary",)),
        name="moe_combine",
    )(pos3, meta, h2, ys)


def _moe(groups, cnt, wp):
    n_pairs = sum(g[0].shape[0] for g in groups) * TOP_K
    rows = n_pairs + N_EXPERTS * EXPERT_TILE
    n_tiles_max = rows // EXPERT_TILE
    counts = cnt[0, :N_EXPERTS].astype(I32)
    padded = (counts + EXPERT_TILE - 1) // EXPERT_TILE * EXPERT_TILE
    ends = jnp.cumsum(padded)
    offs = ends - padded
    n_tiles = (ends[-1] // EXPERT_TILE).reshape(1)
    tile_expert = jnp.minimum(
        jnp.searchsorted(ends // EXPERT_TILE, jnp.arange(n_tiles_max, dtype=I32), side="right"),
        N_EXPERTS - 1).astype(I32)

    xs = jnp.zeros((rows, D_MODEL), F32)
    dests = []
    for _, xn, meta in groups:
        eidx = meta[:, 0:TOP_K].astype(I32)
        dest = offs[eidx] + meta[:, TOP_K:2 * TOP_K].astype(I32)
        dests.append(dest)
        xs = _dispatch(dest, xn, xs)
    ys = _experts(tile_expert, n_tiles, xs, wp)
    return [_combine(dest, meta, h2, ys) for dest, (h2, _, meta) in zip(dests, groups)]


def _prep_weights(norm1_g, w_in, a_q_norm_g, a_k_norm_g, gla_w_alpha, gla_b_alpha, gla_out_norm_g,
                  w_out, norm2_g, mem_norm_g, mem_w_q, mem_w_k, mem_w_v, mem_q_norm_g,
                  mem_k_norm_g, mem_w_o, norm3_g, router_w, router_b, exp_w1, exp_b1, exp_w2,
                  exp_b2):
    main = 3 * A_WIDTH + 2 * B_QK_WIDTH + 2 * B_V_WIDTH
    w_lr = jnp.pad(w_in[:, main:], ((0, 0), (0, LANES - GATE_RANK)))
    rw = jnp.pad(router_w, ((0, 0), (0, LANES - N_EXPERTS)))
    rwh = rw.astype(BF16)
    return {
        "g1": norm1_g[None],
        "w_in": jnp.concatenate([w_in[:, :main], w_lr], axis=1).astype(BF16),
        "gq": jnp.tile(a_q_norm_g, A_HEADS)[None],
        "gk": jnp.tile(a_k_norm_g, A_HEADS)[None],
        "bd64": _block_diag(A_WIDTH, A_HEAD_DIM, 1.0 / A_HEAD_DIM, BF16),
        "wa": jnp.pad(gla_w_alpha, ((0, LANES - GATE_RANK), (0, 0))).astype(BF16),
        "ba": gla_b_alpha[None],
        "bones64": _block_diag(B_QK_WIDTH, B_KEY_DIM, 1.0, BF16),
        "bd128": _block_diag(B_V_WIDTH, B_VAL_DIM, 1.0 / B_VAL_DIM, BF16),
        "gout": jnp.tile(gla_out_norm_g, B_HEADS)[None],
        "woa": w_out[:A_WIDTH].astype(BF16),
        "wob": w_out[A_WIDTH:].astype(BF16),
        "g2": norm2_g[None],
        "gmem": mem_norm_g[None],
        "wmq": mem_w_q.astype(BF16),
        "wmk": mem_w_k.astype(BF16),
        "wmv": mem_w_v.astype(BF16),
        "gmq": jnp.tile(mem_q_norm_g, MEM_HEADS)[None],
        "gmk": jnp.tile(mem_k_norm_g, MEM_HEADS)[None],
        "wmo": mem_w_o.astype(BF16),
        "g3": norm3_g[None],
        "rwh": rwh,
        "rwl": (rw - rwh.astype(F32)).astype(BF16),
        "rb": jnp.pad(router_b, (0, LANES - N_EXPERTS), constant_values=-1e30)[None],
        "w1g": exp_w1[:, :, 0::2].astype(BF16),
        "w1l": exp_w1[:, :, 1::2].astype(BF16),
        "b1g": exp_b1[:, None, 0::2],
        "b1l": exp_b1[:, None, 1::2],
        "w2": exp_w2.astype(BF16),
        "b2": exp_b2[:, None, :],
    }


def _layer(xp, xs, mem_prompt, cache_k, cache_v, state_gla, cache_mk, cache_mv, wp):
    bsz, seq, _ = xp.shape
    db, t_new, _ = xs.shape
    w_buf = cache_k.shape[1]
    w_p = min(max(w for w, _ in DILATED_GROUPS), seq)

    xpf = xp.reshape(bsz * seq, D_MODEL)
    q, k, v, bq, bk, bv, gate, la = _in_proj(xpf, wp, BF16)
    oa = _swa_prompt(q, k, v, bsz, seq)

    def last_rows(a):
        a = a.reshape(A_WIDTH // LANES, bsz, seq, LANES)[:, :, seq - w_p:]
        return a.transpose(1, 2, 0, 3).reshape(bsz, w_p, A_WIDTH)
    s0 = jnp.zeros((bsz, B_QK_WIDTH, B_VAL_DIM), F32)
    ob, s_p = _gla(bq, bk, la, bv, gate, s0, wp, seq, GLA_CHUNK, GLA_SUB, GLA_STEP)
    mk, mv = _mem_kv(mem_prompt.reshape(bsz * N_MEM, D_MODEL), wp)
    cnt0 = jnp.zeros((1, LANES), F32)
    h2_p, xn_p, meta_p, cnt = _post(xpf, oa, ob, mk.reshape(bsz, N_MEM, MEM_WIDTH),
                                    mv.reshape(bsz, N_MEM, MEM_WIDTH), cnt0, wp,
                                    POST_TILE, 1, seq // POST_TILE)

    xsf = xs.reshape(db * t_new, D_MODEL)
    q, ks, vs, bq, bk, bv, gate, la = _in_proj(xsf, wp, F32)
    new3 = lambda a: a.transpose(1, 0, 2).reshape(db, t_new, A_WIDTH)
    oa_s, nk, nv = _swa_sample(new3(q), new3(ks), new3(vs), cache_k, cache_v)
    ob_s, s_s = _gla(bq, bk, la, bv, gate, state_gla, wp, t_new, t_new, t_new, t_new)
    h2_s, xn_s, meta_s, cnt = _post(xsf, oa_s.reshape(db * t_new, A_WIDTH), ob_s, cache_mk,
                                    cache_mv, cnt, wp, SAMPLE_SEQS * t_new, SAMPLE_SEQS, 1)

    y_p, y_s = _moe([(h2_p, xn_p, meta_p), (h2_s, xn_s, meta_s)], cnt, wp)
    return (y_p.reshape(bsz, seq, D_MODEL), y_s.reshape(db, t_new, D_MODEL),
            last_rows(k), last_rows(v), s_p, mk, mv, nk, nv, s_s)


def kernel(x_prompt, x_sample, mem_prompt, cache_swa_k, cache_swa_v, state_gla, cache_mem_k, cache_mem_v, norm1_g, w_in, a_q_norm_g, a_k_norm_g, gla_w_alpha, gla_b_alpha, gla_out_norm_g, w_out, norm2_g, mem_norm_g, mem_w_q, mem_w_k, mem_w_v, mem_q_norm_g, mem_k_norm_g, mem_w_o, norm3_g, router_w, router_b, exp_w1, exp_b1, exp_w2, exp_b2):
    depth = w_in.shape[0]
    bsz = x_prompt.shape[0]
    db, w_buf = cache_swa_k.shape[1], cache_swa_k.shape[2]
    xp, xs = x_prompt, x_sample
    per_layer = []
    for l in range(depth):
        wp = _prep_weights(
            norm1_g[l], w_in[l], a_q_norm_g[l], a_k_norm_g[l], gla_w_alpha[l], gla_b_alpha[l],
            gla_out_norm_g[l], w_out[l], norm2_g[l], mem_norm_g[l], mem_w_q[l], mem_w_k[l],
            mem_w_v[l], mem_q_norm_g[l], mem_k_norm_g[l], mem_w_o[l], norm3_g[l], router_w[l],
            router_b[l], exp_w1[l], exp_b1[l], exp_w2[l], exp_b2[l])
        xp, xs, kp, vp, s_p, mk, mv, nk, nv, s_s = _layer(
            xp, xs, mem_prompt,
            cache_swa_k[l].reshape(db, w_buf, A_WIDTH), cache_swa_v[l].reshape(db, w_buf, A_WIDTH),
            state_gla[l].reshape(db, B_QK_WIDTH, B_VAL_DIM),
            cache_mem_k[l].reshape(db, N_MEM, MEM_WIDTH), cache_mem_v[l].reshape(db, N_MEM, MEM_WIDTH),
            wp)
        w_p = kp.shape[1]
        per_layer.append((
            kp.reshape(bsz, w_p, A_HEADS, A_HEAD_DIM), vp.reshape(bsz, w_p, A_HEADS, A_HEAD_DIM),
            s_p.reshape(bsz, B_HEADS, B_KEY_DIM, B_VAL_DIM),
            mk.reshape(bsz, N_MEM, MEM_HEADS, MEM_HEAD_DIM), mv.reshape(bsz, N_MEM, MEM_HEADS, MEM_HEAD_DIM),
            nk.reshape(db, w_buf, A_HEADS, A_HEAD_DIM), nv.reshape(db, w_buf, A_HEADS, A_HEAD_DIM),
            s_s.reshape(db, B_HEADS, B_KEY_DIM, B_VAL_DIM)))
    stacked = [jnp.stack(t) for t in zip(*per_layer)]
    return (xp, xs, *stacked)
```

```python
import functools

import jax
import jax.numpy as jnp
import numpy as np
from jax import lax
from jax.experimental import pallas as pl
from jax.experimental.pallas import tpu as pltpu

F32 = jnp.float32
BF16 = jnp.bfloat16
I32 = jnp.int32

EPS = 1e-6
D_MODEL = 1024
A_HEADS, A_HEAD_DIM, A_WIDTH = 8, 64, 512
A_SCALE = A_HEAD_DIM ** -0.5
DILATED_GROUPS = ((128, 1), (512, 4), (2048, 16))
SPAN = 128
B_HEADS, B_KEY_DIM, B_VAL_DIM = 4, 64, 128
B_QK_WIDTH, B_V_WIDTH = 256, 512
GATE_RANK = 16
GATE_TEMP = 16.0
N_MEM, MEM_HEADS, MEM_HEAD_DIM, MEM_WIDTH = 256, 4, 128, 512
N_EXPERTS, TOP_K, D_FF = 32, 4, 1024
SWIGLU_ALPHA, SWIGLU_LIMIT = 1.702, 7.0

LANES = 128
VMEM_LIMIT = 56 * 1024 * 1024

PROJ_TILE = 512
SWA_BLOCK = 2048
SWA_UNROLL = 4
GLA_CHUNK, GLA_SUB, GLA_STEP = 64, 16, 512
POST_TILE = 256
SAMPLE_SEQS = 8
EXPERT_TILE = 512
ROW_TILE = 256
DMA_UNROLL = 8


def _cparams(sem, vmem=VMEM_LIMIT, **kw):
    return pltpu.CompilerParams(dimension_semantics=sem, vmem_limit_bytes=vmem, **kw)


def _full(shape):
    n = len(shape)
    return pl.BlockSpec(shape, lambda *_: (0,) * n)


def _rms(x, g):
    ms = jnp.mean(x * x, axis=-1, keepdims=True)
    return x * lax.rsqrt(ms + EPS) * g


def _dot(a, b):
    return jnp.dot(a, b, preferred_element_type=F32)


def _dot_nt(a, b):
    return lax.dot_general(a, b, (((1,), (1,)), ((), ())), preferred_element_type=F32)


def _block_diag(n, blk, val, dtype):
    i = np.arange(n)
    return jnp.asarray(np.where((i[:, None] // blk) == (i[None, :] // blk), val, 0.0), dtype)


def _in_proj_body(x_ref, g1_ref, w_ref, gq_ref, gk_ref, bd_ref, wa_ref, ba_ref,
                  q_ref, k_ref, v_ref, kc_ref, vc_ref, bq_ref, bk_ref, bv_ref, gate_ref, la_ref):
    xn = _rms(x_ref[...], g1_ref[...]).astype(BF16)
    bd = bd_ref[...]

    def proj(lo, hi):
        return _dot(xn, w_ref[:, lo:hi])

    def headnorm(z, g):
        ms = _dot((z * z).astype(BF16), bd)
        return z * lax.rsqrt(ms + EPS) * g

    def put_groups(ref, z):
        for p in range(A_WIDTH // LANES):
            ref[p] = z[:, p * LANES:(p + 1) * LANES]

    put_groups(q_ref, headnorm(proj(0, 512), gq_ref[...]) * A_SCALE)
    k = headnorm(proj(512, 1024), gk_ref[...])
    v = proj(1024, 1536)
    put_groups(k_ref, k)
    put_groups(v_ref, v)
    kc_ref[...] = k
    vc_ref[...] = v
    bq_ref[...] = proj(1536, 1792) * (B_KEY_DIM ** -0.5)
    bk_ref[...] = proj(1792, 2048)
    bv_ref[...] = proj(2048, 2560).astype(bv_ref.dtype)
    br = proj(2560, 3072)
    gate_ref[...] = (br * jax.nn.sigmoid(br)).astype(gate_ref.dtype)
    lr = proj(3072, 3200).astype(BF16)
    pre = _dot(lr, wa_ref[...]) + ba_ref[...]
    log_sig = jnp.minimum(pre, 0.0) - jnp.log1p(jnp.exp(-jnp.abs(pre)))
    la_ref[...] = log_sig * (1.0 / GATE_TEMP)


def _in_proj(x, wp, wide_dtype, seq, keep):
    n = x.shape[0]
    tm = min(PROJ_TILE, n)
    row = lambda w: pl.BlockSpec((tm, w), lambda i: (i, 0))
    ngrp = A_WIDTH // LANES
    grp = pl.BlockSpec((ngrp, tm, LANES), lambda i: (0, i, 0))
    if keep == seq:
        kept = row(A_WIDTH)
    else:
        tps, kt = seq // tm, keep // tm
        assert tps * tm == seq and kt * tm == keep
        kept = pl.BlockSpec((tm, A_WIDTH),
                            lambda i: ((i // tps) * kt + jnp.maximum(i % tps - (tps - kt), 0), 0))
    outs = [(256, F32), (256, F32), (512, wide_dtype), (512, wide_dtype), (256, F32)]
    return pl.pallas_call(
        _in_proj_body,
        grid=(n // tm,),
        in_specs=[row(D_MODEL), _full((1, D_MODEL)), _full((D_MODEL, 3200)), _full((1, 512)),
                  _full((1, 512)), _full((512, 512)), _full((LANES, 256)), _full((1, 256))],
        out_specs=[grp] * 3 + [kept] * 2 + [row(w) for w, _ in outs],
        out_shape=[jax.ShapeDtypeStruct((ngrp, n, LANES), F32)] * 3
        + [jax.ShapeDtypeStruct((n // seq * keep, A_WIDTH), F32)] * 2
        + [jax.ShapeDtypeStruct((n, w), dt) for w, dt in outs],
        compiler_params=_cparams(("arbitrary",)),
        name="in_proj",
    )(x, wp["g1"], wp["w_in"], wp["gq"], wp["gk"], wp["bd64"], wp["wa"], wp["ba"])


def _unroll_for(trips):
    return max(u for u in range(1, SWA_UNROLL + 1) if trips % u == 0)


def _ds(start, size, stride):
    return pl.ds(start, size) if stride == 1 else pl.ds(start, size, stride=stride)


def _swa_prompt_body(q_ref, kp_ref, kc_ref, vp_ref, vc_ref, o_ref, m_s, l_s, acc_s):
    i = pl.program_id(2)
    qb = SWA_BLOCK
    lane = lax.broadcasted_iota(I32, (SPAN, LANES), 1)
    lo_mask = lane < A_HEAD_DIM
    jq = lax.broadcasted_iota(I32, (SPAN, 2 * SPAN), 0)
    jk = lax.broadcasted_iota(I32, (SPAN, 2 * SPAN), 1)
    dist = jq + SPAN - jk
    band = (dist >= 0) & (dist <= SPAN)
    cur_half = jk >= SPAN

    def attend(qp, kp, vp, valid, rows, first_group):
        res = []
        for hh in range(2):
            msk = lo_mask if hh == 0 else jnp.logical_not(lo_mask)
            qm = jnp.where(msk, qp, 0.0).astype(BF16)
            s = jnp.where(valid, _dot_nt(qm, kp), -jnp.inf)
            m = jnp.max(s, axis=1, keepdims=True)
            pr = jnp.exp(s - m)
            l = jnp.sum(pr, axis=1, keepdims=True)
            res.append((m, l, _dot(pr.astype(BF16), vp)))
        m_new = jnp.where(lo_mask, res[0][0], res[1][0])
        l_new = jnp.where(lo_mask, res[0][1], res[1][1])
        o_new = jnp.where(lo_mask, res[0][2], res[1][2])
        if first_group:
            m_s[rows, :] = m_new
            l_s[rows, :] = l_new
            acc_s[rows, :] = o_new
        else:
            m_old = m_s[rows, :]
            m = jnp.maximum(m_old, m_new)
            a_old = jnp.exp(m_old - m)
            a_new = jnp.exp(m_new - m)
            m_s[rows, :] = m
            l_s[rows, :] = l_s[rows, :] * a_old + l_new * a_new
            acc_s[rows, :] = acc_s[rows, :] * a_old + o_new * a_new

    for gi, (_, dil) in enumerate(DILATED_GROUPS):
        unit = dil * SPAN
        nblk = qb // unit
        first = gi == 0

        def head_block(r, carry, dil=dil, unit=unit, first=first):
            rows = _ds(r, SPAN, dil)
            prev = _ds(qb - unit + r, SPAN, dil)
            ks = jnp.concatenate([kp_ref[prev, :], kc_ref[rows, :]], axis=0).astype(BF16)
            vs = jnp.concatenate([vp_ref[prev, :], vc_ref[rows, :]], axis=0).astype(BF16)
            valid = band & (cur_half | (i > 0))
            attend(q_ref[rows, :], ks, vs, valid, rows, first)
            return carry

        lax.fori_loop(0, dil, head_block, 0, unroll=_unroll_for(dil))

        if nblk > 1:
            def tail_block(idx, carry, dil=dil, unit=unit, first=first):
                n = idx // dil + 1
                r = idx % dil
                start = unit * n + r
                rows = _ds(start, SPAN, dil)
                keys = _ds(start - unit, 2 * SPAN, dil)
                attend(q_ref[rows, :], kc_ref[keys, :].astype(BF16), vc_ref[keys, :].astype(BF16),
                       band, rows, first)
                return carry

            lax.fori_loop(0, (nblk - 1) * dil, tail_block, 0, unroll=_unroll_for((nblk - 1) * dil))

    o_ref[...] = (acc_s[...] / l_s[...]).astype(o_ref.dtype)


def _swa_prompt(q, k, v, bsz, seq):
    qb = SWA_BLOCK
    nb = seq // qb
    cur = pl.BlockSpec((None, qb, LANES), lambda p, b, i: (p, b * nb + i, 0))
    prev = pl.BlockSpec((None, qb, LANES), lambda p, b, i: (p, b * nb + jnp.maximum(i - 1, 0), 0))
    return pl.pallas_call(
        _swa_prompt_body,
        grid=(A_WIDTH // LANES, bsz, nb),
        in_specs=[cur, prev, cur, prev, cur],
        out_specs=pl.BlockSpec((qb, LANES), lambda p, b, i: (b * nb + i, p)),
        out_shape=jax.ShapeDtypeStruct((bsz * seq, A_WIDTH), BF16),
        scratch_shapes=[pltpu.VMEM((qb, LANES), F32)] * 3,
        compiler_params=_cparams(("parallel", "parallel", "arbitrary")),
        name="swa_prompt",
    )(q, k, k, v, v)


def _swa_sample_body(q_ref, kn_ref, vn_ref, ck_ref, cv_ref, c1_ref, c2_ref,
                     o_ref, nk_ref, nv_ref):
    t_new = q_ref.shape[1]
    w_buf = ck_ref.shape[1]
    ck, cv = ck_ref[0], cv_ref[0]
    kn, vn = kn_ref[0], vn_ref[0]
    nk_ref[0, 0:w_buf - t_new, :] = ck[t_new:, :]
    nk_ref[0, w_buf - t_new:, :] = kn
    nv_ref[0, 0:w_buf - t_new, :] = cv[t_new:, :]
    nv_ref[0, w_buf - t_new:, :] = vn

    q = q_ref[0]
    lane = lax.broadcasted_iota(I32, (t_new, LANES), 1)
    lo_mask = lane < A_HEAD_DIM
    c1, c2 = c1_ref[...], c2_ref[...]
    outs = []
    for p in range(A_WIDTH // LANES):
        sl = slice(p * LANES, (p + 1) * LANES)
        qp = q[:, sl]
        qblk = jnp.concatenate([jnp.where(lo_mask, qp, 0.0), jnp.where(lo_mask, 0.0, qp)],
                               axis=0).astype(BF16)
        s1 = jnp.where(c1 > 0, _dot_nt(qblk, ck[:, sl].astype(BF16)), -jnp.inf)
        s2 = jnp.where(c2 > 0, _dot_nt(qblk, kn[:, sl].astype(BF16)), -jnp.inf)
        m = jnp.maximum(jnp.max(s1, axis=1, keepdims=True), jnp.max(s2, axis=1, keepdims=True))
        p1 = c1 * jnp.exp(s1 - m)
        p2 = c2 * jnp.exp(s2 - m)
        l = jnp.sum(p1, axis=1, keepdims=True) + jnp.sum(p2, axis=1, keepdims=True)
        o = (_dot(p1.astype(BF16), cv[:, sl].astype(BF16))
             + _dot(p2.astype(BF16), vn[:, sl].astype(BF16))) / l
        outs.append(jnp.where(lo_mask, o[:t_new], o[t_new:]))
    o_ref[0] = jnp.concatenate(outs, axis=1).astype(o_ref.dtype)


def _sample_multiplicity(t_new, w_buf):
    t = np.arange(t_new)[:, None]
    e = np.arange(w_buf + t_new)[None, :]
    d = w_buf + t - e
    c = np.zeros(d.shape, np.float32)
    for window, dil in DILATED_GROUPS:
        c += ((d >= 0) & (d % dil == 0) & (d <= window)).astype(np.float32)
    c = np.concatenate([c, c], axis=0)
    return jnp.asarray(c[:, :w_buf]), jnp.asarray(c[:, w_buf:])


def _swa_sample(q, kn, vn, cache_k, cache_v):
    db, t_new, w = q.shape
    w_buf = cache_k.shape[1]
    assert w_buf >= max(win for win, _ in DILATED_GROUPS) and t_new % 8 == 0
    c1, c2 = _sample_multiplicity(t_new, w_buf)
    new = pl.BlockSpec((1, t_new, w), lambda b: (b, 0, 0))
    cache = pl.BlockSpec((1, w_buf, w), lambda b: (b, 0, 0))
    return pl.pallas_call(
        _swa_sample_body,
        grid=(db,),
        in_specs=[new, new, new, cache, cache, _full(c1.shape), _full(c2.shape)],
        out_specs=[new, cache, cache],
        out_shape=[jax.ShapeDtypeStruct((db, t_new, w), BF16),
                   jax.ShapeDtypeStruct(cache_k.shape, cache_k.dtype),
                   jax.ShapeDtypeStruct(cache_v.shape, cache_v.dtype)],
        compiler_params=_cparams(("parallel",)),
        name="swa_sample",
    )(q, kn, vn, cache_k, cache_v, c1, c2)


def _gla_body(chunk, sub, nch, q_ref, k_ref, g_ref, v_ref, gate_ref, s0_ref, tril_ref, dmask_ref,
              bones_ref, sbm_ref, bd_ref, gout_ref, o_ref, sfin_ref, sbd):
    j = pl.program_id(1)
    sbm = sbm_ref[...]
    nsub = chunk // sub
    pad = B_KEY_DIM - chunk

    @pl.when(j == 0)
    def _():
        s0 = s0_ref[0]
        sbd[...] = jnp.concatenate([s0] * B_HEADS, axis=1) * sbm

    row = lax.broadcasted_iota(I32, (chunk, 1), 0)
    sub_id = row // sub
    lane_w = lax.broadcasted_iota(I32, (chunk, LANES * max(nsub - 1, 1)), 1)
    lo_w = (lane_w % LANES) < B_KEY_DIM

    def one_chunk(c, carry):
        off = pl.multiple_of(c * chunk, chunk)
        rows = pl.ds(off, chunk)
        q, k, g = q_ref[rows, :], k_ref[rows, :], g_ref[rows, :]
        v = v_ref[rows, :].astype(F32)
        g1 = g.astype(BF16)
        r1 = g - g1.astype(F32)
        g2 = r1.astype(BF16)
        g3 = (r1 - g2.astype(F32)).astype(BF16)
        tril = tril_ref[...]
        b = _dot(tril, g1) + _dot(tril, g2) + _dot(tril, g3)
        b_last = b[chunk - 1:chunk, :]
        state = sbd[...]

        o = _dot((q * jnp.exp(b)).astype(BF16), state.astype(BF16))

        bones = bones_ref[...]
        att = _dot((q * k).astype(BF16), bones) * dmask_ref[0]
        for d in range(1, sub):
            kd = pltpu.roll(k, d, 0)
            bd_ = pltpu.roll(b, d, 0)
            w = q * kd * jnp.exp(jnp.minimum(b - bd_, 0.0))
            att = att + _dot(w.astype(BF16), bones) * dmask_ref[d]

        if nsub > 1:
            qx, kx = [], []
            for i in range(1, nsub):
                r_i = b[sub * i - 1:sub * i, :]
                qx.append(jnp.where(sub_id == i, q * jnp.exp(jnp.minimum(b - r_i, 0.0)), 0.0))
                kx.append(jnp.where(sub_id < i, k * jnp.exp(jnp.minimum(r_i - b, 0.0)), 0.0))
            parts = []
            for p in range(B_QK_WIDTH // LANES):
                sl = slice(p * LANES, (p + 1) * LANES)
                qp = jnp.concatenate([x[:, sl] for x in qx], axis=1)
                kp = jnp.concatenate([x[:, sl] for x in kx], axis=1).astype(BF16)
                zero = jnp.zeros_like(kp)
                lhs = jnp.concatenate([jnp.where(lo_w, qp, 0.0), jnp.where(lo_w, 0.0, qp)],
                                      axis=1).astype(BF16)
                rhs = jnp.concatenate([jnp.concatenate([kp, zero], axis=1),
                                       jnp.concatenate([zero, kp], axis=1)], axis=0)
                parts.append(_dot_nt(lhs, rhs))
            att = att + jnp.concatenate(parts, axis=1)

        if pad:
            vrow = jnp.concatenate([v, jnp.zeros((pad, B_V_WIDTH), F32)], axis=0)
        else:
            vrow = v
        vbd = (jnp.concatenate([vrow] * B_HEADS, axis=0) * sbm).astype(BF16)
        o = o + _dot(att.astype(BF16), vbd)

        ke = (k * jnp.exp(b_last - b)).astype(BF16)
        upd = lax.dot_general(ke, v.astype(BF16), (((0,), (0,)), ((), ())),
                              preferred_element_type=F32)
        dec = jnp.transpose(jnp.broadcast_to(jnp.exp(b_last), (8, B_QK_WIDTH)))[:, 0:1]
        sbd[...] = (state * dec + upd) * sbm

        ms = _dot((o * o).astype(BF16), bd_ref[...])
        on = o * lax.rsqrt(ms + EPS) * gout_ref[...] * gate_ref[rows, :].astype(F32)
        o_ref[rows, :] = on.astype(o_ref.dtype)
        return carry

    lax.fori_loop(0, nch, one_chunk, 0)

    @pl.when(j == pl.num_programs(1) - 1)
    def _():
        s = sbd[...]
        sfin_ref[0] = jnp.concatenate(
            [s[h * B_KEY_DIM:(h + 1) * B_KEY_DIM, h * B_VAL_DIM:(h + 1) * B_VAL_DIM]
             for h in range(B_HEADS)], axis=0)


def _gla_consts(chunk, sub):
    t = np.arange(chunk)
    tril = (t[:, None] >= t[None, :]).astype(np.float32)
    lane = np.arange(B_QK_WIDTH)
    dmask = np.zeros((sub, chunk, B_QK_WIDTH), np.float32)
    for d in range(sub):
        ok = (t % sub) >= d
        dmask[d] = ((lane[None, :] % B_KEY_DIM) == (t[:, None] - d)) & ok[:, None]
    r = np.arange(B_QK_WIDTH)[:, None] // B_KEY_DIM
    c = np.arange(B_V_WIDTH)[None, :] // B_VAL_DIM
    sbm = (r == c).astype(np.float32)
    return jnp.asarray(tril, BF16), jnp.asarray(dmask), jnp.asarray(sbm)


def _gla(q, k, g, v, gate, s0, wp, length, chunk, sub, step):
    n = q.shape[0]
    bsz = n // length
    assert chunk == sub or chunk == B_KEY_DIM
    tril, dmask, sbm = _gla_consts(chunk, sub)
    nstep = length // step
    row = lambda w: pl.BlockSpec((step, w), lambda b, j: (b * nstep + j, 0))
    st = pl.BlockSpec((1, B_QK_WIDTH, B_VAL_DIM), lambda b, j: (b, 0, 0))
    return pl.pallas_call(
        functools.partial(_gla_body, chunk, sub, step // chunk),
        grid=(bsz, nstep),
        in_specs=[row(256), row(256), row(256), row(512), row(512), st, _full(tril.shape),
                  _full(dmask.shape), _full((256, 256)), _full(sbm.shape), _full((512, 512)),
                  _full((1, 512))],
        out_specs=[row(512), st],
        out_shape=[jax.ShapeDtypeStruct((n, B_V_WIDTH), BF16),
                   jax.ShapeDtypeStruct(s0.shape, F32)],
        scratch_shapes=[pltpu.VMEM((B_QK_WIDTH, B_V_WIDTH), F32)],
        compiler_params=_cparams(("parallel", "arbitrary")),
        name="gla",
    )(q, k, g, v, gate, s0, tril, dmask, wp["bones64"], sbm, wp["bd128"], wp["gout"])


def _head_rms(z, g, scale=1.0):
    parts = []
    for h in range(MEM_HEADS):
        zh = z[:, h * LANES:(h + 1) * LANES]
        parts.append(zh * lax.rsqrt(jnp.mean(zh * zh, axis=-1, keepdims=True) + EPS))
    return jnp.concatenate(parts, axis=1) * (g * scale)


def _mem_kv_body(m_ref, gn_ref, wk_ref, wv_ref, gk_ref, mk_ref, mv_ref):
    mn = _rms(m_ref[...], gn_ref[...]).astype(BF16)
    mk_ref[...] = _head_rms(_dot(mn, wk_ref[...]), gk_ref[...])
    mv_ref[...] = _dot(mn, wv_ref[...])


def _mem_kv(mem, wp):
    n = mem.shape[0]
    tm = 256
    row = lambda w: pl.BlockSpec((tm, w), lambda i: (i, 0))
    return pl.pallas_call(
        _mem_kv_body,
        grid=(n // tm,),
        in_specs=[row(D_MODEL), _full((1, D_MODEL)), _full((D_MODEL, 512)), _full((D_MODEL, 512)),
                  _full((1, 512))],
        out_specs=[row(512), row(512)],
        out_shape=[jax.ShapeDtypeStruct((n, 512), F32)] * 2,
        compiler_params=_cparams(("parallel",)),
        name="mem_kv",
    )(mem, wp["gmem"], wp["wmk"], wp["wmv"], wp["gmk"])


def _post_body(nseq, x_ref, oa_ref, ob_ref, woa_ref, wob_ref, g2_ref, wq_ref, gmq_ref, mk_ref,
               mv_ref, wo_ref, g3_ref, rwh_ref, rwl_ref, rb_ref, cnt0_ref, tri_ref,
               h2_ref, xn_ref, meta_ref, cnt_ref, carry):
    tm = x_ref.shape[0]

    @pl.when(pl.program_id(0) == 0)
    def _():
        carry[...] = cnt0_ref[...]

    h = x_ref[...] + _dot(oa_ref[...], woa_ref[...]) + _dot(ob_ref[...], wob_ref[...])
    hn = _rms(h, g2_ref[...]).astype(BF16)
    qm = _head_rms(_dot(hn, wq_ref[...]), gmq_ref[...], MEM_HEAD_DIM ** -0.5).astype(BF16)
    nk = nseq * N_MEM
    mk = mk_ref[...].reshape(nk, MEM_WIDTH)
    mv = mv_ref[...].reshape(nk, MEM_WIDTH)
    if nseq > 1:
        rt = lax.broadcasted_iota(I32, (tm, nk), 0) // (tm // nseq)
        ct = lax.broadcasted_iota(I32, (tm, nk), 1) // N_MEM
        same = rt == ct
    outs = []
    for hd in range(MEM_HEADS):
        sl = slice(hd * LANES, (hd + 1) * LANES)
        s = _dot_nt(qm[:, sl], mk[:, sl].astype(BF16))
        if nseq > 1:
            s = jnp.where(same, s, -jnp.inf)
        m = jnp.max(s, axis=1, keepdims=True)
        pr = jnp.exp(s - m)
        l = jnp.sum(pr, axis=1, keepdims=True)
        outs.append(_dot(pr.astype(BF16), mv[:, sl].astype(BF16)) / l)
    h2 = h + _dot(jnp.concatenate(outs, axis=1).astype(BF16), wo_ref[...])
    h2_ref[...] = h2
    xn = _rms(h2, g3_ref[...])
    xn_ref[...] = xn

    x1 = xn.astype(BF16)
    x2 = (xn - x1.astype(F32)).astype(BF16)
    logits = (_dot(x1, rwh_ref[...]) + _dot(x1, rwl_ref[...]) + _dot(x2, rwh_ref[...])
              + rb_ref[...])
    lane = lax.broadcasted_iota(I32, (tm, LANES), 1)
    vals, idxs, hots = [], [], []
    work = logits
    for _ in range(TOP_K):
        m = jnp.max(work, axis=1, keepdims=True)
        idx = jnp.min(jnp.where(work == m, lane, LANES), axis=1, keepdims=True)
        hot = lane == idx
        vals.append(m)
        idxs.append(idx)
        hots.append(hot)
        work = jnp.where(hot, -jnp.inf, work)
    exps = [jnp.exp(v - vals[0]) for v in vals]
    den = exps[0] + exps[1] + exps[2] + exps[3]

    sel = (hots[0] | hots[1] | hots[2] | hots[3]).astype(F32)
    before = _dot(tri_ref[...], sel.astype(BF16)) + carry[...]
    carry[...] = carry[...] + jnp.sum(sel, axis=0, keepdims=True)
    cnt_ref[...] = carry[...]

    meta = jnp.zeros((tm, LANES), F32)
    for kk in range(TOP_K):
        rank = jnp.sum(jnp.where(hots[kk], before, 0.0), axis=1, keepdims=True)
        meta = jnp.where(lane == kk, idxs[kk].astype(F32), meta)
        meta = jnp.where(lane == TOP_K + kk, rank, meta)
        meta = jnp.where(lane == 2 * TOP_K + kk, exps[kk] / den, meta)
    meta_ref[...] = meta


def _post(x, oa, ob, mk, mv, cnt0, wp, tm, nseq, tiles_per_mem):
    n = x.shape[0]
    row = lambda w: pl.BlockSpec((tm, w), lambda i: (i, 0))
    mem = pl.BlockSpec((nseq, N_MEM, MEM_WIDTH), lambda i: (i // tiles_per_mem, 0, 0))
    tri = jnp.asarray(np.tril(np.ones((tm, tm), np.float32), -1), BF16)
    return pl.pallas_call(
        functools.partial(_post_body, nseq),
        grid=(n // tm,),
        in_specs=[row(D_MODEL), row(512), row(512), _full((512, D_MODEL)), _full((512, D_MODEL)),
                  _full((1, D_MODEL)), _full((D_MODEL, 512)), _full((1, 512)), mem, mem,
                  _full((512, D_MODEL)), _full((1, D_MODEL)), _full((D_MODEL, LANES)),
                  _full((D_MODEL, LANES)), _full((1, LANES)), _full((1, LANES)), _full((tm, tm))],
        out_specs=[row(D_MODEL), row(D_MODEL), row(LANES), _full((1, LANES))],
        out_shape=[jax.ShapeDtypeStruct((n, D_MODEL), F32), jax.ShapeDtypeStruct((n, D_MODEL), F32),
                   jax.ShapeDtypeStruct((n, LANES), F32), jax.ShapeDtypeStruct((1, LANES), F32)],
        scratch_shapes=[pltpu.VMEM((1, LANES), F32)],
        compiler_params=_cparams(("arbitrary",)),
        name="post",
    )(x, oa, ob, wp["woa"], wp["wob"], wp["g2"], wp["wmq"], wp["gmq"], mk, mv, wp["wmo"],
      wp["g3"], wp["rwh"], wp["rwl"], wp["rb"], cnt0, tri)


def _row_copy(src, src_row, dst, dst_row, sem):
    return pltpu.make_async_copy(src.at[pl.ds(src_row, 1)], dst.at[pl.ds(dst_row, 1)], sem)


def _dispatch_body(dest_ref, x_ref, xs_in_ref, xs_ref, sem):
    del xs_in_ref
    n = dest_ref.shape[2]

    def issue(j, carry):
        _row_copy(x_ref, j // TOP_K, xs_ref, dest_ref[0, 0, j], sem).start()
        return carry

    lax.fori_loop(0, n, issue, 0, unroll=DMA_UNROLL)
    pltpu.make_async_copy(xs_ref.at[pl.ds(0, n)], xs_ref.at[pl.ds(0, n)], sem).wait()


def _dispatch(dest, xn, xs):
    n = xn.shape[0]
    tt = min(ROW_TILE, n)
    dest3 = dest.reshape(n // tt, 1, tt * TOP_K)
    return pl.pallas_call(
        _dispatch_body,
        grid=(n // tt,),
        in_specs=[pl.BlockSpec((1, 1, tt * TOP_K), lambda i: (i, 0, 0), memory_space=pltpu.SMEM),
                  pl.BlockSpec((tt, D_MODEL), lambda i: (i, 0)),
                  pl.BlockSpec(memory_space=pl.ANY)],
        out_specs=pl.BlockSpec(memory_space=pl.ANY),
        out_shape=jax.ShapeDtypeStruct(xs.shape, xs.dtype),
        scratch_shapes=[pltpu.SemaphoreType.DMA],
        input_output_aliases={2: 0},
        compiler_params=_cparams(("arbitrary",), disable_bounds_checks=True),
        name="moe_dispatch",
    )(dest3, xn, xs)


def _expert_body(te_ref, nt_ref, x_ref, w1g_ref, w1l_ref, b1g_ref, b1l_ref, w2_ref, b2_ref, y_ref):
    @pl.when(pl.program_id(0) < nt_ref[0])
    def _():
        x = x_ref[...].astype(BF16)
        glu = jnp.minimum(_dot(x, w1g_ref[0]) + b1g_ref[0], SWIGLU_LIMIT)
        lin = jnp.clip(_dot(x, w1l_ref[0]) + b1l_ref[0], -SWIGLU_LIMIT, SWIGLU_LIMIT)
        act = glu * jax.nn.sigmoid(SWIGLU_ALPHA * glu) * (lin + 1.0)
        y_ref[...] = _dot(act.astype(BF16), w2_ref[0]) + b2_ref[0]

    @pl.when(pl.program_id(0) >= nt_ref[0])
    def _():
        y_ref[...] = jnp.zeros_like(y_ref)


def _experts(tile_expert, n_tiles, xs, wp):
    rows = xs.shape[0]
    tile = lambda i, te, nt: (jnp.minimum(i, nt[0] - 1), 0)
    out_tile = lambda i, te, nt: (i, 0)
    wsel = lambda i, te, nt: (te[jnp.minimum(i, nt[0] - 1)], 0, 0)
    wspec = lambda r, c: pl.BlockSpec((1, r, c), wsel)
    return pl.pallas_call(
        _expert_body,
        grid_spec=pltpu.PrefetchScalarGridSpec(
            num_scalar_prefetch=2, grid=(rows // EXPERT_TILE,),
            in_specs=[pl.BlockSpec((EXPERT_TILE, D_MODEL), tile),
                      wspec(D_MODEL, D_FF), wspec(D_MODEL, D_FF), wspec(1, D_FF), wspec(1, D_FF),
                      wspec(D_FF, D_MODEL), wspec(1, D_MODEL)],
            out_specs=pl.BlockSpec((EXPERT_TILE, D_MODEL), out_tile)),
        out_shape=jax.ShapeDtypeStruct((rows, D_MODEL), F32),
        compiler_params=_cparams(("arbitrary",)),
        name="moe_experts",
    )(tile_expert, n_tiles, xs, wp["w1g"], wp["w1l"], wp["b1g"], wp["b1l"], wp["w2"], wp["b2"])


def _combine_body(pos_ref, meta_ref, h_ref, ys_ref, o_ref, buf, sem):
    n = pos_ref.shape[2]

    def issue(j, carry):
        _row_copy(ys_ref, pos_ref[0, 0, j], buf.at[j % TOP_K], j // TOP_K, sem).start()
        return carry

    lax.fori_loop(0, n, issue, 0, unroll=DMA_UNROLL)
    pltpu.make_async_copy(buf, buf, sem).wait()
    meta = meta_ref[...]
    out = h_ref[...]
    for kk in range(TOP_K):
        out = out + meta[:, 2 * TOP_K + kk:2 * TOP_K + kk + 1] * buf[kk]
    o_ref[...] = out


def _combine(pos, meta, h2, ys):
    n = h2.shape[0]
    tt = min(ROW_TILE, n)
    pos3 = pos.reshape(n // tt, 1, tt * TOP_K)
    row = lambda w: pl.BlockSpec((tt, w), lambda i: (i, 0))
    return pl.pallas_call(
        _combine_body,
        grid=(n // tt,),
        in_specs=[pl.BlockSpec((1, 1, tt * TOP_K), lambda i: (i, 0, 0), memory_space=pltpu.SMEM),
                  row(LANES), row(D_MODEL), pl.BlockSpec(memory_space=pl.ANY)],
        out_specs=row(D_MODEL),
        out_shape=jax.ShapeDtypeStruct((n, D_MODEL), F32),
        scratch_shapes=[pltpu.VMEM((TOP_K, tt, D_MODEL), F32), pltpu.SemaphoreType.DMA],
        compiler_params=_cparams(("arbitrary",), disable_bounds_checks=True),
        name="moe_combine",
    )(pos3, meta, h2, ys)


def _moe(groups, cnt, wp):
    n_pairs = sum(g[0].shape[0] for g in groups) * TOP_K
    rows = n_pairs + N_EXPERTS * EXPERT_TILE
    n_tiles_max = rows // EXPERT_TILE
    counts = cnt[0, :N_EXPERTS].astype(I32)
    padded = (counts + EXPERT_TILE - 1) // EXPERT_TILE * EXPERT_TILE
    ends = jnp.cumsum(padded)
    offs = ends - padded
    n_tiles = (ends[-1] // EXPERT_TILE).reshape(1)
    tile_ids = jnp.arange(n_tiles_max, dtype=I32)
    tile_expert = jnp.minimum(
        jnp.sum((ends // EXPERT_TILE)[None, :] <= tile_ids[:, None], axis=1), N_EXPERTS - 1).astype(I32)

    xs = jnp.zeros((rows, D_MODEL), F32)
    dests = []
    for _, xn, meta in groups:
        eidx = meta[:, 0:TOP_K].astype(I32)
        dest = offs[eidx] + meta[:, TOP_K:2 * TOP_K].astype(I32)
        dests.append(dest)
        xs = _dispatch(dest, xn, xs)
    ys = _experts(tile_expert, n_tiles, xs, wp)
    return [_combine(dest, meta, h2, ys) for dest, (h2, _, meta) in zip(dests, groups)]


def _w1_split_body(w_ref, sel_ref, g_ref, l_ref):
    sel = sel_ref[...]
    for j in range(D_FF // LANES):
        z = _dot(w_ref[0, :, 2 * LANES * j:2 * LANES * (j + 1)].astype(BF16), sel)
        g_ref[0, :, LANES * j:LANES * (j + 1)] = z[:, :LANES].astype(BF16)
        l_ref[0, :, LANES * j:LANES * (j + 1)] = z[:, LANES:].astype(BF16)


def _w1_split(w1):
    c = np.arange(2 * LANES)
    sel = np.zeros((2 * LANES, 2 * LANES), np.float32)
    sel[c, (c % 2) * LANES + c // 2] = 1.0
    spec = lambda w: pl.BlockSpec((1, D_MODEL, w), lambda e: (e, 0, 0))
    return pl.pallas_call(
        _w1_split_body,
        grid=(N_EXPERTS,),
        in_specs=[spec(2 * D_FF), _full((2 * LANES, 2 * LANES))],
        out_specs=[spec(D_FF), spec(D_FF)],
        out_shape=[jax.ShapeDtypeStruct((N_EXPERTS, D_MODEL, D_FF), BF16)] * 2,
        compiler_params=_cparams(("parallel",)),
        name="w1_split",
    )(w1, jnp.asarray(sel, BF16))


def _prep_weights(norm1_g, w_in, a_q_norm_g, a_k_norm_g, gla_w_alpha, gla_b_alpha, gla_out_norm_g,
                  w_out, norm2_g, mem_norm_g, mem_w_q, mem_w_k, mem_w_v, mem_q_norm_g,
                  mem_k_norm_g, mem_w_o, norm3_g, router_w, router_b, exp_w1, exp_b1, exp_w2,
                  exp_b2):
    main = 3 * A_WIDTH + 2 * B_QK_WIDTH + 2 * B_V_WIDTH
    w_lr = jnp.pad(w_in[:, main:], ((0, 0), (0, LANES - GATE_RANK)))
    rw = jnp.pad(router_w, ((0, 0), (0, LANES - N_EXPERTS)))
    rwh = rw.astype(BF16)
    w1g, w1l = _w1_split(exp_w1)
    return {
        "g1": norm1_g[None],
        "w_in": jnp.concatenate([w_in[:, :main], w_lr], axis=1).astype(BF16),
        "gq": jnp.tile(a_q_norm_g, A_HEADS)[None],
        "gk": jnp.tile(a_k_norm_g, A_HEADS)[None],
        "bd64": _block_diag(A_WIDTH, A_HEAD_DIM, 1.0 / A_HEAD_DIM, BF16),
        "wa": jnp.pad(gla_w_alpha, ((0, LANES - GATE_RANK), (0, 0))).astype(BF16),
        "ba": gla_b_alpha[None],
        "bones64": _block_diag(B_QK_WIDTH, B_KEY_DIM, 1.0, BF16),
        "bd128": _block_diag(B_V_WIDTH, B_VAL_DIM, 1.0 / B_VAL_DIM, BF16),
        "gout": jnp.tile(gla_out_norm_g, B_HEADS)[None],
        "woa": w_out[:A_WIDTH].astype(BF16),
        "wob": w_out[A_WIDTH:].astype(BF16),
        "g2": norm2_g[None],
        "gmem": mem_norm_g[None],
        "wmq": mem_w_q.astype(BF16),
        "wmk": mem_w_k.astype(BF16),
        "wmv": mem_w_v.astype(BF16),
        "gmq": jnp.tile(mem_q_norm_g, MEM_HEADS)[None],
        "gmk": jnp.tile(mem_k_norm_g, MEM_HEADS)[None],
        "wmo": mem_w_o.astype(BF16),
        "g3": norm3_g[None],
        "rwh": rwh,
        "rwl": (rw - rwh.astype(F32)).astype(BF16),
        "rb": jnp.pad(router_b, (0, LANES - N_EXPERTS), constant_values=-1e30)[None],
        "w1g": w1g,
        "w1l": w1l,
        "b1g": exp_b1[:, None, 0::2],
        "b1l": exp_b1[:, None, 1::2],
        "w2": exp_w2.astype(BF16),
        "b2": exp_b2[:, None, :],
    }


def _layer(xp, xs, mem_prompt, cache_k, cache_v, state_gla, cache_mk, cache_mv, wp):
    bsz, seq, _ = xp.shape
    db, t_new, _ = xs.shape
    w_buf = cache_k.shape[1]
    w_p = min(max(w for w, _ in DILATED_GROUPS), seq)

    xpf = xp.reshape(bsz * seq, D_MODEL)
    q, k, v, k_last, v_last, bq, bk, bv, gate, la = _in_proj(xpf, wp, BF16, seq, w_p)
    oa = _swa_prompt(q, k, v, bsz, seq)
    s0 = jnp.zeros((bsz, B_QK_WIDTH, B_VAL_DIM), F32)
    ob, s_p = _gla(bq, bk, la, bv, gate, s0, wp, seq, GLA_CHUNK, GLA_SUB, GLA_STEP)
    mk, mv = _mem_kv(mem_prompt.reshape(bsz * N_MEM, D_MODEL), wp)
    cnt0 = jnp.zeros((1, LANES), F32)
    h2_p, xn_p, meta_p, cnt = _post(xpf, oa, ob, mk.reshape(bsz, N_MEM, MEM_WIDTH),
                                    mv.reshape(bsz, N_MEM, MEM_WIDTH), cnt0, wp,
                                    POST_TILE, 1, seq // POST_TILE)

    xsf = xs.reshape(db * t_new, D_MODEL)
    q, _, _, ks, vs, bq, bk, bv, gate, la = _in_proj(xsf, wp, F32, t_new, t_new)
    new3 = lambda a: a.reshape(db, t_new, A_WIDTH)
    oa_s, nk, nv = _swa_sample(new3(q.transpose(1, 0, 2)), new3(ks), new3(vs), cache_k, cache_v)
    ob_s, s_s = _gla(bq, bk, la, bv, gate, state_gla, wp, t_new, t_new, t_new, t_new)
    h2_s, xn_s, meta_s, cnt = _post(xsf, oa_s.reshape(db * t_new, A_WIDTH), ob_s, cache_mk,
                                    cache_mv, cnt, wp, SAMPLE_SEQS * t_new, SAMPLE_SEQS, 1)

    y_p, y_s = _moe([(h2_p, xn_p, meta_p), (h2_s, xn_s, meta_s)], cnt, wp)
    return (y_p.reshape(bsz, seq, D_MODEL), y_s.reshape(db, t_new, D_MODEL),
            k_last.reshape(bsz, w_p, A_WIDTH), v_last.reshape(bsz, w_p, A_WIDTH),
            s_p, mk, mv, nk, nv, s_s)


def kernel(x_prompt, x_sample, mem_prompt, cache_swa_k, cache_swa_v, state_gla, cache_mem_k, cache_mem_v, norm1_g, w_in, a_q_norm_g, a_k_norm_g, gla_w_alpha, gla_b_alpha, gla_out_norm_g, w_out, norm2_g, mem_norm_g, mem_w_q, mem_w_k, mem_w_v, mem_q_norm_g, mem_k_norm_g, mem_w_o, norm3_g, router_w, router_b, exp_w1, exp_b1, exp_w2, exp_b2):
    depth = w_in.shape[0]
    bsz = x_prompt.shape[0]
    db, w_buf = cache_swa_k.shape[1], cache_swa_k.shape[2]
    xp, xs = x_prompt, x_sample
    per_layer = []
    for l in range(depth):
        wp = _prep_weights(
            norm1_g[l], w_in[l], a_q_norm_g[l], a_k_norm_g[l], gla_w_alpha[l], gla_b_alpha[l],
            gla_out_norm_g[l], w_out[l], norm2_g[l], mem_norm_g[l], mem_w_q[l], mem_w_k[l],
            mem_w_v[l], mem_q_norm_g[l], mem_k_norm_g[l], mem_w_o[l], norm3_g[l], router_w[l],
            router_b[l], exp_w1[l], exp_b1[l], exp_w2[l], exp_b2[l])
        xp, xs, kp, vp, s_p, mk, mv, nk, nv, s_s = _layer(
            xp, xs, mem_prompt,
            cache_swa_k[l].reshape(db, w_buf, A_WIDTH), cache_swa_v[l].reshape(db, w_buf, A_WIDTH),
            state_gla[l].reshape(db, B_QK_WIDTH, B_VAL_DIM),
            cache_mem_k[l].reshape(db, N_MEM, MEM_WIDTH), cache_mem_v[l].reshape(db, N_MEM, MEM_WIDTH),
            wp)
        w_p = kp.shape[1]
        per_layer.append((
            kp.reshape(bsz, w_p, A_HEADS, A_HEAD_DIM), vp.reshape(bsz, w_p, A_HEADS, A_HEAD_DIM),
            s_p.reshape(bsz, B_HEADS, B_KEY_DIM, B_VAL_DIM),
            mk.reshape(bsz, N_MEM, MEM_HEADS, MEM_HEAD_DIM), mv.reshape(bsz, N_MEM, MEM_HEADS, MEM_HEAD_DIM),
            nk.reshape(db, w_buf, A_HEADS, A_HEAD_DIM), nv.reshape(db, w_buf, A_HEADS, A_HEAD_DIM),
            s_s.reshape(db, B_HEADS, B_KEY_DIM, B_VAL_DIM)))
    stacked = [jnp.stack(t) for t in zip(*per_layer)]
    return (xp, xs, *stacked)
```

```python
import functools

import jax
import jax.numpy as jnp
import numpy as np
from jax import lax
from jax.experimental import pallas as pl
from jax.experimental.pallas import tpu as pltpu

F32 = jnp.float32
BF16 = jnp.bfloat16
I32 = jnp.int32

EPS = 1e-6
D_MODEL = 1024
A_HEADS, A_HEAD_DIM, A_WIDTH = 8, 64, 512
A_SCALE = A_HEAD_DIM ** -0.5
DILATED_GROUPS = ((128, 1), (512, 4), (2048, 16))
SPAN = 128
B_HEADS, B_KEY_DIM, B_VAL_DIM = 4, 64, 128
B_QK_WIDTH, B_V_WIDTH = 256, 512
GATE_RANK = 16
GATE_TEMP = 16.0
N_MEM, MEM_HEADS, MEM_HEAD_DIM, MEM_WIDTH = 256, 4, 128, 512
N_EXPERTS, TOP_K, D_FF = 32, 4, 1024
SWIGLU_ALPHA, SWIGLU_LIMIT = 1.702, 7.0

LANES = 128
VMEM_LIMIT = 56 * 1024 * 1024

PROJ_TILE = 512
SWA_BLOCK = 2048
SWA_UNROLL = 4
GLA_CHUNK, GLA_SUB, GLA_STEP = 64, 16, 512
POST_TILE = 256
SAMPLE_SEQS = 8
EXPERT_TILE = 512
DISPATCH_BLOCK = 512
GROUP_ROWS = 8
GROUP_BITS = (DISPATCH_BLOCK // GROUP_ROWS).bit_length()
LOCAL_ROWS = DISPATCH_BLOCK * TOP_K + N_EXPERTS * GROUP_ROWS


def _cparams(sem, vmem=VMEM_LIMIT, **kw):
    return pltpu.CompilerParams(dimension_semantics=sem, vmem_limit_bytes=vmem, **kw)


def _full(shape):
    n = len(shape)
    return pl.BlockSpec(shape, lambda *_: (0,) * n)


def _rms(x, g):
    ms = jnp.mean(x * x, axis=-1, keepdims=True)
    return x * lax.rsqrt(ms + EPS) * g


def _dot(a, b):
    return jnp.dot(a, b, preferred_element_type=F32)


def _dot_nt(a, b):
    return lax.dot_general(a, b, (((1,), (1,)), ((), ())), preferred_element_type=F32)


def _block_diag(n, blk, val, dtype):
    i = np.arange(n)
    return jnp.asarray(np.where((i[:, None] // blk) == (i[None, :] // blk), val, 0.0), dtype)


def _in_proj_body(x_ref, g1_ref, w_ref, gq_ref, gk_ref, bd_ref, wa_ref, ba_ref,
                  q_ref, k_ref, v_ref, kc_ref, vc_ref, bq_ref, bk_ref, bv_ref, gate_ref, la_ref):
    xn = _rms(x_ref[...], g1_ref[...]).astype(BF16)
    bd = bd_ref[...]

    def proj(lo, hi):
        return _dot(xn, w_ref[:, lo:hi])

    def headnorm(z, g):
        ms = _dot((z * z).astype(BF16), bd)
        return z * lax.rsqrt(ms + EPS) * g

    def put_groups(ref, z):
        for p in range(A_WIDTH // LANES):
            ref[p] = z[:, p * LANES:(p + 1) * LANES]

    put_groups(q_ref, headnorm(proj(0, 512), gq_ref[...]) * A_SCALE)
    k = headnorm(proj(512, 1024), gk_ref[...])
    v = proj(1024, 1536)
    put_groups(k_ref, k)
    put_groups(v_ref, v)
    kc_ref[...] = k
    vc_ref[...] = v
    bq_ref[...] = proj(1536, 1792) * (B_KEY_DIM ** -0.5)
    bk_ref[...] = proj(1792, 2048)
    bv_ref[...] = proj(2048, 2560).astype(bv_ref.dtype)
    br = proj(2560, 3072)
    gate_ref[...] = (br * jax.nn.sigmoid(br)).astype(gate_ref.dtype)
    lr = proj(3072, 3200).astype(BF16)
    pre = _dot(lr, wa_ref[...]) + ba_ref[...]
    log_sig = jnp.minimum(pre, 0.0) - jnp.log1p(jnp.exp(-jnp.abs(pre)))
    la_ref[...] = log_sig * (1.0 / GATE_TEMP)


def _in_proj(x, wp, wide_dtype, seq, keep):
    n = x.shape[0]
    tm = min(PROJ_TILE, n)
    row = lambda w: pl.BlockSpec((tm, w), lambda i: (i, 0))
    ngrp = A_WIDTH // LANES
    grp = pl.BlockSpec((ngrp, tm, LANES), lambda i: (0, i, 0))
    if keep == seq:
        kept = row(A_WIDTH)
    else:
        tps, kt = seq // tm, keep // tm
        assert tps * tm == seq and kt * tm == keep
        kept = pl.BlockSpec((tm, A_WIDTH),
                            lambda i: ((i // tps) * kt + jnp.maximum(i % tps - (tps - kt), 0), 0))
    outs = [(256, F32), (256, F32), (512, wide_dtype), (512, wide_dtype), (256, F32)]
    return pl.pallas_call(
        _in_proj_body,
        grid=(n // tm,),
        in_specs=[row(D_MODEL), _full((1, D_MODEL)), _full((D_MODEL, 3200)), _full((1, 512)),
                  _full((1, 512)), _full((512, 512)), _full((LANES, 256)), _full((1, 256))],
        out_specs=[grp] * 3 + [kept] * 2 + [row(w) for w, _ in outs],
        out_shape=[jax.ShapeDtypeStruct((ngrp, n, LANES), F32)] * 3
        + [jax.ShapeDtypeStruct((n // seq * keep, A_WIDTH), F32)] * 2
        + [jax.ShapeDtypeStruct((n, w), dt) for w, dt in outs],
        compiler_params=_cparams(("arbitrary",)),
        name="in_proj",
    )(x, wp["g1"], wp["w_in"], wp["gq"], wp["gk"], wp["bd64"], wp["wa"], wp["ba"])


def _unroll_for(trips):
    return max(u for u in range(1, SWA_UNROLL + 1) if trips % u == 0)


def _ds(start, size, stride):
    return pl.ds(start, size) if stride == 1 else pl.ds(start, size, stride=stride)


def _swa_prompt_body(q_ref, kp_ref, kc_ref, vp_ref, vc_ref, o_ref, m_s, l_s, acc_s):
    i = pl.program_id(2)
    qb = SWA_BLOCK
    lane = lax.broadcasted_iota(I32, (SPAN, LANES), 1)
    lo_mask = lane < A_HEAD_DIM
    jq = lax.broadcasted_iota(I32, (SPAN, 2 * SPAN), 0)
    jk = lax.broadcasted_iota(I32, (SPAN, 2 * SPAN), 1)
    dist = jq + SPAN - jk
    band = (dist >= 0) & (dist <= SPAN)
    cur_half = jk >= SPAN

    def attend(qp, kp, vp, valid, rows, first_group):
        res = []
        for hh in range(2):
            msk = lo_mask if hh == 0 else jnp.logical_not(lo_mask)
            qm = jnp.where(msk, qp, 0.0).astype(BF16)
            s = jnp.where(valid, _dot_nt(qm, kp), -jnp.inf)
            m = jnp.max(s, axis=1, keepdims=True)
            pr = jnp.exp(s - m)
            l = jnp.sum(pr, axis=1, keepdims=True)
            res.append((m, l, _dot(pr.astype(BF16), vp)))
        m_new = jnp.where(lo_mask, res[0][0], res[1][0])
        l_new = jnp.where(lo_mask, res[0][1], res[1][1])
        o_new = jnp.where(lo_mask, res[0][2], res[1][2])
        if first_group:
            m_s[rows, :] = m_new
            l_s[rows, :] = l_new
            acc_s[rows, :] = o_new
        else:
            m_old = m_s[rows, :]
            m = jnp.maximum(m_old, m_new)
            a_old = jnp.exp(m_old - m)
            a_new = jnp.exp(m_new - m)
            m_s[rows, :] = m
            l_s[rows, :] = l_s[rows, :] * a_old + l_new * a_new
            acc_s[rows, :] = acc_s[rows, :] * a_old + o_new * a_new

    for gi, (_, dil) in enumerate(DILATED_GROUPS):
        unit = dil * SPAN
        nblk = qb // unit
        first = gi == 0

        def head_block(r, carry, dil=dil, unit=unit, first=first):
            rows = _ds(r, SPAN, dil)
            prev = _ds(qb - unit + r, SPAN, dil)
            ks = jnp.concatenate([kp_ref[prev, :], kc_ref[rows, :]], axis=0).astype(BF16)
            vs = jnp.concatenate([vp_ref[prev, :], vc_ref[rows, :]], axis=0).astype(BF16)
            valid = band & (cur_half | (i > 0))
            attend(q_ref[rows, :], ks, vs, valid, rows, first)
            return carry

        lax.fori_loop(0, dil, head_block, 0, unroll=_unroll_for(dil))

        if nblk > 1:
            def tail_block(idx, carry, dil=dil, unit=unit, first=first):
                n = idx // dil + 1
                r = idx % dil
                start = unit * n + r
                rows = _ds(start, SPAN, dil)
                keys = _ds(start - unit, 2 * SPAN, dil)
                attend(q_ref[rows, :], kc_ref[keys, :].astype(BF16), vc_ref[keys, :].astype(BF16),
                       band, rows, first)
                return carry

            lax.fori_loop(0, (nblk - 1) * dil, tail_block, 0, unroll=_unroll_for((nblk - 1) * dil))

    o_ref[...] = (acc_s[...] / l_s[...]).astype(o_ref.dtype)


def _swa_prompt(q, k, v, bsz, seq):
    qb = SWA_BLOCK
    nb = seq // qb
    cur = pl.BlockSpec((None, qb, LANES), lambda p, b, i: (p, b * nb + i, 0))
    prev = pl.BlockSpec((None, qb, LANES), lambda p, b, i: (p, b * nb + jnp.maximum(i - 1, 0), 0))
    return pl.pallas_call(
        _swa_prompt_body,
        grid=(A_WIDTH // LANES, bsz, nb),
        in_specs=[cur, prev, cur, prev, cur],
        out_specs=pl.BlockSpec((qb, LANES), lambda p, b, i: (b * nb + i, p)),
        out_shape=jax.ShapeDtypeStruct((bsz * seq, A_WIDTH), BF16),
        scratch_shapes=[pltpu.VMEM((qb, LANES), F32)] * 3,
        compiler_params=_cparams(("parallel", "parallel", "arbitrary")),
        name="swa_prompt",
    )(q, k, k, v, v)


def _swa_sample_body(q_ref, kn_ref, vn_ref, ck_ref, cv_ref, c1_ref, c2_ref,
                     o_ref, nk_ref, nv_ref):
    t_new = q_ref.shape[1]
    w_buf = ck_ref.shape[1]
    ck, cv = ck_ref[0], cv_ref[0]
    kn, vn = kn_ref[0], vn_ref[0]
    nk_ref[0, 0:w_buf - t_new, :] = ck[t_new:, :]
    nk_ref[0, w_buf - t_new:, :] = kn
    nv_ref[0, 0:w_buf - t_new, :] = cv[t_new:, :]
    nv_ref[0, w_buf - t_new:, :] = vn

    q = q_ref[0]
    lane = lax.broadcasted_iota(I32, (t_new, LANES), 1)
    lo_mask = lane < A_HEAD_DIM
    c1, c2 = c1_ref[...], c2_ref[...]
    outs = []
    for p in range(A_WIDTH // LANES):
        sl = slice(p * LANES, (p + 1) * LANES)
        qp = q[:, sl]
        qblk = jnp.concatenate([jnp.where(lo_mask, qp, 0.0), jnp.where(lo_mask, 0.0, qp)],
                               axis=0).astype(BF16)
        s1 = jnp.where(c1 > 0, _dot_nt(qblk, ck[:, sl].astype(BF16)), -jnp.inf)
        s2 = jnp.where(c2 > 0, _dot_nt(qblk, kn[:, sl].astype(BF16)), -jnp.inf)
        m = jnp.maximum(jnp.max(s1, axis=1, keepdims=True), jnp.max(s2, axis=1, keepdims=True))
        p1 = c1 * jnp.exp(s1 - m)
        p2 = c2 * jnp.exp(s2 - m)
        l = jnp.sum(p1, axis=1, keepdims=True) + jnp.sum(p2, axis=1, keepdims=True)
        o = (_dot(p1.astype(BF16), cv[:, sl].astype(BF16))
             + _dot(p2.astype(BF16), vn[:, sl].astype(BF16))) / l
        outs.append(jnp.where(lo_mask, o[:t_new], o[t_new:]))
    o_ref[0] = jnp.concatenate(outs, axis=1).astype(o_ref.dtype)


def _sample_multiplicity(t_new, w_buf):
    t = np.arange(t_new)[:, None]
    e = np.arange(w_buf + t_new)[None, :]
    d = w_buf + t - e
    c = np.zeros(d.shape, np.float32)
    for window, dil in DILATED_GROUPS:
        c += ((d >= 0) & (d % dil == 0) & (d <= window)).astype(np.float32)
    c = np.concatenate([c, c], axis=0)
    return jnp.asarray(c[:, :w_buf]), jnp.asarray(c[:, w_buf:])


def _swa_sample(q, kn, vn, cache_k, cache_v):
    db, t_new, w = q.shape
    w_buf = cache_k.shape[1]
    assert w_buf >= max(win for win, _ in DILATED_GROUPS) and t_new % 8 == 0
    c1, c2 = _sample_multiplicity(t_new, w_buf)
    new = pl.BlockSpec((1, t_new, w), lambda b: (b, 0, 0))
    cache = pl.BlockSpec((1, w_buf, w), lambda b: (b, 0, 0))
    return pl.pallas_call(
        _swa_sample_body,
        grid=(db,),
        in_specs=[new, new, new, cache, cache, _full(c1.shape), _full(c2.shape)],
        out_specs=[new, cache, cache],
        out_shape=[jax.ShapeDtypeStruct((db, t_new, w), BF16),
                   jax.ShapeDtypeStruct(cache_k.shape, cache_k.dtype),
                   jax.ShapeDtypeStruct(cache_v.shape, cache_v.dtype)],
        compiler_params=_cparams(("parallel",)),
        name="swa_sample",
    )(q, kn, vn, cache_k, cache_v, c1, c2)


def _gla_body(chunk, sub, nch, q_ref, k_ref, g_ref, v_ref, gate_ref, s0_ref, tril_ref, dmask_ref,
              bones_ref, sbm_ref, bd_ref, gout_ref, o_ref, sfin_ref, sbd):
    j = pl.program_id(1)
    sbm = sbm_ref[...]
    nsub = chunk // sub
    pad = B_KEY_DIM - chunk

    @pl.when(j == 0)
    def _():
        s0 = s0_ref[0]
        sbd[...] = jnp.concatenate([s0] * B_HEADS, axis=1) * sbm

    row = lax.broadcasted_iota(I32, (chunk, 1), 0)
    sub_id = row // sub
    lane_w = lax.broadcasted_iota(I32, (chunk, LANES * max(nsub - 1, 1)), 1)
    lo_w = (lane_w % LANES) < B_KEY_DIM

    def one_chunk(c, carry):
        off = pl.multiple_of(c * chunk, chunk)
        rows = pl.ds(off, chunk)
        q, k, g = q_ref[rows, :], k_ref[rows, :], g_ref[rows, :]
        v = v_ref[rows, :].astype(F32)
        g1 = g.astype(BF16)
        r1 = g - g1.astype(F32)
        g2 = r1.astype(BF16)
        g3 = (r1 - g2.astype(F32)).astype(BF16)
        tril = tril_ref[...]
        b = _dot(tril, g1) + _dot(tril, g2) + _dot(tril, g3)
        b_last = b[chunk - 1:chunk, :]
        state = sbd[...]

        o = _dot((q * jnp.exp(b)).astype(BF16), state.astype(BF16))

        bones = bones_ref[...]
        att = _dot((q * k).astype(BF16), bones) * dmask_ref[0]
        for d in range(1, sub):
            kd = pltpu.roll(k, d, 0)
            bd_ = pltpu.roll(b, d, 0)
            w = q * kd * jnp.exp(jnp.minimum(b - bd_, 0.0))
            att = att + _dot(w.astype(BF16), bones) * dmask_ref[d]

        if nsub > 1:
            qx, kx = [], []
            for i in range(1, nsub):
                r_i = b[sub * i - 1:sub * i, :]
                qx.append(jnp.where(sub_id == i, q * jnp.exp(jnp.minimum(b - r_i, 0.0)), 0.0))
                kx.append(jnp.where(sub_id < i, k * jnp.exp(jnp.minimum(r_i - b, 0.0)), 0.0))
            parts = []
            for p in range(B_QK_WIDTH // LANES):
                sl = slice(p * LANES, (p + 1) * LANES)
                qp = jnp.concatenate([x[:, sl] for x in qx], axis=1)
                kp = jnp.concatenate([x[:, sl] for x in kx], axis=1).astype(BF16)
                zero = jnp.zeros_like(kp)
                lhs = jnp.concatenate([jnp.where(lo_w, qp, 0.0), jnp.where(lo_w, 0.0, qp)],
                                      axis=1).astype(BF16)
                rhs = jnp.concatenate([jnp.concatenate([kp, zero], axis=1),
                                       jnp.concatenate([zero, kp], axis=1)], axis=0)
                parts.append(_dot_nt(lhs, rhs))
            att = att + jnp.concatenate(parts, axis=1)

        if pad:
            vrow = jnp.concatenate([v, jnp.zeros((pad, B_V_WIDTH), F32)], axis=0)
        else:
            vrow = v
        vbd = (jnp.concatenate([vrow] * B_HEADS, axis=0) * sbm).astype(BF16)
        o = o + _dot(att.astype(BF16), vbd)

        ke = (k * jnp.exp(b_last - b)).astype(BF16)
        upd = lax.dot_general(ke, v.astype(BF16), (((0,), (0,)), ((), ())),
                              preferred_element_type=F32)
        dec = jnp.transpose(jnp.broadcast_to(jnp.exp(b_last), (8, B_QK_WIDTH)))[:, 0:1]
        sbd[...] = (state * dec + upd) * sbm

        ms = _dot((o * o).astype(BF16), bd_ref[...])
        on = o * lax.rsqrt(ms + EPS) * gout_ref[...] * gate_ref[rows, :].astype(F32)
        o_ref[rows, :] = on.astype(o_ref.dtype)
        return carry

    lax.fori_loop(0, nch, one_chunk, 0)

    @pl.when(j == pl.num_programs(1) - 1)
    def _():
        s = sbd[...]
        sfin_ref[0] = jnp.concatenate(
            [s[h * B_KEY_DIM:(h + 1) * B_KEY_DIM, h * B_VAL_DIM:(h + 1) * B_VAL_DIM]
             for h in range(B_HEADS)], axis=0)


def _gla_consts(chunk, sub):
    t = np.arange(chunk)
    tril = (t[:, None] >= t[None, :]).astype(np.float32)
    lane = np.arange(B_QK_WIDTH)
    dmask = np.zeros((sub, chunk, B_QK_WIDTH), np.float32)
    for d in range(sub):
        ok = (t % sub) >= d
        dmask[d] = ((lane[None, :] % B_KEY_DIM) == (t[:, None] - d)) & ok[:, None]
    r = np.arange(B_QK_WIDTH)[:, None] // B_KEY_DIM
    c = np.arange(B_V_WIDTH)[None, :] // B_VAL_DIM
    sbm = (r == c).astype(np.float32)
    return jnp.asarray(tril, BF16), jnp.asarray(dmask), jnp.asarray(sbm)


def _gla(q, k, g, v, gate, s0, wp, length, chunk, sub, step):
    n = q.shape[0]
    bsz = n // length
    assert chunk == sub or chunk == B_KEY_DIM
    tril, dmask, sbm = _gla_consts(chunk, sub)
    nstep = length // step
    row = lambda w: pl.BlockSpec((step, w), lambda b, j: (b * nstep + j, 0))
    st = pl.BlockSpec((1, B_QK_WIDTH, B_VAL_DIM), lambda b, j: (b, 0, 0))
    return pl.pallas_call(
        functools.partial(_gla_body, chunk, sub, step // chunk),
        grid=(bsz, nstep),
        in_specs=[row(256), row(256), row(256), row(512), row(512), st, _full(tril.shape),
                  _full(dmask.shape), _full((256, 256)), _full(sbm.shape), _full((512, 512)),
                  _full((1, 512))],
        out_specs=[row(512), st],
        out_shape=[jax.ShapeDtypeStruct((n, B_V_WIDTH), BF16),
                   jax.ShapeDtypeStruct(s0.shape, F32)],
        scratch_shapes=[pltpu.VMEM((B_QK_WIDTH, B_V_WIDTH), F32)],
        compiler_params=_cparams(("parallel", "arbitrary")),
        name="gla",
    )(q, k, g, v, gate, s0, tril, dmask, wp["bones64"], sbm, wp["bd128"], wp["gout"])


def _head_rms(z, g, scale=1.0):
    parts = []
    for h in range(MEM_HEADS):
        zh = z[:, h * LANES:(h + 1) * LANES]
        parts.append(zh * lax.rsqrt(jnp.mean(zh * zh, axis=-1, keepdims=True) + EPS))
    return jnp.concatenate(parts, axis=1) * (g * scale)


def _mem_kv_body(m_ref, gn_ref, wk_ref, wv_ref, gk_ref, mk_ref, mv_ref):
    mn = _rms(m_ref[...], gn_ref[...]).astype(BF16)
    mk_ref[...] = _head_rms(_dot(mn, wk_ref[...]), gk_ref[...])
    mv_ref[...] = _dot(mn, wv_ref[...])


def _mem_kv(mem, wp):
    n = mem.shape[0]
    tm = 256
    row = lambda w: pl.BlockSpec((tm, w), lambda i: (i, 0))
    return pl.pallas_call(
        _mem_kv_body,
        grid=(n // tm,),
        in_specs=[row(D_MODEL), _full((1, D_MODEL)), _full((D_MODEL, 512)), _full((D_MODEL, 512)),
                  _full((1, 512))],
        out_specs=[row(512), row(512)],
        out_shape=[jax.ShapeDtypeStruct((n, 512), F32)] * 2,
        compiler_params=_cparams(("parallel",)),
        name="mem_kv",
    )(mem, wp["gmem"], wp["wmk"], wp["wmv"], wp["gmk"])


def _post_body(nseq, x_ref, oa_ref, ob_ref, woa_ref, wob_ref, g2_ref, wq_ref, gmq_ref, mk_ref,
               mv_ref, wo_ref, g3_ref, rwh_ref, rwl_ref, rb_ref, cnt0_ref, tri_ref,
               h2_ref, xn_ref, meta_ref, cnt_ref, carry):
    tm = x_ref.shape[0]

    @pl.when(pl.program_id(0) == 0)
    def _():
        carry[...] = cnt0_ref[...]

    h = x_ref[...] + _dot(oa_ref[...], woa_ref[...]) + _dot(ob_ref[...], wob_ref[...])
    hn = _rms(h, g2_ref[...]).astype(BF16)
    qm = _head_rms(_dot(hn, wq_ref[...]), gmq_ref[...], MEM_HEAD_DIM ** -0.5).astype(BF16)
    nk = nseq * N_MEM
    mk = mk_ref[...].reshape(nk, MEM_WIDTH)
    mv = mv_ref[...].reshape(nk, MEM_WIDTH)
    if nseq > 1:
        rt = lax.broadcasted_iota(I32, (tm, nk), 0) // (tm // nseq)
        ct = lax.broadcasted_iota(I32, (tm, nk), 1) // N_MEM
        same = rt == ct
    outs = []
    for hd in range(MEM_HEADS):
        sl = slice(hd * LANES, (hd + 1) * LANES)
        s = _dot_nt(qm[:, sl], mk[:, sl].astype(BF16))
        if nseq > 1:
            s = jnp.where(same, s, -jnp.inf)
        m = jnp.max(s, axis=1, keepdims=True)
        pr = jnp.exp(s - m)
        l = jnp.sum(pr, axis=1, keepdims=True)
        outs.append(_dot(pr.astype(BF16), mv[:, sl].astype(BF16)) / l)
    h2 = h + _dot(jnp.concatenate(outs, axis=1).astype(BF16), wo_ref[...])
    h2_ref[...] = h2
    xn = _rms(h2, g3_ref[...])
    xn_ref[...] = xn

    x1 = xn.astype(BF16)
    x2 = (xn - x1.astype(F32)).astype(BF16)
    logits = (_dot(x1, rwh_ref[...]) + _dot(x1, rwl_ref[...]) + _dot(x2, rwh_ref[...])
              + rb_ref[...])
    lane = lax.broadcasted_iota(I32, (tm, LANES), 1)
    vals, idxs, hots = [], [], []
    work = logits
    for _ in range(TOP_K):
        m = jnp.max(work, axis=1, keepdims=True)
        idx = jnp.min(jnp.where(work == m, lane, LANES), axis=1, keepdims=True)
        hot = lane == idx
        vals.append(m)
        idxs.append(idx)
        hots.append(hot)
        work = jnp.where(hot, -jnp.inf, work)
    exps = [jnp.exp(v - vals[0]) for v in vals]
    den = exps[0] + exps[1] + exps[2] + exps[3]

    sel = (hots[0] | hots[1] | hots[2] | hots[3]).astype(F32)
    before = _dot(tri_ref[...], sel.astype(BF16)) + carry[...]
    carry[...] = carry[...] + jnp.sum(sel, axis=0, keepdims=True)
    cnt_ref[...] = carry[...]

    meta = jnp.zeros((tm, LANES), F32)
    for kk in range(TOP_K):
        rank = jnp.sum(jnp.where(hots[kk], before, 0.0), axis=1, keepdims=True)
        meta = jnp.where(lane == kk, idxs[kk].astype(F32), meta)
        meta = jnp.where(lane == TOP_K + kk, rank, meta)
        meta = jnp.where(lane == 2 * TOP_K + kk, exps[kk] / den, meta)
    meta_ref[...] = meta


def _post(x, oa, ob, mk, mv, cnt0, wp, tm, nseq, tiles_per_mem):
    n = x.shape[0]
    row = lambda w: pl.BlockSpec((tm, w), lambda i: (i, 0))
    mem = pl.BlockSpec((nseq, N_MEM, MEM_WIDTH), lambda i: (i // tiles_per_mem, 0, 0))
    tri = jnp.asarray(np.tril(np.ones((tm, tm), np.float32), -1), BF16)
    return pl.pallas_call(
        functools.partial(_post_body, nseq),
        grid=(n // tm,),
        in_specs=[row(D_MODEL), row(512), row(512), _full((512, D_MODEL)), _full((512, D_MODEL)),
                  _full((1, D_MODEL)), _full((D_MODEL, 512)), _full((1, 512)), mem, mem,
                  _full((512, D_MODEL)), _full((1, D_MODEL)), _full((D_MODEL, LANES)),
                  _full((D_MODEL, LANES)), _full((1, LANES)), _full((1, LANES)), _full((tm, tm))],
        out_specs=[row(D_MODEL), row(D_MODEL), row(LANES), _full((1, LANES))],
        out_shape=[jax.ShapeDtypeStruct((n, D_MODEL), F32), jax.ShapeDtypeStruct((n, D_MODEL), F32),
                   jax.ShapeDtypeStruct((n, LANES), F32), jax.ShapeDtypeStruct((1, LANES), F32)],
        scratch_shapes=[pltpu.VMEM((1, LANES), F32)],
        compiler_params=_cparams(("arbitrary",)),
        name="post",
    )(x, oa, ob, wp["woa"], wp["wob"], wp["g2"], wp["wmq"], wp["gmq"], mk, mv, wp["wmo"],
      wp["g3"], wp["rwh"], wp["rwl"], wp["rb"], cnt0, tri)


def _group_copies(src, src_row, dst, dst_row, units, sem, start):
    off = 0
    for k in reversed(range(GROUP_BITS)):
        size = GROUP_ROWS << k
        bit = (units >> k) & 1
        s = pl.multiple_of(src_row + off, GROUP_ROWS)
        d = pl.multiple_of(dst_row + off, GROUP_ROWS)
        cp = pltpu.make_async_copy(src.at[pl.ds(s, size)], dst.at[pl.ds(d, size)], sem)

        @pl.when(bit == 1)
        def _(cp=cp):
            if start:
                cp.start()
            else:
                cp.wait()

        off = off + bit * size


def _dispatch_body(blk0, loff_ref, gstart_ref, units_ref, ldest_ref, x_ref, xs_in_ref, xs_ref,
                   xloc, sem):
    del xs_in_ref
    b = pl.program_id(0) + blk0
    tb = x_ref.shape[0]
    rows = lax.broadcasted_iota(I32, (LOCAL_ROWS, tb), 0)
    ld = ldest_ref[...]
    hot = rows == ld[0:1, :]
    for kk in range(1, TOP_K):
        hot = hot | (rows == ld[kk:kk + 1, :])
    xloc[...] = _dot(jnp.where(hot, 1.0, 0.0).astype(BF16), x_ref[...].astype(BF16))

    def each(start):
        def body(e, carry):
            j = b * N_EXPERTS + e
            _group_copies(xloc, loff_ref[j], xs_ref, gstart_ref[j], units_ref[j], sem, start)
            return carry
        lax.fori_loop(0, N_EXPERTS, body, 0)

    each(True)
    each(False)


def _dispatch(tables, blk0, ldest_t, xn, xs):
    n = xn.shape[0]
    tb = DISPATCH_BLOCK
    idx = lambda i, *_: (i, 0)
    return pl.pallas_call(
        functools.partial(_dispatch_body, blk0),
        grid_spec=pltpu.PrefetchScalarGridSpec(
            num_scalar_prefetch=3, grid=(n // tb,),
            in_specs=[pl.BlockSpec((TOP_K, tb), lambda i, *_: (0, i)),
                      pl.BlockSpec((tb, D_MODEL), idx),
                      pl.BlockSpec(memory_space=pl.ANY)],
            out_specs=pl.BlockSpec(memory_space=pl.ANY),
            scratch_shapes=[pltpu.VMEM((LOCAL_ROWS, D_MODEL), F32), pltpu.SemaphoreType.DMA]),
        out_shape=jax.ShapeDtypeStruct(xs.shape, xs.dtype),
        input_output_aliases={5: 0},
        compiler_params=_cparams(("arbitrary",)),
        name="moe_dispatch",
    )(*tables, ldest_t, xn, xs)


def _expert_body(te_ref, nt_ref, x_ref, w1g_ref, w1l_ref, b1g_ref, b1l_ref, w2_ref, b2_ref, y_ref):
    @pl.when(pl.program_id(0) < nt_ref[0])
    def _():
        x = x_ref[...].astype(BF16)
        glu = jnp.minimum(_dot(x, w1g_ref[0]) + b1g_ref[0], SWIGLU_LIMIT)
        lin = jnp.clip(_dot(x, w1l_ref[0]) + b1l_ref[0], -SWIGLU_LIMIT, SWIGLU_LIMIT)
        act = glu * jax.nn.sigmoid(SWIGLU_ALPHA * glu) * (lin + 1.0)
        y_ref[...] = _dot(act.astype(BF16), w2_ref[0]) + b2_ref[0]

    @pl.when(pl.program_id(0) >= nt_ref[0])
    def _():
        y_ref[...] = jnp.zeros_like(y_ref)


def _experts(tile_expert, n_tiles, xs, wp):
    rows = xs.shape[0]
    tile = lambda i, te, nt: (jnp.minimum(i, nt[0] - 1), 0)
    out_tile = lambda i, te, nt: (i, 0)
    wsel = lambda i, te, nt: (te[jnp.minimum(i, nt[0] - 1)], 0, 0)
    wspec = lambda r, c: pl.BlockSpec((1, r, c), wsel)
    return pl.pallas_call(
        _expert_body,
        grid_spec=pltpu.PrefetchScalarGridSpec(
            num_scalar_prefetch=2, grid=(rows // EXPERT_TILE,),
            in_specs=[pl.BlockSpec((EXPERT_TILE, D_MODEL), tile),
                      wspec(D_MODEL, D_FF), wspec(D_MODEL, D_FF), wspec(1, D_FF), wspec(1, D_FF),
                      wspec(D_FF, D_MODEL), wspec(1, D_MODEL)],
            out_specs=pl.BlockSpec((EXPERT_TILE, D_MODEL), out_tile)),
        out_shape=jax.ShapeDtypeStruct((rows, D_MODEL), F32),
        compiler_params=_cparams(("arbitrary",)),
        name="moe_experts",
    )(tile_expert, n_tiles, xs, wp["w1g"], wp["w1l"], wp["b1g"], wp["b1l"], wp["w2"], wp["b2"])


def _combine_body(blk0, loff_ref, gstart_ref, units_ref, ldest_ref, meta_ref, h_ref, ys_ref, o_ref,
                  yloc, sem):
    b = pl.program_id(0) + blk0
    tb = h_ref.shape[0]

    @pl.when(pl.program_id(0) == 0)
    def _():
        yloc[...] = jnp.zeros_like(yloc)

    def each(start):
        def body(e, carry):
            j = b * N_EXPERTS + e
            _group_copies(ys_ref, gstart_ref[j], yloc, loff_ref[j], units_ref[j], sem, start)
            return carry
        lax.fori_loop(0, N_EXPERTS, body, 0)

    each(True)
    cols = lax.broadcasted_iota(I32, (tb, LOCAL_ROWS), 1)
    ld = ldest_ref[...]
    meta = meta_ref[...]
    gmat = jnp.zeros((tb, LOCAL_ROWS), F32)
    for kk in range(TOP_K):
        gate = meta[:, 2 * TOP_K + kk:2 * TOP_K + kk + 1]
        gmat = jnp.where(cols == ld[:, kk:kk + 1], gate, gmat)
    g_hi = gmat.astype(BF16)
    g_lo = (gmat - g_hi.astype(F32)).astype(BF16)
    each(False)
    y = yloc[...].astype(BF16)
    o_ref[...] = h_ref[...] + _dot(g_hi, y) + _dot(g_lo, y)


def _combine(tables, blk0, ldest, meta, h2, ys):
    n = h2.shape[0]
    tb = DISPATCH_BLOCK
    row = lambda w: pl.BlockSpec((tb, w), lambda i, *_: (i, 0))
    return pl.pallas_call(
        functools.partial(_combine_body, blk0),
        grid_spec=pltpu.PrefetchScalarGridSpec(
            num_scalar_prefetch=3, grid=(n // tb,),
            in_specs=[row(TOP_K), row(LANES), row(D_MODEL), pl.BlockSpec(memory_space=pl.ANY)],
            out_specs=row(D_MODEL),
            scratch_shapes=[pltpu.VMEM((LOCAL_ROWS, D_MODEL), F32), pltpu.SemaphoreType.DMA]),
        out_shape=jax.ShapeDtypeStruct((n, D_MODEL), F32),
        compiler_params=_cparams(("arbitrary",)),
        name="moe_combine",
    )(*tables, ldest, meta, h2, ys)


def _moe(groups, wp):
    tb = DISPATCH_BLOCK
    sizes = [g[0].shape[0] for g in groups]
    assert all(s % tb == 0 for s in sizes)
    n_tok = sum(sizes)
    nb = n_tok // tb
    rows = n_tok * TOP_K + nb * N_EXPERTS * GROUP_ROWS + N_EXPERTS * EXPERT_TILE
    n_tiles_max = rows // EXPERT_TILE

    eidx = jnp.concatenate([g[2][:, 0:TOP_K] for g in groups]).astype(I32)
    rank = jnp.concatenate([g[2][:, TOP_K:2 * TOP_K] for g in groups]).astype(I32)
    hot = eidx[:, :, None] == jnp.arange(N_EXPERTS, dtype=I32)
    cnt = jnp.sum(hot.reshape(nb, tb * TOP_K, N_EXPERTS), axis=1, dtype=I32)
    npad = (cnt + GROUP_ROWS - 1) // GROUP_ROWS * GROUP_ROWS
    loff = jnp.cumsum(npad, axis=1) - npad
    gsize = jnp.sum(npad, axis=0)
    gpad = (gsize + EXPERT_TILE - 1) // EXPERT_TILE * EXPERT_TILE
    ends = jnp.cumsum(gpad)
    gstart = (ends - gpad)[None, :] + jnp.cumsum(npad, axis=0) - npad
    before = jnp.cumsum(cnt, axis=0) - cnt
    n_tiles = (ends[-1] // EXPERT_TILE).reshape(1)
    tile_ids = jnp.arange(n_tiles_max, dtype=I32)
    tile_expert = jnp.minimum(
        jnp.sum((ends // EXPERT_TILE)[None, :] <= tile_ids[:, None], axis=1), N_EXPERTS - 1).astype(I32)

    base = jnp.repeat(loff - before, tb, axis=0)
    ldest = rank + jnp.sum(jnp.where(hot, base[:, None, :], 0), axis=2)
    ldest_t = ldest.T
    tables = (loff.reshape(-1), gstart.reshape(-1).astype(I32), (npad // GROUP_ROWS).reshape(-1))

    xs = jnp.zeros((rows, D_MODEL), F32)
    starts = np.cumsum([0] + sizes[:-1])
    for (_, xn, _), t0, sz in zip(groups, starts, sizes):
        xs = _dispatch(tables, int(t0) // tb, ldest_t[:, t0:t0 + sz], xn, xs)
    ys = _experts(tile_expert, n_tiles, xs, wp)
    return [_combine(tables, int(t0) // tb, ldest[t0:t0 + sz], meta, h2, ys)
            for (h2, _, meta), t0, sz in zip(groups, starts, sizes)]


def _w1_split_body(w_ref, sel_ref, g_ref, l_ref):
    sel = sel_ref[...]
    for j in range(D_FF // LANES):
        z = _dot(w_ref[0, :, 2 * LANES * j:2 * LANES * (j + 1)].astype(BF16), sel)
        g_ref[0, :, LANES * j:LANES * (j + 1)] = z[:, :LANES].astype(BF16)
        l_ref[0, :, LANES * j:LANES * (j + 1)] = z[:, LANES:].astype(BF16)


def _w1_split(w1):
    c = np.arange(2 * LANES)
    sel = np.zeros((2 * LANES, 2 * LANES), np.float32)
    sel[c, (c % 2) * LANES + c // 2] = 1.0
    spec = lambda w: pl.BlockSpec((1, D_MODEL, w), lambda e: (e, 0, 0))
    return pl.pallas_call(
        _w1_split_body,
        grid=(N_EXPERTS,),
        in_specs=[spec(2 * D_FF), _full((2 * LANES, 2 * LANES))],
        out_specs=[spec(D_FF), spec(D_FF)],
        out_shape=[jax.ShapeDtypeStruct((N_EXPERTS, D_MODEL, D_FF), BF16)] * 2,
        compiler_params=_cparams(("parallel",)),
        name="w1_split",
    )(w1, jnp.asarray(sel, BF16))


def _prep_weights(norm1_g, w_in, a_q_norm_g, a_k_norm_g, gla_w_alpha, gla_b_alpha, gla_out_norm_g,
                  w_out, norm2_g, mem_norm_g, mem_w_q, mem_w_k, mem_w_v, mem_q_norm_g,
                  mem_k_norm_g, mem_w_o, norm3_g, router_w, router_b, exp_w1, exp_b1, exp_w2,
                  exp_b2):
    main = 3 * A_WIDTH + 2 * B_QK_WIDTH + 2 * B_V_WIDTH
    w_lr = jnp.pad(w_in[:, main:], ((0, 0), (0, LANES - GATE_RANK)))
    rw = jnp.pad(router_w, ((0, 0), (0, LANES - N_EXPERTS)))
    rwh = rw.astype(BF16)
    w1g, w1l = _w1_split(exp_w1)
    return {
        "g1": norm1_g[None],
        "w_in": jnp.concatenate([w_in[:, :main], w_lr], axis=1).astype(BF16),
        "gq": jnp.tile(a_q_norm_g, A_HEADS)[None],
        "gk": jnp.tile(a_k_norm_g, A_HEADS)[None],
        "bd64": _block_diag(A_WIDTH, A_HEAD_DIM, 1.0 / A_HEAD_DIM, BF16),
        "wa": jnp.pad(gla_w_alpha, ((0, LANES - GATE_RANK), (0, 0))).astype(BF16),
        "ba": gla_b_alpha[None],
        "bones64": _block_diag(B_QK_WIDTH, B_KEY_DIM, 1.0, BF16),
        "bd128": _block_diag(B_V_WIDTH, B_VAL_DIM, 1.0 / B_VAL_DIM, BF16),
        "gout": jnp.tile(gla_out_norm_g, B_HEADS)[None],
        "woa": w_out[:A_WIDTH].astype(BF16),
        "wob": w_out[A_WIDTH:].astype(BF16),
        "g2": norm2_g[None],
        "gmem": mem_norm_g[None],
        "wmq": mem_w_q.astype(BF16),
        "wmk": mem_w_k.astype(BF16),
        "wmv": mem_w_v.astype(BF16),
        "gmq": jnp.tile(mem_q_norm_g, MEM_HEADS)[None],
        "gmk": jnp.tile(mem_k_norm_g, MEM_HEADS)[None],
        "wmo": mem_w_o.astype(BF16),
        "g3": norm3_g[None],
        "rwh": rwh,
        "rwl": (rw - rwh.astype(F32)).astype(BF16),
        "rb": jnp.pad(router_b, (0, LANES - N_EXPERTS), constant_values=-1e30)[None],
        "w1g": w1g,
        "w1l": w1l,
        "b1g": exp_b1[:, None, 0::2],
        "b1l": exp_b1[:, None, 1::2],
        "w2": exp_w2.astype(BF16),
        "b2": exp_b2[:, None, :],
    }


def _layer(xp, xs, mem_prompt, cache_k, cache_v, state_gla, cache_mk, cache_mv, wp):
    bsz, seq, _ = xp.shape
    db, t_new, _ = xs.shape
    w_buf = cache_k.shape[1]
    w_p = min(max(w for w, _ in DILATED_GROUPS), seq)

    xpf = xp.reshape(bsz * seq, D_MODEL)
    q, k, v, k_last, v_last, bq, bk, bv, gate, la = _in_proj(xpf, wp, BF16, seq, w_p)
    oa = _swa_prompt(q, k, v, bsz, seq)
    s0 = jnp.zeros((bsz, B_QK_WIDTH, B_VAL_DIM), F32)
    ob, s_p = _gla(bq, bk, la, bv, gate, s0, wp, seq, GLA_CHUNK, GLA_SUB, GLA_STEP)
    mk, mv = _mem_kv(mem_prompt.reshape(bsz * N_MEM, D_MODEL), wp)
    cnt0 = jnp.zeros((1, LANES), F32)
    h2_p, xn_p, meta_p, cnt = _post(xpf, oa, ob, mk.reshape(bsz, N_MEM, MEM_WIDTH),
                                    mv.reshape(bsz, N_MEM, MEM_WIDTH), cnt0, wp,
                                    POST_TILE, 1, seq // POST_TILE)

    xsf = xs.reshape(db * t_new, D_MODEL)
    q, _, _, ks, vs, bq, bk, bv, gate, la = _in_proj(xsf, wp, F32, t_new, t_new)
    new3 = lambda a: a.reshape(db, t_new, A_WIDTH)
    oa_s, nk, nv = _swa_sample(new3(q.transpose(1, 0, 2)), new3(ks), new3(vs), cache_k, cache_v)
    ob_s, s_s = _gla(bq, bk, la, bv, gate, state_gla, wp, t_new, t_new, t_new, t_new)
    h2_s, xn_s, meta_s, cnt = _post(xsf, oa_s.reshape(db * t_new, A_WIDTH), ob_s, cache_mk,
                                    cache_mv, cnt, wp, SAMPLE_SEQS * t_new, SAMPLE_SEQS, 1)

    y_p, y_s = _moe([(h2_p, xn_p, meta_p), (h2_s, xn_s, meta_s)], wp)
    return (y_p.reshape(bsz, seq, D_MODEL), y_s.reshape(db, t_new, D_MODEL),
            k_last.reshape(bsz, w_p, A_WIDTH), v_last.reshape(bsz, w_p, A_WIDTH),
            s_p, mk, mv, nk, nv, s_s)


def kernel(x_prompt, x_sample, mem_prompt, cache_swa_k, cache_swa_v, state_gla, cache_mem_k, cache_mem_v, norm1_g, w_in, a_q_norm_g, a_k_norm_g, gla_w_alpha, gla_b_alpha, gla_out_norm_g, w_out, norm2_g, mem_norm_g, mem_w_q, mem_w_k, mem_w_v, mem_q_norm_g, mem_k_norm_g, mem_w_o, norm3_g, router_w, router_b, exp_w1, exp_b1, exp_w2, exp_b2):
    depth = w_in.shape[0]
    bsz = x_prompt.shape[0]
    db, w_buf = cache_swa_k.shape[1], cache_swa_k.shape[2]
    xp, xs = x_prompt, x_sample
    per_layer = []
    for l in range(depth):
        wp = _prep_weights(
            norm1_g[l], w_in[l], a_q_norm_g[l], a_k_norm_g[l], gla_w_alpha[l], gla_b_alpha[l],
            gla_out_norm_g[l], w_out[l], norm2_g[l], mem_norm_g[l], mem_w_q[l], mem_w_k[l],
            mem_w_v[l], mem_q_norm_g[l], mem_k_norm_g[l], mem_w_o[l], norm3_g[l], router_w[l],
            router_b[l], exp_w1[l], exp_b1[l], exp_w2[l], exp_b2[l])
        xp, xs, kp, vp, s_p, mk, mv, nk, nv, s_s = _layer(
            xp, xs, mem_prompt,
            cache_swa_k[l].reshape(db, w_buf, A_WIDTH), cache_swa_v[l].reshape(db, w_buf, A_WIDTH),
            state_gla[l].reshape(db, B_QK_WIDTH, B_VAL_DIM),
            cache_mem_k[l].reshape(db, N_MEM, MEM_WIDTH), cache_mem_v[l].reshape(db, N_MEM, MEM_WIDTH),
            wp)
        w_p = kp.shape[1]
        per_layer.append((
            kp.reshape(bsz, w_p, A_HEADS, A_HEAD_DIM), vp.reshape(bsz, w_p, A_HEADS, A_HEAD_DIM),
            s_p.reshape(bsz, B_HEADS, B_KEY_DIM, B_VAL_DIM),
            mk.reshape(bsz, N_MEM, MEM_HEADS, MEM_HEAD_DIM), mv.reshape(bsz, N_MEM, MEM_HEADS, MEM_HEAD_DIM),
            nk.reshape(db, w_buf, A_HEADS, A_HEAD_DIM), nv.reshape(db, w_buf, A_HEADS, A_HEAD_DIM),
            s_s.reshape(db, B_HEADS, B_KEY_DIM, B_VAL_DIM)))
    stacked = [jnp.stack(t) for t in zip(*per_layer)]
    return (xp, xs, *stacked)
```

```python
import functools

import jax
import jax.numpy as jnp
import numpy as np
from jax import lax
from jax.experimental import pallas as pl
from jax.experimental.pallas import tpu as pltpu

F32 = jnp.float32
BF16 = jnp.bfloat16
I32 = jnp.int32

EPS = 1e-6
D_MODEL = 1024
A_HEADS, A_HEAD_DIM, A_WIDTH = 8, 64, 512
A_SCALE = A_HEAD_DIM ** -0.5
LOG2E = 1.4426950408889634
DILATED_GROUPS = ((128, 1), (512, 4), (2048, 16))
SPAN = 128
B_HEADS, B_KEY_DIM, B_VAL_DIM = 4, 64, 128
B_QK_WIDTH, B_V_WIDTH = 256, 512
GATE_RANK = 16
GATE_TEMP = 16.0
N_MEM, MEM_HEADS, MEM_HEAD_DIM, MEM_WIDTH = 256, 4, 128, 512
N_EXPERTS, TOP_K, D_FF = 32, 4, 1024
SWIGLU_ALPHA, SWIGLU_LIMIT = 1.702, 7.0

LANES = 128
VMEM_LIMIT = 56 * 1024 * 1024

PROJ_TILE = 512
SWA_BLOCK = 2048
SWA_UNROLL = 4
GLA_CHUNK, GLA_SUB, GLA_STEP = 64, 16, 512
POST_TILE = 512
SAMPLE_SEQS = 8
EXPERT_TILE = 512
DISPATCH_BLOCK = 512
GROUP_ROWS = 8
GROUP_BITS = (DISPATCH_BLOCK // GROUP_ROWS).bit_length()
LOCAL_ROWS = DISPATCH_BLOCK * TOP_K + N_EXPERTS * GROUP_ROWS


def _cparams(sem, vmem=VMEM_LIMIT, **kw):
    return pltpu.CompilerParams(dimension_semantics=sem, vmem_limit_bytes=vmem, **kw)


def _full(shape):
    n = len(shape)
    return pl.BlockSpec(shape, lambda *_: (0,) * n)


def _rms(x, g):
    ms = jnp.mean(x * x, axis=-1, keepdims=True)
    return x * lax.rsqrt(ms + EPS) * g


def _dot(a, b):
    return jnp.dot(a, b, preferred_element_type=F32)


def _dot_nt(a, b):
    return lax.dot_general(a, b, (((1,), (1,)), ((), ())), preferred_element_type=F32)


def _block_diag(n, blk, val, dtype):
    i = np.arange(n)
    return jnp.asarray(np.where((i[:, None] // blk) == (i[None, :] // blk), val, 0.0), dtype)


def _in_proj_body(x_ref, g1_ref, w_ref, gq_ref, gk_ref, bd_ref, wa_ref, ba_ref,
                  q_ref, k_ref, v_ref, kc_ref, vc_ref, bq_ref, bk_ref, bv_ref, gate_ref, la_ref):
    xn = _rms(x_ref[...], g1_ref[...]).astype(BF16)
    bd = bd_ref[...]

    def proj(lo, hi):
        return _dot(xn, w_ref[:, lo:hi])

    def headnorm(z, g):
        ms = _dot((z * z).astype(BF16), bd)
        return z * lax.rsqrt(ms + EPS) * g

    def put_groups(ref, z):
        for p in range(A_WIDTH // LANES):
            ref[p] = z[:, p * LANES:(p + 1) * LANES]

    put_groups(q_ref, headnorm(proj(0, 512), gq_ref[...]) * (A_SCALE * LOG2E))
    k = headnorm(proj(512, 1024), gk_ref[...])
    v = proj(1024, 1536)
    put_groups(k_ref, k)
    put_groups(v_ref, v)
    kc_ref[...] = k
    vc_ref[...] = v
    bq_ref[...] = proj(1536, 1792) * (B_KEY_DIM ** -0.5)
    bk_ref[...] = proj(1792, 2048)
    bv_ref[...] = proj(2048, 2560).astype(bv_ref.dtype)
    br = proj(2560, 3072)
    gate_ref[...] = (br * jax.nn.sigmoid(br)).astype(gate_ref.dtype)
    lr = proj(3072, 3200).astype(BF16)
    pre = _dot(lr, wa_ref[...]) + ba_ref[...]
    log_sig = jnp.minimum(pre, 0.0) - jnp.log1p(jnp.exp(-jnp.abs(pre)))
    la_ref[...] = log_sig * (1.0 / GATE_TEMP)


def _in_proj(x, wp, wide_dtype, seq, keep):
    n = x.shape[0]
    tm = min(PROJ_TILE, n)
    row = lambda w: pl.BlockSpec((tm, w), lambda i: (i, 0))
    ngrp = A_WIDTH // LANES
    grp = pl.BlockSpec((ngrp, tm, LANES), lambda i: (0, i, 0))
    if keep == seq:
        kept = row(A_WIDTH)
    else:
        tps, kt = seq // tm, keep // tm
        assert tps * tm == seq and kt * tm == keep
        kept = pl.BlockSpec((tm, A_WIDTH),
                            lambda i: ((i // tps) * kt + jnp.maximum(i % tps - (tps - kt), 0), 0))
    outs = [(256, F32), (256, F32), (512, wide_dtype), (512, wide_dtype), (256, F32)]
    return pl.pallas_call(
        _in_proj_body,
        grid=(n // tm,),
        in_specs=[row(D_MODEL), _full((1, D_MODEL)), _full((D_MODEL, 3200)), _full((1, 512)),
                  _full((1, 512)), _full((512, 512)), _full((LANES, 256)), _full((1, 256))],
        out_specs=[grp] * 3 + [kept] * 2 + [row(w) for w, _ in outs],
        out_shape=[jax.ShapeDtypeStruct((ngrp, n, LANES), F32)] * 3
        + [jax.ShapeDtypeStruct((n // seq * keep, A_WIDTH), F32)] * 2
        + [jax.ShapeDtypeStruct((n, w), dt) for w, dt in outs],
        compiler_params=_cparams(("arbitrary",)),
        name="in_proj",
    )(x, wp["g1"], wp["w_in"], wp["gq"], wp["gk"], wp["bd64"], wp["wa"], wp["ba"])


def _unroll_for(trips):
    return max(u for u in range(1, SWA_UNROLL + 1) if trips % u == 0)


def _ds(start, size, stride):
    return pl.ds(start, size) if stride == 1 else pl.ds(start, size, stride=stride)


def _swa_prompt_body(q_ref, kp_ref, kc_ref, vp_ref, vc_ref, o_ref, m_s, l_s, acc_s):
    i = pl.program_id(2)
    qb = SWA_BLOCK
    lane = lax.broadcasted_iota(I32, (SPAN, LANES), 1)
    lo_mask = lane < A_HEAD_DIM
    jq = lax.broadcasted_iota(I32, (SPAN, 2 * SPAN), 0)
    jk = lax.broadcasted_iota(I32, (SPAN, 2 * SPAN), 1)
    dist = jq + SPAN - jk
    band = (dist >= 0) & (dist <= SPAN)
    cur_half = jk >= SPAN

    def attend(qp, kp, vp, valid, rows, first_group):
        vp1 = jnp.concatenate([vp, jnp.ones_like(vp)], axis=1)
        res = []
        for hh in range(2):
            msk = lo_mask if hh == 0 else jnp.logical_not(lo_mask)
            qm = jnp.where(msk, qp, 0.0).astype(BF16)
            s = jnp.where(valid, _dot_nt(qm, kp), -jnp.inf)
            m = jnp.max(s, axis=1, keepdims=True)
            res.append((m, _dot(jnp.exp2(s - m).astype(BF16), vp1)))
        m_new = jnp.where(lo_mask, res[0][0], res[1][0])
        l_new = jnp.where(lo_mask, res[0][1][:, LANES:], res[1][1][:, LANES:])
        o_new = jnp.where(lo_mask, res[0][1][:, :LANES], res[1][1][:, :LANES])
        if first_group:
            m_s[rows, :] = m_new
            l_s[rows, :] = l_new
            acc_s[rows, :] = o_new
        else:
            m_old = m_s[rows, :]
            m = jnp.maximum(m_old, m_new)
            a_old = jnp.exp2(m_old - m)
            a_new = jnp.exp2(m_new - m)
            m_s[rows, :] = m
            l_s[rows, :] = l_s[rows, :] * a_old + l_new * a_new
            acc_s[rows, :] = acc_s[rows, :] * a_old + o_new * a_new

    for gi, (_, dil) in enumerate(DILATED_GROUPS):
        unit = dil * SPAN
        nblk = qb // unit
        first = gi == 0

        def head_block(r, carry, dil=dil, unit=unit, first=first):
            rows = _ds(r, SPAN, dil)
            prev = _ds(qb - unit + r, SPAN, dil)
            ks = jnp.concatenate([kp_ref[prev, :], kc_ref[rows, :]], axis=0).astype(BF16)
            vs = jnp.concatenate([vp_ref[prev, :], vc_ref[rows, :]], axis=0).astype(BF16)
            valid = band & (cur_half | (i > 0))
            attend(q_ref[rows, :], ks, vs, valid, rows, first)
            return carry

        lax.fori_loop(0, dil, head_block, 0, unroll=_unroll_for(dil))

        if nblk > 1:
            def tail_block(idx, carry, dil=dil, unit=unit, first=first):
                n = idx // dil + 1
                r = idx % dil
                start = unit * n + r
                rows = _ds(start, SPAN, dil)
                keys = _ds(start - unit, 2 * SPAN, dil)
                attend(q_ref[rows, :], kc_ref[keys, :].astype(BF16), vc_ref[keys, :].astype(BF16),
                       band, rows, first)
                return carry

            lax.fori_loop(0, (nblk - 1) * dil, tail_block, 0, unroll=_unroll_for((nblk - 1) * dil))

    o_ref[...] = (acc_s[...] / l_s[...]).astype(o_ref.dtype)


def _swa_prompt(q, k, v, bsz, seq):
    qb = SWA_BLOCK
    nb = seq // qb
    cur = pl.BlockSpec((None, qb, LANES), lambda p, b, i: (p, b * nb + i, 0))
    prev = pl.BlockSpec((None, qb, LANES), lambda p, b, i: (p, b * nb + jnp.maximum(i - 1, 0), 0))
    return pl.pallas_call(
        _swa_prompt_body,
        grid=(A_WIDTH // LANES, bsz, nb),
        in_specs=[cur, prev, cur, prev, cur],
        out_specs=pl.BlockSpec((qb, LANES), lambda p, b, i: (b * nb + i, p)),
        out_shape=jax.ShapeDtypeStruct((bsz * seq, A_WIDTH), BF16),
        scratch_shapes=[pltpu.VMEM((qb, LANES), F32)] * 3,
        compiler_params=_cparams(("parallel", "parallel", "arbitrary")),
        name="swa_prompt",
    )(q, k, k, v, v)


def _swa_sample_body(q_ref, kn_ref, vn_ref, ck_ref, cv_ref, c1_ref, c2_ref,
                     o_ref, nk_ref, nv_ref):
    t_new = q_ref.shape[1]
    w_buf = ck_ref.shape[1]
    ck, cv = ck_ref[0], cv_ref[0]
    kn, vn = kn_ref[0], vn_ref[0]
    nk_ref[0, 0:w_buf - t_new, :] = ck[t_new:, :]
    nk_ref[0, w_buf - t_new:, :] = kn
    nv_ref[0, 0:w_buf - t_new, :] = cv[t_new:, :]
    nv_ref[0, w_buf - t_new:, :] = vn

    q = q_ref[0]
    lane = lax.broadcasted_iota(I32, (t_new, LANES), 1)
    lo_mask = lane < A_HEAD_DIM
    c1, c2 = c1_ref[...], c2_ref[...]
    outs = []
    for p in range(A_WIDTH // LANES):
        sl = slice(p * LANES, (p + 1) * LANES)
        qp = q[:, sl]
        qblk = jnp.concatenate([jnp.where(lo_mask, qp, 0.0), jnp.where(lo_mask, 0.0, qp)],
                               axis=0).astype(BF16)
        s1 = jnp.where(c1 > 0, _dot_nt(qblk, ck[:, sl].astype(BF16)), -jnp.inf)
        s2 = jnp.where(c2 > 0, _dot_nt(qblk, kn[:, sl].astype(BF16)), -jnp.inf)
        m = jnp.maximum(jnp.max(s1, axis=1, keepdims=True), jnp.max(s2, axis=1, keepdims=True))
        p1 = c1 * jnp.exp2(s1 - m)
        p2 = c2 * jnp.exp2(s2 - m)
        l = jnp.sum(p1, axis=1, keepdims=True) + jnp.sum(p2, axis=1, keepdims=True)
        o = (_dot(p1.astype(BF16), cv[:, sl].astype(BF16))
             + _dot(p2.astype(BF16), vn[:, sl].astype(BF16))) / l
        outs.append(jnp.where(lo_mask, o[:t_new], o[t_new:]))
    o_ref[0] = jnp.concatenate(outs, axis=1).astype(o_ref.dtype)


def _sample_multiplicity(t_new, w_buf):
    t = np.arange(t_new)[:, None]
    e = np.arange(w_buf + t_new)[None, :]
    d = w_buf + t - e
    c = np.zeros(d.shape, np.float32)
    for window, dil in DILATED_GROUPS:
        c += ((d >= 0) & (d % dil == 0) & (d <= window)).astype(np.float32)
    c = np.concatenate([c, c], axis=0)
    return jnp.asarray(c[:, :w_buf]), jnp.asarray(c[:, w_buf:])


def _swa_sample(q, kn, vn, cache_k, cache_v):
    db, t_new, w = q.shape
    w_buf = cache_k.shape[1]
    assert w_buf >= max(win for win, _ in DILATED_GROUPS) and t_new % 8 == 0
    c1, c2 = _sample_multiplicity(t_new, w_buf)
    new = pl.BlockSpec((1, t_new, w), lambda b: (b, 0, 0))
    cache = pl.BlockSpec((1, w_buf, w), lambda b: (b, 0, 0))
    return pl.pallas_call(
        _swa_sample_body,
        grid=(db,),
        in_specs=[new, new, new, cache, cache, _full(c1.shape), _full(c2.shape)],
        out_specs=[new, cache, cache],
        out_shape=[jax.ShapeDtypeStruct((db, t_new, w), BF16),
                   jax.ShapeDtypeStruct(cache_k.shape, cache_k.dtype),
                   jax.ShapeDtypeStruct(cache_v.shape, cache_v.dtype)],
        compiler_params=_cparams(("parallel",)),
        name="swa_sample",
    )(q, kn, vn, cache_k, cache_v, c1, c2)


def _gla_body(chunk, sub, nch, q_ref, k_ref, g_ref, v_ref, gate_ref, s0_ref, tril_ref, dmask_ref,
              bones_ref, sbm_ref, bd_ref, gout_ref, o_ref, sfin_ref, sbd):
    j = pl.program_id(1)
    sbm = sbm_ref[...]
    nsub = chunk // sub
    pad = B_KEY_DIM - chunk

    @pl.when(j == 0)
    def _():
        s0 = s0_ref[0]
        sbd[...] = jnp.concatenate([s0] * B_HEADS, axis=1) * sbm

    row = lax.broadcasted_iota(I32, (chunk, 1), 0)
    sub_id = row // sub
    lane_w = lax.broadcasted_iota(I32, (chunk, LANES * max(nsub - 1, 1)), 1)
    lo_w = (lane_w % LANES) < B_KEY_DIM

    def one_chunk(c, carry):
        off = pl.multiple_of(c * chunk, chunk)
        rows = pl.ds(off, chunk)
        q, k, g = q_ref[rows, :], k_ref[rows, :], g_ref[rows, :]
        v = v_ref[rows, :].astype(F32)
        g1 = g.astype(BF16)
        r1 = g - g1.astype(F32)
        g2 = r1.astype(BF16)
        g3 = (r1 - g2.astype(F32)).astype(BF16)
        tril = tril_ref[...]
        b = _dot(tril, g1) + _dot(tril, g2) + _dot(tril, g3)
        b_last = b[chunk - 1:chunk, :]
        state = sbd[...]

        o = _dot((q * jnp.exp(b)).astype(BF16), state.astype(BF16))

        bones = bones_ref[...]
        att = _dot((q * k).astype(BF16), bones) * dmask_ref[0]
        for d in range(1, sub):
            kd = pltpu.roll(k, d, 0)
            bd_ = pltpu.roll(b, d, 0)
            w = q * kd * jnp.exp(jnp.minimum(b - bd_, 0.0))
            att = att + _dot(w.astype(BF16), bones) * dmask_ref[d]

        if nsub > 1:
            qx, kx = [], []
            for i in range(1, nsub):
                r_i = b[sub * i - 1:sub * i, :]
                qx.append(jnp.where(sub_id == i, q * jnp.exp(jnp.minimum(b - r_i, 0.0)), 0.0))
                kx.append(jnp.where(sub_id < i, k * jnp.exp(jnp.minimum(r_i - b, 0.0)), 0.0))
            parts = []
            for p in range(B_QK_WIDTH // LANES):
                sl = slice(p * LANES, (p + 1) * LANES)
                qp = jnp.concatenate([x[:, sl] for x in qx], axis=1)
                kp = jnp.concatenate([x[:, sl] for x in kx], axis=1).astype(BF16)
                zero = jnp.zeros_like(kp)
                lhs = jnp.concatenate([jnp.where(lo_w, qp, 0.0), jnp.where(lo_w, 0.0, qp)],
                                      axis=1).astype(BF16)
                rhs = jnp.concatenate([jnp.concatenate([kp, zero], axis=1),
                                       jnp.concatenate([zero, kp], axis=1)], axis=0)
                parts.append(_dot_nt(lhs, rhs))
            att = att + jnp.concatenate(parts, axis=1)

        if pad:
            vrow = jnp.concatenate([v, jnp.zeros((pad, B_V_WIDTH), F32)], axis=0)
        else:
            vrow = v
        vbd = (jnp.concatenate([vrow] * B_HEADS, axis=0) * sbm).astype(BF16)
        o = o + _dot(att.astype(BF16), vbd)

        ke = (k * jnp.exp(b_last - b)).astype(BF16)
        upd = lax.dot_general(ke, v.astype(BF16), (((0,), (0,)), ((), ())),
                              preferred_element_type=F32)
        dec = jnp.transpose(jnp.broadcast_to(jnp.exp(b_last), (8, B_QK_WIDTH)))[:, 0:1]
        sbd[...] = (state * dec + upd) * sbm

        ms = _dot((o * o).astype(BF16), bd_ref[...])
        on = o * lax.rsqrt(ms + EPS) * gout_ref[...] * gate_ref[rows, :].astype(F32)
        o_ref[rows, :] = on.astype(o_ref.dtype)
        return carry

    lax.fori_loop(0, nch, one_chunk, 0)

    @pl.when(j == pl.num_programs(1) - 1)
    def _():
        s = sbd[...]
        sfin_ref[0] = jnp.concatenate(
            [s[h * B_KEY_DIM:(h + 1) * B_KEY_DIM, h * B_VAL_DIM:(h + 1) * B_VAL_DIM]
             for h in range(B_HEADS)], axis=0)


def _gla_consts(chunk, sub):
    t = np.arange(chunk)
    tril = (t[:, None] >= t[None, :]).astype(np.float32)
    lane = np.arange(B_QK_WIDTH)
    dmask = np.zeros((sub, chunk, B_QK_WIDTH), np.float32)
    for d in range(sub):
        ok = (t % sub) >= d
        dmask[d] = ((lane[None, :] % B_KEY_DIM) == (t[:, None] - d)) & ok[:, None]
    r = np.arange(B_QK_WIDTH)[:, None] // B_KEY_DIM
    c = np.arange(B_V_WIDTH)[None, :] // B_VAL_DIM
    sbm = (r == c).astype(np.float32)
    return jnp.asarray(tril, BF16), jnp.asarray(dmask), jnp.asarray(sbm)


def _gla(q, k, g, v, gate, s0, wp, length, chunk, sub, step):
    n = q.shape[0]
    bsz = n // length
    assert chunk == sub or chunk == B_KEY_DIM
    tril, dmask, sbm = _gla_consts(chunk, sub)
    nstep = length // step
    row = lambda w: pl.BlockSpec((step, w), lambda b, j: (b * nstep + j, 0))
    st = pl.BlockSpec((1, B_QK_WIDTH, B_VAL_DIM), lambda b, j: (b, 0, 0))
    return pl.pallas_call(
        functools.partial(_gla_body, chunk, sub, step // chunk),
        grid=(bsz, nstep),
        in_specs=[row(256), row(256), row(256), row(512), row(512), st, _full(tril.shape),
                  _full(dmask.shape), _full((256, 256)), _full(sbm.shape), _full((512, 512)),
                  _full((1, 512))],
        out_specs=[row(512), st],
        out_shape=[jax.ShapeDtypeStruct((n, B_V_WIDTH), BF16),
                   jax.ShapeDtypeStruct(s0.shape, F32)],
        scratch_shapes=[pltpu.VMEM((B_QK_WIDTH, B_V_WIDTH), F32)],
        compiler_params=_cparams(("parallel", "arbitrary")),
        name="gla",
    )(q, k, g, v, gate, s0, tril, dmask, wp["bones64"], sbm, wp["bd128"], wp["gout"])


def _head_rms(z, g, scale=1.0):
    parts = []
    for h in range(MEM_HEADS):
        zh = z[:, h * LANES:(h + 1) * LANES]
        parts.append(zh * lax.rsqrt(jnp.mean(zh * zh, axis=-1, keepdims=True) + EPS))
    return jnp.concatenate(parts, axis=1) * (g * scale)


def _mem_kv_body(m_ref, gn_ref, wk_ref, wv_ref, gk_ref, mk_ref, mv_ref):
    mn = _rms(m_ref[...], gn_ref[...]).astype(BF16)
    mk_ref[...] = _head_rms(_dot(mn, wk_ref[...]), gk_ref[...])
    mv_ref[...] = _dot(mn, wv_ref[...])


def _mem_kv(mem, wp):
    n = mem.shape[0]
    tm = 256
    row = lambda w: pl.BlockSpec((tm, w), lambda i: (i, 0))
    return pl.pallas_call(
        _mem_kv_body,
        grid=(n // tm,),
        in_specs=[row(D_MODEL), _full((1, D_MODEL)), _full((D_MODEL, 512)), _full((D_MODEL, 512)),
                  _full((1, 512))],
        out_specs=[row(512), row(512)],
        out_shape=[jax.ShapeDtypeStruct((n, 512), F32)] * 2,
        compiler_params=_cparams(("parallel",)),
        name="mem_kv",
    )(mem, wp["gmem"], wp["wmk"], wp["wmv"], wp["gmk"])


def _post_body(nseq, x_ref, oa_ref, ob_ref, woa_ref, wob_ref, g2_ref, wq_ref, gmq_ref, mk_ref,
               mv_ref, wo_ref, g3_ref, rw_ref, rb_ref, cnt0_ref, tri_ref,
               h2_ref, xn_ref, meta_ref, cnt_ref, carry):
    tm = x_ref.shape[0]

    @pl.when(pl.program_id(0) == 0)
    def _():
        carry[...] = cnt0_ref[...]

    h = x_ref[...] + _dot(oa_ref[...], woa_ref[...]) + _dot(ob_ref[...], wob_ref[...])
    hn = _rms(h, g2_ref[...]).astype(BF16)
    qm = _head_rms(_dot(hn, wq_ref[...]), gmq_ref[...], MEM_HEAD_DIM ** -0.5).astype(BF16)
    nk = nseq * N_MEM
    mk = mk_ref[...].reshape(nk, MEM_WIDTH)
    mv = mv_ref[...].reshape(nk, MEM_WIDTH)
    if nseq > 1:
        rt = lax.broadcasted_iota(I32, (tm, nk), 0) // (tm // nseq)
        ct = lax.broadcasted_iota(I32, (tm, nk), 1) // N_MEM
        same = rt == ct
    outs = []
    for hd in range(MEM_HEADS):
        sl = slice(hd * LANES, (hd + 1) * LANES)
        s = _dot_nt(qm[:, sl], mk[:, sl].astype(BF16))
        if nseq > 1:
            s = jnp.where(same, s, -jnp.inf)
        m = jnp.max(s, axis=1, keepdims=True)
        pr = jnp.exp(s - m)
        l = jnp.sum(pr, axis=1, keepdims=True)
        outs.append(_dot(pr.astype(BF16), mv[:, sl].astype(BF16)) / l)
    h2 = h + _dot(jnp.concatenate(outs, axis=1).astype(BF16), wo_ref[...])
    h2_ref[...] = h2
    xn = _rms(h2, g3_ref[...])
    xn_ref[...] = xn

    x1 = xn.astype(BF16)
    x2 = (xn - x1.astype(F32)).astype(BF16)
    prod = _dot(jnp.concatenate([x1, x2], axis=0), rw_ref[...])
    logits = (prod[:tm, :LANES] + prod[:tm, LANES:] + prod[tm:, :LANES] + prod[tm:, LANES:]
              + rb_ref[...])
    lane = lax.broadcasted_iota(I32, (tm, LANES), 1)
    vals, idxs, hots = [], [], []
    work = logits
    for _ in range(TOP_K):
        m = jnp.max(work, axis=1, keepdims=True)
        idx = jnp.min(jnp.where(work == m, lane, LANES), axis=1, keepdims=True)
        hot = lane == idx
        vals.append(m)
        idxs.append(idx)
        hots.append(hot)
        work = jnp.where(hot, -jnp.inf, work)
    exps = [jnp.exp(v - vals[0]) for v in vals]
    den = exps[0] + exps[1] + exps[2] + exps[3]

    sel = (hots[0] | hots[1] | hots[2] | hots[3]).astype(F32)
    before = _dot(tri_ref[...], sel.astype(BF16)) + carry[...]
    carry[...] = carry[...] + jnp.sum(sel, axis=0, keepdims=True)
    cnt_ref[...] = carry[...]

    meta = jnp.zeros((tm, LANES), F32)
    for kk in range(TOP_K):
        rank = jnp.sum(jnp.where(hots[kk], before, 0.0), axis=1, keepdims=True)
        meta = jnp.where(lane == kk, idxs[kk].astype(F32), meta)
        meta = jnp.where(lane == TOP_K + kk, rank, meta)
        meta = jnp.where(lane == 2 * TOP_K + kk, exps[kk] / den, meta)
    meta_ref[...] = meta


def _post(x, oa, ob, mk, mv, cnt0, wp, tm, nseq, tiles_per_mem):
    n = x.shape[0]
    row = lambda w: pl.BlockSpec((tm, w), lambda i: (i, 0))
    mem = pl.BlockSpec((nseq, N_MEM, MEM_WIDTH), lambda i: (i // tiles_per_mem, 0, 0))
    tri = jnp.asarray(np.tril(np.ones((tm, tm), np.float32), -1), BF16)
    return pl.pallas_call(
        functools.partial(_post_body, nseq),
        grid=(n // tm,),
        in_specs=[row(D_MODEL), row(512), row(512), _full((512, D_MODEL)), _full((512, D_MODEL)),
                  _full((1, D_MODEL)), _full((D_MODEL, 512)), _full((1, 512)), mem, mem,
                  _full((512, D_MODEL)), _full((1, D_MODEL)), _full((D_MODEL, 2 * LANES)),
                  _full((1, LANES)), _full((1, LANES)), _full((tm, tm))],
        out_specs=[row(D_MODEL), row(D_MODEL), row(LANES), _full((1, LANES))],
        out_shape=[jax.ShapeDtypeStruct((n, D_MODEL), F32), jax.ShapeDtypeStruct((n, D_MODEL), F32),
                   jax.ShapeDtypeStruct((n, LANES), F32), jax.ShapeDtypeStruct((1, LANES), F32)],
        scratch_shapes=[pltpu.VMEM((1, LANES), F32)],
        compiler_params=_cparams(("arbitrary",)),
        name="post",
    )(x, oa, ob, wp["woa"], wp["wob"], wp["g2"], wp["wmq"], wp["gmq"], mk, mv, wp["wmo"],
      wp["g3"], wp["rw"], wp["rb"], cnt0, tri)


def _group_copies(src, src_row, dst, dst_row, units, sem, start):
    off = jnp.int32(0)
    for k in reversed(range(GROUP_BITS)):
        size = GROUP_ROWS << k
        bit = (units >> k) & 1
        s = pl.multiple_of(src_row + off, GROUP_ROWS)
        d = pl.multiple_of(dst_row + off, GROUP_ROWS)
        cp = pltpu.make_async_copy(src.at[pl.ds(s, size)], dst.at[pl.ds(d, size)], sem)

        @pl.when(bit == 1)
        def _(cp=cp):
            if start:
                cp.start()
            else:
                cp.wait()

        off = off + bit * size


def _pad_rows_body(pstart_ref, punits_ref, nt_ref, xs_ref, zeros, sem):
    zeros[...] = jnp.zeros_like(zeros)
    n_tiles_max = xs_ref.shape[0] // EXPERT_TILE

    def each(start):
        def expert_tail(e, carry):
            _group_copies(zeros, 0, xs_ref, pstart_ref[e], punits_ref[e], sem, start)
            return carry

        def unused_tile(t, carry):
            row = pl.multiple_of(t * EXPERT_TILE, EXPERT_TILE)
            cp = pltpu.make_async_copy(zeros, xs_ref.at[pl.ds(row, EXPERT_TILE)], sem)
            if start:
                cp.start()
            else:
                cp.wait()
            return carry

        lax.fori_loop(0, N_EXPERTS, expert_tail, 0)
        lax.fori_loop(nt_ref[0], n_tiles_max, unused_tile, 0)

    each(True)
    each(False)


def _pad_rows(pad_start, pad_units, n_tiles, rows):
    return pl.pallas_call(
        _pad_rows_body,
        grid_spec=pltpu.PrefetchScalarGridSpec(
            num_scalar_prefetch=3, grid=(1,),
            in_specs=[],
            out_specs=pl.BlockSpec(memory_space=pl.ANY),
            scratch_shapes=[pltpu.VMEM((EXPERT_TILE, D_MODEL), F32), pltpu.SemaphoreType.DMA]),
        out_shape=jax.ShapeDtypeStruct((rows, D_MODEL), F32),
        compiler_params=_cparams(("arbitrary",)),
        name="moe_pad_rows",
    )(pad_start, pad_units, n_tiles)


def _dispatch_body(blk0, loff_ref, gstart_ref, units_ref, ldest_ref, x_ref, xs_in_ref, xs_ref,
                   xloc, sem):
    del xs_in_ref
    b = pl.program_id(0) + blk0
    tb = x_ref.shape[0]
    rows = lax.broadcasted_iota(I32, (LOCAL_ROWS, tb), 0)
    ld = ldest_ref[...]
    hot = rows == ld[0:1, :]
    for kk in range(1, TOP_K):
        hot = hot | (rows == ld[kk:kk + 1, :])
    xloc[...] = _dot(jnp.where(hot, 1.0, 0.0).astype(BF16), x_ref[...].astype(BF16))

    def each(start):
        def body(e, carry):
            j = b * N_EXPERTS + e
            _group_copies(xloc, loff_ref[j], xs_ref, gstart_ref[j], units_ref[j], sem, start)
            return carry
        lax.fori_loop(0, N_EXPERTS, body, 0)

    each(True)
    each(False)


def _dispatch(tables, blk0, ldest_t, xn, xs):
    n = xn.shape[0]
    tb = DISPATCH_BLOCK
    idx = lambda i, *_: (i, 0)
    return pl.pallas_call(
        functools.partial(_dispatch_body, blk0),
        grid_spec=pltpu.PrefetchScalarGridSpec(
            num_scalar_prefetch=3, grid=(n // tb,),
            in_specs=[pl.BlockSpec((TOP_K, tb), lambda i, *_: (0, i)),
                      pl.BlockSpec((tb, D_MODEL), idx),
                      pl.BlockSpec(memory_space=pl.ANY)],
            out_specs=pl.BlockSpec(memory_space=pl.ANY),
            scratch_shapes=[pltpu.VMEM((LOCAL_ROWS, D_MODEL), F32), pltpu.SemaphoreType.DMA]),
        out_shape=jax.ShapeDtypeStruct(xs.shape, xs.dtype),
        input_output_aliases={5: 0},
        compiler_params=_cparams(("arbitrary",)),
        name="moe_dispatch",
    )(*tables, ldest_t, xn, xs)


def _expert_body(te_ref, nt_ref, x_ref, w1g_ref, w1l_ref, b1g_ref, b1l_ref, w2_ref, b2_ref, y_ref):
    @pl.when(pl.program_id(0) < nt_ref[0])
    def _():
        x = x_ref[...].astype(BF16)
        glu = jnp.minimum(_dot(x, w1g_ref[0]) + b1g_ref[0], SWIGLU_LIMIT)
        lin = jnp.clip(_dot(x, w1l_ref[0]) + b1l_ref[0], -SWIGLU_LIMIT, SWIGLU_LIMIT)
        act = glu * jax.nn.sigmoid(SWIGLU_ALPHA * glu) * (lin + 1.0)
        y_ref[...] = _dot(act.astype(BF16), w2_ref[0]) + b2_ref[0]

    @pl.when(pl.program_id(0) >= nt_ref[0])
    def _():
        y_ref[...] = jnp.zeros_like(y_ref)


def _experts(tile_expert, n_tiles, xs, wp):
    rows = xs.shape[0]
    tile = lambda i, te, nt: (jnp.minimum(i, nt[0] - 1), 0)
    out_tile = lambda i, te, nt: (i, 0)
    wsel = lambda i, te, nt: (te[jnp.minimum(i, nt[0] - 1)], 0, 0)
    wspec = lambda r, c: pl.BlockSpec((1, r, c), wsel)
    return pl.pallas_call(
        _expert_body,
        grid_spec=pltpu.PrefetchScalarGridSpec(
            num_scalar_prefetch=2, grid=(rows // EXPERT_TILE,),
            in_specs=[pl.BlockSpec((EXPERT_TILE, D_MODEL), tile),
                      wspec(D_MODEL, D_FF), wspec(D_MODEL, D_FF), wspec(1, D_FF), wspec(1, D_FF),
                      wspec(D_FF, D_MODEL), wspec(1, D_MODEL)],
            out_specs=pl.BlockSpec((EXPERT_TILE, D_MODEL), out_tile)),
        out_shape=jax.ShapeDtypeStruct((rows, D_MODEL), F32),
        compiler_params=_cparams(("arbitrary",)),
        name="moe_experts",
    )(tile_expert, n_tiles, xs, wp["w1g"], wp["w1l"], wp["b1g"], wp["b1l"], wp["w2"], wp["b2"])


def _combine_body(blk0, loff_ref, gstart_ref, units_ref, ldest_ref, meta_ref, h_ref, ys_ref, o_ref,
                  yloc, sem):
    b = pl.program_id(0) + blk0
    tb = h_ref.shape[0]

    @pl.when(pl.program_id(0) == 0)
    def _():
        yloc[...] = jnp.zeros_like(yloc)

    def each(start):
        def body(e, carry):
            j = b * N_EXPERTS + e
            _group_copies(ys_ref, gstart_ref[j], yloc, loff_ref[j], units_ref[j], sem, start)
            return carry
        lax.fori_loop(0, N_EXPERTS, body, 0)

    each(True)
    cols = lax.broadcasted_iota(I32, (tb, LOCAL_ROWS), 1)
    ld = ldest_ref[...]
    meta = meta_ref[...]
    gmat = jnp.zeros((tb, LOCAL_ROWS), F32)
    for kk in range(TOP_K):
        gate = meta[:, 2 * TOP_K + kk:2 * TOP_K + kk + 1]
        gmat = jnp.where(cols == ld[:, kk:kk + 1], gate, gmat)
    g_hi = gmat.astype(BF16)
    g_lo = (gmat - g_hi.astype(F32)).astype(BF16)
    each(False)
    y = yloc[...].astype(BF16)
    o_ref[...] = h_ref[...] + _dot(g_hi, y) + _dot(g_lo, y)


def _combine(tables, blk0, ldest, meta, h2, ys):
    n = h2.shape[0]
    tb = DISPATCH_BLOCK
    row = lambda w: pl.BlockSpec((tb, w), lambda i, *_: (i, 0))
    return pl.pallas_call(
        functools.partial(_combine_body, blk0),
        grid_spec=pltpu.PrefetchScalarGridSpec(
            num_scalar_prefetch=3, grid=(n // tb,),
            in_specs=[row(TOP_K), row(LANES), row(D_MODEL), pl.BlockSpec(memory_space=pl.ANY)],
            out_specs=row(D_MODEL),
            scratch_shapes=[pltpu.VMEM((LOCAL_ROWS, D_MODEL), F32), pltpu.SemaphoreType.DMA]),
        out_shape=jax.ShapeDtypeStruct((n, D_MODEL), F32),
        compiler_params=_cparams(("arbitrary",)),
        name="moe_combine",
    )(*tables, ldest, meta, h2, ys)


def _moe(groups, wp):
    tb = DISPATCH_BLOCK
    sizes = [g[0].shape[0] for g in groups]
    assert all(s % tb == 0 for s in sizes)
    n_tok = sum(sizes)
    nb = n_tok // tb
    rows = n_tok * TOP_K + nb * N_EXPERTS * GROUP_ROWS + N_EXPERTS * EXPERT_TILE
    n_tiles_max = rows // EXPERT_TILE

    eidx = jnp.concatenate([g[2][:, 0:TOP_K] for g in groups]).astype(I32)
    rank = jnp.concatenate([g[2][:, TOP_K:2 * TOP_K] for g in groups]).astype(I32)
    hot = eidx[:, :, None] == jnp.arange(N_EXPERTS, dtype=I32)
    cnt = jnp.sum(hot.reshape(nb, tb * TOP_K, N_EXPERTS), axis=1, dtype=I32)
    npad = (cnt + GROUP_ROWS - 1) // GROUP_ROWS * GROUP_ROWS
    loff = jnp.cumsum(npad, axis=1) - npad
    gsize = jnp.sum(npad, axis=0)
    gpad = (gsize + EXPERT_TILE - 1) // EXPERT_TILE * EXPERT_TILE
    ends = jnp.cumsum(gpad)
    gstart = (ends - gpad)[None, :] + jnp.cumsum(npad, axis=0) - npad
    before = jnp.cumsum(cnt, axis=0) - cnt
    n_tiles = (ends[-1] // EXPERT_TILE).reshape(1)
    tile_ids = jnp.arange(n_tiles_max, dtype=I32)
    tile_expert = jnp.minimum(
        jnp.sum((ends // EXPERT_TILE)[None, :] <= tile_ids[:, None], axis=1), N_EXPERTS - 1).astype(I32)

    base = jnp.repeat(loff - before, tb, axis=0)
    ldest = rank + jnp.sum(jnp.where(hot, base[:, None, :], 0), axis=2)
    ldest_t = ldest.T
    tables = (loff.reshape(-1), gstart.reshape(-1).astype(I32), (npad // GROUP_ROWS).reshape(-1))

    xs = _pad_rows(((ends - gpad) + gsize).astype(I32), ((gpad - gsize) // GROUP_ROWS).astype(I32),
                   n_tiles, rows)
    starts = np.cumsum([0] + sizes[:-1])
    for (_, xn, _), t0, sz in zip(groups, starts, sizes):
        xs = _dispatch(tables, int(t0) // tb, ldest_t[:, t0:t0 + sz], xn, xs)
    ys = _experts(tile_expert, n_tiles, xs, wp)
    return [_combine(tables, int(t0) // tb, ldest[t0:t0 + sz], meta, h2, ys)
            for (h2, _, meta), t0, sz in zip(groups, starts, sizes)]


def _w1_split_body(w_ref, sel_ref, g_ref, l_ref):
    sel = sel_ref[...]
    for j in range(D_FF // LANES):
        z = _dot(w_ref[0, :, 2 * LANES * j:2 * LANES * (j + 1)].astype(BF16), sel)
        g_ref[0, :, LANES * j:LANES * (j + 1)] = z[:, :LANES].astype(BF16)
        l_ref[0, :, LANES * j:LANES * (j + 1)] = z[:, LANES:].astype(BF16)


def _w1_split(w1):
    c = np.arange(2 * LANES)
    sel = np.zeros((2 * LANES, 2 * LANES), np.float32)
    sel[c, (c % 2) * LANES + c // 2] = 1.0
    spec = lambda w: pl.BlockSpec((1, D_MODEL, w), lambda e: (e, 0, 0))
    return pl.pallas_call(
        _w1_split_body,
        grid=(N_EXPERTS,),
        in_specs=[spec(2 * D_FF), _full((2 * LANES, 2 * LANES))],
        out_specs=[spec(D_FF), spec(D_FF)],
        out_shape=[jax.ShapeDtypeStruct((N_EXPERTS, D_MODEL, D_FF), BF16)] * 2,
        compiler_params=_cparams(("parallel",)),
        name="w1_split",
    )(w1, jnp.asarray(sel, BF16))


def _prep_weights(norm1_g, w_in, a_q_norm_g, a_k_norm_g, gla_w_alpha, gla_b_alpha, gla_out_norm_g,
                  w_out, norm2_g, mem_norm_g, mem_w_q, mem_w_k, mem_w_v, mem_q_norm_g,
                  mem_k_norm_g, mem_w_o, norm3_g, router_w, router_b, exp_w1, exp_b1, exp_w2,
                  exp_b2):
    main = 3 * A_WIDTH + 2 * B_QK_WIDTH + 2 * B_V_WIDTH
    w_lr = jnp.pad(w_in[:, main:], ((0, 0), (0, LANES - GATE_RANK)))
    rw = jnp.pad(router_w, ((0, 0), (0, LANES - N_EXPERTS)))
    rwh = rw.astype(BF16)
    w1g, w1l = _w1_split(exp_w1)
    return {
        "g1": norm1_g[None],
        "w_in": jnp.concatenate([w_in[:, :main], w_lr], axis=1).astype(BF16),
        "gq": jnp.tile(a_q_norm_g, A_HEADS)[None],
        "gk": jnp.tile(a_k_norm_g, A_HEADS)[None],
        "bd64": _block_diag(A_WIDTH, A_HEAD_DIM, 1.0 / A_HEAD_DIM, BF16),
        "wa": jnp.pad(gla_w_alpha, ((0, LANES - GATE_RANK), (0, 0))).astype(BF16),
        "ba": gla_b_alpha[None],
        "bones64": _block_diag(B_QK_WIDTH, B_KEY_DIM, 1.0, BF16),
        "bd128": _block_diag(B_V_WIDTH, B_VAL_DIM, 1.0 / B_VAL_DIM, BF16),
        "gout": jnp.tile(gla_out_norm_g, B_HEADS)[None],
        "woa": w_out[:A_WIDTH].astype(BF16),
        "wob": w_out[A_WIDTH:].astype(BF16),
        "g2": norm2_g[None],
        "gmem": mem_norm_g[None],
        "wmq": mem_w_q.astype(BF16),
        "wmk": mem_w_k.astype(BF16),
        "wmv": mem_w_v.astype(BF16),
        "gmq": jnp.tile(mem_q_norm_g, MEM_HEADS)[None],
        "gmk": jnp.tile(mem_k_norm_g, MEM_HEADS)[None],
        "wmo": mem_w_o.astype(BF16),
        "g3": norm3_g[None],
        "rw": jnp.concatenate([rwh, (rw - rwh.astype(F32)).astype(BF16)], axis=1),
        "rb": jnp.pad(router_b, (0, LANES - N_EXPERTS), constant_values=-1e30)[None],
        "w1g": w1g,
        "w1l": w1l,
        "b1g": exp_b1[:, None, 0::2],
        "b1l": exp_b1[:, None, 1::2],
        "w2": exp_w2.astype(BF16),
        "b2": exp_b2[:, None, :],
    }


def _layer(xp, xs, mem_prompt, cache_k, cache_v, state_gla, cache_mk, cache_mv, wp):
    bsz, seq, _ = xp.shape
    db, t_new, _ = xs.shape
    w_buf = cache_k.shape[1]
    w_p = min(max(w for w, _ in DILATED_GROUPS), seq)

    xpf = xp.reshape(bsz * seq, D_MODEL)
    q, k, v, k_last, v_last, bq, bk, bv, gate, la = _in_proj(xpf, wp, BF16, seq, w_p)
    oa = _swa_prompt(q, k, v, bsz, seq)
    s0 = jnp.zeros((bsz, B_QK_WIDTH, B_VAL_DIM), F32)
    ob, s_p = _gla(bq, bk, la, bv, gate, s0, wp, seq, GLA_CHUNK, GLA_SUB, GLA_STEP)
    mk, mv = _mem_kv(mem_prompt.reshape(bsz * N_MEM, D_MODEL), wp)
    cnt0 = jnp.zeros((1, LANES), F32)
    h2_p, xn_p, meta_p, cnt = _post(xpf, oa, ob, mk.reshape(bsz, N_MEM, MEM_WIDTH),
                                    mv.reshape(bsz, N_MEM, MEM_WIDTH), cnt0, wp,
                                    POST_TILE, 1, seq // POST_TILE)

    xsf = xs.reshape(db * t_new, D_MODEL)
    q, _, _, ks, vs, bq, bk, bv, gate, la = _in_proj(xsf, wp, F32, t_new, t_new)
    new3 = lambda a: a.reshape(db, t_new, A_WIDTH)
    oa_s, nk, nv = _swa_sample(new3(q.transpose(1, 0, 2)), new3(ks), new3(vs), cache_k, cache_v)
    ob_s, s_s = _gla(bq, bk, la, bv, gate, state_gla, wp, t_new, t_new, t_new, t_new)
    h2_s, xn_s, meta_s, cnt = _post(xsf, oa_s.reshape(db * t_new, A_WIDTH), ob_s, cache_mk,
                                    cache_mv, cnt, wp, SAMPLE_SEQS * t_new, SAMPLE_SEQS, 1)

    y_p, y_s = _moe([(h2_p, xn_p, meta_p), (h2_s, xn_s, meta_s)], wp)
    return (y_p.reshape(bsz, seq, D_MODEL), y_s.reshape(db, t_new, D_MODEL),
            k_last.reshape(bsz, w_p, A_WIDTH), v_last.reshape(bsz, w_p, A_WIDTH),
            s_p, mk, mv, nk, nv, s_s)


def kernel(x_prompt, x_sample, mem_prompt, cache_swa_k, cache_swa_v, state_gla, cache_mem_k, cache_mem_v, norm1_g, w_in, a_q_norm_g, a_k_norm_g, gla_w_alpha, gla_b_alpha, gla_out_norm_g, w_out, norm2_g, mem_norm_g, mem_w_q, mem_w_k, mem_w_v, mem_q_norm_g, mem_k_norm_g, mem_w_o, norm3_g, router_w, router_b, exp_w1, exp_b1, exp_w2, exp_b2):
    depth = w_in.shape[0]
    bsz = x_prompt.shape[0]
    db, w_buf = cache_swa_k.shape[1], cache_swa_k.shape[2]
    xp, xs = x_prompt, x_sample
    per_layer = []
    for l in range(depth):
        wp = _prep_weights(
            norm1_g[l], w_in[l], a_q_norm_g[l], a_k_norm_g[l], gla_w_alpha[l], gla_b_alpha[l],
            gla_out_norm_g[l], w_out[l], norm2_g[l], mem_norm_g[l], mem_w_q[l], mem_w_k[l],
            mem_w_v[l], mem_q_norm_g[l], mem_k_norm_g[l], mem_w_o[l], norm3_g[l], router_w[l],
            router_b[l], exp_w1[l], exp_b1[l], exp_w2[l], exp_b2[l])
        xp, xs, kp, vp, s_p, mk, mv, nk, nv, s_s = _layer(
            xp, xs, mem_prompt,
            cache_swa_k[l].reshape(db, w_buf, A_WIDTH), cache_swa_v[l].reshape(db, w_buf, A_WIDTH),
            state_gla[l].reshape(db, B_QK_WIDTH, B_VAL_DIM),
            cache_mem_k[l].reshape(db, N_MEM, MEM_WIDTH), cache_mem_v[l].reshape(db, N_MEM, MEM_WIDTH),
            wp)
        w_p = kp.shape[1]
        per_layer.append((
            kp.reshape(bsz, w_p, A_HEADS, A_HEAD_DIM), vp.reshape(bsz, w_p, A_HEADS, A_HEAD_DIM),
            s_p.reshape(bsz, B_HEADS, B_KEY_DIM, B_VAL_DIM),
            mk.reshape(bsz, N_MEM, MEM_HEADS, MEM_HEAD_DIM), mv.reshape(bsz, N_MEM, MEM_HEADS, MEM_HEAD_DIM),
            nk.reshape(db, w_buf, A_HEADS, A_HEAD_DIM), nv.reshape(db, w_buf, A_HEADS, A_HEAD_DIM),
            s_s.reshape(db, B_HEADS, B_KEY_DIM, B_VAL_DIM)))
    stacked = [jnp.stack(t) for t in zip(*per_layer)]
    return (xp, xs, *stacked)
```

```python
import functools

import jax
import jax.numpy as jnp
import numpy as np
from jax import lax
from jax.experimental import pallas as pl
from jax.experimental.pallas import tpu as pltpu

F32 = jnp.float32
BF16 = jnp.bfloat16
I32 = jnp.int32

EPS = 1e-6
D_MODEL = 1024
A_HEADS, A_HEAD_DIM, A_WIDTH = 8, 64, 512
A_SCALE = A_HEAD_DIM ** -0.5
LOG2E = 1.4426950408889634
DILATED_GROUPS = ((128, 1), (512, 4), (2048, 16))
SPAN = 128
B_HEADS, B_KEY_DIM, B_VAL_DIM = 4, 64, 128
B_QK_WIDTH, B_V_WIDTH = 256, 512
GATE_RANK = 16
GATE_TEMP = 16.0
N_MEM, MEM_HEADS, MEM_HEAD_DIM, MEM_WIDTH = 256, 4, 128, 512
N_EXPERTS, TOP_K, D_FF = 32, 4, 1024
SWIGLU_ALPHA, SWIGLU_LIMIT = 1.702, 7.0

LANES = 128
VMEM_LIMIT = 56 * 1024 * 1024

PROJ_TILE = 512
SWA_BLOCK = 2048
SWA_UNROLL = 4
GLA_CHUNK, GLA_SUB, GLA_STEP = 64, 16, 512
POST_TILE = 512
SAMPLE_SEQS = 8
EXPERT_TILE = 512
DISPATCH_BLOCK = 512
GROUP_ROWS = 8
GROUP_BITS = (DISPATCH_BLOCK // GROUP_ROWS).bit_length()
LOCAL_ROWS = DISPATCH_BLOCK * TOP_K + N_EXPERTS * GROUP_ROWS


def _cparams(sem, vmem=VMEM_LIMIT, **kw):
    return pltpu.CompilerParams(dimension_semantics=sem, vmem_limit_bytes=vmem, **kw)


def _full(shape):
    n = len(shape)
    return pl.BlockSpec(shape, lambda *_: (0,) * n)


def _rms(x, g):
    ms = jnp.mean(x * x, axis=-1, keepdims=True)
    return x * lax.rsqrt(ms + EPS) * g


def _dot(a, b):
    return jnp.dot(a, b, preferred_element_type=F32)


def _dot_nt(a, b):
    return lax.dot_general(a, b, (((1,), (1,)), ((), ())), preferred_element_type=F32)


def _block_diag(n, blk, val, dtype):
    i = np.arange(n)
    return jnp.asarray(np.where((i[:, None] // blk) == (i[None, :] // blk), val, 0.0), dtype)


def _in_proj_body(x_ref, g1_ref, w_ref, gq_ref, gk_ref, bd_ref, wa_ref, ba_ref,
                  q_ref, k_ref, v_ref, kc_ref, vc_ref, bq_ref, bk_ref, bv_ref, gate_ref, la_ref):
    xn = _rms(x_ref[...], g1_ref[...]).astype(BF16)
    bd = bd_ref[...]

    def proj(lo, hi):
        return _dot(xn, w_ref[:, lo:hi])

    def headnorm(z, g):
        ms = _dot((z * z).astype(BF16), bd)
        return z * lax.rsqrt(ms + EPS) * g

    def put_groups(ref, z):
        for p in range(A_WIDTH // LANES):
            ref[p] = z[:, p * LANES:(p + 1) * LANES]

    put_groups(q_ref, headnorm(proj(0, 512), gq_ref[...]) * (A_SCALE * LOG2E))
    k = headnorm(proj(512, 1024), gk_ref[...])
    v = proj(1024, 1536)
    put_groups(k_ref, k)
    put_groups(v_ref, v)
    kc_ref[...] = k
    vc_ref[...] = v
    bq_ref[...] = proj(1536, 1792) * (B_KEY_DIM ** -0.5)
    bk_ref[...] = proj(1792, 2048)
    bv_ref[...] = proj(2048, 2560).astype(bv_ref.dtype)
    br = proj(2560, 3072)
    gate_ref[...] = (br * jax.nn.sigmoid(br)).astype(gate_ref.dtype)
    lr = proj(3072, 3200).astype(BF16)
    pre = _dot(lr, wa_ref[...]) + ba_ref[...]
    log_sig = jnp.minimum(pre, 0.0) - jnp.log1p(jnp.exp(-jnp.abs(pre)))
    la_ref[...] = log_sig * (1.0 / GATE_TEMP)


def _in_proj(x, wp, wide_dtype, seq, keep):
    n = x.shape[0]
    tm = min(PROJ_TILE, n)
    row = lambda w: pl.BlockSpec((tm, w), lambda i: (i, 0))
    ngrp = A_WIDTH // LANES
    grp = pl.BlockSpec((ngrp, tm, LANES), lambda i: (0, i, 0))
    if keep == seq:
        kept = row(A_WIDTH)
    else:
        tps, kt = seq // tm, keep // tm
        assert tps * tm == seq and kt * tm == keep
        kept = pl.BlockSpec((tm, A_WIDTH),
                            lambda i: ((i // tps) * kt + jnp.maximum(i % tps - (tps - kt), 0), 0))
    outs = [(256, F32), (256, F32), (512, wide_dtype), (512, wide_dtype), (256, F32)]
    return pl.pallas_call(
        _in_proj_body,
        grid=(n // tm,),
        in_specs=[row(D_MODEL), _full((1, D_MODEL)), _full((D_MODEL, 3200)), _full((1, 512)),
                  _full((1, 512)), _full((512, 512)), _full((LANES, 256)), _full((1, 256))],
        out_specs=[grp] * 3 + [kept] * 2 + [row(w) for w, _ in outs],
        out_shape=[jax.ShapeDtypeStruct((ngrp, n, LANES), F32)] * 3
        + [jax.ShapeDtypeStruct((n // seq * keep, A_WIDTH), F32)] * 2
        + [jax.ShapeDtypeStruct((n, w), dt) for w, dt in outs],
        compiler_params=_cparams(("arbitrary",)),
        name="in_proj",
    )(x, wp["g1"], wp["w_in"], wp["gq"], wp["gk"], wp["bd64"], wp["wa"], wp["ba"])


def _unroll_for(trips):
    return max(u for u in range(1, SWA_UNROLL + 1) if trips % u == 0)


def _ds(start, size, stride):
    return pl.ds(start, size) if stride == 1 else pl.ds(start, size, stride=stride)


def _swa_prompt_body(q_ref, kp_ref, kc_ref, vp_ref, vc_ref, o_ref, m_s, l_s, acc_s):
    i = pl.program_id(2)
    qb = SWA_BLOCK
    lane = lax.broadcasted_iota(I32, (SPAN, LANES), 1)
    lo_mask = lane < A_HEAD_DIM
    jq = lax.broadcasted_iota(I32, (SPAN, 2 * SPAN), 0)
    jk = lax.broadcasted_iota(I32, (SPAN, 2 * SPAN), 1)
    dist = jq + SPAN - jk
    band = (dist >= 0) & (dist <= SPAN)
    cur_half = jk >= SPAN

    def attend(qp, kp, vp, valid, rows, first_group):
        vp1 = jnp.concatenate([vp, jnp.ones_like(vp)], axis=1)
        res = []
        for hh in range(2):
            msk = lo_mask if hh == 0 else jnp.logical_not(lo_mask)
            qm = jnp.where(msk, qp, 0.0).astype(BF16)
            s = jnp.where(valid, _dot_nt(qm, kp), -jnp.inf)
            m = jnp.max(s, axis=1, keepdims=True)
            res.append((m, _dot(jnp.exp2(s - m).astype(BF16), vp1)))
        m_new = jnp.where(lo_mask, res[0][0], res[1][0])
        l_new = jnp.where(lo_mask, res[0][1][:, LANES:], res[1][1][:, LANES:])
        o_new = jnp.where(lo_mask, res[0][1][:, :LANES], res[1][1][:, :LANES])
        if first_group:
            m_s[rows, :] = m_new
            l_s[rows, :] = l_new
            acc_s[rows, :] = o_new
        else:
            m_old = m_s[rows, :]
            m = jnp.maximum(m_old, m_new)
            a_old = jnp.exp2(m_old - m)
            a_new = jnp.exp2(m_new - m)
            m_s[rows, :] = m
            l_s[rows, :] = l_s[rows, :] * a_old + l_new * a_new
            acc_s[rows, :] = acc_s[rows, :] * a_old + o_new * a_new

    for gi, (_, dil) in enumerate(DILATED_GROUPS):
        unit = dil * SPAN
        nblk = qb // unit
        first = gi == 0

        def head_block(r, carry, dil=dil, unit=unit, first=first):
            rows = _ds(r, SPAN, dil)
            prev = _ds(qb - unit + r, SPAN, dil)
            ks = jnp.concatenate([kp_ref[prev, :], kc_ref[rows, :]], axis=0).astype(BF16)
            vs = jnp.concatenate([vp_ref[prev, :], vc_ref[rows, :]], axis=0).astype(BF16)
            valid = band & (cur_half | (i > 0))
            attend(q_ref[rows, :], ks, vs, valid, rows, first)
            return carry

        lax.fori_loop(0, dil, head_block, 0, unroll=_unroll_for(dil))

        if nblk > 1:
            def tail_block(idx, carry, dil=dil, unit=unit, first=first):
                n = idx // dil + 1
                r = idx % dil
                start = unit * n + r
                rows = _ds(start, SPAN, dil)
                keys = _ds(start - unit, 2 * SPAN, dil)
                attend(q_ref[rows, :], kc_ref[keys, :].astype(BF16), vc_ref[keys, :].astype(BF16),
                       band, rows, first)
                return carry

            lax.fori_loop(0, (nblk - 1) * dil, tail_block, 0, unroll=_unroll_for((nblk - 1) * dil))

    o_ref[...] = (acc_s[...] / l_s[...]).astype(o_ref.dtype)


def _swa_prompt(q, k, v, bsz, seq):
    qb = SWA_BLOCK
    nb = seq // qb
    cur = pl.BlockSpec((None, qb, LANES), lambda p, b, i: (p, b * nb + i, 0))
    prev = pl.BlockSpec((None, qb, LANES), lambda p, b, i: (p, b * nb + jnp.maximum(i - 1, 0), 0))
    return pl.pallas_call(
        _swa_prompt_body,
        grid=(A_WIDTH // LANES, bsz, nb),
        in_specs=[cur, prev, cur, prev, cur],
        out_specs=pl.BlockSpec((qb, LANES), lambda p, b, i: (b * nb + i, p)),
        out_shape=jax.ShapeDtypeStruct((bsz * seq, A_WIDTH), BF16),
        scratch_shapes=[pltpu.VMEM((qb, LANES), F32)] * 3,
        compiler_params=_cparams(("parallel", "parallel", "arbitrary")),
        name="swa_prompt",
    )(q, k, k, v, v)


def _swa_sample_body(q_ref, kn_ref, vn_ref, ck_ref, cv_ref, c1_ref, c2_ref,
                     o_ref, nk_ref, nv_ref):
    t_new = q_ref.shape[1]
    w_buf = ck_ref.shape[1]
    ck, cv = ck_ref[0], cv_ref[0]
    kn, vn = kn_ref[0], vn_ref[0]
    nk_ref[0, 0:w_buf - t_new, :] = ck[t_new:, :]
    nk_ref[0, w_buf - t_new:, :] = kn
    nv_ref[0, 0:w_buf - t_new, :] = cv[t_new:, :]
    nv_ref[0, w_buf - t_new:, :] = vn

    q = q_ref[0]
    lane = lax.broadcasted_iota(I32, (t_new, LANES), 1)
    lo_mask = lane < A_HEAD_DIM
    c1, c2 = c1_ref[...], c2_ref[...]
    outs = []
    for p in range(A_WIDTH // LANES):
        sl = slice(p * LANES, (p + 1) * LANES)
        qp = q[:, sl]
        qblk = jnp.concatenate([jnp.where(lo_mask, qp, 0.0), jnp.where(lo_mask, 0.0, qp)],
                               axis=0).astype(BF16)
        s1 = jnp.where(c1 > 0, _dot_nt(qblk, ck[:, sl].astype(BF16)), -jnp.inf)
        s2 = jnp.where(c2 > 0, _dot_nt(qblk, kn[:, sl].astype(BF16)), -jnp.inf)
        m = jnp.maximum(jnp.max(s1, axis=1, keepdims=True), jnp.max(s2, axis=1, keepdims=True))
        p1 = c1 * jnp.exp2(s1 - m)
        p2 = c2 * jnp.exp2(s2 - m)
        l = jnp.sum(p1, axis=1, keepdims=True) + jnp.sum(p2, axis=1, keepdims=True)
        o = (_dot(p1.astype(BF16), cv[:, sl].astype(BF16))
             + _dot(p2.astype(BF16), vn[:, sl].astype(BF16))) / l
        outs.append(jnp.where(lo_mask, o[:t_new], o[t_new:]))
    o_ref[0] = jnp.concatenate(outs, axis=1).astype(o_ref.dtype)


def _sample_multiplicity(t_new, w_buf):
    t = np.arange(t_new)[:, None]
    e = np.arange(w_buf + t_new)[None, :]
    d = w_buf + t - e
    c = np.zeros(d.shape, np.float32)
    for window, dil in DILATED_GROUPS:
        c += ((d >= 0) & (d % dil == 0) & (d <= window)).astype(np.float32)
    c = np.concatenate([c, c], axis=0)
    return jnp.asarray(c[:, :w_buf]), jnp.asarray(c[:, w_buf:])


def _swa_sample(q, kn, vn, cache_k, cache_v):
    db, t_new, w = q.shape
    w_buf = cache_k.shape[1]
    assert w_buf >= max(win for win, _ in DILATED_GROUPS) and t_new % 8 == 0
    c1, c2 = _sample_multiplicity(t_new, w_buf)
    new = pl.BlockSpec((1, t_new, w), lambda b: (b, 0, 0))
    cache = pl.BlockSpec((1, w_buf, w), lambda b: (b, 0, 0))
    return pl.pallas_call(
        _swa_sample_body,
        grid=(db,),
        in_specs=[new, new, new, cache, cache, _full(c1.shape), _full(c2.shape)],
        out_specs=[new, cache, cache],
        out_shape=[jax.ShapeDtypeStruct((db, t_new, w), BF16),
                   jax.ShapeDtypeStruct(cache_k.shape, cache_k.dtype),
                   jax.ShapeDtypeStruct(cache_v.shape, cache_v.dtype)],
        compiler_params=_cparams(("parallel",)),
        name="swa_sample",
    )(q, kn, vn, cache_k, cache_v, c1, c2)


def _gla_body(chunk, sub, nch, q_ref, k_ref, g_ref, v_ref, gate_ref, s0_ref, tril_ref, dmask_ref,
              bones_ref, sbm_ref, bd_ref, gout_ref, o_ref, sfin_ref, sbd):
    j = pl.program_id(1)
    sbm = sbm_ref[...]
    nsub = chunk // sub
    pad = B_KEY_DIM - chunk

    @pl.when(j == 0)
    def _():
        s0 = s0_ref[0]
        sbd[...] = jnp.concatenate([s0] * B_HEADS, axis=1) * sbm

    row = lax.broadcasted_iota(I32, (chunk, 1), 0)
    sub_id = row // sub
    lane_w = lax.broadcasted_iota(I32, (chunk, LANES * max(nsub - 1, 1)), 1)
    lo_w = (lane_w % LANES) < B_KEY_DIM

    def one_chunk(c, carry):
        off = pl.multiple_of(c * chunk, chunk)
        rows = pl.ds(off, chunk)
        q, k, g = q_ref[rows, :], k_ref[rows, :], g_ref[rows, :]
        v = v_ref[rows, :].astype(F32)
        g1 = g.astype(BF16)
        r1 = g - g1.astype(F32)
        g2 = r1.astype(BF16)
        g3 = (r1 - g2.astype(F32)).astype(BF16)
        tril = tril_ref[...]
        b = _dot(tril, g1) + _dot(tril, g2) + _dot(tril, g3)
        b_last = b[chunk - 1:chunk, :]
        state = sbd[...]

        o = _dot((q * jnp.exp(b)).astype(BF16), state.astype(BF16))

        bones = bones_ref[...]
        att = _dot((q * k).astype(BF16), bones) * dmask_ref[0]
        gate = jnp.exp(g)
        decay = gate
        for d in range(1, sub):
            if d > 1:
                decay = decay * pltpu.roll(gate, d - 1, 0)
            w = q * pltpu.roll(k, d, 0) * decay
            att = att + _dot(w.astype(BF16), bones) * dmask_ref[d]

        if nsub > 1:
            qx, kx = [], []
            for i in range(1, nsub):
                r_i = b[sub * i - 1:sub * i, :]
                qx.append(jnp.where(sub_id == i, q * jnp.exp(jnp.minimum(b - r_i, 0.0)), 0.0))
                kx.append(jnp.where(sub_id < i, k * jnp.exp(jnp.minimum(r_i - b, 0.0)), 0.0))
            parts = []
            for p in range(B_QK_WIDTH // LANES):
                sl = slice(p * LANES, (p + 1) * LANES)
                qp = jnp.concatenate([x[:, sl] for x in qx], axis=1)
                kp = jnp.concatenate([x[:, sl] for x in kx], axis=1).astype(BF16)
                zero = jnp.zeros_like(kp)
                lhs = jnp.concatenate([jnp.where(lo_w, qp, 0.0), jnp.where(lo_w, 0.0, qp)],
                                      axis=1).astype(BF16)
                rhs = jnp.concatenate([jnp.concatenate([kp, zero], axis=1),
                                       jnp.concatenate([zero, kp], axis=1)], axis=0)
                parts.append(_dot_nt(lhs, rhs))
            att = att + jnp.concatenate(parts, axis=1)

        if pad:
            vrow = jnp.concatenate([v, jnp.zeros((pad, B_V_WIDTH), F32)], axis=0)
        else:
            vrow = v
        vbd = (jnp.concatenate([vrow] * B_HEADS, axis=0) * sbm).astype(BF16)
        o = o + _dot(att.astype(BF16), vbd)

        ke = (k * jnp.exp(b_last - b)).astype(BF16)
        upd = lax.dot_general(ke, v.astype(BF16), (((0,), (0,)), ((), ())),
                              preferred_element_type=F32)
        dec = jnp.transpose(jnp.broadcast_to(jnp.exp(b_last), (8, B_QK_WIDTH)))[:, 0:1]
        sbd[...] = (state * dec + upd) * sbm

        ms = _dot((o * o).astype(BF16), bd_ref[...])
        on = o * lax.rsqrt(ms + EPS) * gout_ref[...] * gate_ref[rows, :].astype(F32)
        o_ref[rows, :] = on.astype(o_ref.dtype)
        return carry

    lax.fori_loop(0, nch, one_chunk, 0)

    @pl.when(j == pl.num_programs(1) - 1)
    def _():
        s = sbd[...]
        sfin_ref[0] = jnp.concatenate(
            [s[h * B_KEY_DIM:(h + 1) * B_KEY_DIM, h * B_VAL_DIM:(h + 1) * B_VAL_DIM]
             for h in range(B_HEADS)], axis=0)


def _gla_consts(chunk, sub):
    t = np.arange(chunk)
    tril = (t[:, None] >= t[None, :]).astype(np.float32)
    lane = np.arange(B_QK_WIDTH)
    dmask = np.zeros((sub, chunk, B_QK_WIDTH), np.float32)
    for d in range(sub):
        ok = (t % sub) >= d
        dmask[d] = ((lane[None, :] % B_KEY_DIM) == (t[:, None] - d)) & ok[:, None]
    r = np.arange(B_QK_WIDTH)[:, None] // B_KEY_DIM
    c = np.arange(B_V_WIDTH)[None, :] // B_VAL_DIM
    sbm = (r == c).astype(np.float32)
    return jnp.asarray(tril, BF16), jnp.asarray(dmask), jnp.asarray(sbm)


def _gla(q, k, g, v, gate, s0, wp, length, chunk, sub, step):
    n = q.shape[0]
    bsz = n // length
    assert chunk == sub or chunk == B_KEY_DIM
    tril, dmask, sbm = _gla_consts(chunk, sub)
    nstep = length // step
    row = lambda w: pl.BlockSpec((step, w), lambda b, j: (b * nstep + j, 0))
    st = pl.BlockSpec((1, B_QK_WIDTH, B_VAL_DIM), lambda b, j: (b, 0, 0))
    return pl.pallas_call(
        functools.partial(_gla_body, chunk, sub, step // chunk),
        grid=(bsz, nstep),
        in_specs=[row(256), row(256), row(256), row(512), row(512), st, _full(tril.shape),
                  _full(dmask.shape), _full((256, 256)), _full(sbm.shape), _full((512, 512)),
                  _full((1, 512))],
        out_specs=[row(512), st],
        out_shape=[jax.ShapeDtypeStruct((n, B_V_WIDTH), BF16),
                   jax.ShapeDtypeStruct(s0.shape, F32)],
        scratch_shapes=[pltpu.VMEM((B_QK_WIDTH, B_V_WIDTH), F32)],
        compiler_params=_cparams(("parallel", "arbitrary")),
        name="gla",
    )(q, k, g, v, gate, s0, tril, dmask, wp["bones64"], sbm, wp["bd128"], wp["gout"])


def _head_rms(z, g, scale=1.0):
    parts = []
    for h in range(MEM_HEADS):
        zh = z[:, h * LANES:(h + 1) * LANES]
        parts.append(zh * lax.rsqrt(jnp.mean(zh * zh, axis=-1, keepdims=True) + EPS))
    return jnp.concatenate(parts, axis=1) * (g * scale)


def _mem_kv_body(m_ref, gn_ref, wk_ref, wv_ref, gk_ref, mk_ref, mv_ref):
    mn = _rms(m_ref[...], gn_ref[...]).astype(BF16)
    mk_ref[...] = _head_rms(_dot(mn, wk_ref[...]), gk_ref[...])
    mv_ref[...] = _dot(mn, wv_ref[...])


def _mem_kv(mem, wp):
    n = mem.shape[0]
    tm = 256
    row = lambda w: pl.BlockSpec((tm, w), lambda i: (i, 0))
    return pl.pallas_call(
        _mem_kv_body,
        grid=(n // tm,),
        in_specs=[row(D_MODEL), _full((1, D_MODEL)), _full((D_MODEL, 512)), _full((D_MODEL, 512)),
                  _full((1, 512))],
        out_specs=[row(512), row(512)],
        out_shape=[jax.ShapeDtypeStruct((n, 512), F32)] * 2,
        compiler_params=_cparams(("parallel",)),
        name="mem_kv",
    )(mem, wp["gmem"], wp["wmk"], wp["wmv"], wp["gmk"])


def _post_body(nseq, x_ref, oa_ref, ob_ref, woa_ref, wob_ref, g2_ref, wq_ref, gmq_ref, mk_ref,
               mv_ref, wo_ref, g3_ref, rw_ref, rb_ref, cnt0_ref, tri_ref,
               h2_ref, xn_ref, meta_ref, cnt_ref, carry):
    tm = x_ref.shape[0]

    @pl.when(pl.program_id(0) == 0)
    def _():
        carry[...] = cnt0_ref[...]

    h = x_ref[...] + _dot(oa_ref[...], woa_ref[...]) + _dot(ob_ref[...], wob_ref[...])
    hn = _rms(h, g2_ref[...]).astype(BF16)
    qm = _head_rms(_dot(hn, wq_ref[...]), gmq_ref[...], MEM_HEAD_DIM ** -0.5).astype(BF16)
    nk = nseq * N_MEM
    mk = mk_ref[...].reshape(nk, MEM_WIDTH)
    mv = mv_ref[...].reshape(nk, MEM_WIDTH)
    if nseq > 1:
        rt = lax.broadcasted_iota(I32, (tm, nk), 0) // (tm // nseq)
        ct = lax.broadcasted_iota(I32, (tm, nk), 1) // N_MEM
        same = rt == ct
    outs = []
    for hd in range(MEM_HEADS):
        sl = slice(hd * LANES, (hd + 1) * LANES)
        s = _dot_nt(qm[:, sl], mk[:, sl].astype(BF16))
        if nseq > 1:
            s = jnp.where(same, s, -jnp.inf)
        m = jnp.max(s, axis=1, keepdims=True)
        pr = jnp.exp(s - m)
        l = jnp.sum(pr, axis=1, keepdims=True)
        outs.append(_dot(pr.astype(BF16), mv[:, sl].astype(BF16)) / l)
    h2 = h + _dot(jnp.concatenate(outs, axis=1).astype(BF16), wo_ref[...])
    h2_ref[...] = h2
    xn = _rms(h2, g3_ref[...])
    xn_ref[...] = xn

    x1 = xn.astype(BF16)
    x2 = (xn - x1.astype(F32)).astype(BF16)
    prod = _dot(jnp.concatenate([x1, x2], axis=0), rw_ref[...])
    logits = (prod[:tm, :LANES] + prod[:tm, LANES:] + prod[tm:, :LANES] + prod[tm:, LANES:]
              + rb_ref[...])
    lane = lax.broadcasted_iota(I32, (tm, LANES), 1)
    vals, idxs, hots = [], [], []
    work = logits
    for _ in range(TOP_K):
        m = jnp.max(work, axis=1, keepdims=True)
        idx = jnp.min(jnp.where(work == m, lane, LANES), axis=1, keepdims=True)
        hot = lane == idx
        vals.append(m)
        idxs.append(idx)
        hots.append(hot)
        work = jnp.where(hot, -jnp.inf, work)
    exps = [jnp.exp(v - vals[0]) for v in vals]
    den = exps[0] + exps[1] + exps[2] + exps[3]

    sel = (hots[0] | hots[1] | hots[2] | hots[3]).astype(F32)
    before = _dot(tri_ref[...], sel.astype(BF16)) + carry[...]
    carry[...] = carry[...] + jnp.sum(sel, axis=0, keepdims=True)
    cnt_ref[...] = carry[...]

    meta = jnp.zeros((tm, LANES), F32)
    for kk in range(TOP_K):
        rank = jnp.sum(jnp.where(hots[kk], before, 0.0), axis=1, keepdims=True)
        meta = jnp.where(lane == kk, idxs[kk].astype(F32), meta)
        meta = jnp.where(lane == TOP_K + kk, rank, meta)
        meta = jnp.where(lane == 2 * TOP_K + kk, exps[kk] / den, meta)
    meta_ref[...] = meta


def _post(x, oa, ob, mk, mv, cnt0, wp, tm, nseq, tiles_per_mem):
    n = x.shape[0]
    row = lambda w: pl.BlockSpec((tm, w), lambda i: (i, 0))
    mem = pl.BlockSpec((nseq, N_MEM, MEM_WIDTH), lambda i: (i // tiles_per_mem, 0, 0))
    tri = jnp.asarray(np.tril(np.ones((tm, tm), np.float32), -1), BF16)
    return pl.pallas_call(
        functools.partial(_post_body, nseq),
        grid=(n // tm,),
        in_specs=[row(D_MODEL), row(512), row(512), _full((512, D_MODEL)), _full((512, D_MODEL)),
                  _full((1, D_MODEL)), _full((D_MODEL, 512)), _full((1, 512)), mem, mem,
                  _full((512, D_MODEL)), _full((1, D_MODEL)), _full((D_MODEL, 2 * LANES)),
                  _full((1, LANES)), _full((1, LANES)), _full((tm, tm))],
        out_specs=[row(D_MODEL), row(D_MODEL), row(LANES), _full((1, LANES))],
        out_shape=[jax.ShapeDtypeStruct((n, D_MODEL), F32), jax.ShapeDtypeStruct((n, D_MODEL), F32),
                   jax.ShapeDtypeStruct((n, LANES), F32), jax.ShapeDtypeStruct((1, LANES), F32)],
        scratch_shapes=[pltpu.VMEM((1, LANES), F32)],
        compiler_params=_cparams(("arbitrary",)),
        name="post",
    )(x, oa, ob, wp["woa"], wp["wob"], wp["g2"], wp["wmq"], wp["gmq"], mk, mv, wp["wmo"],
      wp["g3"], wp["rw"], wp["rb"], cnt0, tri)


def _group_copies(src, src_row, dst, dst_row, units, sem, start):
    off = jnp.int32(0)
    for k in reversed(range(GROUP_BITS)):
        size = GROUP_ROWS << k
        bit = (units >> k) & 1
        s = pl.multiple_of(src_row + off, GROUP_ROWS)
        d = pl.multiple_of(dst_row + off, GROUP_ROWS)
        cp = pltpu.make_async_copy(src.at[pl.ds(s, size)], dst.at[pl.ds(d, size)], sem)

        @pl.when(bit == 1)
        def _(cp=cp):
            if start:
                cp.start()
            else:
                cp.wait()

        off = off + bit * size


def _pad_rows_body(pstart_ref, punits_ref, nt_ref, xs_ref, zeros, sem):
    zeros[...] = jnp.zeros_like(zeros)
    n_tiles_max = xs_ref.shape[0] // EXPERT_TILE

    def each(start):
        def expert_tail(e, carry):
            _group_copies(zeros, 0, xs_ref, pstart_ref[e], punits_ref[e], sem, start)
            return carry

        def unused_tile(t, carry):
            row = pl.multiple_of(t * EXPERT_TILE, EXPERT_TILE)
            cp = pltpu.make_async_copy(zeros, xs_ref.at[pl.ds(row, EXPERT_TILE)], sem)
            if start:
                cp.start()
            else:
                cp.wait()
            return carry

        lax.fori_loop(0, N_EXPERTS, expert_tail, 0)
        lax.fori_loop(nt_ref[0], n_tiles_max, unused_tile, 0)

    each(True)
    each(False)


def _pad_rows(pad_start, pad_units, n_tiles, rows):
    return pl.pallas_call(
        _pad_rows_body,
        grid_spec=pltpu.PrefetchScalarGridSpec(
            num_scalar_prefetch=3, grid=(1,),
            in_specs=[],
            out_specs=pl.BlockSpec(memory_space=pl.ANY),
            scratch_shapes=[pltpu.VMEM((EXPERT_TILE, D_MODEL), F32), pltpu.SemaphoreType.DMA]),
        out_shape=jax.ShapeDtypeStruct((rows, D_MODEL), F32),
        compiler_params=_cparams(("arbitrary",)),
        name="moe_pad_rows",
    )(pad_start, pad_units, n_tiles)


def _dispatch_body(blk0, loff_ref, gstart_ref, units_ref, ldest_ref, x_ref, xs_in_ref, xs_ref,
                   xloc, sem):
    del xs_in_ref
    b = pl.program_id(0) + blk0
    tb = x_ref.shape[0]
    rows = lax.broadcasted_iota(I32, (LOCAL_ROWS, tb), 0)
    ld = ldest_ref[...]
    hot = rows == ld[0:1, :]
    for kk in range(1, TOP_K):
        hot = hot | (rows == ld[kk:kk + 1, :])
    xloc[...] = _dot(jnp.where(hot, 1.0, 0.0).astype(BF16), x_ref[...].astype(BF16))

    def each(start):
        def body(e, carry):
            j = b * N_EXPERTS + e
            _group_copies(xloc, loff_ref[j], xs_ref, gstart_ref[j], units_ref[j], sem, start)
            return carry
        lax.fori_loop(0, N_EXPERTS, body, 0)

    each(True)
    each(False)


def _dispatch(tables, blk0, ldest_t, xn, xs):
    n = xn.shape[0]
    tb = DISPATCH_BLOCK
    idx = lambda i, *_: (i, 0)
    return pl.pallas_call(
        functools.partial(_dispatch_body, blk0),
        grid_spec=pltpu.PrefetchScalarGridSpec(
            num_scalar_prefetch=3, grid=(n // tb,),
            in_specs=[pl.BlockSpec((TOP_K, tb), lambda i, *_: (0, i)),
                      pl.BlockSpec((tb, D_MODEL), idx),
                      pl.BlockSpec(memory_space=pl.ANY)],
            out_specs=pl.BlockSpec(memory_space=pl.ANY),
            scratch_shapes=[pltpu.VMEM((LOCAL_ROWS, D_MODEL), F32), pltpu.SemaphoreType.DMA]),
        out_shape=jax.ShapeDtypeStruct(xs.shape, xs.dtype),
        input_output_aliases={5: 0},
        compiler_params=_cparams(("arbitrary",)),
        name="moe_dispatch",
    )(*tables, ldest_t, xn, xs)


def _expert_body(te_ref, first_ref, nt_ref, x_ref, w1_ref, sel_ref, b1g_ref, b1l_ref, w2_ref, b2_ref,
                 y_ref, w1g, w1l, w2):
    i = pl.program_id(0)
    live = i < nt_ref[0]

    @pl.when(live & (first_ref[i] == 1))
    def _():
        sel = sel_ref[...]
        for j in range(D_FF // LANES):
            z = _dot(w1_ref[0, :, 2 * LANES * j:2 * LANES * (j + 1)].astype(BF16), sel)
            w1g[:, LANES * j:LANES * (j + 1)] = z[:, :LANES].astype(BF16)
            w1l[:, LANES * j:LANES * (j + 1)] = z[:, LANES:].astype(BF16)
        w2[...] = w2_ref[0].astype(BF16)

    @pl.when(live)
    def _():
        x = x_ref[...].astype(BF16)
        glu = jnp.minimum(_dot(x, w1g[...]) + b1g_ref[0], SWIGLU_LIMIT)
        lin = jnp.clip(_dot(x, w1l[...]) + b1l_ref[0], -SWIGLU_LIMIT, SWIGLU_LIMIT)
        act = glu * jax.nn.sigmoid(SWIGLU_ALPHA * glu) * (lin + 1.0)
        y_ref[...] = _dot(act.astype(BF16), w2[...]) + b2_ref[0]

    @pl.when(jnp.logical_not(live))
    def _():
        y_ref[...] = jnp.zeros_like(y_ref)


def _experts(tile_expert, n_tiles, xs, wp):
    rows = xs.shape[0]
    first = jnp.concatenate([jnp.ones((1,), I32),
                             (tile_expert[1:] != tile_expert[:-1]).astype(I32)])
    c = np.arange(2 * LANES)
    sel = np.zeros((2 * LANES, 2 * LANES), np.float32)
    sel[c, (c % 2) * LANES + c // 2] = 1.0
    tile = lambda i, te, fi, nt: (jnp.minimum(i, nt[0] - 1), 0)
    out_tile = lambda i, te, fi, nt: (i, 0)
    wsel = lambda i, te, fi, nt: (te[jnp.minimum(i, nt[0] - 1)], 0, 0)
    wspec = lambda r, c: pl.BlockSpec((1, r, c), wsel)
    return pl.pallas_call(
        _expert_body,
        grid_spec=pltpu.PrefetchScalarGridSpec(
            num_scalar_prefetch=3, grid=(rows // EXPERT_TILE,),
            in_specs=[pl.BlockSpec((EXPERT_TILE, D_MODEL), tile),
                      wspec(D_MODEL, 2 * D_FF), pl.BlockSpec((2 * LANES, 2 * LANES), lambda *_: (0, 0)),
                      wspec(1, D_FF), wspec(1, D_FF), wspec(D_FF, D_MODEL), wspec(1, D_MODEL)],
            out_specs=pl.BlockSpec((EXPERT_TILE, D_MODEL), out_tile),
            scratch_shapes=[pltpu.VMEM((D_MODEL, D_FF), BF16)] * 2 + [pltpu.VMEM((D_FF, D_MODEL), BF16)]),
        out_shape=jax.ShapeDtypeStruct((rows, D_MODEL), F32),
        compiler_params=_cparams(("arbitrary",)),
        name="moe_experts",
    )(tile_expert, first, n_tiles, xs, wp["w1"], jnp.asarray(sel, BF16), wp["b1g"], wp["b1l"],
      wp["w2"], wp["b2"])


def _combine_body(blk0, loff_ref, gstart_ref, units_ref, ldest_ref, meta_ref, h_ref, ys_ref, o_ref,
                  yloc, sem):
    b = pl.program_id(0) + blk0
    tb = h_ref.shape[0]

    @pl.when(pl.program_id(0) == 0)
    def _():
        yloc[...] = jnp.zeros_like(yloc)

    def each(start):
        def body(e, carry):
            j = b * N_EXPERTS + e
            _group_copies(ys_ref, gstart_ref[j], yloc, loff_ref[j], units_ref[j], sem, start)
            return carry
        lax.fori_loop(0, N_EXPERTS, body, 0)

    each(True)
    cols = lax.broadcasted_iota(I32, (tb, LOCAL_ROWS), 1)
    ld = ldest_ref[...]
    meta = meta_ref[...]
    gmat = jnp.zeros((tb, LOCAL_ROWS), F32)
    for kk in range(TOP_K):
        gate = meta[:, 2 * TOP_K + kk:2 * TOP_K + kk + 1]
        gmat = jnp.where(cols == ld[:, kk:kk + 1], gate, gmat)
    each(False)
    o_ref[...] = h_ref[...] + _dot(gmat.astype(BF16), yloc[...].astype(BF16))


def _combine(tables, blk0, ldest, meta, h2, ys):
    n = h2.shape[0]
    tb = DISPATCH_BLOCK
    row = lambda w: pl.BlockSpec((tb, w), lambda i, *_: (i, 0))
    return pl.pallas_call(
        functools.partial(_combine_body, blk0),
        grid_spec=pltpu.PrefetchScalarGridSpec(
            num_scalar_prefetch=3, grid=(n // tb,),
            in_specs=[row(TOP_K), row(LANES), row(D_MODEL), pl.BlockSpec(memory_space=pl.ANY)],
            out_specs=row(D_MODEL),
            scratch_shapes=[pltpu.VMEM((LOCAL_ROWS, D_MODEL), F32), pltpu.SemaphoreType.DMA]),
        out_shape=jax.ShapeDtypeStruct((n, D_MODEL), F32),
        compiler_params=_cparams(("arbitrary",)),
        name="moe_combine",
    )(*tables, ldest, meta, h2, ys)


def _moe(groups, wp):
    tb = DISPATCH_BLOCK
    sizes = [g[0].shape[0] for g in groups]
    assert all(s % tb == 0 for s in sizes)
    n_tok = sum(sizes)
    nb = n_tok // tb
    rows = n_tok * TOP_K + nb * N_EXPERTS * GROUP_ROWS + N_EXPERTS * EXPERT_TILE
    n_tiles_max = rows // EXPERT_TILE

    eidx = jnp.concatenate([g[2][:, 0:TOP_K] for g in groups]).astype(I32)
    rank = jnp.concatenate([g[2][:, TOP_K:2 * TOP_K] for g in groups]).astype(I32)
    hot = eidx[:, :, None] == jnp.arange(N_EXPERTS, dtype=I32)
    cnt = jnp.sum(hot.reshape(nb, tb * TOP_K, N_EXPERTS), axis=1, dtype=I32)
    npad = (cnt + GROUP_ROWS - 1) // GROUP_ROWS * GROUP_ROWS
    loff = jnp.cumsum(npad, axis=1) - npad
    gsize = jnp.sum(npad, axis=0)
    gpad = (gsize + EXPERT_TILE - 1) // EXPERT_TILE * EXPERT_TILE
    ends = jnp.cumsum(gpad)
    gstart = (ends - gpad)[None, :] + jnp.cumsum(npad, axis=0) - npad
    before = jnp.cumsum(cnt, axis=0) - cnt
    n_tiles = (ends[-1] // EXPERT_TILE).reshape(1)
    tile_ids = jnp.arange(n_tiles_max, dtype=I32)
    tile_expert = jnp.minimum(
        jnp.sum((ends // EXPERT_TILE)[None, :] <= tile_ids[:, None], axis=1), N_EXPERTS - 1).astype(I32)

    base = jnp.repeat(loff - before, tb, axis=0)
    ldest = rank + jnp.sum(jnp.where(hot, base[:, None, :], 0), axis=2)
    ldest_t = ldest.T
    tables = (loff.reshape(-1), gstart.reshape(-1).astype(I32), (npad // GROUP_ROWS).reshape(-1))

    xs = _pad_rows(((ends - gpad) + gsize).astype(I32), ((gpad - gsize) // GROUP_ROWS).astype(I32),
                   n_tiles, rows)
    starts = np.cumsum([0] + sizes[:-1])
    for (_, xn, _), t0, sz in zip(groups, starts, sizes):
        xs = _dispatch(tables, int(t0) // tb, ldest_t[:, t0:t0 + sz], xn, xs)
    ys = _experts(tile_expert, n_tiles, xs, wp)
    return [_combine(tables, int(t0) // tb, ldest[t0:t0 + sz], meta, h2, ys)
            for (h2, _, meta), t0, sz in zip(groups, starts, sizes)]


def _prep_weights(norm1_g, w_in, a_q_norm_g, a_k_norm_g, gla_w_alpha, gla_b_alpha, gla_out_norm_g,
                  w_out, norm2_g, mem_norm_g, mem_w_q, mem_w_k, mem_w_v, mem_q_norm_g,
                  mem_k_norm_g, mem_w_o, norm3_g, router_w, router_b, exp_w1, exp_b1, exp_w2,
                  exp_b2):
    main = 3 * A_WIDTH + 2 * B_QK_WIDTH + 2 * B_V_WIDTH
    w_lr = jnp.pad(w_in[:, main:], ((0, 0), (0, LANES - GATE_RANK)))
    rw = jnp.pad(router_w, ((0, 0), (0, LANES - N_EXPERTS)))
    rwh = rw.astype(BF16)
    return {
        "g1": norm1_g[None],
        "w_in": jnp.concatenate([w_in[:, :main], w_lr], axis=1).astype(BF16),
        "gq": jnp.tile(a_q_norm_g, A_HEADS)[None],
        "gk": jnp.tile(a_k_norm_g, A_HEADS)[None],
        "bd64": _block_diag(A_WIDTH, A_HEAD_DIM, 1.0 / A_HEAD_DIM, BF16),
        "wa": jnp.pad(gla_w_alpha, ((0, LANES - GATE_RANK), (0, 0))).astype(BF16),
        "ba": gla_b_alpha[None],
        "bones64": _block_diag(B_QK_WIDTH, B_KEY_DIM, 1.0, BF16),
        "bd128": _block_diag(B_V_WIDTH, B_VAL_DIM, 1.0 / B_VAL_DIM, BF16),
        "gout": jnp.tile(gla_out_norm_g, B_HEADS)[None],
        "woa": w_out[:A_WIDTH].astype(BF16),
        "wob": w_out[A_WIDTH:].astype(BF16),
        "g2": norm2_g[None],
        "gmem": mem_norm_g[None],
        "wmq": mem_w_q.astype(BF16),
        "wmk": mem_w_k.astype(BF16),
        "wmv": mem_w_v.astype(BF16),
        "gmq": jnp.tile(mem_q_norm_g, MEM_HEADS)[None],
        "gmk": jnp.tile(mem_k_norm_g, MEM_HEADS)[None],
        "wmo": mem_w_o.astype(BF16),
        "g3": norm3_g[None],
        "rw": jnp.concatenate([rwh, (rw - rwh.astype(F32)).astype(BF16)], axis=1),
        "rb": jnp.pad(router_b, (0, LANES - N_EXPERTS), constant_values=-1e30)[None],
        "w1": exp_w1,
        "b1g": exp_b1[:, None, 0::2],
        "b1l": exp_b1[:, None, 1::2],
        "w2": exp_w2,
        "b2": exp_b2[:, None, :],
    }


def _layer(xp, xs, mem_prompt, cache_k, cache_v, state_gla, cache_mk, cache_mv, wp):
    bsz, seq, _ = xp.shape
    db, t_new, _ = xs.shape
    w_buf = cache_k.shape[1]
    w_p = min(max(w for w, _ in DILATED_GROUPS), seq)

    xpf = xp.reshape(bsz * seq, D_MODEL)
    q, k, v, k_last, v_last, bq, bk, bv, gate, la = _in_proj(xpf, wp, BF16, seq, w_p)
    oa = _swa_prompt(q, k, v, bsz, seq)
    s0 = jnp.zeros((bsz, B_QK_WIDTH, B_VAL_DIM), F32)
    ob, s_p = _gla(bq, bk, la, bv, gate, s0, wp, seq, GLA_CHUNK, GLA_SUB, GLA_STEP)
    mk, mv = _mem_kv(mem_prompt.reshape(bsz * N_MEM, D_MODEL), wp)
    cnt0 = jnp.zeros((1, LANES), F32)
    h2_p, xn_p, meta_p, cnt = _post(xpf, oa, ob, mk.reshape(bsz, N_MEM, MEM_WIDTH),
                                    mv.reshape(bsz, N_MEM, MEM_WIDTH), cnt0, wp,
                                    POST_TILE, 1, seq // POST_TILE)

    xsf = xs.reshape(db * t_new, D_MODEL)
    q, _, _, ks, vs, bq, bk, bv, gate, la = _in_proj(xsf, wp, F32, t_new, t_new)
    new3 = lambda a: a.reshape(db, t_new, A_WIDTH)
    oa_s, nk, nv = _swa_sample(new3(q.transpose(1, 0, 2)), new3(ks), new3(vs), cache_k, cache_v)
    ob_s, s_s = _gla(bq, bk, la, bv, gate, state_gla, wp, t_new, t_new, t_new, t_new)
    h2_s, xn_s, meta_s, cnt = _post(xsf, oa_s.reshape(db * t_new, A_WIDTH), ob_s, cache_mk,
                                    cache_mv, cnt, wp, SAMPLE_SEQS * t_new, SAMPLE_SEQS, 1)

    y_p, y_s = _moe([(h2_p, xn_p, meta_p), (h2_s, xn_s, meta_s)], wp)
    return (y_p.reshape(bsz, seq, D_MODEL), y_s.reshape(db, t_new, D_MODEL),
            k_last.reshape(bsz, w_p, A_WIDTH), v_last.reshape(bsz, w_p, A_WIDTH),
            s_p, mk, mv, nk, nv, s_s)


def kernel(x_prompt, x_sample, mem_prompt, cache_swa_k, cache_swa_v, state_gla, cache_mem_k, cache_mem_v, norm1_g, w_in, a_q_norm_g, a_k_norm_g, gla_w_alpha, gla_b_alpha, gla_out_norm_g, w_out, norm2_g, mem_norm_g, mem_w_q, mem_w_k, mem_w_v, mem_q_norm_g, mem_k_norm_g, mem_w_o, norm3_g, router_w, router_b, exp_w1, exp_b1, exp_w2, exp_b2):
    depth = w_in.shape[0]
    bsz = x_prompt.shape[0]
    db, w_buf = cache_swa_k.shape[1], cache_swa_k.shape[2]
    xp, xs = x_prompt, x_sample
    per_layer = []
    for l in range(depth):
        wp = _prep_weights(
            norm1_g[l], w_in[l], a_q_norm_g[l], a_k_norm_g[l], gla_w_alpha[l], gla_b_alpha[l],
            gla_out_norm_g[l], w_out[l], norm2_g[l], mem_norm_g[l], mem_w_q[l], mem_w_k[l],
            mem_w_v[l], mem_q_norm_g[l], mem_k_norm_g[l], mem_w_o[l], norm3_g[l], router_w[l],
            router_b[l], exp_w1[l], exp_b1[l], exp_w2[l], exp_b2[l])
        xp, xs, kp, vp, s_p, mk, mv, nk, nv, s_s = _layer(
            xp, xs, mem_prompt,
            cache_swa_k[l].reshape(db, w_buf, A_WIDTH), cache_swa_v[l].reshape(db, w_buf, A_WIDTH),
            state_gla[l].reshape(db, B_QK_WIDTH, B_VAL_DIM),
            cache_mem_k[l].reshape(db, N_MEM, MEM_WIDTH), cache_mem_v[l].reshape(db, N_MEM, MEM_WIDTH),
            wp)
        w_p = kp.shape[1]
        per_layer.append((
            kp.reshape(bsz, w_p, A_HEADS, A_HEAD_DIM), vp.reshape(bsz, w_p, A_HEADS, A_HEAD_DIM),
            s_p.reshape(bsz, B_HEADS, B_KEY_DIM, B_VAL_DIM),
            mk.reshape(bsz, N_MEM, MEM_HEADS, MEM_HEAD_DIM), mv.reshape(bsz, N_MEM, MEM_HEADS, MEM_HEAD_DIM),
            nk.reshape(db, w_buf, A_HEADS, A_HEAD_DIM), nv.reshape(db, w_buf, A_HEADS, A_HEAD_DIM),
            s_s.reshape(db, B_HEADS, B_KEY_DIM, B_VAL_DIM)))
    stacked = [jnp.stack(t) for t in zip(*per_layer)]
    return (xp, xs, *stacked)
```

```python
import functools

import jax
import jax.numpy as jnp
import numpy as np
from jax import lax
from jax.experimental import pallas as pl
from jax.experimental.pallas import tpu as pltpu

F32 = jnp.float32
BF16 = jnp.bfloat16
I32 = jnp.int32

EPS = 1e-6
D_MODEL = 1024
A_HEADS, A_HEAD_DIM, A_WIDTH = 8, 64, 512
A_SCALE = A_HEAD_DIM ** -0.5
LOG2E = 1.4426950408889634
DILATED_GROUPS = ((128, 1), (512, 4), (2048, 16))
SPAN = 128
B_HEADS, B_KEY_DIM, B_VAL_DIM = 4, 64, 128
B_QK_WIDTH, B_V_WIDTH = 256, 512
GATE_RANK = 16
GATE_TEMP = 16.0
N_MEM, MEM_HEADS, MEM_HEAD_DIM, MEM_WIDTH = 256, 4, 128, 512
N_EXPERTS, TOP_K, D_FF = 32, 4, 1024
SWIGLU_ALPHA, SWIGLU_LIMIT = 1.702, 7.0

LANES = 128
VMEM_LIMIT = 56 * 1024 * 1024

PROJ_TILE = 512
SWA_BLOCK = 2048
SWA_UNROLL = 4
GLA_CHUNK, GLA_SUB, GLA_STEP = 64, 16, 512
POST_TILE = 512
SAMPLE_SEQS = 8
EXPERT_TILE = 512
DISPATCH_BLOCK = 512
GROUP_ROWS = 8
GROUP_BITS = (DISPATCH_BLOCK // GROUP_ROWS).bit_length()
LOCAL_ROWS = DISPATCH_BLOCK * TOP_K + N_EXPERTS * GROUP_ROWS


def _cparams(sem, vmem=VMEM_LIMIT, **kw):
    return pltpu.CompilerParams(dimension_semantics=sem, vmem_limit_bytes=vmem, **kw)


def _full(shape):
    n = len(shape)
    return pl.BlockSpec(shape, lambda *_: (0,) * n)


def _rms(x, g):
    ms = jnp.mean(x * x, axis=-1, keepdims=True)
    return x * lax.rsqrt(ms + EPS) * g


def _dot(a, b):
    return jnp.dot(a, b, preferred_element_type=F32)


def _dot_nt(a, b):
    return lax.dot_general(a, b, (((1,), (1,)), ((), ())), preferred_element_type=F32)


def _block_diag(n, blk, val, dtype):
    i = np.arange(n)
    return jnp.asarray(np.where((i[:, None] // blk) == (i[None, :] // blk), val, 0.0), dtype)


def _in_proj_body(x_ref, g1_ref, w_ref, gq_ref, gk_ref, bd_ref, wa_ref, ba_ref,
                  q_ref, k_ref, v_ref, kc_ref, vc_ref, bq_ref, bk_ref, bv_ref, gate_ref, la_ref):
    xn = _rms(x_ref[...], g1_ref[...]).astype(BF16)
    bd = bd_ref[...]

    def proj(lo, hi):
        return _dot(xn, w_ref[:, lo:hi])

    def headnorm(z, g):
        ms = _dot((z * z).astype(BF16), bd)
        return z * lax.rsqrt(ms + EPS) * g

    def put_groups(ref, z):
        for p in range(A_WIDTH // LANES):
            ref[p] = z[:, p * LANES:(p + 1) * LANES]

    put_groups(q_ref, headnorm(proj(0, 512), gq_ref[...]) * (A_SCALE * LOG2E))
    k = headnorm(proj(512, 1024), gk_ref[...])
    v = proj(1024, 1536)
    put_groups(k_ref, k)
    put_groups(v_ref, v)
    kc_ref[...] = k
    vc_ref[...] = v
    bq_ref[...] = proj(1536, 1792) * (B_KEY_DIM ** -0.5)
    bk_ref[...] = proj(1792, 2048)
    bv_ref[...] = proj(2048, 2560).astype(bv_ref.dtype)
    br = proj(2560, 3072)
    gate_ref[...] = (br * jax.nn.sigmoid(br)).astype(gate_ref.dtype)
    lr = proj(3072, 3200).astype(BF16)
    pre = _dot(lr, wa_ref[...]) + ba_ref[...]
    log_sig = jnp.minimum(pre, 0.0) - jnp.log1p(jnp.exp(-jnp.abs(pre)))
    la_ref[...] = log_sig * (1.0 / GATE_TEMP)


def _in_proj(x, wp, wide_dtype, seq, keep):
    n = x.shape[0]
    tm = min(PROJ_TILE, n)
    row = lambda w: pl.BlockSpec((tm, w), lambda i: (i, 0))
    ngrp = A_WIDTH // LANES
    grp = pl.BlockSpec((ngrp, tm, LANES), lambda i: (0, i, 0))
    if keep == seq:
        kept = row(A_WIDTH)
    else:
        tps, kt = seq // tm, keep // tm
        assert tps * tm == seq and kt * tm == keep
        kept = pl.BlockSpec((tm, A_WIDTH),
                            lambda i: ((i // tps) * kt + jnp.maximum(i % tps - (tps - kt), 0), 0))
    outs = [(256, F32), (256, F32), (512, wide_dtype), (512, wide_dtype), (256, F32)]
    return pl.pallas_call(
        _in_proj_body,
        grid=(n // tm,),
        in_specs=[row(D_MODEL), _full((1, D_MODEL)), _full((D_MODEL, 3200)), _full((1, 512)),
                  _full((1, 512)), _full((512, 512)), _full((LANES, 256)), _full((1, 256))],
        out_specs=[grp] * 3 + [kept] * 2 + [row(w) for w, _ in outs],
        out_shape=[jax.ShapeDtypeStruct((ngrp, n, LANES), F32)] * 3
        + [jax.ShapeDtypeStruct((n // seq * keep, A_WIDTH), F32)] * 2
        + [jax.ShapeDtypeStruct((n, w), dt) for w, dt in outs],
        compiler_params=_cparams(("arbitrary",)),
        name="in_proj",
    )(x, wp["g1"], wp["w_in"], wp["gq"], wp["gk"], wp["bd64"], wp["wa"], wp["ba"])


def _unroll_for(trips):
    return max(u for u in range(1, SWA_UNROLL + 1) if trips % u == 0)


def _ds(start, size, stride):
    return pl.ds(start, size) if stride == 1 else pl.ds(start, size, stride=stride)


def _swa_prompt_body(q_ref, kp_ref, kc_ref, vp_ref, vc_ref, o_ref, m_s, l_s, acc_s):
    i = pl.program_id(2)
    qb = SWA_BLOCK
    lane = lax.broadcasted_iota(I32, (SPAN, LANES), 1)
    lo_mask = lane < A_HEAD_DIM
    jq = lax.broadcasted_iota(I32, (SPAN, 2 * SPAN), 0)
    jk = lax.broadcasted_iota(I32, (SPAN, 2 * SPAN), 1)
    dist = jq + SPAN - jk
    band = (dist >= 0) & (dist <= SPAN)
    cur_half = jk >= SPAN

    def attend(qp, kp, vp, valid, rows, first_group):
        vp1 = jnp.concatenate([vp, jnp.ones_like(vp)], axis=1)
        res = []
        for hh in range(2):
            msk = lo_mask if hh == 0 else jnp.logical_not(lo_mask)
            qm = jnp.where(msk, qp, 0.0).astype(BF16)
            s = jnp.where(valid, _dot_nt(qm, kp), -jnp.inf)
            m = jnp.max(s, axis=1, keepdims=True)
            res.append((m, _dot(jnp.exp2(s - m).astype(BF16), vp1)))
        m_new = jnp.where(lo_mask, res[0][0], res[1][0])
        l_new = jnp.where(lo_mask, res[0][1][:, LANES:], res[1][1][:, LANES:])
        o_new = jnp.where(lo_mask, res[0][1][:, :LANES], res[1][1][:, :LANES])
        if first_group:
            m_s[rows, :] = m_new
            l_s[rows, :] = l_new
            acc_s[rows, :] = o_new
        else:
            m_old = m_s[rows, :]
            m = jnp.maximum(m_old, m_new)
            a_old = jnp.exp2(m_old - m)
            a_new = jnp.exp2(m_new - m)
            m_s[rows, :] = m
            l_s[rows, :] = l_s[rows, :] * a_old + l_new * a_new
            acc_s[rows, :] = acc_s[rows, :] * a_old + o_new * a_new

    for gi, (_, dil) in enumerate(DILATED_GROUPS):
        unit = dil * SPAN
        nblk = qb // unit
        first = gi == 0

        def head_block(r, carry, dil=dil, unit=unit, first=first):
            rows = _ds(r, SPAN, dil)
            prev = _ds(qb - unit + r, SPAN, dil)
            ks = jnp.concatenate([kp_ref[prev, :], kc_ref[rows, :]], axis=0).astype(BF16)
            vs = jnp.concatenate([vp_ref[prev, :], vc_ref[rows, :]], axis=0).astype(BF16)
            valid = band & (cur_half | (i > 0))
            attend(q_ref[rows, :], ks, vs, valid, rows, first)
            return carry

        lax.fori_loop(0, dil, head_block, 0, unroll=_unroll_for(dil))

        if nblk > 1:
            def tail_block(idx, carry, dil=dil, unit=unit, first=first):
                n = idx // dil + 1
                r = idx % dil
                start = unit * n + r
                rows = _ds(start, SPAN, dil)
                keys = _ds(start - unit, 2 * SPAN, dil)
                attend(q_ref[rows, :], kc_ref[keys, :].astype(BF16), vc_ref[keys, :].astype(BF16),
                       band, rows, first)
                return carry

            lax.fori_loop(0, (nblk - 1) * dil, tail_block, 0, unroll=_unroll_for((nblk - 1) * dil))

    o_ref[...] = (acc_s[...] / l_s[...]).astype(o_ref.dtype)


def _swa_prompt(q, k, v, bsz, seq):
    qb = SWA_BLOCK
    nb = seq // qb
    cur = pl.BlockSpec((None, qb, LANES), lambda p, b, i: (p, b * nb + i, 0))
    prev = pl.BlockSpec((None, qb, LANES), lambda p, b, i: (p, b * nb + jnp.maximum(i - 1, 0), 0))
    return pl.pallas_call(
        _swa_prompt_body,
        grid=(A_WIDTH // LANES, bsz, nb),
        in_specs=[cur, prev, cur, prev, cur],
        out_specs=pl.BlockSpec((qb, LANES), lambda p, b, i: (b * nb + i, p)),
        out_shape=jax.ShapeDtypeStruct((bsz * seq, A_WIDTH), BF16),
        scratch_shapes=[pltpu.VMEM((qb, LANES), F32)] * 3,
        compiler_params=_cparams(("parallel", "parallel", "arbitrary")),
        name="swa_prompt",
    )(q, k, k, v, v)


def _swa_sample_body(q_ref, kn_ref, vn_ref, ck_ref, cv_ref, c1_ref, c2_ref,
                     o_ref, nk_ref, nv_ref):
    t_new = q_ref.shape[1]
    w_buf = ck_ref.shape[2]
    ck, cv = ck_ref[0], cv_ref[0]
    kn, vn = kn_ref[0], vn_ref[0]
    tail_lane = lax.broadcasted_iota(I32, (A_WIDTH, LANES), 1) >= LANES - t_new

    def shift_in(old, new, out_ref):
        moved = pltpu.roll(old, w_buf - t_new, 1)
        new_t = jnp.transpose(jnp.concatenate([jnp.zeros((LANES - t_new, A_WIDTH), F32), new], axis=0))
        out_ref[0, :, :w_buf - LANES] = moved[:, :w_buf - LANES]
        out_ref[0, :, w_buf - LANES:] = jnp.where(tail_lane, new_t, moved[:, w_buf - LANES:])

    shift_in(ck, kn, nk_ref)
    shift_in(cv, vn, nv_ref)

    q = q_ref[0]
    lane = lax.broadcasted_iota(I32, (t_new, LANES), 1)
    lo_mask = lane < A_HEAD_DIM
    c1, c2 = c1_ref[...], c2_ref[...]
    outs = []
    for p in range(A_WIDTH // LANES):
        sl = slice(p * LANES, (p + 1) * LANES)
        qp = q[:, sl]
        qblk = jnp.concatenate([jnp.where(lo_mask, qp, 0.0), jnp.where(lo_mask, 0.0, qp)],
                               axis=0).astype(BF16)
        s1 = jnp.where(c1 > 0, _dot(qblk, ck[sl, :].astype(BF16)), -jnp.inf)
        s2 = jnp.where(c2 > 0, _dot_nt(qblk, kn[:, sl].astype(BF16)), -jnp.inf)
        m = jnp.maximum(jnp.max(s1, axis=1, keepdims=True), jnp.max(s2, axis=1, keepdims=True))
        p1 = c1 * jnp.exp2(s1 - m)
        p2 = c2 * jnp.exp2(s2 - m)
        l = jnp.sum(p1, axis=1, keepdims=True) + jnp.sum(p2, axis=1, keepdims=True)
        o = (_dot_nt(p1.astype(BF16), cv[sl, :].astype(BF16))
             + _dot(p2.astype(BF16), vn[:, sl].astype(BF16))) / l
        outs.append(jnp.where(lo_mask, o[:t_new], o[t_new:]))
    o_ref[0] = jnp.concatenate(outs, axis=1).astype(o_ref.dtype)


def _sample_multiplicity(t_new, w_buf):
    t = np.arange(t_new)[:, None]
    e = np.arange(w_buf + t_new)[None, :]
    d = w_buf + t - e
    c = np.zeros(d.shape, np.float32)
    for window, dil in DILATED_GROUPS:
        c += ((d >= 0) & (d % dil == 0) & (d <= window)).astype(np.float32)
    c = np.concatenate([c, c], axis=0)
    return jnp.asarray(c[:, :w_buf]), jnp.asarray(c[:, w_buf:])


def _swa_sample(q, kn, vn, cache_k, cache_v):
    db, t_new, w = q.shape
    w_buf = cache_k.shape[2]
    assert w_buf >= max(win for win, _ in DILATED_GROUPS) and t_new % 8 == 0 and t_new <= LANES
    c1, c2 = _sample_multiplicity(t_new, w_buf)
    new = pl.BlockSpec((1, t_new, w), lambda b: (b, 0, 0))
    cache = pl.BlockSpec((1, w, w_buf), lambda b: (b, 0, 0))
    return pl.pallas_call(
        _swa_sample_body,
        grid=(db,),
        in_specs=[new, new, new, cache, cache, _full(c1.shape), _full(c2.shape)],
        out_specs=[new, cache, cache],
        out_shape=[jax.ShapeDtypeStruct((db, t_new, w), BF16),
                   jax.ShapeDtypeStruct(cache_k.shape, cache_k.dtype),
                   jax.ShapeDtypeStruct(cache_v.shape, cache_v.dtype)],
        compiler_params=_cparams(("parallel",)),
        name="swa_sample",
    )(q, kn, vn, cache_k, cache_v, c1, c2)


def _gla_body(chunk, sub, nch, q_ref, k_ref, g_ref, v_ref, gate_ref, s0_ref, tril_ref, dmask_ref,
              bones_ref, sbm_ref, bd_ref, gout_ref, o_ref, sfin_ref, sbd):
    j = pl.program_id(1)
    sbm = sbm_ref[...]
    nsub = chunk // sub
    pad = B_KEY_DIM - chunk

    @pl.when(j == 0)
    def _():
        s0 = s0_ref[0]
        sbd[...] = jnp.concatenate([s0] * B_HEADS, axis=1) * sbm

    row = lax.broadcasted_iota(I32, (chunk, 1), 0)
    sub_id = row // sub
    lane_w = lax.broadcasted_iota(I32, (chunk, LANES * max(nsub - 1, 1)), 1)
    lo_w = (lane_w % LANES) < B_KEY_DIM

    def one_chunk(c, carry):
        off = pl.multiple_of(c * chunk, chunk)
        rows = pl.ds(off, chunk)
        q, k, g = q_ref[rows, :], k_ref[rows, :], g_ref[rows, :]
        v = v_ref[rows, :].astype(F32)
        g1 = g.astype(BF16)
        r1 = g - g1.astype(F32)
        g2 = r1.astype(BF16)
        g3 = (r1 - g2.astype(F32)).astype(BF16)
        tril = tril_ref[...]
        b = _dot(tril, g1) + _dot(tril, g2) + _dot(tril, g3)
        b_last = b[chunk - 1:chunk, :]
        state = sbd[...]

        o = _dot((q * jnp.exp(b)).astype(BF16), state.astype(BF16))

        bones = bones_ref[...]
        att = _dot((q * k).astype(BF16), bones) * dmask_ref[0]
        gate = jnp.exp(g)
        decay = gate
        for d in range(1, sub):
            if d > 1:
                decay = decay * pltpu.roll(gate, d - 1, 0)
            w = q * pltpu.roll(k, d, 0) * decay
            att = att + _dot(w.astype(BF16), bones) * dmask_ref[d]

        if nsub > 1:
            qx, kx = [], []
            for i in range(1, nsub):
                r_i = b[sub * i - 1:sub * i, :]
                qx.append(jnp.where(sub_id == i, q * jnp.exp(jnp.minimum(b - r_i, 0.0)), 0.0))
                kx.append(jnp.where(sub_id < i, k * jnp.exp(jnp.minimum(r_i - b, 0.0)), 0.0))
            parts = []
            for p in range(B_QK_WIDTH // LANES):
                sl = slice(p * LANES, (p + 1) * LANES)
                qp = jnp.concatenate([x[:, sl] for x in qx], axis=1)
                kp = jnp.concatenate([x[:, sl] for x in kx], axis=1).astype(BF16)
                zero = jnp.zeros_like(kp)
                lhs = jnp.concatenate([jnp.where(lo_w, qp, 0.0), jnp.where(lo_w, 0.0, qp)],
                                      axis=1).astype(BF16)
                rhs = jnp.concatenate([jnp.concatenate([kp, zero], axis=1),
                                       jnp.concatenate([zero, kp], axis=1)], axis=0)
                parts.append(_dot_nt(lhs, rhs))
            att = att + jnp.concatenate(parts, axis=1)

        if pad:
            vrow = jnp.concatenate([v, jnp.zeros((pad, B_V_WIDTH), F32)], axis=0)
        else:
            vrow = v
        vbd = (jnp.concatenate([vrow] * B_HEADS, axis=0) * sbm).astype(BF16)
        o = o + _dot(att.astype(BF16), vbd)

        ke = (k * jnp.exp(b_last - b)).astype(BF16)
        upd = lax.dot_general(ke, v.astype(BF16), (((0,), (0,)), ((), ())),
                              preferred_element_type=F32)
        dec = jnp.transpose(jnp.broadcast_to(jnp.exp(b_last), (8, B_QK_WIDTH)))[:, 0:1]
        sbd[...] = (state * dec + upd) * sbm

        ms = _dot((o * o).astype(BF16), bd_ref[...])
        on = o * lax.rsqrt(ms + EPS) * gout_ref[...] * gate_ref[rows, :].astype(F32)
        o_ref[rows, :] = on.astype(o_ref.dtype)
        return carry

    lax.fori_loop(0, nch, one_chunk, 0)

    @pl.when(j == pl.num_programs(1) - 1)
    def _():
        s = sbd[...]
        sfin_ref[0] = jnp.concatenate(
            [s[h * B_KEY_DIM:(h + 1) * B_KEY_DIM, h * B_VAL_DIM:(h + 1) * B_VAL_DIM]
             for h in range(B_HEADS)], axis=0)


def _gla_consts(chunk, sub):
    t = np.arange(chunk)
    tril = (t[:, None] >= t[None, :]).astype(np.float32)
    lane = np.arange(B_QK_WIDTH)
    dmask = np.zeros((sub, chunk, B_QK_WIDTH), np.float32)
    for d in range(sub):
        ok = (t % sub) >= d
        dmask[d] = ((lane[None, :] % B_KEY_DIM) == (t[:, None] - d)) & ok[:, None]
    r = np.arange(B_QK_WIDTH)[:, None] // B_KEY_DIM
    c = np.arange(B_V_WIDTH)[None, :] // B_VAL_DIM
    sbm = (r == c).astype(np.float32)
    return jnp.asarray(tril, BF16), jnp.asarray(dmask), jnp.asarray(sbm)


def _gla(q, k, g, v, gate, s0, wp, length, chunk, sub, step):
    n = q.shape[0]
    bsz = n // length
    assert chunk == sub or chunk == B_KEY_DIM
    tril, dmask, sbm = _gla_consts(chunk, sub)
    nstep = length // step
    row = lambda w: pl.BlockSpec((step, w), lambda b, j: (b * nstep + j, 0))
    st = pl.BlockSpec((1, B_QK_WIDTH, B_VAL_DIM), lambda b, j: (b, 0, 0))
    return pl.pallas_call(
        functools.partial(_gla_body, chunk, sub, step // chunk),
        grid=(bsz, nstep),
        in_specs=[row(256), row(256), row(256), row(512), row(512), st, _full(tril.shape),
                  _full(dmask.shape), _full((256, 256)), _full(sbm.shape), _full((512, 512)),
                  _full((1, 512))],
        out_specs=[row(512), st],
        out_shape=[jax.ShapeDtypeStruct((n, B_V_WIDTH), BF16),
                   jax.ShapeDtypeStruct(s0.shape, F32)],
        scratch_shapes=[pltpu.VMEM((B_QK_WIDTH, B_V_WIDTH), F32)],
        compiler_params=_cparams(("parallel", "arbitrary")),
        name="gla",
    )(q, k, g, v, gate, s0, tril, dmask, wp["bones64"], sbm, wp["bd128"], wp["gout"])


def _head_rms(z, g, scale=1.0):
    parts = []
    for h in range(MEM_HEADS):
        zh = z[:, h * LANES:(h + 1) * LANES]
        parts.append(zh * lax.rsqrt(jnp.mean(zh * zh, axis=-1, keepdims=True) + EPS))
    return jnp.concatenate(parts, axis=1) * (g * scale)


def _mem_kv_body(m_ref, gn_ref, wk_ref, wv_ref, gk_ref, mk_ref, mv_ref):
    mn = _rms(m_ref[...], gn_ref[...]).astype(BF16)
    mk_ref[...] = _head_rms(_dot(mn, wk_ref[...]), gk_ref[...])
    mv_ref[...] = _dot(mn, wv_ref[...])


def _mem_kv(mem, wp):
    n = mem.shape[0]
    tm = 256
    row = lambda w: pl.BlockSpec((tm, w), lambda i: (i, 0))
    return pl.pallas_call(
        _mem_kv_body,
        grid=(n // tm,),
        in_specs=[row(D_MODEL), _full((1, D_MODEL)), _full((D_MODEL, 512)), _full((D_MODEL, 512)),
                  _full((1, 512))],
        out_specs=[row(512), row(512)],
        out_shape=[jax.ShapeDtypeStruct((n, 512), F32)] * 2,
        compiler_params=_cparams(("parallel",)),
        name="mem_kv",
    )(mem, wp["gmem"], wp["wmk"], wp["wmv"], wp["gmk"])


def _post_body(nseq, x_ref, oa_ref, ob_ref, woa_ref, wob_ref, g2_ref, wq_ref, gmq_ref, mk_ref,
               mv_ref, wo_ref, g3_ref, rw_ref, rb_ref, cnt0_ref, tri_ref,
               h2_ref, xn_ref, meta_ref, cnt_ref, carry):
    tm = x_ref.shape[0]

    @pl.when(pl.program_id(0) == 0)
    def _():
        carry[...] = cnt0_ref[...]

    h = x_ref[...] + _dot(oa_ref[...], woa_ref[...]) + _dot(ob_ref[...], wob_ref[...])
    hn = _rms(h, g2_ref[...]).astype(BF16)
    qm = _head_rms(_dot(hn, wq_ref[...]), gmq_ref[...], MEM_HEAD_DIM ** -0.5).astype(BF16)
    nk = nseq * N_MEM
    mk = mk_ref[...].reshape(nk, MEM_WIDTH)
    mv = mv_ref[...].reshape(nk, MEM_WIDTH)
    if nseq > 1:
        rt = lax.broadcasted_iota(I32, (tm, nk), 0) // (tm // nseq)
        ct = lax.broadcasted_iota(I32, (tm, nk), 1) // N_MEM
        same = rt == ct
    outs = []
    for hd in range(MEM_HEADS):
        sl = slice(hd * LANES, (hd + 1) * LANES)
        s = _dot_nt(qm[:, sl], mk[:, sl].astype(BF16))
        if nseq > 1:
            s = jnp.where(same, s, -jnp.inf)
        m = jnp.max(s, axis=1, keepdims=True)
        pr = jnp.exp(s - m)
        l = jnp.sum(pr, axis=1, keepdims=True)
        outs.append(_dot(pr.astype(BF16), mv[:, sl].astype(BF16)) / l)
    h2 = h + _dot(jnp.concatenate(outs, axis=1).astype(BF16), wo_ref[...])
    h2_ref[...] = h2
    xn = _rms(h2, g3_ref[...])
    xn_ref[...] = xn

    x1 = xn.astype(BF16)
    x2 = (xn - x1.astype(F32)).astype(BF16)
    prod = _dot(jnp.concatenate([x1, x2], axis=0), rw_ref[...])
    logits = (prod[:tm, :LANES] + prod[:tm, LANES:] + prod[tm:, :LANES] + prod[tm:, LANES:]
              + rb_ref[...])
    lane = lax.broadcasted_iota(I32, (tm, LANES), 1)
    vals, idxs, hots = [], [], []
    work = logits
    for _ in range(TOP_K):
        m = jnp.max(work, axis=1, keepdims=True)
        idx = jnp.min(jnp.where(work == m, lane, LANES), axis=1, keepdims=True)
        hot = lane == idx
        vals.append(m)
        idxs.append(idx)
        hots.append(hot)
        work = jnp.where(hot, -jnp.inf, work)
    exps = [jnp.exp(v - vals[0]) for v in vals]
    den = exps[0] + exps[1] + exps[2] + exps[3]

    sel = (hots[0] | hots[1] | hots[2] | hots[3]).astype(F32)
    before = _dot(tri_ref[...], sel.astype(BF16)) + carry[...]
    carry[...] = carry[...] + jnp.sum(sel, axis=0, keepdims=True)
    cnt_ref[...] = carry[...]

    meta = jnp.zeros((tm, LANES), F32)
    for kk in range(TOP_K):
        rank = jnp.sum(jnp.where(hots[kk], before, 0.0), axis=1, keepdims=True)
        meta = jnp.where(lane == kk, idxs[kk].astype(F32), meta)
        meta = jnp.where(lane == TOP_K + kk, rank, meta)
        meta = jnp.where(lane == 2 * TOP_K + kk, exps[kk] / den, meta)
    meta_ref[...] = meta


def _post(x, oa, ob, mk, mv, cnt0, wp, tm, nseq, tiles_per_mem):
    n = x.shape[0]
    row = lambda w: pl.BlockSpec((tm, w), lambda i: (i, 0))
    mem = pl.BlockSpec((nseq, N_MEM, MEM_WIDTH), lambda i: (i // tiles_per_mem, 0, 0))
    tri = jnp.asarray(np.tril(np.ones((tm, tm), np.float32), -1), BF16)
    return pl.pallas_call(
        functools.partial(_post_body, nseq),
        grid=(n // tm,),
        in_specs=[row(D_MODEL), row(512), row(512), _full((512, D_MODEL)), _full((512, D_MODEL)),
                  _full((1, D_MODEL)), _full((D_MODEL, 512)), _full((1, 512)), mem, mem,
                  _full((512, D_MODEL)), _full((1, D_MODEL)), _full((D_MODEL, 2 * LANES)),
                  _full((1, LANES)), _full((1, LANES)), _full((tm, tm))],
        out_specs=[row(D_MODEL), row(D_MODEL), row(LANES), _full((1, LANES))],
        out_shape=[jax.ShapeDtypeStruct((n, D_MODEL), F32), jax.ShapeDtypeStruct((n, D_MODEL), F32),
                   jax.ShapeDtypeStruct((n, LANES), F32), jax.ShapeDtypeStruct((1, LANES), F32)],
        scratch_shapes=[pltpu.VMEM((1, LANES), F32)],
        compiler_params=_cparams(("arbitrary",)),
        name="post",
    )(x, oa, ob, wp["woa"], wp["wob"], wp["g2"], wp["wmq"], wp["gmq"], mk, mv, wp["wmo"],
      wp["g3"], wp["rw"], wp["rb"], cnt0, tri)


def _group_copies(src, src_row, dst, dst_row, units, sem, start):
    off = jnp.int32(0)
    for k in reversed(range(GROUP_BITS)):
        size = GROUP_ROWS << k
        bit = (units >> k) & 1
        s = pl.multiple_of(src_row + off, GROUP_ROWS)
        d = pl.multiple_of(dst_row + off, GROUP_ROWS)
        cp = pltpu.make_async_copy(src.at[pl.ds(s, size)], dst.at[pl.ds(d, size)], sem)

        @pl.when(bit == 1)
        def _(cp=cp):
            if start:
                cp.start()
            else:
                cp.wait()

        off = off + bit * size


def _pad_rows_body(pstart_ref, punits_ref, nt_ref, xs_ref, zeros, sem):
    zeros[...] = jnp.zeros_like(zeros)
    n_tiles_max = xs_ref.shape[0] // EXPERT_TILE

    def each(start):
        def expert_tail(e, carry):
            _group_copies(zeros, 0, xs_ref, pstart_ref[e], punits_ref[e], sem, start)
            return carry

        def unused_tile(t, carry):
            row = pl.multiple_of(t * EXPERT_TILE, EXPERT_TILE)
            cp = pltpu.make_async_copy(zeros, xs_ref.at[pl.ds(row, EXPERT_TILE)], sem)
            if start:
                cp.start()
            else:
                cp.wait()
            return carry

        lax.fori_loop(0, N_EXPERTS, expert_tail, 0)
        lax.fori_loop(nt_ref[0], n_tiles_max, unused_tile, 0)

    each(True)
    each(False)


def _pad_rows(pad_start, pad_units, n_tiles, rows):
    return pl.pallas_call(
        _pad_rows_body,
        grid_spec=pltpu.PrefetchScalarGridSpec(
            num_scalar_prefetch=3, grid=(1,),
            in_specs=[],
            out_specs=pl.BlockSpec(memory_space=pl.ANY),
            scratch_shapes=[pltpu.VMEM((EXPERT_TILE, D_MODEL), F32), pltpu.SemaphoreType.DMA]),
        out_shape=jax.ShapeDtypeStruct((rows, D_MODEL), F32),
        compiler_params=_cparams(("arbitrary",)),
        name="moe_pad_rows",
    )(pad_start, pad_units, n_tiles)


def _dispatch_body(blk0, loff_ref, gstart_ref, units_ref, ldest_ref, x_ref, xs_in_ref, xs_ref,
                   xloc, sem):
    del xs_in_ref
    b = pl.program_id(0) + blk0
    tb = x_ref.shape[0]
    rows = lax.broadcasted_iota(I32, (LOCAL_ROWS, tb), 0)
    ld = ldest_ref[...]
    hot = rows == ld[0:1, :]
    for kk in range(1, TOP_K):
        hot = hot | (rows == ld[kk:kk + 1, :])
    xloc[...] = _dot(jnp.where(hot, 1.0, 0.0).astype(BF16), x_ref[...].astype(BF16))

    def each(start):
        def body(e, carry):
            j = b * N_EXPERTS + e
            _group_copies(xloc, loff_ref[j], xs_ref, gstart_ref[j], units_ref[j], sem, start)
            return carry
        lax.fori_loop(0, N_EXPERTS, body, 0)

    each(True)
    each(False)


def _dispatch(tables, blk0, ldest_t, xn, xs):
    n = xn.shape[0]
    tb = DISPATCH_BLOCK
    idx = lambda i, *_: (i, 0)
    return pl.pallas_call(
        functools.partial(_dispatch_body, blk0),
        grid_spec=pltpu.PrefetchScalarGridSpec(
            num_scalar_prefetch=3, grid=(n // tb,),
            in_specs=[pl.BlockSpec((TOP_K, tb), lambda i, *_: (0, i)),
                      pl.BlockSpec((tb, D_MODEL), idx),
                      pl.BlockSpec(memory_space=pl.ANY)],
            out_specs=pl.BlockSpec(memory_space=pl.ANY),
            scratch_shapes=[pltpu.VMEM((LOCAL_ROWS, D_MODEL), F32), pltpu.SemaphoreType.DMA]),
        out_shape=jax.ShapeDtypeStruct(xs.shape, xs.dtype),
        input_output_aliases={5: 0},
        compiler_params=_cparams(("arbitrary",)),
        name="moe_dispatch",
    )(*tables, ldest_t, xn, xs)


def _expert_body(te_ref, first_ref, nt_ref, x_ref, w1_ref, sel_ref, b1g_ref, b1l_ref, w2_ref, b2_ref,
                 y_ref, w1g, w1l, w2):
    i = pl.program_id(0)
    live = i < nt_ref[0]

    @pl.when(live & (first_ref[i] == 1))
    def _():
        sel = sel_ref[...]
        for j in range(D_FF // LANES):
            z = _dot(w1_ref[0, :, 2 * LANES * j:2 * LANES * (j + 1)].astype(BF16), sel)
            w1g[:, LANES * j:LANES * (j + 1)] = z[:, :LANES].astype(BF16)
            w1l[:, LANES * j:LANES * (j + 1)] = z[:, LANES:].astype(BF16)
        w2[...] = w2_ref[0].astype(BF16)

    @pl.when(live)
    def _():
        x = x_ref[...].astype(BF16)
        glu = jnp.minimum(_dot(x, w1g[...]) + b1g_ref[0], SWIGLU_LIMIT)
        lin = jnp.clip(_dot(x, w1l[...]) + b1l_ref[0], -SWIGLU_LIMIT, SWIGLU_LIMIT)
        act = glu * jax.nn.sigmoid(SWIGLU_ALPHA * glu) * (lin + 1.0)
        y_ref[...] = _dot(act.astype(BF16), w2[...]) + b2_ref[0]

    @pl.when(jnp.logical_not(live))
    def _():
        y_ref[...] = jnp.zeros_like(y_ref)


def _experts(tile_expert, n_tiles, xs, wp):
    rows = xs.shape[0]
    first = jnp.concatenate([jnp.ones((1,), I32),
                             (tile_expert[1:] != tile_expert[:-1]).astype(I32)])
    c = np.arange(2 * LANES)
    sel = np.zeros((2 * LANES, 2 * LANES), np.float32)
    sel[c, (c % 2) * LANES + c // 2] = 1.0
    tile = lambda i, te, fi, nt: (jnp.minimum(i, nt[0] - 1), 0)
    out_tile = lambda i, te, fi, nt: (i, 0)
    wsel = lambda i, te, fi, nt: (te[jnp.minimum(i, nt[0] - 1)], 0, 0)
    wspec = lambda r, c: pl.BlockSpec((1, r, c), wsel)
    return pl.pallas_call(
        _expert_body,
        grid_spec=pltpu.PrefetchScalarGridSpec(
            num_scalar_prefetch=3, grid=(rows // EXPERT_TILE,),
            in_specs=[pl.BlockSpec((EXPERT_TILE, D_MODEL), tile),
                      wspec(D_MODEL, 2 * D_FF), pl.BlockSpec((2 * LANES, 2 * LANES), lambda *_: (0, 0)),
                      wspec(1, D_FF), wspec(1, D_FF), wspec(D_FF, D_MODEL), wspec(1, D_MODEL)],
            out_specs=pl.BlockSpec((EXPERT_TILE, D_MODEL), out_tile),
            scratch_shapes=[pltpu.VMEM((D_MODEL, D_FF), BF16)] * 2 + [pltpu.VMEM((D_FF, D_MODEL), BF16)]),
        out_shape=jax.ShapeDtypeStruct((rows, D_MODEL), F32),
        compiler_params=_cparams(("arbitrary",)),
        name="moe_experts",
    )(tile_expert, first, n_tiles, xs, wp["w1"], jnp.asarray(sel, BF16), wp["b1g"], wp["b1l"],
      wp["w2"], wp["b2"])


def _combine_body(blk0, loff_ref, gstart_ref, units_ref, ldest_ref, meta_ref, h_ref, ys_ref, o_ref,
                  yloc, sem):
    b = pl.program_id(0) + blk0
    tb = h_ref.shape[0]

    @pl.when(pl.program_id(0) == 0)
    def _():
        yloc[...] = jnp.zeros_like(yloc)

    def each(start):
        def body(e, carry):
            j = b * N_EXPERTS + e
            _group_copies(ys_ref, gstart_ref[j], yloc, loff_ref[j], units_ref[j], sem, start)
            return carry
        lax.fori_loop(0, N_EXPERTS, body, 0)

    each(True)
    cols = lax.broadcasted_iota(I32, (tb, LOCAL_ROWS), 1)
    ld = ldest_ref[...]
    meta = meta_ref[...]
    gmat = jnp.zeros((tb, LOCAL_ROWS), F32)
    for kk in range(TOP_K):
        gate = meta[:, 2 * TOP_K + kk:2 * TOP_K + kk + 1]
        gmat = jnp.where(cols == ld[:, kk:kk + 1], gate, gmat)
    each(False)
    o_ref[...] = h_ref[...] + _dot(gmat.astype(BF16), yloc[...].astype(BF16))


def _combine(tables, blk0, ldest, meta, h2, ys):
    n = h2.shape[0]
    tb = DISPATCH_BLOCK
    row = lambda w: pl.BlockSpec((tb, w), lambda i, *_: (i, 0))
    return pl.pallas_call(
        functools.partial(_combine_body, blk0),
        grid_spec=pltpu.PrefetchScalarGridSpec(
            num_scalar_prefetch=3, grid=(n // tb,),
            in_specs=[row(TOP_K), row(LANES), row(D_MODEL), pl.BlockSpec(memory_space=pl.ANY)],
            out_specs=row(D_MODEL),
            scratch_shapes=[pltpu.VMEM((LOCAL_ROWS, D_MODEL), F32), pltpu.SemaphoreType.DMA]),
        out_shape=jax.ShapeDtypeStruct((n, D_MODEL), F32),
        compiler_params=_cparams(("arbitrary",)),
        name="moe_combine",
    )(*tables, ldest, meta, h2, ys)


def _moe(groups, wp):
    tb = DISPATCH_BLOCK
    sizes = [g[0].shape[0] for g in groups]
    assert all(s % tb == 0 for s in sizes)
    n_tok = sum(sizes)
    nb = n_tok // tb
    rows = n_tok * TOP_K + nb * N_EXPERTS * GROUP_ROWS + N_EXPERTS * EXPERT_TILE
    n_tiles_max = rows // EXPERT_TILE

    eidx = jnp.concatenate([g[2][:, 0:TOP_K] for g in groups]).astype(I32)
    rank = jnp.concatenate([g[2][:, TOP_K:2 * TOP_K] for g in groups]).astype(I32)
    hot = eidx[:, :, None] == jnp.arange(N_EXPERTS, dtype=I32)
    cnt = jnp.sum(hot.reshape(nb, tb * TOP_K, N_EXPERTS), axis=1, dtype=I32)
    npad = (cnt + GROUP_ROWS - 1) // GROUP_ROWS * GROUP_ROWS
    loff = jnp.cumsum(npad, axis=1) - npad
    gsize = jnp.sum(npad, axis=0)
    gpad = (gsize + EXPERT_TILE - 1) // EXPERT_TILE * EXPERT_TILE
    ends = jnp.cumsum(gpad)
    gstart = (ends - gpad)[None, :] + jnp.cumsum(npad, axis=0) - npad
    before = jnp.cumsum(cnt, axis=0) - cnt
    n_tiles = (ends[-1] // EXPERT_TILE).reshape(1)
    tile_ids = jnp.arange(n_tiles_max, dtype=I32)
    tile_expert = jnp.minimum(
        jnp.sum((ends // EXPERT_TILE)[None, :] <= tile_ids[:, None], axis=1), N_EXPERTS - 1).astype(I32)

    base = jnp.repeat(loff - before, tb, axis=0)
    ldest = rank + jnp.sum(jnp.where(hot, base[:, None, :], 0), axis=2)
    ldest_t = ldest.T
    tables = (loff.reshape(-1), gstart.reshape(-1).astype(I32), (npad // GROUP_ROWS).reshape(-1))

    xs = _pad_rows(((ends - gpad) + gsize).astype(I32), ((gpad - gsize) // GROUP_ROWS).astype(I32),
                   n_tiles, rows)
    starts = np.cumsum([0] + sizes[:-1])
    for (_, xn, _), t0, sz in zip(groups, starts, sizes):
        xs = _dispatch(tables, int(t0) // tb, ldest_t[:, t0:t0 + sz], xn, xs)
    ys = _experts(tile_expert, n_tiles, xs, wp)
    return [_combine(tables, int(t0) // tb, ldest[t0:t0 + sz], meta, h2, ys)
            for (h2, _, meta), t0, sz in zip(groups, starts, sizes)]


def _prep_weights(norm1_g, w_in, a_q_norm_g, a_k_norm_g, gla_w_alpha, gla_b_alpha, gla_out_norm_g,
                  w_out, norm2_g, mem_norm_g, mem_w_q, mem_w_k, mem_w_v, mem_q_norm_g,
                  mem_k_norm_g, mem_w_o, norm3_g, router_w, router_b, exp_w1, exp_b1, exp_w2,
                  exp_b2):
    main = 3 * A_WIDTH + 2 * B_QK_WIDTH + 2 * B_V_WIDTH
    w_lr = jnp.pad(w_in[:, main:], ((0, 0), (0, LANES - GATE_RANK)))
    rw = jnp.pad(router_w, ((0, 0), (0, LANES - N_EXPERTS)))
    rwh = rw.astype(BF16)
    return {
        "g1": norm1_g[None],
        "w_in": jnp.concatenate([w_in[:, :main], w_lr], axis=1).astype(BF16),
        "gq": jnp.tile(a_q_norm_g, A_HEADS)[None],
        "gk": jnp.tile(a_k_norm_g, A_HEADS)[None],
        "bd64": _block_diag(A_WIDTH, A_HEAD_DIM, 1.0 / A_HEAD_DIM, BF16),
        "wa": jnp.pad(gla_w_alpha, ((0, LANES - GATE_RANK), (0, 0))).astype(BF16),
        "ba": gla_b_alpha[None],
        "bones64": _block_diag(B_QK_WIDTH, B_KEY_DIM, 1.0, BF16),
        "bd128": _block_diag(B_V_WIDTH, B_VAL_DIM, 1.0 / B_VAL_DIM, BF16),
        "gout": jnp.tile(gla_out_norm_g, B_HEADS)[None],
        "woa": w_out[:A_WIDTH].astype(BF16),
        "wob": w_out[A_WIDTH:].astype(BF16),
        "g2": norm2_g[None],
        "gmem": mem_norm_g[None],
        "wmq": mem_w_q.astype(BF16),
        "wmk": mem_w_k.astype(BF16),
        "wmv": mem_w_v.astype(BF16),
        "gmq": jnp.tile(mem_q_norm_g, MEM_HEADS)[None],
        "gmk": jnp.tile(mem_k_norm_g, MEM_HEADS)[None],
        "wmo": mem_w_o.astype(BF16),
        "g3": norm3_g[None],
        "rw": jnp.concatenate([rwh, (rw - rwh.astype(F32)).astype(BF16)], axis=1),
        "rb": jnp.pad(router_b, (0, LANES - N_EXPERTS), constant_values=-1e30)[None],
        "w1": exp_w1,
        "b1g": exp_b1[:, None, 0::2],
        "b1l": exp_b1[:, None, 1::2],
        "w2": exp_w2,
        "b2": exp_b2[:, None, :],
    }


def _layer(xp, xs, mem_prompt, cache_k, cache_v, state_gla, cache_mk, cache_mv, wp):
    bsz, seq, _ = xp.shape
    db, t_new, _ = xs.shape
    w_p = min(max(w for w, _ in DILATED_GROUPS), seq)

    xpf = xp.reshape(bsz * seq, D_MODEL)
    q, k, v, k_last, v_last, bq, bk, bv, gate, la = _in_proj(xpf, wp, BF16, seq, w_p)
    oa = _swa_prompt(q, k, v, bsz, seq)
    s0 = jnp.zeros((bsz, B_QK_WIDTH, B_VAL_DIM), F32)
    ob, s_p = _gla(bq, bk, la, bv, gate, s0, wp, seq, GLA_CHUNK, GLA_SUB, GLA_STEP)
    mk, mv = _mem_kv(mem_prompt.reshape(bsz * N_MEM, D_MODEL), wp)
    cnt0 = jnp.zeros((1, LANES), F32)
    h2_p, xn_p, meta_p, cnt = _post(xpf, oa, ob, mk.reshape(bsz, N_MEM, MEM_WIDTH),
                                    mv.reshape(bsz, N_MEM, MEM_WIDTH), cnt0, wp,
                                    POST_TILE, 1, seq // POST_TILE)

    xsf = xs.reshape(db * t_new, D_MODEL)
    q, _, _, ks, vs, bq, bk, bv, gate, la = _in_proj(xsf, wp, F32, t_new, t_new)
    new3 = lambda a: a.reshape(db, t_new, A_WIDTH)
    oa_s, nk, nv = _swa_sample(new3(q.transpose(1, 0, 2)), new3(ks), new3(vs), cache_k, cache_v)
    ob_s, s_s = _gla(bq, bk, la, bv, gate, state_gla, wp, t_new, t_new, t_new, t_new)
    h2_s, xn_s, meta_s, cnt = _post(xsf, oa_s.reshape(db * t_new, A_WIDTH), ob_s, cache_mk,
                                    cache_mv, cnt, wp, SAMPLE_SEQS * t_new, SAMPLE_SEQS, 1)

    y_p, y_s = _moe([(h2_p, xn_p, meta_p), (h2_s, xn_s, meta_s)], wp)
    return (y_p.reshape(bsz, seq, D_MODEL), y_s.reshape(db, t_new, D_MODEL),
            k_last.reshape(bsz, w_p, A_WIDTH), v_last.reshape(bsz, w_p, A_WIDTH),
            s_p, mk, mv, nk, nv, s_s)


def kernel(x_prompt, x_sample, mem_prompt, cache_swa_k, cache_swa_v, state_gla, cache_mem_k, cache_mem_v, norm1_g, w_in, a_q_norm_g, a_k_norm_g, gla_w_alpha, gla_b_alpha, gla_out_norm_g, w_out, norm2_g, mem_norm_g, mem_w_q, mem_w_k, mem_w_v, mem_q_norm_g, mem_k_norm_g, mem_w_o, norm3_g, router_w, router_b, exp_w1, exp_b1, exp_w2, exp_b2):
    depth = w_in.shape[0]
    bsz = x_prompt.shape[0]
    db, w_buf = cache_swa_k.shape[1], cache_swa_k.shape[2]
    xp, xs = x_prompt, x_sample
    per_layer = []
    for l in range(depth):
        wp = _prep_weights(
            norm1_g[l], w_in[l], a_q_norm_g[l], a_k_norm_g[l], gla_w_alpha[l], gla_b_alpha[l],
            gla_out_norm_g[l], w_out[l], norm2_g[l], mem_norm_g[l], mem_w_q[l], mem_w_k[l],
            mem_w_v[l], mem_q_norm_g[l], mem_k_norm_g[l], mem_w_o[l], norm3_g[l], router_w[l],
            router_b[l], exp_w1[l], exp_b1[l], exp_w2[l], exp_b2[l])
        xp, xs, kp, vp, s_p, mk, mv, nk, nv, s_s = _layer(
            xp, xs, mem_prompt,
            cache_swa_k[l].reshape(db, w_buf, A_WIDTH).transpose(0, 2, 1),
            cache_swa_v[l].reshape(db, w_buf, A_WIDTH).transpose(0, 2, 1),
            state_gla[l].reshape(db, B_QK_WIDTH, B_VAL_DIM),
            cache_mem_k[l].reshape(db, N_MEM, MEM_WIDTH), cache_mem_v[l].reshape(db, N_MEM, MEM_WIDTH),
            wp)
        w_p = kp.shape[1]
        per_layer.append((
            kp.reshape(bsz, w_p, A_HEADS, A_HEAD_DIM), vp.reshape(bsz, w_p, A_HEADS, A_HEAD_DIM),
            s_p.reshape(bsz, B_HEADS, B_KEY_DIM, B_VAL_DIM),
            mk.reshape(bsz, N_MEM, MEM_HEADS, MEM_HEAD_DIM), mv.reshape(bsz, N_MEM, MEM_HEADS, MEM_HEAD_DIM),
            nk.transpose(0, 2, 1).reshape(db, w_buf, A_HEADS, A_HEAD_DIM),
            nv.transpose(0, 2, 1).reshape(db, w_buf, A_HEADS, A_HEAD_DIM),
            s_s.reshape(db, B_HEADS, B_KEY_DIM, B_VAL_DIM)))
    stacked = [jnp.stack(t) for t in zip(*per_layer)]
    return (xp, xs, *stacked)
```

```python
import functools

import jax
import jax.numpy as jnp
import numpy as np
from jax import lax
from jax.experimental import pallas as pl
from jax.experimental.pallas import tpu as pltpu

F32 = jnp.float32
BF16 = jnp.bfloat16
I32 = jnp.int32

EPS = 1e-6
D_MODEL = 1024
A_HEADS, A_HEAD_DIM, A_WIDTH = 8, 64, 512
A_SCALE = A_HEAD_DIM ** -0.5
LOG2E = 1.4426950408889634
DILATED_GROUPS = ((128, 1), (512, 4), (2048, 16))
SPAN = 128
B_HEADS, B_KEY_DIM, B_VAL_DIM = 4, 64, 128
B_QK_WIDTH, B_V_WIDTH = 256, 512
GATE_RANK = 16
GATE_TEMP = 16.0
N_MEM, MEM_HEADS, MEM_HEAD_DIM, MEM_WIDTH = 256, 4, 128, 512
N_EXPERTS, TOP_K, D_FF = 32, 4, 1024
SWIGLU_ALPHA, SWIGLU_LIMIT = 1.702, 7.0

LANES = 128
VMEM_LIMIT = 56 * 1024 * 1024

PROJ_TILE = 512
SWA_BLOCK = 2048
SWA_UNROLL = 4
GLA_CHUNK, GLA_SUB, GLA_STEP = 64, 8, 512
POST_TILE = 512
SAMPLE_SEQS = 8
EXPERT_TILE = 512
DISPATCH_BLOCK = 512
GROUP_ROWS = 8
LOCAL_ROWS = DISPATCH_BLOCK * TOP_K + N_EXPERTS * GROUP_ROWS


def _cparams(sem, vmem=VMEM_LIMIT, **kw):
    return pltpu.CompilerParams(dimension_semantics=sem, vmem_limit_bytes=vmem, **kw)


def _full(shape):
    n = len(shape)
    return pl.BlockSpec(shape, lambda *_: (0,) * n)


def _rms(x, g):
    ms = jnp.mean(x * x, axis=-1, keepdims=True)
    return x * lax.rsqrt(ms + EPS) * g


def _dot(a, b):
    return jnp.dot(a, b, preferred_element_type=F32)


def _dot_nt(a, b):
    return lax.dot_general(a, b, (((1,), (1,)), ((), ())), preferred_element_type=F32)


def _block_diag(n, blk, val, dtype):
    i = np.arange(n)
    return jnp.asarray(np.where((i[:, None] // blk) == (i[None, :] // blk), val, 0.0), dtype)


def _in_proj_body(x_ref, g1_ref, w_ref, gq_ref, gk_ref, bd_ref, wa_ref, ba_ref,
                  q_ref, k_ref, v_ref, kc_ref, vc_ref, bq_ref, bk_ref, bv_ref, gate_ref, la_ref):
    xn = _rms(x_ref[...], g1_ref[...]).astype(BF16)
    bd = bd_ref[...]

    def proj(lo, hi):
        return _dot(xn, w_ref[:, lo:hi])

    def headnorm(z, g):
        ms = _dot((z * z).astype(BF16), bd)
        return z * lax.rsqrt(ms + EPS) * g

    def put_groups(ref, z):
        for p in range(A_WIDTH // LANES):
            ref[p] = z[:, p * LANES:(p + 1) * LANES]

    put_groups(q_ref, headnorm(proj(0, 512), gq_ref[...]) * (A_SCALE * LOG2E))
    k = headnorm(proj(512, 1024), gk_ref[...])
    v = proj(1024, 1536)
    put_groups(k_ref, k)
    put_groups(v_ref, v)
    kc_ref[...] = k
    vc_ref[...] = v
    bq_ref[...] = proj(1536, 1792) * (B_KEY_DIM ** -0.5)
    bk_ref[...] = proj(1792, 2048)
    bv_ref[...] = proj(2048, 2560).astype(bv_ref.dtype)
    br = proj(2560, 3072)
    gate_ref[...] = (br * jax.nn.sigmoid(br)).astype(gate_ref.dtype)
    lr = proj(3072, 3200).astype(BF16)
    pre = _dot(lr, wa_ref[...]) + ba_ref[...]
    log_sig = jnp.minimum(pre, 0.0) - jnp.log1p(jnp.exp(-jnp.abs(pre)))
    la_ref[...] = log_sig * (1.0 / GATE_TEMP)


def _in_proj(x, wp, wide_dtype, seq, keep):
    n = x.shape[0]
    tm = min(PROJ_TILE, n)
    row = lambda w: pl.BlockSpec((tm, w), lambda i: (i, 0))
    ngrp = A_WIDTH // LANES
    grp = pl.BlockSpec((ngrp, tm, LANES), lambda i: (0, i, 0))
    if keep == seq:
        kept = row(A_WIDTH)
    else:
        tps, kt = seq // tm, keep // tm
        assert tps * tm == seq and kt * tm == keep
        kept = pl.BlockSpec((tm, A_WIDTH),
                            lambda i: ((i // tps) * kt + jnp.maximum(i % tps - (tps - kt), 0), 0))
    outs = [(256, F32), (256, F32), (512, wide_dtype), (512, wide_dtype), (256, F32)]
    return pl.pallas_call(
        _in_proj_body,
        grid=(n // tm,),
        in_specs=[row(D_MODEL), _full((1, D_MODEL)), _full((D_MODEL, 3200)), _full((1, 512)),
                  _full((1, 512)), _full((512, 512)), _full((LANES, 256)), _full((1, 256))],
        out_specs=[grp] * 3 + [kept] * 2 + [row(w) for w, _ in outs],
        out_shape=[jax.ShapeDtypeStruct((ngrp, n, LANES), F32)] * 3
        + [jax.ShapeDtypeStruct((n // seq * keep, A_WIDTH), F32)] * 2
        + [jax.ShapeDtypeStruct((n, w), dt) for w, dt in outs],
        compiler_params=_cparams(("arbitrary",)),
        name="in_proj",
    )(x, wp["g1"], wp["w_in"], wp["gq"], wp["gk"], wp["bd64"], wp["wa"], wp["ba"])


def _unroll_for(trips):
    return max(u for u in range(1, SWA_UNROLL + 1) if trips % u == 0)


def _ds(start, size, stride):
    return pl.ds(start, size) if stride == 1 else pl.ds(start, size, stride=stride)


def _swa_prompt_body(q_ref, kp_ref, kc_ref, vp_ref, vc_ref, o_ref, m_s, l_s, acc_s):
    i = pl.program_id(2)
    qb = SWA_BLOCK
    lane = lax.broadcasted_iota(I32, (SPAN, LANES), 1)
    lo_mask = lane < A_HEAD_DIM
    jq = lax.broadcasted_iota(I32, (SPAN, 2 * SPAN), 0)
    jk = lax.broadcasted_iota(I32, (SPAN, 2 * SPAN), 1)
    dist = jq + SPAN - jk
    band = (dist >= 0) & (dist <= SPAN)
    cur_half = jk >= SPAN

    def attend(qp, kp, vp, valid, rows, first_group):
        vp1 = jnp.concatenate([vp, jnp.ones_like(vp)], axis=1)
        res = []
        for hh in range(2):
            msk = lo_mask if hh == 0 else jnp.logical_not(lo_mask)
            qm = jnp.where(msk, qp, 0.0).astype(BF16)
            s = jnp.where(valid, _dot_nt(qm, kp), -jnp.inf)
            m = jnp.max(s, axis=1, keepdims=True)
            res.append((m, _dot(jnp.exp2(s - m).astype(BF16), vp1)))
        m_new = jnp.where(lo_mask, res[0][0], res[1][0])
        l_new = jnp.where(lo_mask, res[0][1][:, LANES:], res[1][1][:, LANES:])
        o_new = jnp.where(lo_mask, res[0][1][:, :LANES], res[1][1][:, :LANES])
        if first_group:
            m_s[rows, :] = m_new
            l_s[rows, :] = l_new
            acc_s[rows, :] = o_new
        else:
            m_old = m_s[rows, :]
            m = jnp.maximum(m_old, m_new)
            a_old = jnp.exp2(m_old - m)
            a_new = jnp.exp2(m_new - m)
            m_s[rows, :] = m
            l_s[rows, :] = l_s[rows, :] * a_old + l_new * a_new
            acc_s[rows, :] = acc_s[rows, :] * a_old + o_new * a_new

    for gi, (_, dil) in enumerate(DILATED_GROUPS):
        unit = dil * SPAN
        nblk = qb // unit
        first = gi == 0

        def head_block(r, carry, dil=dil, unit=unit, first=first):
            rows = _ds(r, SPAN, dil)
            prev = _ds(qb - unit + r, SPAN, dil)
            ks = jnp.concatenate([kp_ref[prev, :], kc_ref[rows, :]], axis=0).astype(BF16)
            vs = jnp.concatenate([vp_ref[prev, :], vc_ref[rows, :]], axis=0).astype(BF16)
            valid = band & (cur_half | (i > 0))
            attend(q_ref[rows, :], ks, vs, valid, rows, first)
            return carry

        lax.fori_loop(0, dil, head_block, 0, unroll=_unroll_for(dil))

        if nblk > 1:
            def tail_block(idx, carry, dil=dil, unit=unit, first=first):
                n = idx // dil + 1
                r = idx % dil
                start = unit * n + r
                rows = _ds(start, SPAN, dil)
                keys = _ds(start - unit, 2 * SPAN, dil)
                attend(q_ref[rows, :], kc_ref[keys, :].astype(BF16), vc_ref[keys, :].astype(BF16),
                       band, rows, first)
                return carry

            lax.fori_loop(0, (nblk - 1) * dil, tail_block, 0, unroll=_unroll_for((nblk - 1) * dil))

    o_ref[...] = (acc_s[...] / l_s[...]).astype(o_ref.dtype)


def _swa_prompt(q, k, v, bsz, seq):
    qb = SWA_BLOCK
    nb = seq // qb
    cur = pl.BlockSpec((None, qb, LANES), lambda p, b, i: (p, b * nb + i, 0))
    prev = pl.BlockSpec((None, qb, LANES), lambda p, b, i: (p, b * nb + jnp.maximum(i - 1, 0), 0))
    return pl.pallas_call(
        _swa_prompt_body,
        grid=(A_WIDTH // LANES, bsz, nb),
        in_specs=[cur, prev, cur, prev, cur],
        out_specs=pl.BlockSpec((qb, LANES), lambda p, b, i: (b * nb + i, p)),
        out_shape=jax.ShapeDtypeStruct((bsz * seq, A_WIDTH), BF16),
        scratch_shapes=[pltpu.VMEM((qb, LANES), F32)] * 3,
        compiler_params=_cparams(("parallel", "parallel", "arbitrary")),
        name="swa_prompt",
    )(q, k, k, v, v)


def _swa_sample_body(q_ref, kn_ref, vn_ref, ck_ref, cv_ref, c1_ref, c2_ref,
                     o_ref, nk_ref, nv_ref):
    t_new = q_ref.shape[1]
    w_buf = ck_ref.shape[2]
    ck, cv = ck_ref[0], cv_ref[0]
    kn, vn = kn_ref[0], vn_ref[0]
    tail_lane = lax.broadcasted_iota(I32, (A_WIDTH, LANES), 1) >= LANES - t_new

    def shift_in(old, new, out_ref):
        moved = pltpu.roll(old, w_buf - t_new, 1)
        new_t = jnp.transpose(jnp.concatenate([jnp.zeros((LANES - t_new, A_WIDTH), F32), new], axis=0))
        out_ref[0, :, :w_buf - LANES] = moved[:, :w_buf - LANES]
        out_ref[0, :, w_buf - LANES:] = jnp.where(tail_lane, new_t, moved[:, w_buf - LANES:])

    shift_in(ck, kn, nk_ref)
    shift_in(cv, vn, nv_ref)

    q = q_ref[0]
    lane = lax.broadcasted_iota(I32, (t_new, LANES), 1)
    lo_mask = lane < A_HEAD_DIM
    c1, c2 = c1_ref[...], c2_ref[...]
    outs = []
    for p in range(A_WIDTH // LANES):
        sl = slice(p * LANES, (p + 1) * LANES)
        qp = q[:, sl]
        qblk = jnp.concatenate([jnp.where(lo_mask, qp, 0.0), jnp.where(lo_mask, 0.0, qp)],
                               axis=0).astype(BF16)
        s1 = jnp.where(c1 > 0, _dot(qblk, ck[sl, :].astype(BF16)), -jnp.inf)
        s2 = jnp.where(c2 > 0, _dot_nt(qblk, kn[:, sl].astype(BF16)), -jnp.inf)
        m = jnp.maximum(jnp.max(s1, axis=1, keepdims=True), jnp.max(s2, axis=1, keepdims=True))
        p1 = c1 * jnp.exp2(s1 - m)
        p2 = c2 * jnp.exp2(s2 - m)
        l = jnp.sum(p1, axis=1, keepdims=True) + jnp.sum(p2, axis=1, keepdims=True)
        o = (_dot_nt(p1.astype(BF16), cv[sl, :].astype(BF16))
             + _dot(p2.astype(BF16), vn[:, sl].astype(BF16))) / l
        outs.append(jnp.where(lo_mask, o[:t_new], o[t_new:]))
    o_ref[0] = jnp.concatenate(outs, axis=1).astype(o_ref.dtype)


def _sample_multiplicity(t_new, w_buf):
    t = np.arange(t_new)[:, None]
    e = np.arange(w_buf + t_new)[None, :]
    d = w_buf + t - e
    c = np.zeros(d.shape, np.float32)
    for window, dil in DILATED_GROUPS:
        c += ((d >= 0) & (d % dil == 0) & (d <= window)).astype(np.float32)
    c = np.concatenate([c, c], axis=0)
    return jnp.asarray(c[:, :w_buf]), jnp.asarray(c[:, w_buf:])


def _swa_sample(q, kn, vn, cache_k, cache_v):
    db, t_new, w = q.shape
    w_buf = cache_k.shape[2]
    assert w_buf >= max(win for win, _ in DILATED_GROUPS) and t_new % 8 == 0 and t_new <= LANES
    c1, c2 = _sample_multiplicity(t_new, w_buf)
    new = pl.BlockSpec((1, t_new, w), lambda b: (b, 0, 0))
    cache = pl.BlockSpec((1, w, w_buf), lambda b: (b, 0, 0))
    return pl.pallas_call(
        _swa_sample_body,
        grid=(db,),
        in_specs=[new, new, new, cache, cache, _full(c1.shape), _full(c2.shape)],
        out_specs=[new, cache, cache],
        out_shape=[jax.ShapeDtypeStruct((db, t_new, w), BF16),
                   jax.ShapeDtypeStruct(cache_k.shape, cache_k.dtype),
                   jax.ShapeDtypeStruct(cache_v.shape, cache_v.dtype)],
        compiler_params=_cparams(("parallel",)),
        name="swa_sample",
    )(q, kn, vn, cache_k, cache_v, c1, c2)


def _gla_body(chunk, sub, nch, q_ref, k_ref, g_ref, v_ref, gate_ref, s0_ref, tril_ref, dmask_ref,
              bones_ref, sbm_ref, bd_ref, gout_ref, o_ref, sfin_ref, sbd):
    j = pl.program_id(1)
    sbm = sbm_ref[...]
    nsub = chunk // sub
    pad = B_KEY_DIM - chunk

    @pl.when(j == 0)
    def _():
        s0 = s0_ref[0]
        sbd[...] = jnp.concatenate([s0] * B_HEADS, axis=1) * sbm

    row = lax.broadcasted_iota(I32, (chunk, 1), 0)
    sub_id = row // sub
    lane_w = lax.broadcasted_iota(I32, (chunk, LANES * max(nsub - 1, 1)), 1)
    lo_w = (lane_w % LANES) < B_KEY_DIM

    def one_chunk(c, carry):
        off = pl.multiple_of(c * chunk, chunk)
        rows = pl.ds(off, chunk)
        q, k, g = q_ref[rows, :], k_ref[rows, :], g_ref[rows, :]
        v = v_ref[rows, :].astype(F32)
        g1 = g.astype(BF16)
        r1 = g - g1.astype(F32)
        g2 = r1.astype(BF16)
        g3 = (r1 - g2.astype(F32)).astype(BF16)
        tril = tril_ref[...]
        b = _dot(tril, g1) + _dot(tril, g2) + _dot(tril, g3)
        b_last = b[chunk - 1:chunk, :]
        state = sbd[...]

        o = _dot((q * jnp.exp(b)).astype(BF16), state.astype(BF16))

        bones = bones_ref[...]
        att = _dot((q * k).astype(BF16), bones) * dmask_ref[0]
        gate = jnp.exp(g)
        decay = gate
        for d in range(1, sub):
            if d > 1:
                decay = decay * pltpu.roll(gate, d - 1, 0)
            w = q * pltpu.roll(k, d, 0) * decay
            att = att + _dot(w.astype(BF16), bones) * dmask_ref[d]

        if nsub > 1:
            qx, kx = [], []
            for i in range(1, nsub):
                r_i = b[sub * i - 1:sub * i, :]
                qx.append(jnp.where(sub_id == i, q * jnp.exp(jnp.minimum(b - r_i, 0.0)), 0.0))
                kx.append(jnp.where(sub_id < i, k * jnp.exp(jnp.minimum(r_i - b, 0.0)), 0.0))
            parts = []
            for p in range(B_QK_WIDTH // LANES):
                sl = slice(p * LANES, (p + 1) * LANES)
                qp = jnp.concatenate([x[:, sl] for x in qx], axis=1)
                kp = jnp.concatenate([x[:, sl] for x in kx], axis=1).astype(BF16)
                zero = jnp.zeros_like(kp)
                lhs = jnp.concatenate([jnp.where(lo_w, qp, 0.0), jnp.where(lo_w, 0.0, qp)],
                                      axis=1).astype(BF16)
                rhs = jnp.concatenate([jnp.concatenate([kp, zero], axis=1),
                                       jnp.concatenate([zero, kp], axis=1)], axis=0)
                parts.append(_dot_nt(lhs, rhs))
            att = att + jnp.concatenate(parts, axis=1)

        if pad:
            vrow = jnp.concatenate([v, jnp.zeros((pad, B_V_WIDTH), F32)], axis=0)
        else:
            vrow = v
        vbd = (jnp.concatenate([vrow] * B_HEADS, axis=0) * sbm).astype(BF16)
        o = o + _dot(att.astype(BF16), vbd)

        ke = (k * jnp.exp(b_last - b)).astype(BF16)
        upd = lax.dot_general(ke, v.astype(BF16), (((0,), (0,)), ((), ())),
                              preferred_element_type=F32)
        dec = jnp.transpose(jnp.broadcast_to(jnp.exp(b_last), (8, B_QK_WIDTH)))[:, 0:1]
        sbd[...] = (state * dec + upd) * sbm

        ms = _dot((o * o).astype(BF16), bd_ref[...])
        on = o * lax.rsqrt(ms + EPS) * gout_ref[...] * gate_ref[rows, :].astype(F32)
        o_ref[rows, :] = on.astype(o_ref.dtype)
        return carry

    lax.fori_loop(0, nch, one_chunk, 0, unroll=2 if nch % 2 == 0 else 1)

    @pl.when(j == pl.num_programs(1) - 1)
    def _():
        s = sbd[...]
        sfin_ref[0] = jnp.concatenate(
            [s[h * B_KEY_DIM:(h + 1) * B_KEY_DIM, h * B_VAL_DIM:(h + 1) * B_VAL_DIM]
             for h in range(B_HEADS)], axis=0)


def _gla_consts(chunk, sub):
    t = np.arange(chunk)
    tril = (t[:, None] >= t[None, :]).astype(np.float32)
    lane = np.arange(B_QK_WIDTH)
    dmask = np.zeros((sub, chunk, B_QK_WIDTH), np.float32)
    for d in range(sub):
        ok = (t % sub) >= d
        dmask[d] = ((lane[None, :] % B_KEY_DIM) == (t[:, None] - d)) & ok[:, None]
    r = np.arange(B_QK_WIDTH)[:, None] // B_KEY_DIM
    c = np.arange(B_V_WIDTH)[None, :] // B_VAL_DIM
    sbm = (r == c).astype(np.float32)
    return jnp.asarray(tril, BF16), jnp.asarray(dmask), jnp.asarray(sbm)


def _gla(q, k, g, v, gate, s0, wp, length, chunk, sub, step):
    n = q.shape[0]
    bsz = n // length
    assert chunk == sub or chunk == B_KEY_DIM
    tril, dmask, sbm = _gla_consts(chunk, sub)
    nstep = length // step
    row = lambda w: pl.BlockSpec((step, w), lambda b, j: (b * nstep + j, 0))
    st = pl.BlockSpec((1, B_QK_WIDTH, B_VAL_DIM), lambda b, j: (b, 0, 0))
    return pl.pallas_call(
        functools.partial(_gla_body, chunk, sub, step // chunk),
        grid=(bsz, nstep),
        in_specs=[row(256), row(256), row(256), row(512), row(512), st, _full(tril.shape),
                  _full(dmask.shape), _full((256, 256)), _full(sbm.shape), _full((512, 512)),
                  _full((1, 512))],
        out_specs=[row(512), st],
        out_shape=[jax.ShapeDtypeStruct((n, B_V_WIDTH), BF16),
                   jax.ShapeDtypeStruct(s0.shape, F32)],
        scratch_shapes=[pltpu.VMEM((B_QK_WIDTH, B_V_WIDTH), F32)],
        compiler_params=_cparams(("parallel", "arbitrary")),
        name="gla",
    )(q, k, g, v, gate, s0, tril, dmask, wp["bones64"], sbm, wp["bd128"], wp["gout"])


def _head_rms(z, g, scale=1.0):
    parts = []
    for h in range(MEM_HEADS):
        zh = z[:, h * LANES:(h + 1) * LANES]
        parts.append(zh * lax.rsqrt(jnp.mean(zh * zh, axis=-1, keepdims=True) + EPS))
    return jnp.concatenate(parts, axis=1) * (g * scale)


def _mem_kv_body(m_ref, gn_ref, wk_ref, wv_ref, gk_ref, mk_ref, mv_ref):
    mn = _rms(m_ref[...], gn_ref[...]).astype(BF16)
    mk_ref[...] = _head_rms(_dot(mn, wk_ref[...]), gk_ref[...])
    mv_ref[...] = _dot(mn, wv_ref[...])


def _mem_kv(mem, wp):
    n = mem.shape[0]
    tm = 256
    row = lambda w: pl.BlockSpec((tm, w), lambda i: (i, 0))
    return pl.pallas_call(
        _mem_kv_body,
        grid=(n // tm,),
        in_specs=[row(D_MODEL), _full((1, D_MODEL)), _full((D_MODEL, 512)), _full((D_MODEL, 512)),
                  _full((1, 512))],
        out_specs=[row(512), row(512)],
        out_shape=[jax.ShapeDtypeStruct((n, 512), F32)] * 2,
        compiler_params=_cparams(("parallel",)),
        name="mem_kv",
    )(mem, wp["gmem"], wp["wmk"], wp["wmv"], wp["gmk"])


def _post_body(nseq, x_ref, oa_ref, ob_ref, woa_ref, wob_ref, g2_ref, wq_ref, gmq_ref, mk_ref,
               mv_ref, wo_ref, g3_ref, rw_ref, rb_ref, cnt0_ref, tri_ref,
               h2_ref, xn_ref, meta_ref, cnt_ref, carry):
    tm = x_ref.shape[0]

    @pl.when(pl.program_id(0) == 0)
    def _():
        carry[...] = cnt0_ref[...]

    h = x_ref[...] + _dot(oa_ref[...], woa_ref[...]) + _dot(ob_ref[...], wob_ref[...])
    hn = _rms(h, g2_ref[...]).astype(BF16)
    qm = _head_rms(_dot(hn, wq_ref[...]), gmq_ref[...], MEM_HEAD_DIM ** -0.5).astype(BF16)
    nk = nseq * N_MEM
    mk = mk_ref[...].reshape(nk, MEM_WIDTH)
    mv = mv_ref[...].reshape(nk, MEM_WIDTH)
    if nseq > 1:
        rt = lax.broadcasted_iota(I32, (tm, nk), 0) // (tm // nseq)
        ct = lax.broadcasted_iota(I32, (tm, nk), 1) // N_MEM
        same = rt == ct
    outs = []
    for hd in range(MEM_HEADS):
        sl = slice(hd * LANES, (hd + 1) * LANES)
        s = _dot_nt(qm[:, sl], mk[:, sl].astype(BF16))
        if nseq > 1:
            s = jnp.where(same, s, -jnp.inf)
        m = jnp.max(s, axis=1, keepdims=True)
        pr = jnp.exp(s - m)
        l = jnp.sum(pr, axis=1, keepdims=True)
        outs.append(_dot(pr.astype(BF16), mv[:, sl].astype(BF16)) / l)
    h2 = h + _dot(jnp.concatenate(outs, axis=1).astype(BF16), wo_ref[...])
    h2_ref[...] = h2
    xn = _rms(h2, g3_ref[...])
    xn_ref[...] = xn

    x1 = xn.astype(BF16)
    x2 = (xn - x1.astype(F32)).astype(BF16)
    prod = _dot(jnp.concatenate([x1, x2], axis=0), rw_ref[...])
    logits = (prod[:tm, :LANES] + prod[:tm, LANES:] + prod[tm:, :LANES] + prod[tm:, LANES:]
              + rb_ref[...])
    lane = lax.broadcasted_iota(I32, (tm, LANES), 1)
    vals, idxs, hots = [], [], []
    work = logits
    for _ in range(TOP_K):
        m = jnp.max(work, axis=1, keepdims=True)
        idx = jnp.min(jnp.where(work == m, lane, LANES), axis=1, keepdims=True)
        hot = lane == idx
        vals.append(m)
        idxs.append(idx)
        hots.append(hot)
        work = jnp.where(hot, -jnp.inf, work)
    exps = [jnp.exp(v - vals[0]) for v in vals]
    den = exps[0] + exps[1] + exps[2] + exps[3]

    sel = (hots[0] | hots[1] | hots[2] | hots[3]).astype(F32)
    before = _dot(tri_ref[...], sel.astype(BF16)) + carry[...]
    carry[...] = carry[...] + jnp.sum(sel, axis=0, keepdims=True)
    cnt_ref[...] = carry[...]

    meta = jnp.zeros((tm, LANES), F32)
    for kk in range(TOP_K):
        rank = jnp.sum(jnp.where(hots[kk], before, 0.0), axis=1, keepdims=True)
        meta = jnp.where(lane == kk, idxs[kk].astype(F32), meta)
        meta = jnp.where(lane == TOP_K + kk, rank, meta)
        meta = jnp.where(lane == 2 * TOP_K + kk, exps[kk] / den, meta)
    meta_ref[...] = meta


def _post(x, oa, ob, mk, mv, cnt0, wp, tm, nseq, tiles_per_mem):
    n = x.shape[0]
    row = lambda w: pl.BlockSpec((tm, w), lambda i: (i, 0))
    mem = pl.BlockSpec((nseq, N_MEM, MEM_WIDTH), lambda i: (i // tiles_per_mem, 0, 0))
    tri = jnp.asarray(np.tril(np.ones((tm, tm), np.float32), -1), BF16)
    return pl.pallas_call(
        functools.partial(_post_body, nseq),
        grid=(n // tm,),
        in_specs=[row(D_MODEL), row(512), row(512), _full((512, D_MODEL)), _full((512, D_MODEL)),
                  _full((1, D_MODEL)), _full((D_MODEL, 512)), _full((1, 512)), mem, mem,
                  _full((512, D_MODEL)), _full((1, D_MODEL)), _full((D_MODEL, 2 * LANES)),
                  _full((1, LANES)), _full((1, LANES)), _full((tm, tm))],
        out_specs=[row(D_MODEL), row(D_MODEL), row(LANES), _full((1, LANES))],
        out_shape=[jax.ShapeDtypeStruct((n, D_MODEL), F32), jax.ShapeDtypeStruct((n, D_MODEL), F32),
                   jax.ShapeDtypeStruct((n, LANES), F32), jax.ShapeDtypeStruct((1, LANES), F32)],
        scratch_shapes=[pltpu.VMEM((1, LANES), F32)],
        compiler_params=_cparams(("arbitrary",)),
        name="post",
    )(x, oa, ob, wp["woa"], wp["wob"], wp["g2"], wp["wmq"], wp["gmq"], mk, mv, wp["wmo"],
      wp["g3"], wp["rw"], wp["rb"], cnt0, tri)


def _start_rows(src, src_row, dst, dst_row, n_rows, sem):
    s = pl.multiple_of(jnp.asarray(src_row, I32), GROUP_ROWS)
    d = pl.multiple_of(jnp.asarray(dst_row, I32), GROUP_ROWS)
    n_rows = pl.multiple_of(jnp.asarray(n_rows, I32), GROUP_ROWS)

    @pl.when(n_rows > 0)
    def _():
        pltpu.make_async_copy(src.at[pl.ds(s, n_rows)], dst.at[pl.ds(d, n_rows)], sem).start()

    return n_rows


def _wait_rows(like_src, like_dst, n_rows, sem):
    n_rows = pl.multiple_of(n_rows, GROUP_ROWS)

    @pl.when(n_rows > 0)
    def _():
        pltpu.make_async_copy(like_src.at[pl.ds(0, n_rows)], like_dst.at[pl.ds(0, n_rows)], sem).wait()


def _pad_rows_body(pstart_ref, prows_ref, nt_ref, xs_ref, zeros, sem):
    zeros[...] = jnp.zeros_like(zeros)
    n_tiles_max = xs_ref.shape[0] // EXPERT_TILE

    def expert_tail(e, total):
        return total + _start_rows(zeros, 0, xs_ref, pstart_ref[e], prows_ref[e], sem)

    def unused_tile(t, total):
        return total + _start_rows(zeros, 0, xs_ref, t * EXPERT_TILE, EXPERT_TILE, sem)

    total = lax.fori_loop(0, N_EXPERTS, expert_tail, jnp.int32(0))
    total = lax.fori_loop(nt_ref[0], n_tiles_max, unused_tile, total)
    _wait_rows(xs_ref, xs_ref, total, sem)


def _pad_rows(pad_start, pad_units, n_tiles, rows):
    return pl.pallas_call(
        _pad_rows_body,
        grid_spec=pltpu.PrefetchScalarGridSpec(
            num_scalar_prefetch=3, grid=(1,),
            in_specs=[],
            out_specs=pl.BlockSpec(memory_space=pltpu.HBM),
            scratch_shapes=[pltpu.VMEM((EXPERT_TILE, D_MODEL), F32), pltpu.SemaphoreType.DMA]),
        out_shape=jax.ShapeDtypeStruct((rows, D_MODEL), F32),
        compiler_params=_cparams(("arbitrary",)),
        name="moe_pad_rows",
    )(pad_start, pad_units, n_tiles)


def _dispatch_body(blk0, loff_ref, gstart_ref, nrows_ref, ldest_ref, x_ref, xs_in_ref, xs_ref,
                   xloc, sem):
    del xs_in_ref
    b = pl.program_id(0) + blk0
    tb = x_ref.shape[0]
    rows = lax.broadcasted_iota(I32, (LOCAL_ROWS, tb), 0)
    ld = ldest_ref[...]
    hot = rows == ld[0:1, :]
    for kk in range(1, TOP_K):
        hot = hot | (rows == ld[kk:kk + 1, :])
    xloc[...] = _dot(jnp.where(hot, 1.0, 0.0).astype(BF16), x_ref[...].astype(BF16))

    def group(e, total):
        j = b * N_EXPERTS + e
        return total + _start_rows(xloc, loff_ref[j], xs_ref, gstart_ref[j], nrows_ref[j], sem)

    total = lax.fori_loop(0, N_EXPERTS, group, jnp.int32(0))
    _wait_rows(xloc, xs_ref, total, sem)


def _dispatch(tables, blk0, ldest_t, xn, xs):
    n = xn.shape[0]
    tb = DISPATCH_BLOCK
    idx = lambda i, *_: (i, 0)
    return pl.pallas_call(
        functools.partial(_dispatch_body, blk0),
        grid_spec=pltpu.PrefetchScalarGridSpec(
            num_scalar_prefetch=3, grid=(n // tb,),
            in_specs=[pl.BlockSpec((TOP_K, tb), lambda i, *_: (0, i)),
                      pl.BlockSpec((tb, D_MODEL), idx),
                      pl.BlockSpec(memory_space=pltpu.HBM)],
            out_specs=pl.BlockSpec(memory_space=pltpu.HBM),
            scratch_shapes=[pltpu.VMEM((LOCAL_ROWS, D_MODEL), F32), pltpu.SemaphoreType.DMA]),
        out_shape=jax.ShapeDtypeStruct(xs.shape, xs.dtype),
        input_output_aliases={5: 0},
        compiler_params=_cparams(("arbitrary",)),
        name="moe_dispatch",
    )(*tables, ldest_t, xn, xs)


def _expert_body(te_ref, first_ref, nt_ref, x_ref, w1_ref, sel_ref, b1g_ref, b1l_ref, w2_ref, b2_ref,
                 y_ref, w1g, w1l, w2):
    i = pl.program_id(0)
    live = i < nt_ref[0]

    @pl.when(live & (first_ref[i] == 1))
    def _():
        sel = sel_ref[...]
        for j in range(D_FF // LANES):
            z = _dot(w1_ref[0, :, 2 * LANES * j:2 * LANES * (j + 1)].astype(BF16), sel)
            w1g[:, LANES * j:LANES * (j + 1)] = z[:, :LANES].astype(BF16)
            w1l[:, LANES * j:LANES * (j + 1)] = z[:, LANES:].astype(BF16)
        w2[...] = w2_ref[0].astype(BF16)

    @pl.when(live)
    def _():
        x = x_ref[...].astype(BF16)
        glu = jnp.minimum(_dot(x, w1g[...]) + b1g_ref[0], SWIGLU_LIMIT)
        lin = jnp.clip(_dot(x, w1l[...]) + b1l_ref[0], -SWIGLU_LIMIT, SWIGLU_LIMIT)
        act = glu * jax.nn.sigmoid(SWIGLU_ALPHA * glu) * (lin + 1.0)
        y_ref[...] = _dot(act.astype(BF16), w2[...]) + b2_ref[0]

    @pl.when(jnp.logical_not(live))
    def _():
        y_ref[...] = jnp.zeros_like(y_ref)


def _experts(tile_expert, n_tiles, xs, wp):
    rows = xs.shape[0]
    first = jnp.concatenate([jnp.ones((1,), I32),
                             (tile_expert[1:] != tile_expert[:-1]).astype(I32)])
    c = np.arange(2 * LANES)
    sel = np.zeros((2 * LANES, 2 * LANES), np.float32)
    sel[c, (c % 2) * LANES + c // 2] = 1.0
    tile = lambda i, te, fi, nt: (jnp.minimum(i, nt[0] - 1), 0)
    out_tile = lambda i, te, fi, nt: (i, 0)
    wsel = lambda i, te, fi, nt: (te[jnp.minimum(i, nt[0] - 1)], 0, 0)
    wspec = lambda r, c: pl.BlockSpec((1, r, c), wsel)
    return pl.pallas_call(
        _expert_body,
        grid_spec=pltpu.PrefetchScalarGridSpec(
            num_scalar_prefetch=3, grid=(rows // EXPERT_TILE,),
            in_specs=[pl.BlockSpec((EXPERT_TILE, D_MODEL), tile),
                      wspec(D_MODEL, 2 * D_FF), pl.BlockSpec((2 * LANES, 2 * LANES), lambda *_: (0, 0)),
                      wspec(1, D_FF), wspec(1, D_FF), wspec(D_FF, D_MODEL), wspec(1, D_MODEL)],
            out_specs=pl.BlockSpec((EXPERT_TILE, D_MODEL), out_tile),
            scratch_shapes=[pltpu.VMEM((D_MODEL, D_FF), BF16)] * 2 + [pltpu.VMEM((D_FF, D_MODEL), BF16)]),
        out_shape=jax.ShapeDtypeStruct((rows, D_MODEL), F32),
        compiler_params=_cparams(("arbitrary",)),
        name="moe_experts",
    )(tile_expert, first, n_tiles, xs, wp["w1"], jnp.asarray(sel, BF16), wp["b1g"], wp["b1l"],
      wp["w2"], wp["b2"])


def _combine_body(blk0, loff_ref, gstart_ref, nrows_ref, ldest_ref, meta_ref, h_ref, ys_ref, o_ref,
                  yloc, sem):
    b = pl.program_id(0) + blk0
    tb = h_ref.shape[0]

    @pl.when(pl.program_id(0) == 0)
    def _():
        yloc[...] = jnp.zeros_like(yloc)

    def group(e, total):
        j = b * N_EXPERTS + e
        return total + _start_rows(ys_ref, gstart_ref[j], yloc, loff_ref[j], nrows_ref[j], sem)

    total = lax.fori_loop(0, N_EXPERTS, group, jnp.int32(0))
    cols = lax.broadcasted_iota(I32, (tb, LOCAL_ROWS), 1)
    ld = ldest_ref[...]
    meta = meta_ref[...]
    gmat = jnp.zeros((tb, LOCAL_ROWS), F32)
    for kk in range(TOP_K):
        gate = meta[:, 2 * TOP_K + kk:2 * TOP_K + kk + 1]
        gmat = jnp.where(cols == ld[:, kk:kk + 1], gate, gmat)
    _wait_rows(ys_ref, yloc, total, sem)
    o_ref[...] = h_ref[...] + _dot(gmat.astype(BF16), yloc[...].astype(BF16))


def _combine(tables, blk0, ldest, meta, h2, ys):
    n = h2.shape[0]
    tb = DISPATCH_BLOCK
    row = lambda w: pl.BlockSpec((tb, w), lambda i, *_: (i, 0))
    return pl.pallas_call(
        functools.partial(_combine_body, blk0),
        grid_spec=pltpu.PrefetchScalarGridSpec(
            num_scalar_prefetch=3, grid=(n // tb,),
            in_specs=[row(TOP_K), row(LANES), row(D_MODEL), pl.BlockSpec(memory_space=pltpu.HBM)],
            out_specs=row(D_MODEL),
            scratch_shapes=[pltpu.VMEM((LOCAL_ROWS, D_MODEL), F32), pltpu.SemaphoreType.DMA]),
        out_shape=jax.ShapeDtypeStruct((n, D_MODEL), F32),
        compiler_params=_cparams(("arbitrary",)),
        name="moe_combine",
    )(*tables, ldest, meta, h2, ys)


def _moe(groups, wp):
    tb = DISPATCH_BLOCK
    sizes = [g[0].shape[0] for g in groups]
    assert all(s % tb == 0 for s in sizes)
    n_tok = sum(sizes)
    nb = n_tok // tb
    rows = n_tok * TOP_K + nb * N_EXPERTS * GROUP_ROWS + N_EXPERTS * EXPERT_TILE
    n_tiles_max = rows // EXPERT_TILE

    eidx = jnp.concatenate([g[2][:, 0:TOP_K] for g in groups]).astype(I32)
    rank = jnp.concatenate([g[2][:, TOP_K:2 * TOP_K] for g in groups]).astype(I32)
    hot = eidx[:, :, None] == jnp.arange(N_EXPERTS, dtype=I32)
    cnt = jnp.sum(hot.reshape(nb, tb * TOP_K, N_EXPERTS), axis=1, dtype=I32)
    npad = (cnt + GROUP_ROWS - 1) // GROUP_ROWS * GROUP_ROWS
    loff = jnp.cumsum(npad, axis=1) - npad
    gsize = jnp.sum(npad, axis=0)
    gpad = (gsize + EXPERT_TILE - 1) // EXPERT_TILE * EXPERT_TILE
    ends = jnp.cumsum(gpad)
    gstart = (ends - gpad)[None, :] + jnp.cumsum(npad, axis=0) - npad
    before = jnp.cumsum(cnt, axis=0) - cnt
    n_tiles = (ends[-1] // EXPERT_TILE).reshape(1)
    tile_ids = jnp.arange(n_tiles_max, dtype=I32)
    tile_expert = jnp.minimum(
        jnp.sum((ends // EXPERT_TILE)[None, :] <= tile_ids[:, None], axis=1), N_EXPERTS - 1).astype(I32)

    base = jnp.repeat(loff - before, tb, axis=0)
    ldest = rank + jnp.sum(jnp.where(hot, base[:, None, :], 0), axis=2)
    ldest_t = ldest.T
    tables = (loff.reshape(-1), gstart.reshape(-1).astype(I32), npad.reshape(-1))

    xs = _pad_rows(((ends - gpad) + gsize).astype(I32), (gpad - gsize).astype(I32),
                   n_tiles, rows)
    starts = np.cumsum([0] + sizes[:-1])
    for (_, xn, _), t0, sz in zip(groups, starts, sizes):
        xs = _dispatch(tables, int(t0) // tb, ldest_t[:, t0:t0 + sz], xn, xs)
    ys = _experts(tile_expert, n_tiles, xs, wp)
    return [_combine(tables, int(t0) // tb, ldest[t0:t0 + sz], meta, h2, ys)
            for (h2, _, meta), t0, sz in zip(groups, starts, sizes)]


def _prep_weights(norm1_g, w_in, a_q_norm_g, a_k_norm_g, gla_w_alpha, gla_b_alpha, gla_out_norm_g,
                  w_out, norm2_g, mem_norm_g, mem_w_q, mem_w_k, mem_w_v, mem_q_norm_g,
                  mem_k_norm_g, mem_w_o, norm3_g, router_w, router_b, exp_w1, exp_b1, exp_w2,
                  exp_b2):
    main = 3 * A_WIDTH + 2 * B_QK_WIDTH + 2 * B_V_WIDTH
    w_lr = jnp.pad(w_in[:, main:], ((0, 0), (0, LANES - GATE_RANK)))
    rw = jnp.pad(router_w, ((0, 0), (0, LANES - N_EXPERTS)))
    rwh = rw.astype(BF16)
    return {
        "g1": norm1_g[None],
        "w_in": jnp.concatenate([w_in[:, :main], w_lr], axis=1).astype(BF16),
        "gq": jnp.tile(a_q_norm_g, A_HEADS)[None],
        "gk": jnp.tile(a_k_norm_g, A_HEADS)[None],
        "bd64": _block_diag(A_WIDTH, A_HEAD_DIM, 1.0 / A_HEAD_DIM, BF16),
        "wa": jnp.pad(gla_w_alpha, ((0, LANES - GATE_RANK), (0, 0))).astype(BF16),
        "ba": gla_b_alpha[None],
        "bones64": _block_diag(B_QK_WIDTH, B_KEY_DIM, 1.0, BF16),
        "bd128": _block_diag(B_V_WIDTH, B_VAL_DIM, 1.0 / B_VAL_DIM, BF16),
        "gout": jnp.tile(gla_out_norm_g, B_HEADS)[None],
        "woa": w_out[:A_WIDTH].astype(BF16),
        "wob": w_out[A_WIDTH:].astype(BF16),
        "g2": norm2_g[None],
        "gmem": mem_norm_g[None],
        "wmq": mem_w_q.astype(BF16),
        "wmk": mem_w_k.astype(BF16),
        "wmv": mem_w_v.astype(BF16),
        "gmq": jnp.tile(mem_q_norm_g, MEM_HEADS)[None],
        "gmk": jnp.tile(mem_k_norm_g, MEM_HEADS)[None],
        "wmo": mem_w_o.astype(BF16),
        "g3": norm3_g[None],
        "rw": jnp.concatenate([rwh, (rw - rwh.astype(F32)).astype(BF16)], axis=1),
        "rb": jnp.pad(router_b, (0, LANES - N_EXPERTS), constant_values=-1e30)[None],
        "w1": exp_w1,
        "b1g": exp_b1[:, None, 0::2],
        "b1l": exp_b1[:, None, 1::2],
        "w2": exp_w2,
        "b2": exp_b2[:, None, :],
    }


def _layer(xp, xs, mem_prompt, cache_k, cache_v, state_gla, cache_mk, cache_mv, wp):
    bsz, seq, _ = xp.shape
    db, t_new, _ = xs.shape
    w_p = min(max(w for w, _ in DILATED_GROUPS), seq)

    xpf = xp.reshape(bsz * seq, D_MODEL)
    q, k, v, k_last, v_last, bq, bk, bv, gate, la = _in_proj(xpf, wp, BF16, seq, w_p)
    oa = _swa_prompt(q, k, v, bsz, seq)
    s0 = jnp.zeros((bsz, B_QK_WIDTH, B_VAL_DIM), F32)
    ob, s_p = _gla(bq, bk, la, bv, gate, s0, wp, seq, GLA_CHUNK, GLA_SUB, GLA_STEP)
    mk, mv = _mem_kv(mem_prompt.reshape(bsz * N_MEM, D_MODEL), wp)
    cnt0 = jnp.zeros((1, LANES), F32)
    h2_p, xn_p, meta_p, cnt = _post(xpf, oa, ob, mk.reshape(bsz, N_MEM, MEM_WIDTH),
                                    mv.reshape(bsz, N_MEM, MEM_WIDTH), cnt0, wp,
                                    POST_TILE, 1, seq // POST_TILE)

    xsf = xs.reshape(db * t_new, D_MODEL)
    q, _, _, ks, vs, bq, bk, bv, gate, la = _in_proj(xsf, wp, F32, t_new, t_new)
    new3 = lambda a: a.reshape(db, t_new, A_WIDTH)
    oa_s, nk, nv = _swa_sample(new3(q.transpose(1, 0, 2)), new3(ks), new3(vs), cache_k, cache_v)
    ob_s, s_s = _gla(bq, bk, la, bv, gate, state_gla, wp, t_new, t_new, t_new, t_new)
    h2_s, xn_s, meta_s, cnt = _post(xsf, oa_s.reshape(db * t_new, A_WIDTH), ob_s, cache_mk,
                                    cache_mv, cnt, wp, SAMPLE_SEQS * t_new, SAMPLE_SEQS, 1)

    y_p, y_s = _moe([(h2_p, xn_p, meta_p), (h2_s, xn_s, meta_s)], wp)
    return (y_p.reshape(bsz, seq, D_MODEL), y_s.reshape(db, t_new, D_MODEL),
            k_last.reshape(bsz, w_p, A_WIDTH), v_last.reshape(bsz, w_p, A_WIDTH),
            s_p, mk, mv, nk, nv, s_s)


def kernel(x_prompt, x_sample, mem_prompt, cache_swa_k, cache_swa_v, state_gla, cache_mem_k, cache_mem_v, norm1_g, w_in, a_q_norm_g, a_k_norm_g, gla_w_alpha, gla_b_alpha, gla_out_norm_g, w_out, norm2_g, mem_norm_g, mem_w_q, mem_w_k, mem_w_v, mem_q_norm_g, mem_k_norm_g, mem_w_o, norm3_g, router_w, router_b, exp_w1, exp_b1, exp_w2, exp_b2):
    depth = w_in.shape[0]
    bsz = x_prompt.shape[0]
    db, w_buf = cache_swa_k.shape[1], cache_swa_k.shape[2]
    xp, xs = x_prompt, x_sample
    per_layer = []
    for l in range(depth):
        wp = _prep_weights(
            norm1_g[l], w_in[l], a_q_norm_g[l], a_k_norm_g[l], gla_w_alpha[l], gla_b_alpha[l],
            gla_out_norm_g[l], w_out[l], norm2_g[l], mem_norm_g[l], mem_w_q[l], mem_w_k[l],
            mem_w_v[l], mem_q_norm_g[l], mem_k_norm_g[l], mem_w_o[l], norm3_g[l], router_w[l],
            router_b[l], exp_w1[l], exp_b1[l], exp_w2[l], exp_b2[l])
        xp, xs, kp, vp, s_p, mk, mv, nk, nv, s_s = _layer(
            xp, xs, mem_prompt,
            cache_swa_k[l].reshape(db, w_buf, A_WIDTH).transpose(0, 2, 1),
            cache_swa_v[l].reshape(db, w_buf, A_WIDTH).transpose(0, 2, 1),
            state_gla[l].reshape(db, B_QK_WIDTH, B_VAL_DIM),
            cache_mem_k[l].reshape(db, N_MEM, MEM_WIDTH), cache_mem_v[l].reshape(db, N_MEM, MEM_WIDTH),
            wp)
        w_p = kp.shape[1]
        per_layer.append((
            kp.reshape(bsz, w_p, A_HEADS, A_HEAD_DIM), vp.reshape(bsz, w_p, A_HEADS, A_HEAD_DIM),
            s_p.reshape(bsz, B_HEADS, B_KEY_DIM, B_VAL_DIM),
            mk.reshape(bsz, N_MEM, MEM_HEADS, MEM_HEAD_DIM), mv.reshape(bsz, N_MEM, MEM_HEADS, MEM_HEAD_DIM),
            nk.transpose(0, 2, 1).reshape(db, w_buf, A_HEADS, A_HEAD_DIM),
            nv.transpose(0, 2, 1).reshape(db, w_buf, A_HEADS, A_HEAD_DIM),
            s_s.reshape(db, B_HEADS, B_KEY_DIM, B_VAL_DIM)))
    stacked = [jnp.stack(t) for t in zip(*per_layer)]
    return (xp, xs, *stacked)
```

```python
import functools

import jax
import jax.numpy as jnp
import numpy as np
from jax import lax
from jax.experimental import pallas as pl
from jax.experimental.pallas import tpu as pltpu

F32 = jnp.float32
BF16 = jnp.bfloat16
I32 = jnp.int32

EPS = 1e-6
D_MODEL = 1024
A_HEADS, A_HEAD_DIM, A_WIDTH = 8, 64, 512
A_SCALE = A_HEAD_DIM ** -0.5
LOG2E = 1.4426950408889634
DILATED_GROUPS = ((128, 1), (512, 4), (2048, 16))
SPAN = 128
B_HEADS, B_KEY_DIM, B_VAL_DIM = 4, 64, 128
B_QK_WIDTH, B_V_WIDTH = 256, 512
GATE_RANK = 16
GATE_TEMP = 16.0
N_MEM, MEM_HEADS, MEM_HEAD_DIM, MEM_WIDTH = 256, 4, 128, 512
N_EXPERTS, TOP_K, D_FF = 32, 4, 1024
SWIGLU_ALPHA, SWIGLU_LIMIT = 1.702, 7.0

LANES = 128
VMEM_LIMIT = 56 * 1024 * 1024

PROJ_TILE = 512
SWA_BLOCK = 2048
SWA_UNROLL = 4
GLA_CHUNK, GLA_SUB, GLA_STEP = 64, 8, 512
POST_TILE = 512
SAMPLE_SEQS = 8
EXPERT_TILE = 512
DISPATCH_BLOCK = 512
GROUP_ROWS = 8
LOCAL_ROWS = DISPATCH_BLOCK * TOP_K + N_EXPERTS * GROUP_ROWS


def _cparams(sem, vmem=VMEM_LIMIT, **kw):
    return pltpu.CompilerParams(dimension_semantics=sem, vmem_limit_bytes=vmem, **kw)


def _full(shape):
    n = len(shape)
    return pl.BlockSpec(shape, lambda *_: (0,) * n)


def _rms(x, g):
    ms = jnp.mean(x * x, axis=-1, keepdims=True)
    return x * lax.rsqrt(ms + EPS) * g


def _dot(a, b):
    return jnp.dot(a, b, preferred_element_type=F32)


def _dot_nt(a, b):
    return lax.dot_general(a, b, (((1,), (1,)), ((), ())), preferred_element_type=F32)


def _block_diag(n, blk, val, dtype):
    i = np.arange(n)
    return jnp.asarray(np.where((i[:, None] // blk) == (i[None, :] // blk), val, 0.0), dtype)


def _in_proj_body(x_ref, g1_ref, w_ref, gq_ref, gk_ref, bd_ref, wa_ref, ba_ref,
                  q_ref, k_ref, v_ref, kc_ref, vc_ref, bq_ref, bk_ref, bv_ref, gate_ref, la_ref):
    xn = _rms(x_ref[...], g1_ref[...]).astype(BF16)
    bd = bd_ref[...]

    def proj(lo, hi):
        return _dot(xn, w_ref[:, lo:hi])

    def headnorm(z, g):
        ms = _dot((z * z).astype(BF16), bd)
        return z * lax.rsqrt(ms + EPS) * g

    def put_groups(ref, z):
        for p in range(A_WIDTH // LANES):
            ref[p] = z[:, p * LANES:(p + 1) * LANES]

    put_groups(q_ref, headnorm(proj(0, 512), gq_ref[...]) * (A_SCALE * LOG2E))
    k = headnorm(proj(512, 1024), gk_ref[...])
    v = proj(1024, 1536)
    put_groups(k_ref, k)
    put_groups(v_ref, v)
    kc_ref[...] = k
    vc_ref[...] = v
    bq_ref[...] = proj(1536, 1792) * (B_KEY_DIM ** -0.5)
    bk_ref[...] = proj(1792, 2048)
    bv_ref[...] = proj(2048, 2560).astype(bv_ref.dtype)
    br = proj(2560, 3072)
    gate_ref[...] = (br * jax.nn.sigmoid(br)).astype(gate_ref.dtype)
    lr = proj(3072, 3200).astype(BF16)
    pre = _dot(lr, wa_ref[...]) + ba_ref[...]
    log_sig = jnp.minimum(pre, 0.0) - jnp.log1p(jnp.exp(-jnp.abs(pre)))
    la_ref[...] = log_sig * (1.0 / GATE_TEMP)


def _in_proj(x, wp, wide_dtype, seq, keep):
    n = x.shape[0]
    tm = min(PROJ_TILE, n)
    row = lambda w: pl.BlockSpec((tm, w), lambda i: (i, 0))
    ngrp = A_WIDTH // LANES
    grp = pl.BlockSpec((ngrp, tm, LANES), lambda i: (0, i, 0))
    if keep == seq:
        kept = row(A_WIDTH)
    else:
        tps, kt = seq // tm, keep // tm
        assert tps * tm == seq and kt * tm == keep
        kept = pl.BlockSpec((tm, A_WIDTH),
                            lambda i: ((i // tps) * kt + jnp.maximum(i % tps - (tps - kt), 0), 0))
    outs = [(256, F32), (256, F32), (512, wide_dtype), (512, wide_dtype), (256, F32)]
    return pl.pallas_call(
        _in_proj_body,
        grid=(n // tm,),
        in_specs=[row(D_MODEL), _full((1, D_MODEL)), _full((D_MODEL, 3200)), _full((1, 512)),
                  _full((1, 512)), _full((512, 512)), _full((LANES, 256)), _full((1, 256))],
        out_specs=[grp] * 3 + [kept] * 2 + [row(w) for w, _ in outs],
        out_shape=[jax.ShapeDtypeStruct((ngrp, n, LANES), F32)] * 3
        + [jax.ShapeDtypeStruct((n // seq * keep, A_WIDTH), F32)] * 2
        + [jax.ShapeDtypeStruct((n, w), dt) for w, dt in outs],
        compiler_params=_cparams(("arbitrary",)),
        name="in_proj",
    )(x, wp["g1"], wp["w_in"], wp["gq"], wp["gk"], wp["bd64"], wp["wa"], wp["ba"])


def _unroll_for(trips):
    return max(u for u in range(1, SWA_UNROLL + 1) if trips % u == 0)


def _ds(start, size, stride):
    return pl.ds(start, size) if stride == 1 else pl.ds(start, size, stride=stride)


def _swa_prompt_body(q_ref, kp_ref, kc_ref, vp_ref, vc_ref, o_ref, m_s, l_s, acc_s):
    i = pl.program_id(2)
    qb = SWA_BLOCK
    lane = lax.broadcasted_iota(I32, (SPAN, LANES), 1)
    lo_mask = lane < A_HEAD_DIM
    jq = lax.broadcasted_iota(I32, (SPAN, 2 * SPAN), 0)
    jk = lax.broadcasted_iota(I32, (SPAN, 2 * SPAN), 1)
    dist = jq + SPAN - jk
    band = (dist >= 0) & (dist <= SPAN)
    cur_half = jk >= SPAN

    def attend(qp, kp, vp, valid, rows, first_group):
        vp1 = jnp.concatenate([vp, jnp.ones_like(vp)], axis=1)
        res = []
        for hh in range(2):
            msk = lo_mask if hh == 0 else jnp.logical_not(lo_mask)
            qm = jnp.where(msk, qp, 0.0).astype(BF16)
            s = jnp.where(valid, _dot_nt(qm, kp), -jnp.inf)
            m = jnp.max(s, axis=1, keepdims=True)
            res.append((m, _dot(jnp.exp2(s - m).astype(BF16), vp1)))
        m_new = jnp.where(lo_mask, res[0][0], res[1][0])
        l_new = jnp.where(lo_mask, res[0][1][:, LANES:], res[1][1][:, LANES:])
        o_new = jnp.where(lo_mask, res[0][1][:, :LANES], res[1][1][:, :LANES])
        if first_group:
            m_s[rows, :] = m_new
            l_s[rows, :] = l_new
            acc_s[rows, :] = o_new
        else:
            m_old = m_s[rows, :]
            m = jnp.maximum(m_old, m_new)
            a_old = jnp.exp2(m_old - m)
            a_new = jnp.exp2(m_new - m)
            m_s[rows, :] = m
            l_s[rows, :] = l_s[rows, :] * a_old + l_new * a_new
            acc_s[rows, :] = acc_s[rows, :] * a_old + o_new * a_new

    for gi, (_, dil) in enumerate(DILATED_GROUPS):
        unit = dil * SPAN
        nblk = qb // unit
        first = gi == 0

        def head_block(r, carry, dil=dil, unit=unit, first=first):
            rows = _ds(r, SPAN, dil)
            prev = _ds(qb - unit + r, SPAN, dil)
            ks = jnp.concatenate([kp_ref[prev, :], kc_ref[rows, :]], axis=0).astype(BF16)
            vs = jnp.concatenate([vp_ref[prev, :], vc_ref[rows, :]], axis=0).astype(BF16)
            valid = band & (cur_half | (i > 0))
            attend(q_ref[rows, :], ks, vs, valid, rows, first)
            return carry

        lax.fori_loop(0, dil, head_block, 0, unroll=_unroll_for(dil))

        if nblk > 1:
            def tail_block(idx, carry, dil=dil, unit=unit, first=first):
                n = idx // dil + 1
                r = idx % dil
                start = unit * n + r
                rows = _ds(start, SPAN, dil)
                keys = _ds(start - unit, 2 * SPAN, dil)
                attend(q_ref[rows, :], kc_ref[keys, :].astype(BF16), vc_ref[keys, :].astype(BF16),
                       band, rows, first)
                return carry

            lax.fori_loop(0, (nblk - 1) * dil, tail_block, 0, unroll=_unroll_for((nblk - 1) * dil))

    o_ref[...] = (acc_s[...] / l_s[...]).astype(o_ref.dtype)


def _swa_prompt(q, k, v, bsz, seq):
    qb = SWA_BLOCK
    nb = seq // qb
    cur = pl.BlockSpec((None, qb, LANES), lambda p, b, i: (p, b * nb + i, 0))
    prev = pl.BlockSpec((None, qb, LANES), lambda p, b, i: (p, b * nb + jnp.maximum(i - 1, 0), 0))
    return pl.pallas_call(
        _swa_prompt_body,
        grid=(A_WIDTH // LANES, bsz, nb),
        in_specs=[cur, prev, cur, prev, cur],
        out_specs=pl.BlockSpec((qb, LANES), lambda p, b, i: (b * nb + i, p)),
        out_shape=jax.ShapeDtypeStruct((bsz * seq, A_WIDTH), BF16),
        scratch_shapes=[pltpu.VMEM((qb, LANES), F32)] * 3,
        compiler_params=_cparams(("parallel", "parallel", "arbitrary")),
        name="swa_prompt",
    )(q, k, k, v, v)


def _swa_sample_body(q_ref, kn_ref, vn_ref, ck_ref, cv_ref, c1_ref, c2_ref,
                     o_ref, nk_ref, nv_ref):
    t_new = q_ref.shape[1]
    w_buf = ck_ref.shape[2]
    ck, cv = ck_ref[0], cv_ref[0]
    kn, vn = kn_ref[0], vn_ref[0]
    tail_lane = lax.broadcasted_iota(I32, (A_WIDTH, LANES), 1) >= LANES - t_new

    def shift_in(old, new, out_ref):
        moved = pltpu.roll(old, w_buf - t_new, 1)
        new_t = jnp.transpose(jnp.concatenate([jnp.zeros((LANES - t_new, A_WIDTH), F32), new], axis=0))
        out_ref[0, :, :w_buf - LANES] = moved[:, :w_buf - LANES]
        out_ref[0, :, w_buf - LANES:] = jnp.where(tail_lane, new_t, moved[:, w_buf - LANES:])

    shift_in(ck, kn, nk_ref)
    shift_in(cv, vn, nv_ref)

    q = q_ref[0]
    lane = lax.broadcasted_iota(I32, (t_new, LANES), 1)
    lo_mask = lane < A_HEAD_DIM
    c1, c2 = c1_ref[...], c2_ref[...]
    outs = []
    for p in range(A_WIDTH // LANES):
        sl = slice(p * LANES, (p + 1) * LANES)
        qp = q[:, sl]
        qblk = jnp.concatenate([jnp.where(lo_mask, qp, 0.0), jnp.where(lo_mask, 0.0, qp)],
                               axis=0).astype(BF16)
        s1 = jnp.where(c1 > 0, _dot(qblk, ck[sl, :].astype(BF16)), -jnp.inf)
        s2 = jnp.where(c2 > 0, _dot_nt(qblk, kn[:, sl].astype(BF16)), -jnp.inf)
        m = jnp.maximum(jnp.max(s1, axis=1, keepdims=True), jnp.max(s2, axis=1, keepdims=True))
        p1 = c1 * jnp.exp2(s1 - m)
        p2 = c2 * jnp.exp2(s2 - m)
        l = jnp.sum(p1, axis=1, keepdims=True) + jnp.sum(p2, axis=1, keepdims=True)
        o = (_dot_nt(p1.astype(BF16), cv[sl, :].astype(BF16))
             + _dot(p2.astype(BF16), vn[:, sl].astype(BF16))) / l
        outs.append(jnp.where(lo_mask, o[:t_new], o[t_new:]))
    o_ref[0] = jnp.concatenate(outs, axis=1).astype(o_ref.dtype)


def _sample_multiplicity(t_new, w_buf):
    t = np.arange(t_new)[:, None]
    e = np.arange(w_buf + t_new)[None, :]
    d = w_buf + t - e
    c = np.zeros(d.shape, np.float32)
    for window, dil in DILATED_GROUPS:
        c += ((d >= 0) & (d % dil == 0) & (d <= window)).astype(np.float32)
    c = np.concatenate([c, c], axis=0)
    return jnp.asarray(c[:, :w_buf]), jnp.asarray(c[:, w_buf:])


def _swa_sample(q, kn, vn, cache_k, cache_v):
    db, t_new, w = q.shape
    w_buf = cache_k.shape[2]
    assert w_buf >= max(win for win, _ in DILATED_GROUPS) and t_new % 8 == 0 and t_new <= LANES
    c1, c2 = _sample_multiplicity(t_new, w_buf)
    new = pl.BlockSpec((1, t_new, w), lambda b: (b, 0, 0))
    cache = pl.BlockSpec((1, w, w_buf), lambda b: (b, 0, 0))
    return pl.pallas_call(
        _swa_sample_body,
        grid=(db,),
        in_specs=[new, new, new, cache, cache, _full(c1.shape), _full(c2.shape)],
        out_specs=[new, cache, cache],
        out_shape=[jax.ShapeDtypeStruct((db, t_new, w), BF16),
                   jax.ShapeDtypeStruct(cache_k.shape, cache_k.dtype),
                   jax.ShapeDtypeStruct(cache_v.shape, cache_v.dtype)],
        compiler_params=_cparams(("parallel",)),
        name="swa_sample",
    )(q, kn, vn, cache_k, cache_v, c1, c2)


def _gla_body(chunk, sub, nch, q_ref, k_ref, g_ref, v_ref, gate_ref, s0_ref, tril_ref, dmask_ref,
              bones_ref, sbm_ref, bd_ref, gout_ref, o_ref, sfin_ref, sbd):
    j = pl.program_id(1)
    sbm = sbm_ref[...]
    nsub = chunk // sub
    pad = B_KEY_DIM - chunk

    @pl.when(j == 0)
    def _():
        s0 = s0_ref[0]
        sbd[...] = jnp.concatenate([s0] * B_HEADS, axis=1) * sbm

    row = lax.broadcasted_iota(I32, (chunk, 1), 0)
    sub_id = row // sub
    lane_w = lax.broadcasted_iota(I32, (chunk, LANES * max(nsub - 1, 1)), 1)
    lo_w = (lane_w % LANES) < B_KEY_DIM

    def one_chunk(c, carry):
        off = pl.multiple_of(c * chunk, chunk)
        rows = pl.ds(off, chunk)
        q, k, g = q_ref[rows, :], k_ref[rows, :], g_ref[rows, :]
        v = v_ref[rows, :].astype(F32)
        g1 = g.astype(BF16)
        r1 = g - g1.astype(F32)
        g2 = r1.astype(BF16)
        g3 = (r1 - g2.astype(F32)).astype(BF16)
        tril = tril_ref[...]
        b = _dot(tril, g1) + _dot(tril, g2) + _dot(tril, g3)
        b_last = b[chunk - 1:chunk, :]
        state = sbd[...]

        o = _dot((q * jnp.exp(b)).astype(BF16), state.astype(BF16))

        bones = bones_ref[...]
        att = _dot((q * k).astype(BF16), bones) * dmask_ref[0]
        gate = jnp.exp(g)
        decay = gate
        for d in range(1, sub):
            if d > 1:
                decay = decay * pltpu.roll(gate, d - 1, 0)
            w = q * pltpu.roll(k, d, 0) * decay
            att = att + _dot(w.astype(BF16), bones) * dmask_ref[d]

        if nsub > 1:
            qx, kx = [], []
            for i in range(1, nsub):
                r_i = b[sub * i - 1:sub * i, :]
                qx.append(jnp.where(sub_id == i, q * jnp.exp(jnp.minimum(b - r_i, 0.0)), 0.0))
                kx.append(jnp.where(sub_id < i, k * jnp.exp(jnp.minimum(r_i - b, 0.0)), 0.0))
            parts = []
            for p in range(B_QK_WIDTH // LANES):
                sl = slice(p * LANES, (p + 1) * LANES)
                qp = jnp.concatenate([x[:, sl] for x in qx], axis=1)
                kp = jnp.concatenate([x[:, sl] for x in kx], axis=1).astype(BF16)
                zero = jnp.zeros_like(kp)
                lhs = jnp.concatenate([jnp.where(lo_w, qp, 0.0), jnp.where(lo_w, 0.0, qp)],
                                      axis=1).astype(BF16)
                rhs = jnp.concatenate([jnp.concatenate([kp, zero], axis=1),
                                       jnp.concatenate([zero, kp], axis=1)], axis=0)
                parts.append(_dot_nt(lhs, rhs))
            att = att + jnp.concatenate(parts, axis=1)

        if pad:
            vrow = jnp.concatenate([v, jnp.zeros((pad, B_V_WIDTH), F32)], axis=0)
        else:
            vrow = v
        vbd = (jnp.concatenate([vrow] * B_HEADS, axis=0) * sbm).astype(BF16)
        o = o + _dot(att.astype(BF16), vbd)

        ke = (k * jnp.exp(b_last - b)).astype(BF16)
        upd = lax.dot_general(ke, v.astype(BF16), (((0,), (0,)), ((), ())),
                              preferred_element_type=F32)
        dec = jnp.transpose(jnp.broadcast_to(jnp.exp(b_last), (8, B_QK_WIDTH)))[:, 0:1]
        sbd[...] = (state * dec + upd) * sbm

        ms = _dot((o * o).astype(BF16), bd_ref[...])
        on = o * lax.rsqrt(ms + EPS) * gout_ref[...] * gate_ref[rows, :].astype(F32)
        o_ref[rows, :] = on.astype(o_ref.dtype)
        return carry

    lax.fori_loop(0, nch, one_chunk, 0, unroll=2 if nch % 2 == 0 else 1)

    @pl.when(j == pl.num_programs(1) - 1)
    def _():
        s = sbd[...]
        sfin_ref[0] = jnp.concatenate(
            [s[h * B_KEY_DIM:(h + 1) * B_KEY_DIM, h * B_VAL_DIM:(h + 1) * B_VAL_DIM]
             for h in range(B_HEADS)], axis=0)


def _gla_consts(chunk, sub):
    t = np.arange(chunk)
    tril = (t[:, None] >= t[None, :]).astype(np.float32)
    lane = np.arange(B_QK_WIDTH)
    dmask = np.zeros((sub, chunk, B_QK_WIDTH), np.float32)
    for d in range(sub):
        ok = (t % sub) >= d
        dmask[d] = ((lane[None, :] % B_KEY_DIM) == (t[:, None] - d)) & ok[:, None]
    r = np.arange(B_QK_WIDTH)[:, None] // B_KEY_DIM
    c = np.arange(B_V_WIDTH)[None, :] // B_VAL_DIM
    sbm = (r == c).astype(np.float32)
    return jnp.asarray(tril, BF16), jnp.asarray(dmask), jnp.asarray(sbm)


def _gla(q, k, g, v, gate, s0, wp, length, chunk, sub, step):
    n = q.shape[0]
    bsz = n // length
    assert chunk == sub or chunk == B_KEY_DIM
    tril, dmask, sbm = _gla_consts(chunk, sub)
    nstep = length // step
    row = lambda w: pl.BlockSpec((step, w), lambda b, j: (b * nstep + j, 0))
    st = pl.BlockSpec((1, B_QK_WIDTH, B_VAL_DIM), lambda b, j: (b, 0, 0))
    return pl.pallas_call(
        functools.partial(_gla_body, chunk, sub, step // chunk),
        grid=(bsz, nstep),
        in_specs=[row(256), row(256), row(256), row(512), row(512), st, _full(tril.shape),
                  _full(dmask.shape), _full((256, 256)), _full(sbm.shape), _full((512, 512)),
                  _full((1, 512))],
        out_specs=[row(512), st],
        out_shape=[jax.ShapeDtypeStruct((n, B_V_WIDTH), BF16),
                   jax.ShapeDtypeStruct(s0.shape, F32)],
        scratch_shapes=[pltpu.VMEM((B_QK_WIDTH, B_V_WIDTH), F32)],
        compiler_params=_cparams(("parallel", "arbitrary")),
        name="gla",
    )(q, k, g, v, gate, s0, tril, dmask, wp["bones64"], sbm, wp["bd128"], wp["gout"])


def _head_rms(z, g, scale=1.0):
    parts = []
    for h in range(MEM_HEADS):
        zh = z[:, h * LANES:(h + 1) * LANES]
        parts.append(zh * lax.rsqrt(jnp.mean(zh * zh, axis=-1, keepdims=True) + EPS))
    return jnp.concatenate(parts, axis=1) * (g * scale)


def _mem_kv_body(m_ref, gn_ref, wk_ref, wv_ref, gk_ref, mk_ref, mv_ref):
    mn = _rms(m_ref[...], gn_ref[...]).astype(BF16)
    mk_ref[...] = _head_rms(_dot(mn, wk_ref[...]), gk_ref[...])
    mv_ref[...] = _dot(mn, wv_ref[...])


def _mem_kv(mem, wp):
    n = mem.shape[0]
    tm = 256
    row = lambda w: pl.BlockSpec((tm, w), lambda i: (i, 0))
    return pl.pallas_call(
        _mem_kv_body,
        grid=(n // tm,),
        in_specs=[row(D_MODEL), _full((1, D_MODEL)), _full((D_MODEL, 512)), _full((D_MODEL, 512)),
                  _full((1, 512))],
        out_specs=[row(512), row(512)],
        out_shape=[jax.ShapeDtypeStruct((n, 512), F32)] * 2,
        compiler_params=_cparams(("parallel",)),
        name="mem_kv",
    )(mem, wp["gmem"], wp["wmk"], wp["wmv"], wp["gmk"])


def _post_body(nseq, x_ref, oa_ref, ob_ref, woa_ref, wob_ref, g2_ref, wq_ref, gmq_ref, mk_ref,
               mv_ref, wo_ref, g3_ref, rw_ref, rb_ref, cnt0_ref, tri_ref,
               h2_ref, xn_ref, meta_ref, cnt_ref, carry):
    tm = x_ref.shape[0]

    @pl.when(pl.program_id(0) == 0)
    def _():
        carry[...] = cnt0_ref[...]

    h = x_ref[...] + _dot(oa_ref[...], woa_ref[...]) + _dot(ob_ref[...], wob_ref[...])
    hn = _rms(h, g2_ref[...]).astype(BF16)
    qm = _head_rms(_dot(hn, wq_ref[...]), gmq_ref[...], MEM_HEAD_DIM ** -0.5).astype(BF16)
    nk = nseq * N_MEM
    mk = mk_ref[...].reshape(nk, MEM_WIDTH)
    mv = mv_ref[...].reshape(nk, MEM_WIDTH)
    if nseq > 1:
        rt = lax.broadcasted_iota(I32, (tm, nk), 0) // (tm // nseq)
        ct = lax.broadcasted_iota(I32, (tm, nk), 1) // N_MEM
        same = rt == ct
    outs = []
    for hd in range(MEM_HEADS):
        sl = slice(hd * LANES, (hd + 1) * LANES)
        s = _dot_nt(qm[:, sl], mk[:, sl].astype(BF16))
        if nseq > 1:
            s = jnp.where(same, s, -jnp.inf)
        m = jnp.max(s, axis=1, keepdims=True)
        pr = jnp.exp(s - m)
        l = jnp.sum(pr, axis=1, keepdims=True)
        outs.append(_dot(pr.astype(BF16), mv[:, sl].astype(BF16)) / l)
    h2 = h + _dot(jnp.concatenate(outs, axis=1).astype(BF16), wo_ref[...])
    h2_ref[...] = h2
    xn = _rms(h2, g3_ref[...])
    xn_ref[...] = xn

    x1 = xn.astype(BF16)
    x2 = (xn - x1.astype(F32)).astype(BF16)
    prod = _dot(jnp.concatenate([x1, x2], axis=0), rw_ref[...])
    logits = (prod[:tm, :LANES] + prod[:tm, LANES:] + prod[tm:, :LANES] + prod[tm:, LANES:]
              + rb_ref[...])
    lane = lax.broadcasted_iota(I32, (tm, LANES), 1)
    vals, idxs, hots = [], [], []
    work = logits
    for _ in range(TOP_K):
        m = jnp.max(work, axis=1, keepdims=True)
        idx = jnp.min(jnp.where(work == m, lane, LANES), axis=1, keepdims=True)
        hot = lane == idx
        vals.append(m)
        idxs.append(idx)
        hots.append(hot)
        work = jnp.where(hot, -jnp.inf, work)
    exps = [jnp.exp(v - vals[0]) for v in vals]
    den = exps[0] + exps[1] + exps[2] + exps[3]

    sel = (hots[0] | hots[1] | hots[2] | hots[3]).astype(F32)
    before = _dot(tri_ref[...], sel.astype(BF16)) + carry[...]
    carry[...] = carry[...] + jnp.sum(sel, axis=0, keepdims=True)
    cnt_ref[...] = carry[...]

    meta = jnp.zeros((tm, LANES), F32)
    for kk in range(TOP_K):
        rank = jnp.sum(jnp.where(hots[kk], before, 0.0), axis=1, keepdims=True)
        meta = jnp.where(lane == kk, idxs[kk].astype(F32), meta)
        meta = jnp.where(lane == TOP_K + kk, rank, meta)
        meta = jnp.where(lane == 2 * TOP_K + kk, exps[kk] / den, meta)
    meta_ref[...] = meta


def _post(x, oa, ob, mk, mv, cnt0, wp, tm, nseq, tiles_per_mem):
    n = x.shape[0]
    row = lambda w: pl.BlockSpec((tm, w), lambda i: (i, 0))
    mem = pl.BlockSpec((nseq, N_MEM, MEM_WIDTH), lambda i: (i // tiles_per_mem, 0, 0))
    tri = jnp.asarray(np.tril(np.ones((tm, tm), np.float32), -1), BF16)
    return pl.pallas_call(
        functools.partial(_post_body, nseq),
        grid=(n // tm,),
        in_specs=[row(D_MODEL), row(512), row(512), _full((512, D_MODEL)), _full((512, D_MODEL)),
                  _full((1, D_MODEL)), _full((D_MODEL, 512)), _full((1, 512)), mem, mem,
                  _full((512, D_MODEL)), _full((1, D_MODEL)), _full((D_MODEL, 2 * LANES)),
                  _full((1, LANES)), _full((1, LANES)), _full((tm, tm))],
        out_specs=[row(D_MODEL), row(D_MODEL), row(LANES), _full((1, LANES))],
        out_shape=[jax.ShapeDtypeStruct((n, D_MODEL), F32), jax.ShapeDtypeStruct((n, D_MODEL), F32),
                   jax.ShapeDtypeStruct((n, LANES), F32), jax.ShapeDtypeStruct((1, LANES), F32)],
        scratch_shapes=[pltpu.VMEM((1, LANES), F32)],
        compiler_params=_cparams(("arbitrary",)),
        name="post",
    )(x, oa, ob, wp["woa"], wp["wob"], wp["g2"], wp["wmq"], wp["gmq"], mk, mv, wp["wmo"],
      wp["g3"], wp["rw"], wp["rb"], cnt0, tri)


def _start_rows(src, src_row, dst, dst_row, n_rows, sem):
    s = pl.multiple_of(jnp.asarray(src_row, I32), GROUP_ROWS)
    d = pl.multiple_of(jnp.asarray(dst_row, I32), GROUP_ROWS)
    n_rows = pl.multiple_of(jnp.asarray(n_rows, I32), GROUP_ROWS)

    @pl.when(n_rows > 0)
    def _():
        pltpu.make_async_copy(src.at[pl.ds(s, n_rows)], dst.at[pl.ds(d, n_rows)], sem).start()

    return n_rows


def _wait_rows(like_src, like_dst, n_rows, sem):
    n_rows = pl.multiple_of(n_rows, GROUP_ROWS)

    @pl.when(n_rows > 0)
    def _():
        pltpu.make_async_copy(like_src.at[pl.ds(0, n_rows)], like_dst.at[pl.ds(0, n_rows)], sem).wait()


def _pad_rows_body(pstart_ref, prows_ref, nt_ref, xs_ref, zeros, sem):
    zeros[...] = jnp.zeros_like(zeros)
    n_tiles_max = xs_ref.shape[0] // EXPERT_TILE

    def expert_tail(e, total):
        return total + _start_rows(zeros, 0, xs_ref, pstart_ref[e], prows_ref[e], sem)

    def unused_tile(t, total):
        return total + _start_rows(zeros, 0, xs_ref, t * EXPERT_TILE, EXPERT_TILE, sem)

    total = lax.fori_loop(0, N_EXPERTS, expert_tail, jnp.int32(0))
    total = lax.fori_loop(nt_ref[0], n_tiles_max, unused_tile, total)
    _wait_rows(xs_ref, xs_ref, total, sem)


def _pad_rows(pad_start, pad_units, n_tiles, rows):
    return pl.pallas_call(
        _pad_rows_body,
        grid_spec=pltpu.PrefetchScalarGridSpec(
            num_scalar_prefetch=3, grid=(1,),
            in_specs=[],
            out_specs=pl.BlockSpec(memory_space=pltpu.HBM),
            scratch_shapes=[pltpu.VMEM((EXPERT_TILE, D_MODEL), F32), pltpu.SemaphoreType.DMA]),
        out_shape=jax.ShapeDtypeStruct((rows, D_MODEL), F32),
        compiler_params=_cparams(("arbitrary",)),
        name="moe_pad_rows",
    )(pad_start, pad_units, n_tiles)


def _dispatch_body(blk0, loff_ref, gstart_ref, nrows_ref, ldest_ref, x_ref, xs_in_ref, xs_ref,
                   xloc, sems):
    del xs_in_ref
    i = pl.program_id(0)
    b = i + blk0
    slot = i % 2
    tb = x_ref.shape[0]
    rows = lax.broadcasted_iota(I32, (LOCAL_ROWS, tb), 0)
    ld = ldest_ref[...]
    hot = rows == ld[0:1, :]
    for kk in range(1, TOP_K):
        hot = hot | (rows == ld[kk:kk + 1, :])
    xloc[slot] = _dot(jnp.where(hot, 1.0, 0.0).astype(BF16), x_ref[...].astype(BF16))

    def block_rows(bb):
        j = bb * N_EXPERTS + N_EXPERTS - 1
        return loff_ref[j] + nrows_ref[j]

    @pl.when(i > 0)
    def _():
        _wait_rows(xloc.at[1 - slot], xs_ref, block_rows(b - 1), sems.at[1 - slot])

    def group(e, carry):
        j = b * N_EXPERTS + e
        _start_rows(xloc.at[slot], loff_ref[j], xs_ref, gstart_ref[j], nrows_ref[j], sems.at[slot])
        return carry

    lax.fori_loop(0, N_EXPERTS, group, 0)

    @pl.when(i == pl.num_programs(0) - 1)
    def _():
        _wait_rows(xloc.at[slot], xs_ref, block_rows(b), sems.at[slot])


def _dispatch(tables, blk0, ldest_t, xn, xs):
    n = xn.shape[0]
    tb = DISPATCH_BLOCK
    idx = lambda i, *_: (i, 0)
    return pl.pallas_call(
        functools.partial(_dispatch_body, blk0),
        grid_spec=pltpu.PrefetchScalarGridSpec(
            num_scalar_prefetch=3, grid=(n // tb,),
            in_specs=[pl.BlockSpec((TOP_K, tb), lambda i, *_: (0, i)),
                      pl.BlockSpec((tb, D_MODEL), idx),
                      pl.BlockSpec(memory_space=pltpu.HBM)],
            out_specs=pl.BlockSpec(memory_space=pltpu.HBM),
            scratch_shapes=[pltpu.VMEM((2, LOCAL_ROWS, D_MODEL), F32), pltpu.SemaphoreType.DMA((2,))]),
        out_shape=jax.ShapeDtypeStruct(xs.shape, xs.dtype),
        input_output_aliases={5: 0},
        compiler_params=_cparams(("arbitrary",)),
        name="moe_dispatch",
    )(*tables, ldest_t, xn, xs)


def _expert_body(te_ref, first_ref, nt_ref, x_ref, w1_ref, sel_ref, b1g_ref, b1l_ref, w2_ref, b2_ref,
                 y_ref, w1g, w1l, w2):
    i = pl.program_id(0)
    live = i < nt_ref[0]

    @pl.when(live & (first_ref[i] == 1))
    def _():
        sel = sel_ref[...]
        for j in range(D_FF // LANES):
            z = _dot(w1_ref[0, :, 2 * LANES * j:2 * LANES * (j + 1)].astype(BF16), sel)
            w1g[:, LANES * j:LANES * (j + 1)] = z[:, :LANES].astype(BF16)
            w1l[:, LANES * j:LANES * (j + 1)] = z[:, LANES:].astype(BF16)
        w2[...] = w2_ref[0].astype(BF16)

    @pl.when(live)
    def _():
        x = x_ref[...].astype(BF16)
        glu = jnp.minimum(_dot(x, w1g[...]) + b1g_ref[0], SWIGLU_LIMIT)
        lin = jnp.clip(_dot(x, w1l[...]) + b1l_ref[0], -SWIGLU_LIMIT, SWIGLU_LIMIT)
        act = glu * jax.nn.sigmoid(SWIGLU_ALPHA * glu) * (lin + 1.0)
        y_ref[...] = _dot(act.astype(BF16), w2[...]) + b2_ref[0]

    @pl.when(jnp.logical_not(live))
    def _():
        y_ref[...] = jnp.zeros_like(y_ref)


def _experts(tile_expert, n_tiles, xs, wp):
    rows = xs.shape[0]
    first = jnp.concatenate([jnp.ones((1,), I32),
                             (tile_expert[1:] != tile_expert[:-1]).astype(I32)])
    c = np.arange(2 * LANES)
    sel = np.zeros((2 * LANES, 2 * LANES), np.float32)
    sel[c, (c % 2) * LANES + c // 2] = 1.0
    tile = lambda i, te, fi, nt: (jnp.minimum(i, nt[0] - 1), 0)
    out_tile = lambda i, te, fi, nt: (i, 0)
    wsel = lambda i, te, fi, nt: (te[jnp.minimum(i, nt[0] - 1)], 0, 0)
    wspec = lambda r, c: pl.BlockSpec((1, r, c), wsel)
    return pl.pallas_call(
        _expert_body,
        grid_spec=pltpu.PrefetchScalarGridSpec(
            num_scalar_prefetch=3, grid=(rows // EXPERT_TILE,),
            in_specs=[pl.BlockSpec((EXPERT_TILE, D_MODEL), tile),
                      wspec(D_MODEL, 2 * D_FF), pl.BlockSpec((2 * LANES, 2 * LANES), lambda *_: (0, 0)),
                      wspec(1, D_FF), wspec(1, D_FF), wspec(D_FF, D_MODEL), wspec(1, D_MODEL)],
            out_specs=pl.BlockSpec((EXPERT_TILE, D_MODEL), out_tile),
            scratch_shapes=[pltpu.VMEM((D_MODEL, D_FF), BF16)] * 2 + [pltpu.VMEM((D_FF, D_MODEL), BF16)]),
        out_shape=jax.ShapeDtypeStruct((rows, D_MODEL), F32),
        compiler_params=_cparams(("arbitrary",)),
        name="moe_experts",
    )(tile_expert, first, n_tiles, xs, wp["w1"], jnp.asarray(sel, BF16), wp["b1g"], wp["b1l"],
      wp["w2"], wp["b2"])


def _combine_body(blk0, loff_ref, gstart_ref, nrows_ref, ldest_ref, meta_ref, h_ref, ys_ref, o_ref,
                  yloc, sems):
    i = pl.program_id(0)
    b = i + blk0
    slot = i % 2
    tb = h_ref.shape[0]

    def gather(bb, s):
        def group(e, carry):
            j = bb * N_EXPERTS + e
            _start_rows(ys_ref, gstart_ref[j], yloc.at[s], loff_ref[j], nrows_ref[j], sems.at[s])
            return carry
        lax.fori_loop(0, N_EXPERTS, group, 0)

    @pl.when(i == 0)
    def _():
        yloc[...] = jnp.zeros_like(yloc)
        gather(b, slot)

    @pl.when(i + 1 < pl.num_programs(0))
    def _():
        gather(b + 1, 1 - slot)

    j_last = b * N_EXPERTS + N_EXPERTS - 1
    total = loff_ref[j_last] + nrows_ref[j_last]
    cols = lax.broadcasted_iota(I32, (tb, LOCAL_ROWS), 1)
    ld = ldest_ref[...]
    meta = meta_ref[...]
    gmat = jnp.zeros((tb, LOCAL_ROWS), F32)
    for kk in range(TOP_K):
        gate = meta[:, 2 * TOP_K + kk:2 * TOP_K + kk + 1]
        gmat = jnp.where(cols == ld[:, kk:kk + 1], gate, gmat)
    _wait_rows(ys_ref, yloc.at[slot], total, sems.at[slot])
    o_ref[...] = h_ref[...] + _dot(gmat.astype(BF16), yloc[slot].astype(BF16))


def _combine(tables, blk0, ldest, meta, h2, ys):
    n = h2.shape[0]
    tb = DISPATCH_BLOCK
    row = lambda w: pl.BlockSpec((tb, w), lambda i, *_: (i, 0))
    return pl.pallas_call(
        functools.partial(_combine_body, blk0),
        grid_spec=pltpu.PrefetchScalarGridSpec(
            num_scalar_prefetch=3, grid=(n // tb,),
            in_specs=[row(TOP_K), row(LANES), row(D_MODEL), pl.BlockSpec(memory_space=pltpu.HBM)],
            out_specs=row(D_MODEL),
            scratch_shapes=[pltpu.VMEM((2, LOCAL_ROWS, D_MODEL), F32), pltpu.SemaphoreType.DMA((2,))]),
        out_shape=jax.ShapeDtypeStruct((n, D_MODEL), F32),
        compiler_params=_cparams(("arbitrary",)),
        name="moe_combine",
    )(*tables, ldest, meta, h2, ys)


def _moe(groups, wp):
    tb = DISPATCH_BLOCK
    sizes = [g[0].shape[0] for g in groups]
    assert all(s % tb == 0 for s in sizes)
    n_tok = sum(sizes)
    nb = n_tok // tb
    rows = n_tok * TOP_K + nb * N_EXPERTS * GROUP_ROWS + N_EXPERTS * EXPERT_TILE
    n_tiles_max = rows // EXPERT_TILE

    eidx = jnp.concatenate([g[2][:, 0:TOP_K] for g in groups]).astype(I32)
    rank = jnp.concatenate([g[2][:, TOP_K:2 * TOP_K] for g in groups]).astype(I32)
    hot = eidx[:, :, None] == jnp.arange(N_EXPERTS, dtype=I32)
    cnt = jnp.sum(hot.reshape(nb, tb * TOP_K, N_EXPERTS), axis=1, dtype=I32)
    npad = (cnt + GROUP_ROWS - 1) // GROUP_ROWS * GROUP_ROWS
    loff = jnp.cumsum(npad, axis=1) - npad
    gsize = jnp.sum(npad, axis=0)
    gpad = (gsize + EXPERT_TILE - 1) // EXPERT_TILE * EXPERT_TILE
    ends = jnp.cumsum(gpad)
    gstart = (ends - gpad)[None, :] + jnp.cumsum(npad, axis=0) - npad
    before = jnp.cumsum(cnt, axis=0) - cnt
    n_tiles = (ends[-1] // EXPERT_TILE).reshape(1)
    tile_ids = jnp.arange(n_tiles_max, dtype=I32)
    tile_expert = jnp.minimum(
        jnp.sum((ends // EXPERT_TILE)[None, :] <= tile_ids[:, None], axis=1), N_EXPERTS - 1).astype(I32)

    base = jnp.repeat(loff - before, tb, axis=0)
    ldest = rank + jnp.sum(jnp.where(hot, base[:, None, :], 0), axis=2)
    ldest_t = ldest.T
    tables = (loff.reshape(-1), gstart.reshape(-1).astype(I32), npad.reshape(-1))

    xs = _pad_rows(((ends - gpad) + gsize).astype(I32), (gpad - gsize).astype(I32),
                   n_tiles, rows)
    starts = np.cumsum([0] + sizes[:-1])
    for (_, xn, _), t0, sz in zip(groups, starts, sizes):
        xs = _dispatch(tables, int(t0) // tb, ldest_t[:, t0:t0 + sz], xn, xs)
    ys = _experts(tile_expert, n_tiles, xs, wp)
    return [_combine(tables, int(t0) // tb, ldest[t0:t0 + sz], meta, h2, ys)
            for (h2, _, meta), t0, sz in zip(groups, starts, sizes)]


def _prep_weights(norm1_g, w_in, a_q_norm_g, a_k_norm_g, gla_w_alpha, gla_b_alpha, gla_out_norm_g,
                  w_out, norm2_g, mem_norm_g, mem_w_q, mem_w_k, mem_w_v, mem_q_norm_g,
                  mem_k_norm_g, mem_w_o, norm3_g, router_w, router_b, exp_w1, exp_b1, exp_w2,
                  exp_b2):
    main = 3 * A_WIDTH + 2 * B_QK_WIDTH + 2 * B_V_WIDTH
    w_lr = jnp.pad(w_in[:, main:], ((0, 0), (0, LANES - GATE_RANK)))
    rw = jnp.pad(router_w, ((0, 0), (0, LANES - N_EXPERTS)))
    rwh = rw.astype(BF16)
    return {
        "g1": norm1_g[None],
        "w_in": jnp.concatenate([w_in[:, :main], w_lr], axis=1).astype(BF16),
        "gq": jnp.tile(a_q_norm_g, A_HEADS)[None],
        "gk": jnp.tile(a_k_norm_g, A_HEADS)[None],
        "bd64": _block_diag(A_WIDTH, A_HEAD_DIM, 1.0 / A_HEAD_DIM, BF16),
        "wa": jnp.pad(gla_w_alpha, ((0, LANES - GATE_RANK), (0, 0))).astype(BF16),
        "ba": gla_b_alpha[None],
        "bones64": _block_diag(B_QK_WIDTH, B_KEY_DIM, 1.0, BF16),
        "bd128": _block_diag(B_V_WIDTH, B_VAL_DIM, 1.0 / B_VAL_DIM, BF16),
        "gout": jnp.tile(gla_out_norm_g, B_HEADS)[None],
        "woa": w_out[:A_WIDTH].astype(BF16),
        "wob": w_out[A_WIDTH:].astype(BF16),
        "g2": norm2_g[None],
        "gmem": mem_norm_g[None],
        "wmq": mem_w_q.astype(BF16),
        "wmk": mem_w_k.astype(BF16),
        "wmv": mem_w_v.astype(BF16),
        "gmq": jnp.tile(mem_q_norm_g, MEM_HEADS)[None],
        "gmk": jnp.tile(mem_k_norm_g, MEM_HEADS)[None],
        "wmo": mem_w_o.astype(BF16),
        "g3": norm3_g[None],
        "rw": jnp.concatenate([rwh, (rw - rwh.astype(F32)).astype(BF16)], axis=1),
        "rb": jnp.pad(router_b, (0, LANES - N_EXPERTS), constant_values=-1e30)[None],
        "w1": exp_w1,
        "b1g": exp_b1[:, None, 0::2],
        "b1l": exp_b1[:, None, 1::2],
        "w2": exp_w2,
        "b2": exp_b2[:, None, :],
    }


def _layer(xp, xs, mem_prompt, cache_k, cache_v, state_gla, cache_mk, cache_mv, wp):
    bsz, seq, _ = xp.shape
    db, t_new, _ = xs.shape
    w_p = min(max(w for w, _ in DILATED_GROUPS), seq)

    xpf = xp.reshape(bsz * seq, D_MODEL)
    q, k, v, k_last, v_last, bq, bk, bv, gate, la = _in_proj(xpf, wp, BF16, seq, w_p)
    oa = _swa_prompt(q, k, v, bsz, seq)
    s0 = jnp.zeros((bsz, B_QK_WIDTH, B_VAL_DIM), F32)
    ob, s_p = _gla(bq, bk, la, bv, gate, s0, wp, seq, GLA_CHUNK, GLA_SUB, GLA_STEP)
    mk, mv = _mem_kv(mem_prompt.reshape(bsz * N_MEM, D_MODEL), wp)
    cnt0 = jnp.zeros((1, LANES), F32)
    h2_p, xn_p, meta_p, cnt = _post(xpf, oa, ob, mk.reshape(bsz, N_MEM, MEM_WIDTH),
                                    mv.reshape(bsz, N_MEM, MEM_WIDTH), cnt0, wp,
                                    POST_TILE, 1, seq // POST_TILE)

    xsf = xs.reshape(db * t_new, D_MODEL)
    q, _, _, ks, vs, bq, bk, bv, gate, la = _in_proj(xsf, wp, F32, t_new, t_new)
    new3 = lambda a: a.reshape(db, t_new, A_WIDTH)
    oa_s, nk, nv = _swa_sample(new3(q.transpose(1, 0, 2)), new3(ks), new3(vs), cache_k, cache_v)
    ob_s, s_s = _gla(bq, bk, la, bv, gate, state_gla, wp, t_new, t_new, t_new, t_new)
    h2_s, xn_s, meta_s, cnt = _post(xsf, oa_s.reshape(db * t_new, A_WIDTH), ob_s, cache_mk,
                                    cache_mv, cnt, wp, SAMPLE_SEQS * t_new, SAMPLE_SEQS, 1)

    y_p, y_s = _moe([(h2_p, xn_p, meta_p), (h2_s, xn_s, meta_s)], wp)
    return (y_p.reshape(bsz, seq, D_MODEL), y_s.reshape(db, t_new, D_MODEL),
            k_last.reshape(bsz, w_p, A_WIDTH), v_last.reshape(bsz, w_p, A_WIDTH),
            s_p, mk, mv, nk, nv, s_s)


def kernel(x_prompt, x_sample, mem_prompt, cache_swa_k, cache_swa_v, state_gla, cache_mem_k, cache_mem_v, norm1_g, w_in, a_q_norm_g, a_k_norm_g, gla_w_alpha, gla_b_alpha, gla_out_norm_g, w_out, norm2_g, mem_norm_g, mem_w_q, mem_w_k, mem_w_v, mem_q_norm_g, mem_k_norm_g, mem_w_o, norm3_g, router_w, router_b, exp_w1, exp_b1, exp_w2, exp_b2):
    depth = w_in.shape[0]
    bsz = x_prompt.shape[0]
    db, w_buf = cache_swa_k.shape[1], cache_swa_k.shape[2]
    xp, xs = x_prompt, x_sample
    per_layer = []
    for l in range(depth):
        wp = _prep_weights(
            norm1_g[l], w_in[l], a_q_norm_g[l], a_k_norm_g[l], gla_w_alpha[l], gla_b_alpha[l],
            gla_out_norm_g[l], w_out[l], norm2_g[l], mem_norm_g[l], mem_w_q[l], mem_w_k[l],
            mem_w_v[l], mem_q_norm_g[l], mem_k_norm_g[l], mem_w_o[l], norm3_g[l], router_w[l],
            router_b[l], exp_w1[l], exp_b1[l], exp_w2[l], exp_b2[l])
        xp, xs, kp, vp, s_p, mk, mv, nk, nv, s_s = _layer(
            xp, xs, mem_prompt,
            cache_swa_k[l].reshape(db, w_buf, A_WIDTH).transpose(0, 2, 1),
            cache_swa_v[l].reshape(db, w_buf, A_WIDTH).transpose(0, 2, 1),
            state_gla[l].reshape(db, B_QK_WIDTH, B_VAL_DIM),
            cache_mem_k[l].reshape(db, N_MEM, MEM_WIDTH), cache_mem_v[l].reshape(db, N_MEM, MEM_WIDTH),
            wp)
        w_p = kp.shape[1]
        per_layer.append((
            kp.reshape(bsz, w_p, A_HEADS, A_HEAD_DIM), vp.reshape(bsz, w_p, A_HEADS, A_HEAD_DIM),
            s_p.reshape(bsz, B_HEADS, B_KEY_DIM, B_VAL_DIM),
            mk.reshape(bsz, N_MEM, MEM_HEADS, MEM_HEAD_DIM), mv.reshape(bsz, N_MEM, MEM_HEADS, MEM_HEAD_DIM),
            nk.transpose(0, 2, 1).reshape(db, w_buf, A_HEADS, A_HEAD_DIM),
            nv.transpose(0, 2, 1).reshape(db, w_buf, A_HEADS, A_HEAD_DIM),
            s_s.reshape(db, B_HEADS, B_KEY_DIM, B_VAL_DIM)))
    stacked = [jnp.stack(t) for t in zip(*per_layer)]
    return (xp, xs, *stacked)
```

```python
import functools

import jax
import jax.numpy as jnp
import numpy as np
from jax import lax
from jax.experimental import pallas as pl
from jax.experimental.pallas import tpu as pltpu

F32 = jnp.float32
BF16 = jnp.bfloat16
I32 = jnp.int32

EPS = 1e-6
D_MODEL = 1024
A_HEADS, A_HEAD_DIM, A_WIDTH = 8, 64, 512
A_SCALE = A_HEAD_DIM ** -0.5
LOG2E = 1.4426950408889634
DILATED_GROUPS = ((128, 1), (512, 4), (2048, 16))
SPAN = 128
B_HEADS, B_KEY_DIM, B_VAL_DIM = 4, 64, 128
B_QK_WIDTH, B_V_WIDTH = 256, 512
GATE_RANK = 16
GATE_TEMP = 16.0
N_MEM, MEM_HEADS, MEM_HEAD_DIM, MEM_WIDTH = 256, 4, 128, 512
N_EXPERTS, TOP_K, D_FF = 32, 4, 1024
SWIGLU_ALPHA, SWIGLU_LIMIT = 1.702, 7.0

LANES = 128
VMEM_LIMIT = 56 * 1024 * 1024

PROJ_TILE = 512
SWA_BLOCK = 2048
SWA_UNROLL = 8
GLA_CHUNK, GLA_SUB, GLA_STEP = 64, 8, 512
POST_TILE = 512
SAMPLE_SEQS = 8
EXPERT_TILE = 512
DISPATCH_BLOCK = 512
GROUP_ROWS = 8
LOCAL_ROWS = DISPATCH_BLOCK * TOP_K + N_EXPERTS * GROUP_ROWS


def _cparams(sem, vmem=VMEM_LIMIT, **kw):
    return pltpu.CompilerParams(dimension_semantics=sem, vmem_limit_bytes=vmem, **kw)


def _full(shape):
    n = len(shape)
    return pl.BlockSpec(shape, lambda *_: (0,) * n)


def _rms(x, g):
    ms = jnp.mean(x * x, axis=-1, keepdims=True)
    return x * lax.rsqrt(ms + EPS) * g


def _dot(a, b):
    return jnp.dot(a, b, preferred_element_type=F32)


def _dot_nt(a, b):
    return lax.dot_general(a, b, (((1,), (1,)), ((), ())), preferred_element_type=F32)


def _block_diag(n, blk, val, dtype):
    i = np.arange(n)
    return jnp.asarray(np.where((i[:, None] // blk) == (i[None, :] // blk), val, 0.0), dtype)


def _in_proj_body(x_ref, g1_ref, w_ref, gq_ref, gk_ref, bd_ref, wa_ref, ba_ref,
                  q_ref, k_ref, v_ref, kc_ref, vc_ref, bq_ref, bk_ref, bv_ref, gate_ref, la_ref):
    xn = _rms(x_ref[...], g1_ref[...]).astype(BF16)
    bd = bd_ref[...]

    def proj(lo, hi):
        return _dot(xn, w_ref[:, lo:hi])

    def headnorm(z, g):
        ms = _dot((z * z).astype(BF16), bd)
        return z * lax.rsqrt(ms + EPS) * g

    def put_groups(ref, z):
        for p in range(A_WIDTH // LANES):
            ref[p] = z[:, p * LANES:(p + 1) * LANES]

    put_groups(q_ref, headnorm(proj(0, 512), gq_ref[...]) * (A_SCALE * LOG2E))
    k = headnorm(proj(512, 1024), gk_ref[...])
    v = proj(1024, 1536)
    put_groups(k_ref, k)
    put_groups(v_ref, v)
    kc_ref[...] = k
    vc_ref[...] = v
    bq_ref[...] = proj(1536, 1792) * (B_KEY_DIM ** -0.5)
    bk_ref[...] = proj(1792, 2048)
    bv_ref[...] = proj(2048, 2560).astype(bv_ref.dtype)
    br = proj(2560, 3072)
    gate_ref[...] = (br * jax.nn.sigmoid(br)).astype(gate_ref.dtype)
    lr = proj(3072, 3200).astype(BF16)
    pre = _dot(lr, wa_ref[...]) + ba_ref[...]
    log_sig = jnp.minimum(pre, 0.0) - jnp.log1p(jnp.exp(-jnp.abs(pre)))
    la_ref[...] = log_sig * (1.0 / GATE_TEMP)


def _in_proj(x, wp, wide_dtype, seq, keep):
    n = x.shape[0]
    tm = min(PROJ_TILE, n)
    row = lambda w: pl.BlockSpec((tm, w), lambda i: (i, 0))
    ngrp = A_WIDTH // LANES
    grp = pl.BlockSpec((ngrp, tm, LANES), lambda i: (0, i, 0))
    if keep == seq:
        kept = row(A_WIDTH)
    else:
        tps, kt = seq // tm, keep // tm
        assert tps * tm == seq and kt * tm == keep
        kept = pl.BlockSpec((tm, A_WIDTH),
                            lambda i: ((i // tps) * kt + jnp.maximum(i % tps - (tps - kt), 0), 0))
    outs = [(256, F32), (256, F32), (512, wide_dtype), (512, wide_dtype), (256, F32)]
    return pl.pallas_call(
        _in_proj_body,
        grid=(n // tm,),
        in_specs=[row(D_MODEL), _full((1, D_MODEL)), _full((D_MODEL, 3200)), _full((1, 512)),
                  _full((1, 512)), _full((512, 512)), _full((LANES, 256)), _full((1, 256))],
        out_specs=[grp] * 3 + [kept] * 2 + [row(w) for w, _ in outs],
        out_shape=[jax.ShapeDtypeStruct((ngrp, n, LANES), F32)] * 3
        + [jax.ShapeDtypeStruct((n // seq * keep, A_WIDTH), F32)] * 2
        + [jax.ShapeDtypeStruct((n, w), dt) for w, dt in outs],
        compiler_params=_cparams(("arbitrary",)),
        name="in_proj",
    )(x, wp["g1"], wp["w_in"], wp["gq"], wp["gk"], wp["bd64"], wp["wa"], wp["ba"])


def _unroll_for(trips):
    return max(u for u in range(1, SWA_UNROLL + 1) if trips % u == 0)


def _ds(start, size, stride):
    return pl.ds(start, size) if stride == 1 else pl.ds(start, size, stride=stride)


def _swa_prompt_body(q_ref, kp_ref, kc_ref, vp_ref, vc_ref, o_ref, m_s, l_s, acc_s, k_keep, v_keep):
    i = pl.program_id(2)
    qb = SWA_BLOCK

    @pl.when(i == 0)
    def _():
        k_keep[...] = jnp.zeros_like(k_keep)
        v_keep[...] = jnp.zeros_like(v_keep)

    lane = lax.broadcasted_iota(I32, (SPAN, LANES), 1)
    lo_mask = lane < A_HEAD_DIM
    jq = lax.broadcasted_iota(I32, (SPAN, 2 * SPAN), 0)
    jk = lax.broadcasted_iota(I32, (SPAN, 2 * SPAN), 1)
    dist = jq + SPAN - jk
    band = (dist >= 0) & (dist <= SPAN)
    cur_half = jk >= SPAN

    def attend(qp, kp, vp, valid, rows, first_group):
        vp1 = jnp.concatenate([vp, jnp.ones_like(vp)], axis=1)
        res = []
        for hh in range(2):
            msk = lo_mask if hh == 0 else jnp.logical_not(lo_mask)
            qm = jnp.where(msk, qp, 0.0).astype(BF16)
            s = jnp.where(valid, _dot_nt(qm, kp), -jnp.inf)
            m = jnp.max(s, axis=1, keepdims=True)
            res.append((m, _dot(jnp.exp2(s - m).astype(BF16), vp1)))
        m_new = jnp.where(lo_mask, res[0][0], res[1][0])
        l_new = jnp.where(lo_mask, res[0][1][:, LANES:], res[1][1][:, LANES:])
        o_new = jnp.where(lo_mask, res[0][1][:, :LANES], res[1][1][:, :LANES])
        if first_group:
            m_s[rows, :] = m_new
            l_s[rows, :] = l_new
            acc_s[rows, :] = o_new
        else:
            m_old = m_s[rows, :]
            m = jnp.maximum(m_old, m_new)
            a_old = jnp.exp2(m_old - m)
            a_new = jnp.exp2(m_new - m)
            m_s[rows, :] = m
            l_s[rows, :] = l_s[rows, :] * a_old + l_new * a_new
            acc_s[rows, :] = acc_s[rows, :] * a_old + o_new * a_new

    for gi, (_, dil) in enumerate(reversed(DILATED_GROUPS)):
        unit = dil * SPAN
        nblk = qb // unit
        first = gi == 0

        def head_block(r, carry, dil=dil, unit=unit, first=first, nblk=nblk):
            rows = _ds(r, SPAN, dil)
            valid = band & (cur_half | (i > 0))
            if nblk == 1:
                keep = pl.ds(pl.multiple_of(r * SPAN, SPAN), SPAN)
                k_cur = kc_ref[rows, :].astype(BF16)
                v_cur = vc_ref[rows, :].astype(BF16)
                ks = jnp.concatenate([k_keep[keep, :], k_cur], axis=0)
                vs = jnp.concatenate([v_keep[keep, :], v_cur], axis=0)
                k_keep[keep, :] = k_cur
                v_keep[keep, :] = v_cur
            else:
                prev = _ds(qb - unit + r, SPAN, dil)
                ks = jnp.concatenate([kp_ref[prev, :], kc_ref[rows, :]], axis=0).astype(BF16)
                vs = jnp.concatenate([vp_ref[prev, :], vc_ref[rows, :]], axis=0).astype(BF16)
            attend(q_ref[rows, :], ks, vs, valid, rows, first)
            return carry

        lax.fori_loop(0, dil, head_block, 0, unroll=_unroll_for(dil))

        if nblk > 1:
            def tail_block(idx, carry, dil=dil, unit=unit, first=first):
                n = idx // dil + 1
                r = idx % dil
                start = unit * n + r
                rows = _ds(start, SPAN, dil)
                keys = _ds(start - unit, 2 * SPAN, dil)
                attend(q_ref[rows, :], kc_ref[keys, :].astype(BF16), vc_ref[keys, :].astype(BF16),
                       band, rows, first)
                return carry

            lax.fori_loop(0, (nblk - 1) * dil, tail_block, 0, unroll=_unroll_for((nblk - 1) * dil))

    o_ref[...] = (acc_s[...] / l_s[...]).astype(o_ref.dtype)


def _swa_prompt(q, k, v, bsz, seq):
    qb = SWA_BLOCK
    nb = seq // qb
    cur = pl.BlockSpec((None, qb, LANES), lambda p, b, i: (p, b * nb + i, 0))
    prev = pl.BlockSpec((None, qb, LANES), lambda p, b, i: (p, b * nb + jnp.maximum(i - 1, 0), 0))
    return pl.pallas_call(
        _swa_prompt_body,
        grid=(A_WIDTH // LANES, bsz, nb),
        in_specs=[cur, prev, cur, prev, cur],
        out_specs=pl.BlockSpec((qb, LANES), lambda p, b, i: (b * nb + i, p)),
        out_shape=jax.ShapeDtypeStruct((bsz * seq, A_WIDTH), BF16),
        scratch_shapes=[pltpu.VMEM((qb, LANES), F32)] * 3 + [pltpu.VMEM((qb, LANES), BF16)] * 2,
        compiler_params=_cparams(("parallel", "parallel", "arbitrary")),
        name="swa_prompt",
    )(q, k, k, v, v)


def _swa_sample_body(q_ref, kn_ref, vn_ref, ck_ref, cv_ref, c1_ref, c2_ref,
                     o_ref, nk_ref, nv_ref):
    t_new = q_ref.shape[1]
    w_buf = ck_ref.shape[2]
    ck, cv = ck_ref[0], cv_ref[0]
    kn, vn = kn_ref[0], vn_ref[0]
    tail_lane = lax.broadcasted_iota(I32, (A_WIDTH, LANES), 1) >= LANES - t_new

    def shift_in(old, new, out_ref):
        moved = pltpu.roll(old, w_buf - t_new, 1)
        new_t = jnp.transpose(jnp.concatenate([jnp.zeros((LANES - t_new, A_WIDTH), F32), new], axis=0))
        out_ref[0, :, :w_buf - LANES] = moved[:, :w_buf - LANES]
        out_ref[0, :, w_buf - LANES:] = jnp.where(tail_lane, new_t, moved[:, w_buf - LANES:])

    shift_in(ck, kn, nk_ref)
    shift_in(cv, vn, nv_ref)

    q = q_ref[0]
    lane = lax.broadcasted_iota(I32, (t_new, LANES), 1)
    lo_mask = lane < A_HEAD_DIM
    c1, c2 = c1_ref[...], c2_ref[...]
    outs = []
    for p in range(A_WIDTH // LANES):
        sl = slice(p * LANES, (p + 1) * LANES)
        qp = q[:, sl]
        qblk = jnp.concatenate([jnp.where(lo_mask, qp, 0.0), jnp.where(lo_mask, 0.0, qp)],
                               axis=0).astype(BF16)
        s1 = jnp.where(c1 > 0, _dot(qblk, ck[sl, :].astype(BF16)), -jnp.inf)
        s2 = jnp.where(c2 > 0, _dot_nt(qblk, kn[:, sl].astype(BF16)), -jnp.inf)
        m = jnp.maximum(jnp.max(s1, axis=1, keepdims=True), jnp.max(s2, axis=1, keepdims=True))
        p1 = c1 * jnp.exp2(s1 - m)
        p2 = c2 * jnp.exp2(s2 - m)
        l = jnp.sum(p1, axis=1, keepdims=True) + jnp.sum(p2, axis=1, keepdims=True)
        o = (_dot_nt(p1.astype(BF16), cv[sl, :].astype(BF16))
             + _dot(p2.astype(BF16), vn[:, sl].astype(BF16))) / l
        outs.append(jnp.where(lo_mask, o[:t_new], o[t_new:]))
    o_ref[0] = jnp.concatenate(outs, axis=1).astype(o_ref.dtype)


def _sample_multiplicity(t_new, w_buf):
    t = np.arange(t_new)[:, None]
    e = np.arange(w_buf + t_new)[None, :]
    d = w_buf + t - e
    c = np.zeros(d.shape, np.float32)
    for window, dil in DILATED_GROUPS:
        c += ((d >= 0) & (d % dil == 0) & (d <= window)).astype(np.float32)
    c = np.concatenate([c, c], axis=0)
    return jnp.asarray(c[:, :w_buf]), jnp.asarray(c[:, w_buf:])


def _swa_sample(q, kn, vn, cache_k, cache_v):
    db, t_new, w = q.shape
    w_buf = cache_k.shape[2]
    assert w_buf >= max(win for win, _ in DILATED_GROUPS) and t_new % 8 == 0 and t_new <= LANES
    c1, c2 = _sample_multiplicity(t_new, w_buf)
    new = pl.BlockSpec((1, t_new, w), lambda b: (b, 0, 0))
    cache = pl.BlockSpec((1, w, w_buf), lambda b: (b, 0, 0))
    return pl.pallas_call(
        _swa_sample_body,
        grid=(db,),
        in_specs=[new, new, new, cache, cache, _full(c1.shape), _full(c2.shape)],
        out_specs=[new, cache, cache],
        out_shape=[jax.ShapeDtypeStruct((db, t_new, w), BF16),
                   jax.ShapeDtypeStruct(cache_k.shape, cache_k.dtype),
                   jax.ShapeDtypeStruct(cache_v.shape, cache_v.dtype)],
        compiler_params=_cparams(("parallel",)),
        name="swa_sample",
    )(q, kn, vn, cache_k, cache_v, c1, c2)


def _gla_body(chunk, sub, nch, q_ref, k_ref, g_ref, v_ref, gate_ref, s0_ref, tril_ref, dmask_ref,
              bones_ref, sbm_ref, bd_ref, gout_ref, o_ref, sfin_ref, sbd):
    j = pl.program_id(1)
    sbm = sbm_ref[...]
    nsub = chunk // sub
    pad = B_KEY_DIM - chunk

    @pl.when(j == 0)
    def _():
        s0 = s0_ref[0]
        sbd[...] = jnp.concatenate([s0] * B_HEADS, axis=1) * sbm

    row = lax.broadcasted_iota(I32, (chunk, 1), 0)
    sub_id = row // sub
    lane_w = lax.broadcasted_iota(I32, (chunk, LANES * max(nsub - 1, 1)), 1)
    lo_w = (lane_w % LANES) < B_KEY_DIM

    def one_chunk(c, carry):
        off = pl.multiple_of(c * chunk, chunk)
        rows = pl.ds(off, chunk)
        q, k, g = q_ref[rows, :], k_ref[rows, :], g_ref[rows, :]
        v = v_ref[rows, :].astype(F32)
        g1 = g.astype(BF16)
        r1 = g - g1.astype(F32)
        g2 = r1.astype(BF16)
        g3 = (r1 - g2.astype(F32)).astype(BF16)
        tril = tril_ref[...]
        b = _dot(tril, g1) + _dot(tril, g2) + _dot(tril, g3)
        b_last = b[chunk - 1:chunk, :]
        state = sbd[...]

        o = _dot((q * jnp.exp(b)).astype(BF16), state.astype(BF16))

        bones = bones_ref[...]
        att = _dot((q * k).astype(BF16), bones) * dmask_ref[0]
        gate = jnp.exp(g)
        decay = gate
        for d in range(1, sub):
            if d > 1:
                decay = decay * pltpu.roll(gate, d - 1, 0)
            w = q * pltpu.roll(k, d, 0) * decay
            att = att + _dot(w.astype(BF16), bones) * dmask_ref[d]

        if nsub > 1:
            qx, kx = [], []
            for i in range(1, nsub):
                r_i = b[sub * i - 1:sub * i, :]
                qx.append(jnp.where(sub_id == i, q * jnp.exp(jnp.minimum(b - r_i, 0.0)), 0.0))
                kx.append(jnp.where(sub_id < i, k * jnp.exp(jnp.minimum(r_i - b, 0.0)), 0.0))
            parts = []
            for p in range(B_QK_WIDTH // LANES):
                sl = slice(p * LANES, (p + 1) * LANES)
                qp = jnp.concatenate([x[:, sl] for x in qx], axis=1)
                kp = jnp.concatenate([x[:, sl] for x in kx], axis=1).astype(BF16)
                zero = jnp.zeros_like(kp)
                lhs = jnp.concatenate([jnp.where(lo_w, qp, 0.0), jnp.where(lo_w, 0.0, qp)],
                                      axis=1).astype(BF16)
                rhs = jnp.concatenate([jnp.concatenate([kp, zero], axis=1),
                                       jnp.concatenate([zero, kp], axis=1)], axis=0)
                parts.append(_dot_nt(lhs, rhs))
            att = att + jnp.concatenate(parts, axis=1)

        if pad:
            vrow = jnp.concatenate([v, jnp.zeros((pad, B_V_WIDTH), F32)], axis=0)
        else:
            vrow = v
        vbd = (jnp.concatenate([vrow] * B_HEADS, axis=0) * sbm).astype(BF16)
        o = o + _dot(att.astype(BF16), vbd)

        ke = (k * jnp.exp(b_last - b)).astype(BF16)
        upd = lax.dot_general(ke, v.astype(BF16), (((0,), (0,)), ((), ())),
                              preferred_element_type=F32)
        dec = jnp.transpose(jnp.broadcast_to(jnp.exp(b_last), (8, B_QK_WIDTH)))[:, 0:1]
        sbd[...] = (state * dec + upd) * sbm

        ms = _dot((o * o).astype(BF16), bd_ref[...])
        on = o * lax.rsqrt(ms + EPS) * gout_ref[...] * gate_ref[rows, :].astype(F32)
        o_ref[rows, :] = on.astype(o_ref.dtype)
        return carry

    lax.fori_loop(0, nch, one_chunk, 0, unroll=2 if nch % 2 == 0 else 1)

    @pl.when(j == pl.num_programs(1) - 1)
    def _():
        s = sbd[...]
        sfin_ref[0] = jnp.concatenate(
            [s[h * B_KEY_DIM:(h + 1) * B_KEY_DIM, h * B_VAL_DIM:(h + 1) * B_VAL_DIM]
             for h in range(B_HEADS)], axis=0)


def _gla_consts(chunk, sub):
    t = np.arange(chunk)
    tril = (t[:, None] >= t[None, :]).astype(np.float32)
    lane = np.arange(B_QK_WIDTH)
    dmask = np.zeros((sub, chunk, B_QK_WIDTH), np.float32)
    for d in range(sub):
        ok = (t % sub) >= d
        dmask[d] = ((lane[None, :] % B_KEY_DIM) == (t[:, None] - d)) & ok[:, None]
    r = np.arange(B_QK_WIDTH)[:, None] // B_KEY_DIM
    c = np.arange(B_V_WIDTH)[None, :] // B_VAL_DIM
    sbm = (r == c).astype(np.float32)
    return jnp.asarray(tril, BF16), jnp.asarray(dmask), jnp.asarray(sbm)


def _gla(q, k, g, v, gate, s0, wp, length, chunk, sub, step):
    n = q.shape[0]
    bsz = n // length
    assert chunk == sub or chunk == B_KEY_DIM
    tril, dmask, sbm = _gla_consts(chunk, sub)
    nstep = length // step
    row = lambda w: pl.BlockSpec((step, w), lambda b, j: (b * nstep + j, 0))
    st = pl.BlockSpec((1, B_QK_WIDTH, B_VAL_DIM), lambda b, j: (b, 0, 0))
    return pl.pallas_call(
        functools.partial(_gla_body, chunk, sub, step // chunk),
        grid=(bsz, nstep),
        in_specs=[row(256), row(256), row(256), row(512), row(512), st, _full(tril.shape),
                  _full(dmask.shape), _full((256, 256)), _full(sbm.shape), _full((512, 512)),
                  _full((1, 512))],
        out_specs=[row(512), st],
        out_shape=[jax.ShapeDtypeStruct((n, B_V_WIDTH), BF16),
                   jax.ShapeDtypeStruct(s0.shape, F32)],
        scratch_shapes=[pltpu.VMEM((B_QK_WIDTH, B_V_WIDTH), F32)],
        compiler_params=_cparams(("parallel", "arbitrary")),
        name="gla",
    )(q, k, g, v, gate, s0, tril, dmask, wp["bones64"], sbm, wp["bd128"], wp["gout"])


def _head_rms(z, g, scale=1.0):
    parts = []
    for h in range(MEM_HEADS):
        zh = z[:, h * LANES:(h + 1) * LANES]
        parts.append(zh * lax.rsqrt(jnp.mean(zh * zh, axis=-1, keepdims=True) + EPS))
    return jnp.concatenate(parts, axis=1) * (g * scale)


def _mem_kv_body(m_ref, gn_ref, wk_ref, wv_ref, gk_ref, mk_ref, mv_ref):
    mn = _rms(m_ref[...], gn_ref[...]).astype(BF16)
    mk_ref[...] = _head_rms(_dot(mn, wk_ref[...]), gk_ref[...])
    mv_ref[...] = _dot(mn, wv_ref[...])


def _mem_kv(mem, wp):
    n = mem.shape[0]
    tm = 256
    row = lambda w: pl.BlockSpec((tm, w), lambda i: (i, 0))
    return pl.pallas_call(
        _mem_kv_body,
        grid=(n // tm,),
        in_specs=[row(D_MODEL), _full((1, D_MODEL)), _full((D_MODEL, 512)), _full((D_MODEL, 512)),
                  _full((1, 512))],
        out_specs=[row(512), row(512)],
        out_shape=[jax.ShapeDtypeStruct((n, 512), F32)] * 2,
        compiler_params=_cparams(("parallel",)),
        name="mem_kv",
    )(mem, wp["gmem"], wp["wmk"], wp["wmv"], wp["gmk"])


def _post_body(nseq, x_ref, oa_ref, ob_ref, woa_ref, wob_ref, g2_ref, wq_ref, gmq_ref, mk_ref,
               mv_ref, wo_ref, g3_ref, rw_ref, rb_ref, cnt0_ref, tri_ref,
               h2_ref, xn_ref, meta_ref, cnt_ref, carry):
    tm = x_ref.shape[0]

    @pl.when(pl.program_id(0) == 0)
    def _():
        carry[...] = cnt0_ref[...]

    h = x_ref[...] + _dot(oa_ref[...], woa_ref[...]) + _dot(ob_ref[...], wob_ref[...])
    hn = _rms(h, g2_ref[...]).astype(BF16)
    qm = _head_rms(_dot(hn, wq_ref[...]), gmq_ref[...], MEM_HEAD_DIM ** -0.5).astype(BF16)
    nk = nseq * N_MEM
    mk = mk_ref[...].reshape(nk, MEM_WIDTH)
    mv = mv_ref[...].reshape(nk, MEM_WIDTH)
    if nseq > 1:
        rt = lax.broadcasted_iota(I32, (tm, nk), 0) // (tm // nseq)
        ct = lax.broadcasted_iota(I32, (tm, nk), 1) // N_MEM
        same = rt == ct
    outs = []
    for hd in range(MEM_HEADS):
        sl = slice(hd * LANES, (hd + 1) * LANES)
        s = _dot_nt(qm[:, sl], mk[:, sl].astype(BF16))
        if nseq > 1:
            s = jnp.where(same, s, -jnp.inf)
        m = jnp.max(s, axis=1, keepdims=True)
        pr = jnp.exp(s - m)
        l = jnp.sum(pr, axis=1, keepdims=True)
        outs.append(_dot(pr.astype(BF16), mv[:, sl].astype(BF16)) / l)
    h2 = h + _dot(jnp.concatenate(outs, axis=1).astype(BF16), wo_ref[...])
    h2_ref[...] = h2
    xn = _rms(h2, g3_ref[...])
    xn_ref[...] = xn

    x1 = xn.astype(BF16)
    x2 = (xn - x1.astype(F32)).astype(BF16)
    prod = _dot(jnp.concatenate([x1, x2], axis=0), rw_ref[...])
    logits = (prod[:tm, :LANES] + prod[:tm, LANES:] + prod[tm:, :LANES] + prod[tm:, LANES:]
              + rb_ref[...])
    lane = lax.broadcasted_iota(I32, (tm, LANES), 1)
    vals, idxs, hots = [], [], []
    work = logits
    for _ in range(TOP_K):
        m = jnp.max(work, axis=1, keepdims=True)
        idx = jnp.min(jnp.where(work == m, lane, LANES), axis=1, keepdims=True)
        hot = lane == idx
        vals.append(m)
        idxs.append(idx)
        hots.append(hot)
        work = jnp.where(hot, -jnp.inf, work)
    exps = [jnp.exp(v - vals[0]) for v in vals]
    den = exps[0] + exps[1] + exps[2] + exps[3]

    sel = (hots[0] | hots[1] | hots[2] | hots[3]).astype(F32)
    before = _dot(tri_ref[...], sel.astype(BF16)) + carry[...]
    carry[...] = carry[...] + jnp.sum(sel, axis=0, keepdims=True)
    cnt_ref[...] = carry[...]

    meta = jnp.zeros((tm, LANES), F32)
    for kk in range(TOP_K):
        rank = jnp.sum(jnp.where(hots[kk], before, 0.0), axis=1, keepdims=True)
        meta = jnp.where(lane == kk, idxs[kk].astype(F32), meta)
        meta = jnp.where(lane == TOP_K + kk, rank, meta)
        meta = jnp.where(lane == 2 * TOP_K + kk, exps[kk] / den, meta)
    meta_ref[...] = meta


def _post(x, oa, ob, mk, mv, cnt0, wp, tm, nseq, tiles_per_mem):
    n = x.shape[0]
    row = lambda w: pl.BlockSpec((tm, w), lambda i: (i, 0))
    mem = pl.BlockSpec((nseq, N_MEM, MEM_WIDTH), lambda i: (i // tiles_per_mem, 0, 0))
    tri = jnp.asarray(np.tril(np.ones((tm, tm), np.float32), -1), BF16)
    return pl.pallas_call(
        functools.partial(_post_body, nseq),
        grid=(n // tm,),
        in_specs=[row(D_MODEL), row(512), row(512), _full((512, D_MODEL)), _full((512, D_MODEL)),
                  _full((1, D_MODEL)), _full((D_MODEL, 512)), _full((1, 512)), mem, mem,
                  _full((512, D_MODEL)), _full((1, D_MODEL)), _full((D_MODEL, 2 * LANES)),
                  _full((1, LANES)), _full((1, LANES)), _full((tm, tm))],
        out_specs=[row(D_MODEL), row(D_MODEL), row(LANES), _full((1, LANES))],
        out_shape=[jax.ShapeDtypeStruct((n, D_MODEL), F32), jax.ShapeDtypeStruct((n, D_MODEL), F32),
                   jax.ShapeDtypeStruct((n, LANES), F32), jax.ShapeDtypeStruct((1, LANES), F32)],
        scratch_shapes=[pltpu.VMEM((1, LANES), F32)],
        compiler_params=_cparams(("arbitrary",)),
        name="post",
    )(x, oa, ob, wp["woa"], wp["wob"], wp["g2"], wp["wmq"], wp["gmq"], mk, mv, wp["wmo"],
      wp["g3"], wp["rw"], wp["rb"], cnt0, tri)


def _start_rows(src, src_row, dst, dst_row, n_rows, sem):
    s = pl.multiple_of(jnp.asarray(src_row, I32), GROUP_ROWS)
    d = pl.multiple_of(jnp.asarray(dst_row, I32), GROUP_ROWS)
    n_rows = pl.multiple_of(jnp.asarray(n_rows, I32), GROUP_ROWS)

    @pl.when(n_rows > 0)
    def _():
        pltpu.make_async_copy(src.at[pl.ds(s, n_rows)], dst.at[pl.ds(d, n_rows)], sem).start()

    return n_rows


def _wait_rows(like_src, like_dst, n_rows, sem):
    n_rows = pl.multiple_of(n_rows, GROUP_ROWS)

    @pl.when(n_rows > 0)
    def _():
        pltpu.make_async_copy(like_src.at[pl.ds(0, n_rows)], like_dst.at[pl.ds(0, n_rows)], sem).wait()


def _pad_rows_body(pstart_ref, prows_ref, nt_ref, xs_ref, zeros, sem):
    zeros[...] = jnp.zeros_like(zeros)
    n_tiles_max = xs_ref.shape[0] // EXPERT_TILE

    def expert_tail(e, total):
        return total + _start_rows(zeros, 0, xs_ref, pstart_ref[e], prows_ref[e], sem)

    def unused_tile(t, total):
        return total + _start_rows(zeros, 0, xs_ref, t * EXPERT_TILE, EXPERT_TILE, sem)

    total = lax.fori_loop(0, N_EXPERTS, expert_tail, jnp.int32(0))
    total = lax.fori_loop(nt_ref[0], n_tiles_max, unused_tile, total)
    _wait_rows(xs_ref, xs_ref, total, sem)


def _pad_rows(pad_start, pad_units, n_tiles, rows):
    return pl.pallas_call(
        _pad_rows_body,
        grid_spec=pltpu.PrefetchScalarGridSpec(
            num_scalar_prefetch=3, grid=(1,),
            in_specs=[],
            out_specs=pl.BlockSpec(memory_space=pltpu.HBM),
            scratch_shapes=[pltpu.VMEM((EXPERT_TILE, D_MODEL), F32), pltpu.SemaphoreType.DMA]),
        out_shape=jax.ShapeDtypeStruct((rows, D_MODEL), F32),
        compiler_params=_cparams(("arbitrary",)),
        name="moe_pad_rows",
    )(pad_start, pad_units, n_tiles)


def _dispatch_body(blk0, loff_ref, gstart_ref, nrows_ref, ldest_ref, x_ref, xs_in_ref, xs_ref,
                   xloc, sems):
    del xs_in_ref
    i = pl.program_id(0)
    b = i + blk0
    slot = i % 2
    tb = x_ref.shape[0]
    rows = lax.broadcasted_iota(I32, (LOCAL_ROWS, tb), 0)
    ld = ldest_ref[...]
    hot = rows == ld[0:1, :]
    for kk in range(1, TOP_K):
        hot = hot | (rows == ld[kk:kk + 1, :])
    xloc[slot] = _dot(jnp.where(hot, 1.0, 0.0).astype(BF16), x_ref[...].astype(BF16))

    def block_rows(bb):
        j = bb * N_EXPERTS + N_EXPERTS - 1
        return loff_ref[j] + nrows_ref[j]

    @pl.when(i > 0)
    def _():
        _wait_rows(xloc.at[1 - slot], xs_ref, block_rows(b - 1), sems.at[1 - slot])

    def group(e, carry):
        j = b * N_EXPERTS + e
        _start_rows(xloc.at[slot], loff_ref[j], xs_ref, gstart_ref[j], nrows_ref[j], sems.at[slot])
        return carry

    lax.fori_loop(0, N_EXPERTS, group, 0)

    @pl.when(i == pl.num_programs(0) - 1)
    def _():
        _wait_rows(xloc.at[slot], xs_ref, block_rows(b), sems.at[slot])


def _dispatch(tables, blk0, ldest_t, xn, xs):
    n = xn.shape[0]
    tb = DISPATCH_BLOCK
    idx = lambda i, *_: (i, 0)
    return pl.pallas_call(
        functools.partial(_dispatch_body, blk0),
        grid_spec=pltpu.PrefetchScalarGridSpec(
            num_scalar_prefetch=3, grid=(n // tb,),
            in_specs=[pl.BlockSpec((TOP_K, tb), lambda i, *_: (0, i)),
                      pl.BlockSpec((tb, D_MODEL), idx),
                      pl.BlockSpec(memory_space=pltpu.HBM)],
            out_specs=pl.BlockSpec(memory_space=pltpu.HBM),
            scratch_shapes=[pltpu.VMEM((2, LOCAL_ROWS, D_MODEL), F32), pltpu.SemaphoreType.DMA((2,))]),
        out_shape=jax.ShapeDtypeStruct(xs.shape, xs.dtype),
        input_output_aliases={5: 0},
        compiler_params=_cparams(("arbitrary",)),
        name="moe_dispatch",
    )(*tables, ldest_t, xn, xs)


def _expert_body(te_ref, first_ref, nt_ref, x_ref, w1_ref, sel_ref, b1g_ref, b1l_ref, w2_ref, b2_ref,
                 y_ref, w1g, w1l, w2):
    i = pl.program_id(0)
    live = i < nt_ref[0]

    @pl.when(live & (first_ref[i] == 1))
    def _():
        sel = sel_ref[...]
        for j in range(D_FF // LANES):
            z = _dot(w1_ref[0, :, 2 * LANES * j:2 * LANES * (j + 1)].astype(BF16), sel)
            w1g[:, LANES * j:LANES * (j + 1)] = z[:, :LANES].astype(BF16)
            w1l[:, LANES * j:LANES * (j + 1)] = z[:, LANES:].astype(BF16)
        w2[...] = w2_ref[0].astype(BF16)

    @pl.when(live)
    def _():
        x = x_ref[...].astype(BF16)
        glu = jnp.minimum(_dot(x, w1g[...]) + b1g_ref[0], SWIGLU_LIMIT)
        lin = jnp.clip(_dot(x, w1l[...]) + b1l_ref[0], -SWIGLU_LIMIT, SWIGLU_LIMIT)
        act = glu * jax.nn.sigmoid(SWIGLU_ALPHA * glu) * (lin + 1.0)
        y_ref[...] = _dot(act.astype(BF16), w2[...]) + b2_ref[0]

    @pl.when(jnp.logical_not(live))
    def _():
        y_ref[...] = jnp.zeros_like(y_ref)


def _experts(tile_expert, n_tiles, xs, wp):
    rows = xs.shape[0]
    first = jnp.concatenate([jnp.ones((1,), I32),
                             (tile_expert[1:] != tile_expert[:-1]).astype(I32)])
    c = np.arange(2 * LANES)
    sel = np.zeros((2 * LANES, 2 * LANES), np.float32)
    sel[c, (c % 2) * LANES + c // 2] = 1.0
    tile = lambda i, te, fi, nt: (jnp.minimum(i, nt[0] - 1), 0)
    out_tile = lambda i, te, fi, nt: (i, 0)
    wsel = lambda i, te, fi, nt: (te[jnp.minimum(i, nt[0] - 1)], 0, 0)
    wspec = lambda r, c: pl.BlockSpec((1, r, c), wsel)
    return pl.pallas_call(
        _expert_body,
        grid_spec=pltpu.PrefetchScalarGridSpec(
            num_scalar_prefetch=3, grid=(rows // EXPERT_TILE,),
            in_specs=[pl.BlockSpec((EXPERT_TILE, D_MODEL), tile),
                      wspec(D_MODEL, 2 * D_FF), pl.BlockSpec((2 * LANES, 2 * LANES), lambda *_: (0, 0)),
                      wspec(1, D_FF), wspec(1, D_FF), wspec(D_FF, D_MODEL), wspec(1, D_MODEL)],
            out_specs=pl.BlockSpec((EXPERT_TILE, D_MODEL), out_tile),
            scratch_shapes=[pltpu.VMEM((D_MODEL, D_FF), BF16)] * 2 + [pltpu.VMEM((D_FF, D_MODEL), BF16)]),
        out_shape=jax.ShapeDtypeStruct((rows, D_MODEL), F32),
        compiler_params=_cparams(("arbitrary",)),
        name="moe_experts",
    )(tile_expert, first, n_tiles, xs, wp["w1"], jnp.asarray(sel, BF16), wp["b1g"], wp["b1l"],
      wp["w2"], wp["b2"])


def _combine_body(blk0, loff_ref, gstart_ref, nrows_ref, ldest_ref, meta_ref, h_ref, ys_ref, o_ref,
                  yloc, sems):
    i = pl.program_id(0)
    b = i + blk0
    slot = i % 2
    tb = h_ref.shape[0]

    def gather(bb, s):
        def group(e, carry):
            j = bb * N_EXPERTS + e
            _start_rows(ys_ref, gstart_ref[j], yloc.at[s], loff_ref[j], nrows_ref[j], sems.at[s])
            return carry
        lax.fori_loop(0, N_EXPERTS, group, 0)

    @pl.when(i == 0)
    def _():
        yloc[...] = jnp.zeros_like(yloc)
        gather(b, slot)

    @pl.when(i + 1 < pl.num_programs(0))
    def _():
        gather(b + 1, 1 - slot)

    j_last = b * N_EXPERTS + N_EXPERTS - 1
    total = loff_ref[j_last] + nrows_ref[j_last]
    cols = lax.broadcasted_iota(I32, (tb, LOCAL_ROWS), 1)
    ld = ldest_ref[...]
    meta = meta_ref[...]
    gmat = jnp.zeros((tb, LOCAL_ROWS), F32)
    for kk in range(TOP_K):
        gate = meta[:, 2 * TOP_K + kk:2 * TOP_K + kk + 1]
        gmat = jnp.where(cols == ld[:, kk:kk + 1], gate, gmat)
    _wait_rows(ys_ref, yloc.at[slot], total, sems.at[slot])
    o_ref[...] = h_ref[...] + _dot(gmat.astype(BF16), yloc[slot].astype(BF16))


def _combine(tables, blk0, ldest, meta, h2, ys):
    n = h2.shape[0]
    tb = DISPATCH_BLOCK
    row = lambda w: pl.BlockSpec((tb, w), lambda i, *_: (i, 0))
    return pl.pallas_call(
        functools.partial(_combine_body, blk0),
        grid_spec=pltpu.PrefetchScalarGridSpec(
            num_scalar_prefetch=3, grid=(n // tb,),
            in_specs=[row(TOP_K), row(LANES), row(D_MODEL), pl.BlockSpec(memory_space=pltpu.HBM)],
            out_specs=row(D_MODEL),
            scratch_shapes=[pltpu.VMEM((2, LOCAL_ROWS, D_MODEL), F32), pltpu.SemaphoreType.DMA((2,))]),
        out_shape=jax.ShapeDtypeStruct((n, D_MODEL), F32),
        compiler_params=_cparams(("arbitrary",)),
        name="moe_combine",
    )(*tables, ldest, meta, h2, ys)


def _moe(groups, wp):
    tb = DISPATCH_BLOCK
    sizes = [g[0].shape[0] for g in groups]
    assert all(s % tb == 0 for s in sizes)
    n_tok = sum(sizes)
    nb = n_tok // tb
    rows = n_tok * TOP_K + nb * N_EXPERTS * GROUP_ROWS + N_EXPERTS * EXPERT_TILE
    n_tiles_max = rows // EXPERT_TILE

    eidx = jnp.concatenate([g[2][:, 0:TOP_K] for g in groups]).astype(I32)
    rank = jnp.concatenate([g[2][:, TOP_K:2 * TOP_K] for g in groups]).astype(I32)
    hot = eidx[:, :, None] == jnp.arange(N_EXPERTS, dtype=I32)
    cnt = jnp.sum(hot.reshape(nb, tb * TOP_K, N_EXPERTS), axis=1, dtype=I32)
    npad = (cnt + GROUP_ROWS - 1) // GROUP_ROWS * GROUP_ROWS
    loff = jnp.cumsum(npad, axis=1) - npad
    gsize = jnp.sum(npad, axis=0)
    gpad = (gsize + EXPERT_TILE - 1) // EXPERT_TILE * EXPERT_TILE
    ends = jnp.cumsum(gpad)
    gstart = (ends - gpad)[None, :] + jnp.cumsum(npad, axis=0) - npad
    before = jnp.cumsum(cnt, axis=0) - cnt
    n_tiles = (ends[-1] // EXPERT_TILE).reshape(1)
    tile_ids = jnp.arange(n_tiles_max, dtype=I32)
    tile_expert = jnp.minimum(
        jnp.sum((ends // EXPERT_TILE)[None, :] <= tile_ids[:, None], axis=1), N_EXPERTS - 1).astype(I32)

    base = jnp.repeat(loff - before, tb, axis=0)
    ldest = rank + jnp.sum(jnp.where(hot, base[:, None, :], 0), axis=2)
    ldest_t = ldest.T
    tables = (loff.reshape(-1), gstart.reshape(-1).astype(I32), npad.reshape(-1))

    xs = _pad_rows(((ends - gpad) + gsize).astype(I32), (gpad - gsize).astype(I32),
                   n_tiles, rows)
    starts = np.cumsum([0] + sizes[:-1])
    for (_, xn, _), t0, sz in zip(groups, starts, sizes):
        xs = _dispatch(tables, int(t0) // tb, ldest_t[:, t0:t0 + sz], xn, xs)
    ys = _experts(tile_expert, n_tiles, xs, wp)
    return [_combine(tables, int(t0) // tb, ldest[t0:t0 + sz], meta, h2, ys)
            for (h2, _, meta), t0, sz in zip(groups, starts, sizes)]


def _prep_weights(norm1_g, w_in, a_q_norm_g, a_k_norm_g, gla_w_alpha, gla_b_alpha, gla_out_norm_g,
                  w_out, norm2_g, mem_norm_g, mem_w_q, mem_w_k, mem_w_v, mem_q_norm_g,
                  mem_k_norm_g, mem_w_o, norm3_g, router_w, router_b, exp_w1, exp_b1, exp_w2,
                  exp_b2):
    main = 3 * A_WIDTH + 2 * B_QK_WIDTH + 2 * B_V_WIDTH
    w_lr = jnp.pad(w_in[:, main:], ((0, 0), (0, LANES - GATE_RANK)))
    rw = jnp.pad(router_w, ((0, 0), (0, LANES - N_EXPERTS)))
    rwh = rw.astype(BF16)
    return {
        "g1": norm1_g[None],
        "w_in": jnp.concatenate([w_in[:, :main], w_lr], axis=1).astype(BF16),
        "gq": jnp.tile(a_q_norm_g, A_HEADS)[None],
        "gk": jnp.tile(a_k_norm_g, A_HEADS)[None],
        "bd64": _block_diag(A_WIDTH, A_HEAD_DIM, 1.0 / A_HEAD_DIM, BF16),
        "wa": jnp.pad(gla_w_alpha, ((0, LANES - GATE_RANK), (0, 0))).astype(BF16),
        "ba": gla_b_alpha[None],
        "bones64": _block_diag(B_QK_WIDTH, B_KEY_DIM, 1.0, BF16),
        "bd128": _block_diag(B_V_WIDTH, B_VAL_DIM, 1.0 / B_VAL_DIM, BF16),
        "gout": jnp.tile(gla_out_norm_g, B_HEADS)[None],
        "woa": w_out[:A_WIDTH].astype(BF16),
        "wob": w_out[A_WIDTH:].astype(BF16),
        "g2": norm2_g[None],
        "gmem": mem_norm_g[None],
        "wmq": mem_w_q.astype(BF16),
        "wmk": mem_w_k.astype(BF16),
        "wmv": mem_w_v.astype(BF16),
        "gmq": jnp.tile(mem_q_norm_g, MEM_HEADS)[None],
        "gmk": jnp.tile(mem_k_norm_g, MEM_HEADS)[None],
        "wmo": mem_w_o.astype(BF16),
        "g3": norm3_g[None],
        "rw": jnp.concatenate([rwh, (rw - rwh.astype(F32)).astype(BF16)], axis=1),
        "rb": jnp.pad(router_b, (0, LANES - N_EXPERTS), constant_values=-1e30)[None],
        "w1": exp_w1,
        "b1g": exp_b1[:, None, 0::2],
        "b1l": exp_b1[:, None, 1::2],
        "w2": exp_w2,
        "b2": exp_b2[:, None, :],
    }


def _layer(xp, xs, mem_prompt, cache_k, cache_v, state_gla, cache_mk, cache_mv, wp):
    bsz, seq, _ = xp.shape
    db, t_new, _ = xs.shape
    w_p = min(max(w for w, _ in DILATED_GROUPS), seq)

    xpf = xp.reshape(bsz * seq, D_MODEL)
    q, k, v, k_last, v_last, bq, bk, bv, gate, la = _in_proj(xpf, wp, BF16, seq, w_p)
    oa = _swa_prompt(q, k, v, bsz, seq)
    s0 = jnp.zeros((bsz, B_QK_WIDTH, B_VAL_DIM), F32)
    ob, s_p = _gla(bq, bk, la, bv, gate, s0, wp, seq, GLA_CHUNK, GLA_SUB, GLA_STEP)
    mk, mv = _mem_kv(mem_prompt.reshape(bsz * N_MEM, D_MODEL), wp)
    cnt0 = jnp.zeros((1, LANES), F32)
    h2_p, xn_p, meta_p, cnt = _post(xpf, oa, ob, mk.reshape(bsz, N_MEM, MEM_WIDTH),
                                    mv.reshape(bsz, N_MEM, MEM_WIDTH), cnt0, wp,
                                    POST_TILE, 1, seq // POST_TILE)

    xsf = xs.reshape(db * t_new, D_MODEL)
    q, _, _, ks, vs, bq, bk, bv, gate, la = _in_proj(xsf, wp, F32, t_new, t_new)
    new3 = lambda a: a.reshape(db, t_new, A_WIDTH)
    oa_s, nk, nv = _swa_sample(new3(q.transpose(1, 0, 2)), new3(ks), new3(vs), cache_k, cache_v)
    ob_s, s_s = _gla(bq, bk, la, bv, gate, state_gla, wp, t_new, t_new, t_new, t_new)
    h2_s, xn_s, meta_s, cnt = _post(xsf, oa_s.reshape(db * t_new, A_WIDTH), ob_s, cache_mk,
                                    cache_mv, cnt, wp, SAMPLE_SEQS * t_new, SAMPLE_SEQS, 1)

    y_p, y_s = _moe([(h2_p, xn_p, meta_p), (h2_s, xn_s, meta_s)], wp)
    return (y_p.reshape(bsz, seq, D_MODEL), y_s.reshape(db, t_new, D_MODEL),
            k_last.reshape(bsz, w_p, A_WIDTH), v_last.reshape(bsz, w_p, A_WIDTH),
            s_p, mk, mv, nk, nv, s_s)


def kernel(x_prompt, x_sample, mem_prompt, cache_swa_k, cache_swa_v, state_gla, cache_mem_k, cache_mem_v, norm1_g, w_in, a_q_norm_g, a_k_norm_g, gla_w_alpha, gla_b_alpha, gla_out_norm_g, w_out, norm2_g, mem_norm_g, mem_w_q, mem_w_k, mem_w_v, mem_q_norm_g, mem_k_norm_g, mem_w_o, norm3_g, router_w, router_b, exp_w1, exp_b1, exp_w2, exp_b2):
    depth = w_in.shape[0]
    bsz = x_prompt.shape[0]
    db, w_buf = cache_swa_k.shape[1], cache_swa_k.shape[2]
    xp, xs = x_prompt, x_sample
    per_layer = []
    for l in range(depth):
        wp = _prep_weights(
            norm1_g[l], w_in[l], a_q_norm_g[l], a_k_norm_g[l], gla_w_alpha[l], gla_b_alpha[l],
            gla_out_norm_g[l], w_out[l], norm2_g[l], mem_norm_g[l], mem_w_q[l], mem_w_k[l],
            mem_w_v[l], mem_q_norm_g[l], mem_k_norm_g[l], mem_w_o[l], norm3_g[l], router_w[l],
            router_b[l], exp_w1[l], exp_b1[l], exp_w2[l], exp_b2[l])
        xp, xs, kp, vp, s_p, mk, mv, nk, nv, s_s = _layer(
            xp, xs, mem_prompt,
            cache_swa_k[l].reshape(db, w_buf, A_WIDTH).transpose(0, 2, 1),
            cache_swa_v[l].reshape(db, w_buf, A_WIDTH).transpose(0, 2, 1),
            state_gla[l].reshape(db, B_QK_WIDTH, B_VAL_DIM),
            cache_mem_k[l].reshape(db, N_MEM, MEM_WIDTH), cache_mem_v[l].reshape(db, N_MEM, MEM_WIDTH),
            wp)
        w_p = kp.shape[1]
        per_layer.append((
            kp.reshape(bsz, w_p, A_HEADS, A_HEAD_DIM), vp.reshape(bsz, w_p, A_HEADS, A_HEAD_DIM),
            s_p.reshape(bsz, B_HEADS, B_KEY_DIM, B_VAL_DIM),
            mk.reshape(bsz, N_MEM, MEM_HEADS, MEM_HEAD_DIM), mv.reshape(bsz, N_MEM, MEM_HEADS, MEM_HEAD_DIM),
            nk.transpose(0, 2, 1).reshape(db, w_buf, A_HEADS, A_HEAD_DIM),
            nv.transpose(0, 2, 1).reshape(db, w_buf, A_HEADS, A_HEAD_DIM),
            s_s.reshape(db, B_HEADS, B_KEY_DIM, B_VAL_DIM)))
    stacked = [jnp.stack(t) for t in zip(*per_layer)]
    return (xp, xs, *stacked)
```

```python
import functools

import jax
import jax.numpy as jnp
import numpy as np
from jax import lax
from jax.experimental import pallas as pl
from jax.experimental.pallas import tpu as pltpu

F32 = jnp.float32
BF16 = jnp.bfloat16
I32 = jnp.int32

EPS = 1e-6
D_MODEL = 1024
A_HEADS, A_HEAD_DIM, A_WIDTH = 8, 64, 512
A_SCALE = A_HEAD_DIM ** -0.5
LOG2E = 1.4426950408889634
DILATED_GROUPS = ((128, 1), (512, 4), (2048, 16))
SPAN = 128
B_HEADS, B_KEY_DIM, B_VAL_DIM = 4, 64, 128
B_QK_WIDTH, B_V_WIDTH = 256, 512
GATE_RANK = 16
GATE_TEMP = 16.0
N_MEM, MEM_HEADS, MEM_HEAD_DIM, MEM_WIDTH = 256, 4, 128, 512
N_EXPERTS, TOP_K, D_FF = 32, 4, 1024
SWIGLU_ALPHA, SWIGLU_LIMIT = 1.702, 7.0

LANES = 128
VMEM_LIMIT = 56 * 1024 * 1024

PROJ_TILE = 512
SWA_BLOCK = 2048
SWA_UNROLL = 8
GLA_CHUNK, GLA_SUB, GLA_STEP = 64, 8, 512
POST_TILE = 1024
SAMPLE_SEQS = 8
EXPERT_TILE = 512
DISPATCH_BLOCK = 512
GROUP_ROWS = 8
LOCAL_ROWS = DISPATCH_BLOCK * TOP_K + N_EXPERTS * GROUP_ROWS


def _cparams(sem, vmem=VMEM_LIMIT, **kw):
    return pltpu.CompilerParams(dimension_semantics=sem, vmem_limit_bytes=vmem, **kw)


def _full(shape):
    n = len(shape)
    return pl.BlockSpec(shape, lambda *_: (0,) * n)


def _rms(x, g):
    ms = jnp.mean(x * x, axis=-1, keepdims=True)
    return x * lax.rsqrt(ms + EPS) * g


def _dot(a, b):
    return jnp.dot(a, b, preferred_element_type=F32)


def _dot_nt(a, b):
    return lax.dot_general(a, b, (((1,), (1,)), ((), ())), preferred_element_type=F32)


def _block_diag(n, blk, val, dtype):
    i = np.arange(n)
    return jnp.asarray(np.where((i[:, None] // blk) == (i[None, :] // blk), val, 0.0), dtype)


def _in_proj_body(keep_transposed, x_ref, g1_ref, w_ref, gq_ref, gk_ref, bd_ref, wa_ref, ba_ref,
                  q_ref, k_ref, v_ref, kc_ref, vc_ref, bq_ref, bk_ref, bv_ref, gate_ref, la_ref):
    xn = _rms(x_ref[...], g1_ref[...]).astype(BF16)
    bd = bd_ref[...]

    def proj(lo, hi):
        return _dot(xn, w_ref[:, lo:hi])

    def headnorm(z, g):
        ms = _dot((z * z).astype(BF16), bd)
        return z * lax.rsqrt(ms + EPS) * g

    def put_groups(ref, z):
        for p in range(A_WIDTH // LANES):
            ref[p] = z[:, p * LANES:(p + 1) * LANES]

    put_groups(q_ref, headnorm(proj(0, 512), gq_ref[...]) * (A_SCALE * LOG2E))
    k = headnorm(proj(512, 1024), gk_ref[...])
    v = proj(1024, 1536)
    put_groups(k_ref, k)
    put_groups(v_ref, v)
    kc_ref[...] = jnp.transpose(k) if keep_transposed else k
    vc_ref[...] = jnp.transpose(v) if keep_transposed else v
    bq_ref[...] = proj(1536, 1792) * (B_KEY_DIM ** -0.5)
    bk_ref[...] = proj(1792, 2048)
    bv_ref[...] = proj(2048, 2560).astype(bv_ref.dtype)
    br = proj(2560, 3072)
    gate_ref[...] = (br * jax.nn.sigmoid(br)).astype(gate_ref.dtype)
    lr = proj(3072, 3200).astype(BF16)
    pre = _dot(lr, wa_ref[...]) + ba_ref[...]
    log_sig = jnp.minimum(pre, 0.0) - jnp.log1p(jnp.exp(-jnp.abs(pre)))
    la_ref[...] = log_sig * (1.0 / GATE_TEMP)


def _in_proj(x, wp, wide_dtype, seq, keep):
    n = x.shape[0]
    tm = min(PROJ_TILE, n)
    row = lambda w: pl.BlockSpec((tm, w), lambda i: (i, 0))
    ngrp = A_WIDTH // LANES
    grp = pl.BlockSpec((ngrp, tm, LANES), lambda i: (0, i, 0))
    if keep == seq:
        kept = row(A_WIDTH)
        kept_shape = (n, A_WIDTH)
    else:
        tps, kt = seq // tm, keep // tm
        assert tps * tm == seq and kt * tm == keep
        kept = pl.BlockSpec((None, A_WIDTH, tm),
                            lambda i: (i // tps, 0, jnp.maximum(i % tps - (tps - kt), 0)))
        kept_shape = (n // seq, A_WIDTH, keep)
    outs = [(256, F32), (256, F32), (512, wide_dtype), (512, wide_dtype), (256, F32)]
    return pl.pallas_call(
        functools.partial(_in_proj_body, keep != seq),
        grid=(n // tm,),
        in_specs=[row(D_MODEL), _full((1, D_MODEL)), _full((D_MODEL, 3200)), _full((1, 512)),
                  _full((1, 512)), _full((512, 512)), _full((LANES, 256)), _full((1, 256))],
        out_specs=[grp] * 3 + [kept] * 2 + [row(w) for w, _ in outs],
        out_shape=[jax.ShapeDtypeStruct((ngrp, n, LANES), F32)] * 3
        + [jax.ShapeDtypeStruct(kept_shape, F32)] * 2
        + [jax.ShapeDtypeStruct((n, w), dt) for w, dt in outs],
        compiler_params=_cparams(("arbitrary",)),
        name="in_proj",
    )(x, wp["g1"], wp["w_in"], wp["gq"], wp["gk"], wp["bd64"], wp["wa"], wp["ba"])


def _unroll_for(trips):
    return max(u for u in range(1, SWA_UNROLL + 1) if trips % u == 0)


def _ds(start, size, stride):
    return pl.ds(start, size) if stride == 1 else pl.ds(start, size, stride=stride)


def _swa_prompt_body(q_ref, kp_ref, kc_ref, vp_ref, vc_ref, o_ref, m_s, l_s, acc_s, k_keep, v_keep):
    i = pl.program_id(2)
    qb = SWA_BLOCK

    @pl.when(i == 0)
    def _():
        k_keep[...] = jnp.zeros_like(k_keep)
        v_keep[...] = jnp.zeros_like(v_keep)

    lane = lax.broadcasted_iota(I32, (SPAN, LANES), 1)
    lo_mask = lane < A_HEAD_DIM
    jq = lax.broadcasted_iota(I32, (SPAN, 2 * SPAN), 0)
    jk = lax.broadcasted_iota(I32, (SPAN, 2 * SPAN), 1)
    dist = jq + SPAN - jk
    band = (dist >= 0) & (dist <= SPAN)
    cur_half = jk >= SPAN

    def attend(qp, kp, vp, valid, rows, first_group):
        vp1 = jnp.concatenate([vp, jnp.ones_like(vp)], axis=1)
        res = []
        for hh in range(2):
            msk = lo_mask if hh == 0 else jnp.logical_not(lo_mask)
            qm = jnp.where(msk, qp, 0.0).astype(BF16)
            s = jnp.where(valid, _dot_nt(qm, kp), -jnp.inf)
            m = jnp.max(s, axis=1, keepdims=True)
            res.append((m, _dot(jnp.exp2(s - m).astype(BF16), vp1)))
        m_new = jnp.where(lo_mask, res[0][0], res[1][0])
        l_new = jnp.where(lo_mask, res[0][1][:, LANES:], res[1][1][:, LANES:])
        o_new = jnp.where(lo_mask, res[0][1][:, :LANES], res[1][1][:, :LANES])
        if first_group:
            m_s[rows, :] = m_new
            l_s[rows, :] = l_new
            acc_s[rows, :] = o_new
        else:
            m_old = m_s[rows, :]
            m = jnp.maximum(m_old, m_new)
            a_old = jnp.exp2(m_old - m)
            a_new = jnp.exp2(m_new - m)
            m_s[rows, :] = m
            l_s[rows, :] = l_s[rows, :] * a_old + l_new * a_new
            acc_s[rows, :] = acc_s[rows, :] * a_old + o_new * a_new

    for gi, (_, dil) in enumerate(reversed(DILATED_GROUPS)):
        unit = dil * SPAN
        nblk = qb // unit
        first = gi == 0

        def head_block(r, carry, dil=dil, unit=unit, first=first, nblk=nblk):
            rows = _ds(r, SPAN, dil)
            valid = band & (cur_half | (i > 0))
            if nblk == 1:
                keep = pl.ds(pl.multiple_of(r * SPAN, SPAN), SPAN)
                k_cur = kc_ref[rows, :].astype(BF16)
                v_cur = vc_ref[rows, :].astype(BF16)
                ks = jnp.concatenate([k_keep[keep, :], k_cur], axis=0)
                vs = jnp.concatenate([v_keep[keep, :], v_cur], axis=0)
                k_keep[keep, :] = k_cur
                v_keep[keep, :] = v_cur
            else:
                prev = _ds(qb - unit + r, SPAN, dil)
                ks = jnp.concatenate([kp_ref[prev, :], kc_ref[rows, :]], axis=0).astype(BF16)
                vs = jnp.concatenate([vp_ref[prev, :], vc_ref[rows, :]], axis=0).astype(BF16)
            attend(q_ref[rows, :], ks, vs, valid, rows, first)
            return carry

        lax.fori_loop(0, dil, head_block, 0, unroll=_unroll_for(dil))

        if nblk > 1:
            def tail_block(idx, carry, dil=dil, unit=unit, first=first):
                n = idx // dil + 1
                r = idx % dil
                start = unit * n + r
                rows = _ds(start, SPAN, dil)
                keys = _ds(start - unit, 2 * SPAN, dil)
                attend(q_ref[rows, :], kc_ref[keys, :].astype(BF16), vc_ref[keys, :].astype(BF16),
                       band, rows, first)
                return carry

            lax.fori_loop(0, (nblk - 1) * dil, tail_block, 0, unroll=_unroll_for((nblk - 1) * dil))

    o_ref[...] = (acc_s[...] / l_s[...]).astype(o_ref.dtype)


def _swa_prompt(q, k, v, bsz, seq):
    qb = SWA_BLOCK
    nb = seq // qb
    cur = pl.BlockSpec((None, qb, LANES), lambda p, b, i: (p, b * nb + i, 0))
    prev = pl.BlockSpec((None, qb, LANES), lambda p, b, i: (p, b * nb + jnp.maximum(i - 1, 0), 0))
    return pl.pallas_call(
        _swa_prompt_body,
        grid=(A_WIDTH // LANES, bsz, nb),
        in_specs=[cur, prev, cur, prev, cur],
        out_specs=pl.BlockSpec((qb, LANES), lambda p, b, i: (b * nb + i, p)),
        out_shape=jax.ShapeDtypeStruct((bsz * seq, A_WIDTH), BF16),
        scratch_shapes=[pltpu.VMEM((qb, LANES), F32)] * 3 + [pltpu.VMEM((qb, LANES), BF16)] * 2,
        compiler_params=_cparams(("parallel", "parallel", "arbitrary")),
        name="swa_prompt",
    )(q, k, k, v, v)


def _swa_sample_body(q_ref, kn_ref, vn_ref, ck_ref, cv_ref, c1_ref, c2_ref,
                     o_ref, nk_ref, nv_ref):
    t_new = q_ref.shape[1]
    w_buf = ck_ref.shape[2]
    ck, cv = ck_ref[0], cv_ref[0]
    kn, vn = kn_ref[0], vn_ref[0]
    tail_lane = lax.broadcasted_iota(I32, (A_WIDTH, LANES), 1) >= LANES - t_new

    def shift_in(old, new, out_ref):
        moved = pltpu.roll(old, w_buf - t_new, 1)
        new_t = jnp.transpose(jnp.concatenate([jnp.zeros((LANES - t_new, A_WIDTH), F32), new], axis=0))
        out_ref[0, :, :w_buf - LANES] = moved[:, :w_buf - LANES]
        out_ref[0, :, w_buf - LANES:] = jnp.where(tail_lane, new_t, moved[:, w_buf - LANES:])

    shift_in(ck, kn, nk_ref)
    shift_in(cv, vn, nv_ref)

    q = q_ref[0]
    lane = lax.broadcasted_iota(I32, (t_new, LANES), 1)
    lo_mask = lane < A_HEAD_DIM
    c1, c2 = c1_ref[...], c2_ref[...]
    outs = []
    for p in range(A_WIDTH // LANES):
        sl = slice(p * LANES, (p + 1) * LANES)
        qp = q[:, sl]
        qblk = jnp.concatenate([jnp.where(lo_mask, qp, 0.0), jnp.where(lo_mask, 0.0, qp)],
                               axis=0).astype(BF16)
        s1 = jnp.where(c1 > 0, _dot(qblk, ck[sl, :].astype(BF16)), -jnp.inf)
        s2 = jnp.where(c2 > 0, _dot_nt(qblk, kn[:, sl].astype(BF16)), -jnp.inf)
        m = jnp.maximum(jnp.max(s1, axis=1, keepdims=True), jnp.max(s2, axis=1, keepdims=True))
        p1 = c1 * jnp.exp2(s1 - m)
        p2 = c2 * jnp.exp2(s2 - m)
        l = jnp.sum(p1, axis=1, keepdims=True) + jnp.sum(p2, axis=1, keepdims=True)
        o = (_dot_nt(p1.astype(BF16), cv[sl, :].astype(BF16))
             + _dot(p2.astype(BF16), vn[:, sl].astype(BF16))) / l
        outs.append(jnp.where(lo_mask, o[:t_new], o[t_new:]))
    o_ref[0] = jnp.concatenate(outs, axis=1).astype(o_ref.dtype)


def _sample_multiplicity(t_new, w_buf):
    t = np.arange(t_new)[:, None]
    e = np.arange(w_buf + t_new)[None, :]
    d = w_buf + t - e
    c = np.zeros(d.shape, np.float32)
    for window, dil in DILATED_GROUPS:
        c += ((d >= 0) & (d % dil == 0) & (d <= window)).astype(np.float32)
    c = np.concatenate([c, c], axis=0)
    return jnp.asarray(c[:, :w_buf]), jnp.asarray(c[:, w_buf:])


def _swa_sample(q, kn, vn, cache_k, cache_v):
    db, t_new, w = q.shape
    w_buf = cache_k.shape[2]
    assert w_buf >= max(win for win, _ in DILATED_GROUPS) and t_new % 8 == 0 and t_new <= LANES
    c1, c2 = _sample_multiplicity(t_new, w_buf)
    new = pl.BlockSpec((1, t_new, w), lambda b: (b, 0, 0))
    cache = pl.BlockSpec((1, w, w_buf), lambda b: (b, 0, 0))
    return pl.pallas_call(
        _swa_sample_body,
        grid=(db,),
        in_specs=[new, new, new, cache, cache, _full(c1.shape), _full(c2.shape)],
        out_specs=[new, cache, cache],
        out_shape=[jax.ShapeDtypeStruct((db, t_new, w), BF16),
                   jax.ShapeDtypeStruct(cache_k.shape, cache_k.dtype),
                   jax.ShapeDtypeStruct(cache_v.shape, cache_v.dtype)],
        compiler_params=_cparams(("parallel",)),
        name="swa_sample",
    )(q, kn, vn, cache_k, cache_v, c1, c2)


def _gla_body(chunk, sub, nch, q_ref, k_ref, g_ref, v_ref, gate_ref, s0_ref, tril_ref, dmask_ref,
              bones_ref, sbm_ref, bd_ref, gout_ref, o_ref, sfin_ref, sbd):
    j = pl.program_id(1)
    sbm = sbm_ref[...]
    nsub = chunk // sub
    pad = B_KEY_DIM - chunk

    @pl.when(j == 0)
    def _():
        s0 = s0_ref[0]
        sbd[...] = jnp.concatenate([s0] * B_HEADS, axis=1) * sbm

    row = lax.broadcasted_iota(I32, (chunk, 1), 0)
    sub_id = row // sub
    lane_w = lax.broadcasted_iota(I32, (chunk, LANES * max(nsub - 1, 1)), 1)
    lo_w = (lane_w % LANES) < B_KEY_DIM

    def one_chunk(c, carry):
        off = pl.multiple_of(c * chunk, chunk)
        rows = pl.ds(off, chunk)
        q, k, g = q_ref[rows, :], k_ref[rows, :], g_ref[rows, :]
        v = v_ref[rows, :].astype(F32)
        g1 = g.astype(BF16)
        r1 = g - g1.astype(F32)
        g2 = r1.astype(BF16)
        g3 = (r1 - g2.astype(F32)).astype(BF16)
        tril = tril_ref[...]
        b = _dot(tril, g1) + _dot(tril, g2) + _dot(tril, g3)
        b_last = b[chunk - 1:chunk, :]
        state = sbd[...]

        o = _dot((q * jnp.exp(b)).astype(BF16), state.astype(BF16))

        bones = bones_ref[...]
        att = _dot((q * k).astype(BF16), bones) * dmask_ref[0]
        gate = jnp.exp(g)
        decay = gate
        for d in range(1, sub):
            if d > 1:
                decay = decay * pltpu.roll(gate, d - 1, 0)
            w = q * pltpu.roll(k, d, 0) * decay
            att = att + _dot(w.astype(BF16), bones) * dmask_ref[d]

        if nsub > 1:
            qx, kx = [], []
            for i in range(1, nsub):
                r_i = b[sub * i - 1:sub * i, :]
                qx.append(jnp.where(sub_id == i, q * jnp.exp(jnp.minimum(b - r_i, 0.0)), 0.0))
                kx.append(jnp.where(sub_id < i, k * jnp.exp(jnp.minimum(r_i - b, 0.0)), 0.0))
            parts = []
            for p in range(B_QK_WIDTH // LANES):
                sl = slice(p * LANES, (p + 1) * LANES)
                qp = jnp.concatenate([x[:, sl] for x in qx], axis=1)
                kp = jnp.concatenate([x[:, sl] for x in kx], axis=1).astype(BF16)
                zero = jnp.zeros_like(kp)
                lhs = jnp.concatenate([jnp.where(lo_w, qp, 0.0), jnp.where(lo_w, 0.0, qp)],
                                      axis=1).astype(BF16)
                rhs = jnp.concatenate([jnp.concatenate([kp, zero], axis=1),
                                       jnp.concatenate([zero, kp], axis=1)], axis=0)
                parts.append(_dot_nt(lhs, rhs))
            att = att + jnp.concatenate(parts, axis=1)

        if pad:
            vrow = jnp.concatenate([v, jnp.zeros((pad, B_V_WIDTH), F32)], axis=0)
        else:
            vrow = v
        vbd = (jnp.concatenate([vrow] * B_HEADS, axis=0) * sbm).astype(BF16)
        o = o + _dot(att.astype(BF16), vbd)

        ke = (k * jnp.exp(b_last - b)).astype(BF16)
        upd = lax.dot_general(ke, v.astype(BF16), (((0,), (0,)), ((), ())),
                              preferred_element_type=F32)
        dec = jnp.transpose(jnp.broadcast_to(jnp.exp(b_last), (8, B_QK_WIDTH)))[:, 0:1]
        sbd[...] = (state * dec + upd) * sbm

        ms = _dot((o * o).astype(BF16), bd_ref[...])
        on = o * lax.rsqrt(ms + EPS) * gout_ref[...] * gate_ref[rows, :].astype(F32)
        o_ref[rows, :] = on.astype(o_ref.dtype)
        return carry

    lax.fori_loop(0, nch, one_chunk, 0, unroll=2 if nch % 2 == 0 else 1)

    @pl.when(j == pl.num_programs(1) - 1)
    def _():
        s = sbd[...]
        sfin_ref[0] = jnp.concatenate(
            [s[h * B_KEY_DIM:(h + 1) * B_KEY_DIM, h * B_VAL_DIM:(h + 1) * B_VAL_DIM]
             for h in range(B_HEADS)], axis=0)


def _gla_consts(chunk, sub):
    t = np.arange(chunk)
    tril = (t[:, None] >= t[None, :]).astype(np.float32)
    lane = np.arange(B_QK_WIDTH)
    dmask = np.zeros((sub, chunk, B_QK_WIDTH), np.float32)
    for d in range(sub):
        ok = (t % sub) >= d
        dmask[d] = ((lane[None, :] % B_KEY_DIM) == (t[:, None] - d)) & ok[:, None]
    r = np.arange(B_QK_WIDTH)[:, None] // B_KEY_DIM
    c = np.arange(B_V_WIDTH)[None, :] // B_VAL_DIM
    sbm = (r == c).astype(np.float32)
    return jnp.asarray(tril, BF16), jnp.asarray(dmask), jnp.asarray(sbm)


def _gla(q, k, g, v, gate, s0, wp, length, chunk, sub, step):
    n = q.shape[0]
    bsz = n // length
    assert chunk == sub or chunk == B_KEY_DIM
    tril, dmask, sbm = _gla_consts(chunk, sub)
    nstep = length // step
    row = lambda w: pl.BlockSpec((step, w), lambda b, j: (b * nstep + j, 0))
    st = pl.BlockSpec((1, B_QK_WIDTH, B_VAL_DIM), lambda b, j: (b, 0, 0))
    return pl.pallas_call(
        functools.partial(_gla_body, chunk, sub, step // chunk),
        grid=(bsz, nstep),
        in_specs=[row(256), row(256), row(256), row(512), row(512), st, _full(tril.shape),
                  _full(dmask.shape), _full((256, 256)), _full(sbm.shape), _full((512, 512)),
                  _full((1, 512))],
        out_specs=[row(512), st],
        out_shape=[jax.ShapeDtypeStruct((n, B_V_WIDTH), BF16),
                   jax.ShapeDtypeStruct(s0.shape, F32)],
        scratch_shapes=[pltpu.VMEM((B_QK_WIDTH, B_V_WIDTH), F32)],
        compiler_params=_cparams(("parallel", "arbitrary")),
        name="gla",
    )(q, k, g, v, gate, s0, tril, dmask, wp["bones64"], sbm, wp["bd128"], wp["gout"])


def _head_rms(z, g, scale=1.0):
    parts = []
    for h in range(MEM_HEADS):
        zh = z[:, h * LANES:(h + 1) * LANES]
        parts.append(zh * lax.rsqrt(jnp.mean(zh * zh, axis=-1, keepdims=True) + EPS))
    return jnp.concatenate(parts, axis=1) * (g * scale)


def _mem_kv_body(m_ref, gn_ref, wk_ref, wv_ref, gk_ref, mk_ref, mv_ref):
    mn = _rms(m_ref[...], gn_ref[...]).astype(BF16)
    mk_ref[...] = _head_rms(_dot(mn, wk_ref[...]), gk_ref[...])
    mv_ref[...] = _dot(mn, wv_ref[...])


def _mem_kv(mem, wp):
    n = mem.shape[0]
    tm = 256
    row = lambda w: pl.BlockSpec((tm, w), lambda i: (i, 0))
    return pl.pallas_call(
        _mem_kv_body,
        grid=(n // tm,),
        in_specs=[row(D_MODEL), _full((1, D_MODEL)), _full((D_MODEL, 512)), _full((D_MODEL, 512)),
                  _full((1, 512))],
        out_specs=[row(512), row(512)],
        out_shape=[jax.ShapeDtypeStruct((n, 512), F32)] * 2,
        compiler_params=_cparams(("parallel",)),
        name="mem_kv",
    )(mem, wp["gmem"], wp["wmk"], wp["wmv"], wp["gmk"])


def _post_body(nseq, x_ref, oa_ref, ob_ref, woa_ref, wob_ref, g2_ref, wq_ref, gmq_ref, mk_ref,
               mv_ref, wo_ref, g3_ref, rw_ref, rb_ref, cnt0_ref, tri_ref,
               h2_ref, xn_ref, meta_ref, cnt_ref, carry):
    tm = x_ref.shape[0]

    @pl.when(pl.program_id(0) == 0)
    def _():
        carry[...] = cnt0_ref[...]

    nk = nseq * N_MEM
    mk = mk_ref[...].reshape(nk, MEM_WIDTH).astype(BF16)
    mv = mv_ref[...].reshape(nk, MEM_WIDTH).astype(BF16)
    if nseq > 1:
        rt = lax.broadcasted_iota(I32, (tm, nk), 0) // (tm // nseq)
        ct = lax.broadcasted_iota(I32, (tm, nk), 1) // N_MEM
        same = rt == ct

    def router_logits():
        h = x_ref[...] + _dot(oa_ref[...], woa_ref[...]) + _dot(ob_ref[...], wob_ref[...])
        hn = _rms(h, g2_ref[...]).astype(BF16)
        qm = _head_rms(_dot(hn, wq_ref[...]), gmq_ref[...], MEM_HEAD_DIM ** -0.5).astype(BF16)
        outs = []
        for hd in range(MEM_HEADS):
            sl = slice(hd * LANES, (hd + 1) * LANES)
            s = _dot_nt(qm[:, sl], mk[:, sl])
            if nseq > 1:
                s = jnp.where(same, s, -jnp.inf)
            m = jnp.max(s, axis=1, keepdims=True)
            pr = jnp.exp(s - m)
            l = jnp.sum(pr, axis=1, keepdims=True)
            outs.append(_dot(pr.astype(BF16), mv[:, sl]) / l)
        h2 = h + _dot(jnp.concatenate(outs, axis=1).astype(BF16), wo_ref[...])
        h2_ref[...] = h2
        xn = _rms(h2, g3_ref[...])
        xn_ref[...] = xn
        x1 = xn.astype(BF16)
        x2 = (xn - x1.astype(F32)).astype(BF16)
        prod = _dot(jnp.concatenate([x1, x2], axis=0), rw_ref[...])
        return prod[:tm, :LANES] + prod[:tm, LANES:] + prod[tm:, :LANES] + prod[tm:, LANES:]

    logits = router_logits() + rb_ref[...]
    lane = lax.broadcasted_iota(I32, (tm, LANES), 1)
    vals, idxs, hots = [], [], []
    work = logits
    for _ in range(TOP_K):
        m = jnp.max(work, axis=1, keepdims=True)
        idx = jnp.min(jnp.where(work == m, lane, LANES), axis=1, keepdims=True)
        hot = lane == idx
        vals.append(m)
        idxs.append(idx)
        hots.append(hot)
        work = jnp.where(hot, -jnp.inf, work)
    exps = [jnp.exp(v - vals[0]) for v in vals]
    den = exps[0] + exps[1] + exps[2] + exps[3]

    sel = (hots[0] | hots[1] | hots[2] | hots[3]).astype(F32)
    before = _dot(tri_ref[...], sel.astype(BF16)) + carry[...]
    carry[...] = carry[...] + jnp.sum(sel, axis=0, keepdims=True)
    cnt_ref[...] = carry[...]

    meta = jnp.zeros((tm, LANES), F32)
    for kk in range(TOP_K):
        rank = jnp.sum(jnp.where(hots[kk], before, 0.0), axis=1, keepdims=True)
        meta = jnp.where(lane == kk, idxs[kk].astype(F32), meta)
        meta = jnp.where(lane == TOP_K + kk, rank, meta)
        meta = jnp.where(lane == 2 * TOP_K + kk, exps[kk] / den, meta)
    meta_ref[...] = meta


def _post(x, oa, ob, mk, mv, cnt0, wp, tm, nseq, tiles_per_mem):
    n = x.shape[0]
    row = lambda w: pl.BlockSpec((tm, w), lambda i: (i, 0))
    mem = pl.BlockSpec((nseq, N_MEM, MEM_WIDTH), lambda i: (i // tiles_per_mem, 0, 0))
    tri = jnp.asarray(np.tril(np.ones((tm, tm), np.float32), -1), BF16)
    return pl.pallas_call(
        functools.partial(_post_body, nseq),
        grid=(n // tm,),
        in_specs=[row(D_MODEL), row(512), row(512), _full((512, D_MODEL)), _full((512, D_MODEL)),
                  _full((1, D_MODEL)), _full((D_MODEL, 512)), _full((1, 512)), mem, mem,
                  _full((512, D_MODEL)), _full((1, D_MODEL)), _full((D_MODEL, 2 * LANES)),
                  _full((1, LANES)), _full((1, LANES)), _full((tm, tm))],
        out_specs=[row(D_MODEL), row(D_MODEL), row(LANES), _full((1, LANES))],
        out_shape=[jax.ShapeDtypeStruct((n, D_MODEL), F32), jax.ShapeDtypeStruct((n, D_MODEL), F32),
                   jax.ShapeDtypeStruct((n, LANES), F32), jax.ShapeDtypeStruct((1, LANES), F32)],
        scratch_shapes=[pltpu.VMEM((1, LANES), F32)],
        compiler_params=_cparams(("arbitrary",)),
        name="post",
    )(x, oa, ob, wp["woa"], wp["wob"], wp["g2"], wp["wmq"], wp["gmq"], mk, mv, wp["wmo"],
      wp["g3"], wp["rw"], wp["rb"], cnt0, tri)


def _start_rows(src, src_row, dst, dst_row, n_rows, sem):
    s = pl.multiple_of(jnp.asarray(src_row, I32), GROUP_ROWS)
    d = pl.multiple_of(jnp.asarray(dst_row, I32), GROUP_ROWS)
    n_rows = pl.multiple_of(jnp.asarray(n_rows, I32), GROUP_ROWS)

    @pl.when(n_rows > 0)
    def _():
        pltpu.make_async_copy(src.at[pl.ds(s, n_rows)], dst.at[pl.ds(d, n_rows)], sem).start()

    return n_rows


def _wait_rows(like_src, like_dst, n_rows, sem):
    n_rows = pl.multiple_of(n_rows, GROUP_ROWS)

    @pl.when(n_rows > 0)
    def _():
        pltpu.make_async_copy(like_src.at[pl.ds(0, n_rows)], like_dst.at[pl.ds(0, n_rows)], sem).wait()


def _pad_rows_body(pstart_ref, prows_ref, nt_ref, xs_ref, zeros, sem):
    zeros[...] = jnp.zeros_like(zeros)
    n_tiles_max = xs_ref.shape[0] // EXPERT_TILE

    def expert_tail(e, total):
        return total + _start_rows(zeros, 0, xs_ref, pstart_ref[e], prows_ref[e], sem)

    def unused_tile(t, total):
        return total + _start_rows(zeros, 0, xs_ref, t * EXPERT_TILE, EXPERT_TILE, sem)

    total = lax.fori_loop(0, N_EXPERTS, expert_tail, jnp.int32(0))
    total = lax.fori_loop(nt_ref[0], n_tiles_max, unused_tile, total)
    _wait_rows(xs_ref, xs_ref, total, sem)


def _pad_rows(pad_start, pad_units, n_tiles, rows):
    return pl.pallas_call(
        _pad_rows_body,
        grid_spec=pltpu.PrefetchScalarGridSpec(
            num_scalar_prefetch=3, grid=(1,),
            in_specs=[],
            out_specs=pl.BlockSpec(memory_space=pltpu.HBM),
            scratch_shapes=[pltpu.VMEM((EXPERT_TILE, D_MODEL), F32), pltpu.SemaphoreType.DMA]),
        out_shape=jax.ShapeDtypeStruct((rows, D_MODEL), F32),
        compiler_params=_cparams(("arbitrary",)),
        name="moe_pad_rows",
    )(pad_start, pad_units, n_tiles)


def _dispatch_body(blk0, loff_ref, gstart_ref, nrows_ref, ldest_ref, x_ref, xs_in_ref, xs_ref,
                   xloc, sems):
    del xs_in_ref
    i = pl.program_id(0)
    b = i + blk0
    slot = i % 2
    tb = x_ref.shape[0]
    rows = lax.broadcasted_iota(I32, (LOCAL_ROWS, tb), 0)
    ld = ldest_ref[...]
    hot = rows == ld[0:1, :]
    for kk in range(1, TOP_K):
        hot = hot | (rows == ld[kk:kk + 1, :])
    xloc[slot] = _dot(jnp.where(hot, 1.0, 0.0).astype(BF16), x_ref[...].astype(BF16))

    def block_rows(bb):
        j = bb * N_EXPERTS + N_EXPERTS - 1
        return loff_ref[j] + nrows_ref[j]

    @pl.when(i > 0)
    def _():
        _wait_rows(xloc.at[1 - slot], xs_ref, block_rows(b - 1), sems.at[1 - slot])

    def group(e, carry):
        j = b * N_EXPERTS + e
        _start_rows(xloc.at[slot], loff_ref[j], xs_ref, gstart_ref[j], nrows_ref[j], sems.at[slot])
        return carry

    lax.fori_loop(0, N_EXPERTS, group, 0)

    @pl.when(i == pl.num_programs(0) - 1)
    def _():
        _wait_rows(xloc.at[slot], xs_ref, block_rows(b), sems.at[slot])


def _dispatch(tables, blk0, ldest_t, xn, xs):
    n = xn.shape[0]
    tb = DISPATCH_BLOCK
    idx = lambda i, *_: (i, 0)
    return pl.pallas_call(
        functools.partial(_dispatch_body, blk0),
        grid_spec=pltpu.PrefetchScalarGridSpec(
            num_scalar_prefetch=3, grid=(n // tb,),
            in_specs=[pl.BlockSpec((TOP_K, tb), lambda i, *_: (0, i)),
                      pl.BlockSpec((tb, D_MODEL), idx),
                      pl.BlockSpec(memory_space=pltpu.HBM)],
            out_specs=pl.BlockSpec(memory_space=pltpu.HBM),
            scratch_shapes=[pltpu.VMEM((2, LOCAL_ROWS, D_MODEL), F32), pltpu.SemaphoreType.DMA((2,))]),
        out_shape=jax.ShapeDtypeStruct(xs.shape, xs.dtype),
        input_output_aliases={5: 0},
        compiler_params=_cparams(("arbitrary",)),
        name="moe_dispatch",
    )(*tables, ldest_t, xn, xs)


def _expert_body(te_ref, first_ref, nt_ref, x_ref, w1_ref, sel_ref, b1g_ref, b1l_ref, w2_ref, b2_ref,
                 y_ref, w1g, w1l, w2):
    i = pl.program_id(0)
    live = i < nt_ref[0]

    @pl.when(live & (first_ref[i] == 1))
    def _():
        sel = sel_ref[...]
        for j in range(D_FF // LANES):
            z = _dot(w1_ref[0, :, 2 * LANES * j:2 * LANES * (j + 1)].astype(BF16), sel)
            w1g[:, LANES * j:LANES * (j + 1)] = z[:, :LANES].astype(BF16)
            w1l[:, LANES * j:LANES * (j + 1)] = z[:, LANES:].astype(BF16)
        w2[...] = w2_ref[0].astype(BF16)

    @pl.when(live)
    def _():
        x = x_ref[...].astype(BF16)
        glu = jnp.minimum(_dot(x, w1g[...]) + b1g_ref[0], SWIGLU_LIMIT)
        lin = jnp.clip(_dot(x, w1l[...]) + b1l_ref[0], -SWIGLU_LIMIT, SWIGLU_LIMIT)
        act = glu * jax.nn.sigmoid(SWIGLU_ALPHA * glu) * (lin + 1.0)
        y_ref[...] = _dot(act.astype(BF16), w2[...]) + b2_ref[0]

    @pl.when(jnp.logical_not(live))
    def _():
        y_ref[...] = jnp.zeros_like(y_ref)


def _experts(tile_expert, n_tiles, xs, wp):
    rows = xs.shape[0]
    first = jnp.concatenate([jnp.ones((1,), I32),
                             (tile_expert[1:] != tile_expert[:-1]).astype(I32)])
    c = np.arange(2 * LANES)
    sel = np.zeros((2 * LANES, 2 * LANES), np.float32)
    sel[c, (c % 2) * LANES + c // 2] = 1.0
    tile = lambda i, te, fi, nt: (jnp.minimum(i, nt[0] - 1), 0)
    out_tile = lambda i, te, fi, nt: (i, 0)
    wsel = lambda i, te, fi, nt: (te[jnp.minimum(i, nt[0] - 1)], 0, 0)
    wspec = lambda r, c: pl.BlockSpec((1, r, c), wsel)
    return pl.pallas_call(
        _expert_body,
        grid_spec=pltpu.PrefetchScalarGridSpec(
            num_scalar_prefetch=3, grid=(rows // EXPERT_TILE,),
            in_specs=[pl.BlockSpec((EXPERT_TILE, D_MODEL), tile),
                      wspec(D_MODEL, 2 * D_FF), pl.BlockSpec((2 * LANES, 2 * LANES), lambda *_: (0, 0)),
                      wspec(1, D_FF), wspec(1, D_FF), wspec(D_FF, D_MODEL), wspec(1, D_MODEL)],
            out_specs=pl.BlockSpec((EXPERT_TILE, D_MODEL), out_tile),
            scratch_shapes=[pltpu.VMEM((D_MODEL, D_FF), BF16)] * 2 + [pltpu.VMEM((D_FF, D_MODEL), BF16)]),
        out_shape=jax.ShapeDtypeStruct((rows, D_MODEL), F32),
        compiler_params=_cparams(("arbitrary",)),
        name="moe_experts",
    )(tile_expert, first, n_tiles, xs, wp["w1"], jnp.asarray(sel, BF16), wp["b1g"], wp["b1l"],
      wp["w2"], wp["b2"])


def _combine_body(blk0, loff_ref, gstart_ref, nrows_ref, ldest_ref, meta_ref, h_ref, ys_ref, o_ref,
                  yloc, sems):
    i = pl.program_id(0)
    b = i + blk0
    slot = i % 2
    tb = h_ref.shape[0]

    def gather(bb, s):
        def group(e, carry):
            j = bb * N_EXPERTS + e
            _start_rows(ys_ref, gstart_ref[j], yloc.at[s], loff_ref[j], nrows_ref[j], sems.at[s])
            return carry
        lax.fori_loop(0, N_EXPERTS, group, 0)

    @pl.when(i == 0)
    def _():
        yloc[...] = jnp.zeros_like(yloc)
        gather(b, slot)

    @pl.when(i + 1 < pl.num_programs(0))
    def _():
        gather(b + 1, 1 - slot)

    j_last = b * N_EXPERTS + N_EXPERTS - 1
    total = loff_ref[j_last] + nrows_ref[j_last]
    cols = lax.broadcasted_iota(I32, (tb, LOCAL_ROWS), 1)
    ld = ldest_ref[...]
    meta = meta_ref[...]
    gmat = jnp.zeros((tb, LOCAL_ROWS), F32)
    for kk in range(TOP_K):
        gate = meta[:, 2 * TOP_K + kk:2 * TOP_K + kk + 1]
        gmat = jnp.where(cols == ld[:, kk:kk + 1], gate, gmat)
    _wait_rows(ys_ref, yloc.at[slot], total, sems.at[slot])
    o_ref[...] = h_ref[...] + _dot(gmat.astype(BF16), yloc[slot].astype(BF16))


def _combine(tables, blk0, ldest, meta, h2, ys):
    n = h2.shape[0]
    tb = DISPATCH_BLOCK
    row = lambda w: pl.BlockSpec((tb, w), lambda i, *_: (i, 0))
    return pl.pallas_call(
        functools.partial(_combine_body, blk0),
        grid_spec=pltpu.PrefetchScalarGridSpec(
            num_scalar_prefetch=3, grid=(n // tb,),
            in_specs=[row(TOP_K), row(LANES), row(D_MODEL), pl.BlockSpec(memory_space=pltpu.HBM)],
            out_specs=row(D_MODEL),
            scratch_shapes=[pltpu.VMEM((2, LOCAL_ROWS, D_MODEL), F32), pltpu.SemaphoreType.DMA((2,))]),
        out_shape=jax.ShapeDtypeStruct((n, D_MODEL), F32),
        compiler_params=_cparams(("arbitrary",)),
        name="moe_combine",
    )(*tables, ldest, meta, h2, ys)


def _moe(groups, wp):
    tb = DISPATCH_BLOCK
    sizes = [g[0].shape[0] for g in groups]
    assert all(s % tb == 0 for s in sizes)
    n_tok = sum(sizes)
    nb = n_tok // tb
    rows = n_tok * TOP_K + nb * N_EXPERTS * GROUP_ROWS + N_EXPERTS * EXPERT_TILE
    n_tiles_max = rows // EXPERT_TILE

    eidx = jnp.concatenate([g[2][:, 0:TOP_K] for g in groups]).astype(I32)
    rank = jnp.concatenate([g[2][:, TOP_K:2 * TOP_K] for g in groups]).astype(I32)
    hot = eidx[:, :, None] == jnp.arange(N_EXPERTS, dtype=I32)
    cnt = jnp.sum(hot.reshape(nb, tb * TOP_K, N_EXPERTS), axis=1, dtype=I32)
    npad = (cnt + GROUP_ROWS - 1) // GROUP_ROWS * GROUP_ROWS
    loff = jnp.cumsum(npad, axis=1) - npad
    gsize = jnp.sum(npad, axis=0)
    gpad = (gsize + EXPERT_TILE - 1) // EXPERT_TILE * EXPERT_TILE
    ends = jnp.cumsum(gpad)
    gstart = (ends - gpad)[None, :] + jnp.cumsum(npad, axis=0) - npad
    before = jnp.cumsum(cnt, axis=0) - cnt
    n_tiles = (ends[-1] // EXPERT_TILE).reshape(1)
    tile_ids = jnp.arange(n_tiles_max, dtype=I32)
    tile_expert = jnp.minimum(
        jnp.sum((ends // EXPERT_TILE)[None, :] <= tile_ids[:, None], axis=1), N_EXPERTS - 1).astype(I32)

    base = jnp.repeat(loff - before, tb, axis=0)
    ldest = rank + jnp.sum(jnp.where(hot, base[:, None, :], 0), axis=2)
    ldest_t = ldest.T
    tables = (loff.reshape(-1), gstart.reshape(-1).astype(I32), npad.reshape(-1))

    xs = _pad_rows(((ends - gpad) + gsize).astype(I32), (gpad - gsize).astype(I32),
                   n_tiles, rows)
    starts = np.cumsum([0] + sizes[:-1])
    for (_, xn, _), t0, sz in zip(groups, starts, sizes):
        xs = _dispatch(tables, int(t0) // tb, ldest_t[:, t0:t0 + sz], xn, xs)
    ys = _experts(tile_expert, n_tiles, xs, wp)
    return [_combine(tables, int(t0) // tb, ldest[t0:t0 + sz], meta, h2, ys)
            for (h2, _, meta), t0, sz in zip(groups, starts, sizes)]


def _prep_weights(norm1_g, w_in, a_q_norm_g, a_k_norm_g, gla_w_alpha, gla_b_alpha, gla_out_norm_g,
                  w_out, norm2_g, mem_norm_g, mem_w_q, mem_w_k, mem_w_v, mem_q_norm_g,
                  mem_k_norm_g, mem_w_o, norm3_g, router_w, router_b, exp_w1, exp_b1, exp_w2,
                  exp_b2):
    main = 3 * A_WIDTH + 2 * B_QK_WIDTH + 2 * B_V_WIDTH
    w_lr = jnp.pad(w_in[:, main:], ((0, 0), (0, LANES - GATE_RANK)))
    rw = jnp.pad(router_w, ((0, 0), (0, LANES - N_EXPERTS)))
    rwh = rw.astype(BF16)
    return {
        "g1": norm1_g[None],
        "w_in": jnp.concatenate([w_in[:, :main], w_lr], axis=1).astype(BF16),
        "gq": jnp.tile(a_q_norm_g, A_HEADS)[None],
        "gk": jnp.tile(a_k_norm_g, A_HEADS)[None],
        "bd64": _block_diag(A_WIDTH, A_HEAD_DIM, 1.0 / A_HEAD_DIM, BF16),
        "wa": jnp.pad(gla_w_alpha, ((0, LANES - GATE_RANK), (0, 0))).astype(BF16),
        "ba": gla_b_alpha[None],
        "bones64": _block_diag(B_QK_WIDTH, B_KEY_DIM, 1.0, BF16),
        "bd128": _block_diag(B_V_WIDTH, B_VAL_DIM, 1.0 / B_VAL_DIM, BF16),
        "gout": jnp.tile(gla_out_norm_g, B_HEADS)[None],
        "woa": w_out[:A_WIDTH].astype(BF16),
        "wob": w_out[A_WIDTH:].astype(BF16),
        "g2": norm2_g[None],
        "gmem": mem_norm_g[None],
        "wmq": mem_w_q.astype(BF16),
        "wmk": mem_w_k.astype(BF16),
        "wmv": mem_w_v.astype(BF16),
        "gmq": jnp.tile(mem_q_norm_g, MEM_HEADS)[None],
        "gmk": jnp.tile(mem_k_norm_g, MEM_HEADS)[None],
        "wmo": mem_w_o.astype(BF16),
        "g3": norm3_g[None],
        "rw": jnp.concatenate([rwh, (rw - rwh.astype(F32)).astype(BF16)], axis=1),
        "rb": jnp.pad(router_b, (0, LANES - N_EXPERTS), constant_values=-1e30)[None],
        "w1": exp_w1,
        "b1g": exp_b1[:, None, 0::2],
        "b1l": exp_b1[:, None, 1::2],
        "w2": exp_w2,
        "b2": exp_b2[:, None, :],
    }


def _layer(xp, xs, mem_prompt, cache_k, cache_v, state_gla, cache_mk, cache_mv, wp):
    bsz, seq, _ = xp.shape
    db, t_new, _ = xs.shape
    w_p = min(max(w for w, _ in DILATED_GROUPS), seq)

    xpf = xp.reshape(bsz * seq, D_MODEL)
    q, k, v, k_last, v_last, bq, bk, bv, gate, la = _in_proj(xpf, wp, BF16, seq, w_p)

    def last_rows(a):
        return a.transpose(0, 2, 1) if w_p != seq else a.reshape(bsz, w_p, A_WIDTH)
    oa = _swa_prompt(q, k, v, bsz, seq)
    s0 = jnp.zeros((bsz, B_QK_WIDTH, B_VAL_DIM), F32)
    ob, s_p = _gla(bq, bk, la, bv, gate, s0, wp, seq, GLA_CHUNK, GLA_SUB, GLA_STEP)
    mk, mv = _mem_kv(mem_prompt.reshape(bsz * N_MEM, D_MODEL), wp)
    cnt0 = jnp.zeros((1, LANES), F32)
    h2_p, xn_p, meta_p, cnt = _post(xpf, oa, ob, mk.reshape(bsz, N_MEM, MEM_WIDTH),
                                    mv.reshape(bsz, N_MEM, MEM_WIDTH), cnt0, wp,
                                    POST_TILE, 1, seq // POST_TILE)

    xsf = xs.reshape(db * t_new, D_MODEL)
    q, _, _, ks, vs, bq, bk, bv, gate, la = _in_proj(xsf, wp, F32, t_new, t_new)
    new3 = lambda a: a.reshape(db, t_new, A_WIDTH)
    oa_s, nk, nv = _swa_sample(new3(q.transpose(1, 0, 2)), new3(ks), new3(vs), cache_k, cache_v)
    ob_s, s_s = _gla(bq, bk, la, bv, gate, state_gla, wp, t_new, t_new, t_new, t_new)
    h2_s, xn_s, meta_s, cnt = _post(xsf, oa_s.reshape(db * t_new, A_WIDTH), ob_s, cache_mk,
                                    cache_mv, cnt, wp, SAMPLE_SEQS * t_new, SAMPLE_SEQS, 1)

    y_p, y_s = _moe([(h2_p, xn_p, meta_p), (h2_s, xn_s, meta_s)], wp)
    return (y_p.reshape(bsz, seq, D_MODEL), y_s.reshape(db, t_new, D_MODEL),
            last_rows(k_last), last_rows(v_last),
            s_p, mk, mv, nk, nv, s_s)


def kernel(x_prompt, x_sample, mem_prompt, cache_swa_k, cache_swa_v, state_gla, cache_mem_k, cache_mem_v, norm1_g, w_in, a_q_norm_g, a_k_norm_g, gla_w_alpha, gla_b_alpha, gla_out_norm_g, w_out, norm2_g, mem_norm_g, mem_w_q, mem_w_k, mem_w_v, mem_q_norm_g, mem_k_norm_g, mem_w_o, norm3_g, router_w, router_b, exp_w1, exp_b1, exp_w2, exp_b2):
    depth = w_in.shape[0]
    bsz = x_prompt.shape[0]
    db, w_buf = cache_swa_k.shape[1], cache_swa_k.shape[2]
    xp, xs = x_prompt, x_sample
    per_layer = []
    for l in range(depth):
        wp = _prep_weights(
            norm1_g[l], w_in[l], a_q_norm_g[l], a_k_norm_g[l], gla_w_alpha[l], gla_b_alpha[l],
            gla_out_norm_g[l], w_out[l], norm2_g[l], mem_norm_g[l], mem_w_q[l], mem_w_k[l],
            mem_w_v[l], mem_q_norm_g[l], mem_k_norm_g[l], mem_w_o[l], norm3_g[l], router_w[l],
            router_b[l], exp_w1[l], exp_b1[l], exp_w2[l], exp_b2[l])
        xp, xs, kp, vp, s_p, mk, mv, nk, nv, s_s = _layer(
            xp, xs, mem_prompt,
            cache_swa_k[l].reshape(db, w_buf, A_WIDTH).transpose(0, 2, 1),
            cache_swa_v[l].reshape(db, w_buf, A_WIDTH).transpose(0, 2, 1),
            state_gla[l].reshape(db, B_QK_WIDTH, B_VAL_DIM),
            cache_mem_k[l].reshape(db, N_MEM, MEM_WIDTH), cache_mem_v[l].reshape(db, N_MEM, MEM_WIDTH),
            wp)
        w_p = kp.shape[1]
        per_layer.append((
            kp.reshape(bsz, w_p, A_HEADS, A_HEAD_DIM), vp.reshape(bsz, w_p, A_HEADS, A_HEAD_DIM),
            s_p.reshape(bsz, B_HEADS, B_KEY_DIM, B_VAL_DIM),
            mk.reshape(bsz, N_MEM, MEM_HEADS, MEM_HEAD_DIM), mv.reshape(bsz, N_MEM, MEM_HEADS, MEM_HEAD_DIM),
            nk.transpose(0, 2, 1).reshape(db, w_buf, A_HEADS, A_HEAD_DIM),
            nv.transpose(0, 2, 1).reshape(db, w_buf, A_HEADS, A_HEAD_DIM),
            s_s.reshape(db, B_HEADS, B_KEY_DIM, B_VAL_DIM)))
    stacked = [jnp.stack(t) for t in zip(*per_layer)]
    return (xp, xs, *stacked)
```

```python
import functools

import jax
import jax.numpy as jnp
import numpy as np
from jax import lax
from jax.experimental import pallas as pl
from jax.experimental.pallas import tpu as pltpu

F32 = jnp.float32
BF16 = jnp.bfloat16
I32 = jnp.int32

EPS = 1e-6
D_MODEL = 1024
A_HEADS, A_HEAD_DIM, A_WIDTH = 8, 64, 512
A_SCALE = A_HEAD_DIM ** -0.5
LOG2E = 1.4426950408889634
DILATED_GROUPS = ((128, 1), (512, 4), (2048, 16))
SPAN = 128
B_HEADS, B_KEY_DIM, B_VAL_DIM = 4, 64, 128
B_QK_WIDTH, B_V_WIDTH = 256, 512
GATE_RANK = 16
GATE_TEMP = 16.0
N_MEM, MEM_HEADS, MEM_HEAD_DIM, MEM_WIDTH = 256, 4, 128, 512
N_EXPERTS, TOP_K, D_FF = 32, 4, 1024
SWIGLU_ALPHA, SWIGLU_LIMIT = 1.702, 7.0

LANES = 128
VMEM_LIMIT = 56 * 1024 * 1024

PROJ_TILE = 512
SWA_BLOCK = 2048
SWA_UNROLL = 8
GLA_CHUNK, GLA_SUB, GLA_STEP = 64, 8, 1024
POST_TILE = 1024
SAMPLE_SEQS = 8
EXPERT_TILE = 512
DISPATCH_BLOCK = 512
GROUP_ROWS = 8
LOCAL_ROWS = DISPATCH_BLOCK * TOP_K + N_EXPERTS * GROUP_ROWS


def _cparams(sem, vmem=VMEM_LIMIT, **kw):
    return pltpu.CompilerParams(dimension_semantics=sem, vmem_limit_bytes=vmem, **kw)


def _full(shape):
    n = len(shape)
    return pl.BlockSpec(shape, lambda *_: (0,) * n)


def _rms(x, g):
    ms = jnp.mean(x * x, axis=-1, keepdims=True)
    return x * lax.rsqrt(ms + EPS) * g


def _dot(a, b):
    return jnp.dot(a, b, preferred_element_type=F32)


def _dot_nt(a, b):
    return lax.dot_general(a, b, (((1,), (1,)), ((), ())), preferred_element_type=F32)


def _block_diag(n, blk, val, dtype):
    i = np.arange(n)
    return jnp.asarray(np.where((i[:, None] // blk) == (i[None, :] // blk), val, 0.0), dtype)


def _in_proj_body(keep_transposed, x_ref, g1_ref, w_ref, gq_ref, gk_ref, bd_ref, wa_ref, ba_ref,
                  q_ref, k_ref, v_ref, kc_ref, vc_ref, bq_ref, bk_ref, bv_ref, gate_ref, la_ref):
    xn = _rms(x_ref[...], g1_ref[...]).astype(BF16)
    bd = bd_ref[...]

    def proj(lo, hi):
        return _dot(xn, w_ref[:, lo:hi])

    def headnorm(z, g):
        ms = _dot((z * z).astype(BF16), bd)
        return z * lax.rsqrt(ms + EPS) * g

    def put_groups(ref, z):
        for p in range(A_WIDTH // LANES):
            ref[p] = z[:, p * LANES:(p + 1) * LANES]

    put_groups(q_ref, headnorm(proj(0, 512), gq_ref[...]) * (A_SCALE * LOG2E))
    k = headnorm(proj(512, 1024), gk_ref[...])
    v = proj(1024, 1536)
    put_groups(k_ref, k)
    put_groups(v_ref, v)
    kc_ref[...] = jnp.transpose(k) if keep_transposed else k
    vc_ref[...] = jnp.transpose(v) if keep_transposed else v
    bq_ref[...] = proj(1536, 1792) * (B_KEY_DIM ** -0.5)
    bk_ref[...] = proj(1792, 2048)
    bv_ref[...] = proj(2048, 2560).astype(bv_ref.dtype)
    br = proj(2560, 3072)
    gate_ref[...] = (br * jax.nn.sigmoid(br)).astype(gate_ref.dtype)
    lr = proj(3072, 3200).astype(BF16)
    pre = _dot(lr, wa_ref[...]) + ba_ref[...]
    log_sig = jnp.minimum(pre, 0.0) - jnp.log1p(jnp.exp(-jnp.abs(pre)))
    la_ref[...] = log_sig * (1.0 / GATE_TEMP)


def _in_proj(x, wp, wide_dtype, seq, keep):
    n = x.shape[0]
    tm = min(PROJ_TILE, n)
    row = lambda w: pl.BlockSpec((tm, w), lambda i: (i, 0))
    ngrp = A_WIDTH // LANES
    grp = pl.BlockSpec((ngrp, tm, LANES), lambda i: (0, i, 0))
    if keep == seq:
        kept = row(A_WIDTH)
        kept_shape = (n, A_WIDTH)
    else:
        tps, kt = seq // tm, keep // tm
        assert tps * tm == seq and kt * tm == keep
        kept = pl.BlockSpec((None, A_WIDTH, tm),
                            lambda i: (i // tps, 0, jnp.maximum(i % tps - (tps - kt), 0)))
        kept_shape = (n // seq, A_WIDTH, keep)
    outs = [(256, F32), (256, F32), (512, wide_dtype), (512, wide_dtype), (256, F32)]
    return pl.pallas_call(
        functools.partial(_in_proj_body, keep != seq),
        grid=(n // tm,),
        in_specs=[row(D_MODEL), _full((1, D_MODEL)), _full((D_MODEL, 3200)), _full((1, 512)),
                  _full((1, 512)), _full((512, 512)), _full((LANES, 256)), _full((1, 256))],
        out_specs=[grp] * 3 + [kept] * 2 + [row(w) for w, _ in outs],
        out_shape=[jax.ShapeDtypeStruct((ngrp, n, LANES), F32)] * 3
        + [jax.ShapeDtypeStruct(kept_shape, F32)] * 2
        + [jax.ShapeDtypeStruct((n, w), dt) for w, dt in outs],
        compiler_params=_cparams(("arbitrary",)),
        name="in_proj",
    )(x, wp["g1"], wp["w_in"], wp["gq"], wp["gk"], wp["bd64"], wp["wa"], wp["ba"])


def _unroll_for(trips):
    return max(u for u in range(1, SWA_UNROLL + 1) if trips % u == 0)


def _ds(start, size, stride):
    return pl.ds(start, size) if stride == 1 else pl.ds(start, size, stride=stride)


def _swa_prompt_body(q_ref, kp_ref, kc_ref, vp_ref, vc_ref, o_ref, m_s, l_s, acc_s, k_keep, v_keep):
    i = pl.program_id(2)
    qb = SWA_BLOCK

    @pl.when(i == 0)
    def _():
        k_keep[...] = jnp.zeros_like(k_keep)
        v_keep[...] = jnp.zeros_like(v_keep)

    lane = lax.broadcasted_iota(I32, (SPAN, LANES), 1)
    lo_mask = lane < A_HEAD_DIM
    jq = lax.broadcasted_iota(I32, (SPAN, 2 * SPAN), 0)
    jk = lax.broadcasted_iota(I32, (SPAN, 2 * SPAN), 1)
    dist = jq + SPAN - jk
    band = (dist >= 0) & (dist <= SPAN)
    cur_half = jk >= SPAN

    def attend(qp, kp, vp, valid, rows, first_group):
        vp1 = jnp.concatenate([vp, jnp.ones_like(vp)], axis=1)
        res = []
        for hh in range(2):
            msk = lo_mask if hh == 0 else jnp.logical_not(lo_mask)
            qm = jnp.where(msk, qp, 0.0).astype(BF16)
            s = jnp.where(valid, _dot_nt(qm, kp), -jnp.inf)
            m = jnp.max(s, axis=1, keepdims=True)
            res.append((m, _dot(jnp.exp2(s - m).astype(BF16), vp1)))
        m_new = jnp.where(lo_mask, res[0][0], res[1][0])
        l_new = jnp.where(lo_mask, res[0][1][:, LANES:], res[1][1][:, LANES:])
        o_new = jnp.where(lo_mask, res[0][1][:, :LANES], res[1][1][:, :LANES])
        if first_group:
            m_s[rows, :] = m_new
            l_s[rows, :] = l_new
            acc_s[rows, :] = o_new
        else:
            m_old = m_s[rows, :]
            m = jnp.maximum(m_old, m_new)
            a_old = jnp.exp2(m_old - m)
            a_new = jnp.exp2(m_new - m)
            m_s[rows, :] = m
            l_s[rows, :] = l_s[rows, :] * a_old + l_new * a_new
            acc_s[rows, :] = acc_s[rows, :] * a_old + o_new * a_new

    for gi, (_, dil) in enumerate(reversed(DILATED_GROUPS)):
        unit = dil * SPAN
        nblk = qb // unit
        first = gi == 0

        def head_block(r, carry, dil=dil, unit=unit, first=first, nblk=nblk):
            rows = _ds(r, SPAN, dil)
            valid = band & (cur_half | (i > 0))
            if nblk == 1:
                keep = pl.ds(pl.multiple_of(r * SPAN, SPAN), SPAN)
                k_cur = kc_ref[rows, :].astype(BF16)
                v_cur = vc_ref[rows, :].astype(BF16)
                ks = jnp.concatenate([k_keep[keep, :], k_cur], axis=0)
                vs = jnp.concatenate([v_keep[keep, :], v_cur], axis=0)
                k_keep[keep, :] = k_cur
                v_keep[keep, :] = v_cur
            else:
                prev = _ds(qb - unit + r, SPAN, dil)
                ks = jnp.concatenate([kp_ref[prev, :], kc_ref[rows, :]], axis=0).astype(BF16)
                vs = jnp.concatenate([vp_ref[prev, :], vc_ref[rows, :]], axis=0).astype(BF16)
            attend(q_ref[rows, :], ks, vs, valid, rows, first)
            return carry

        lax.fori_loop(0, dil, head_block, 0, unroll=_unroll_for(dil))

        if nblk > 1:
            def tail_block(idx, carry, dil=dil, unit=unit, first=first):
                n = idx // dil + 1
                r = idx % dil
                start = unit * n + r
                rows = _ds(start, SPAN, dil)
                keys = _ds(start - unit, 2 * SPAN, dil)
                attend(q_ref[rows, :], kc_ref[keys, :].astype(BF16), vc_ref[keys, :].astype(BF16),
                       band, rows, first)
                return carry

            lax.fori_loop(0, (nblk - 1) * dil, tail_block, 0, unroll=_unroll_for((nblk - 1) * dil))

    o_ref[...] = (acc_s[...] / l_s[...]).astype(o_ref.dtype)


def _swa_prompt(q, k, v, bsz, seq):
    qb = SWA_BLOCK
    nb = seq // qb
    cur = pl.BlockSpec((None, qb, LANES), lambda p, b, i: (p, b * nb + i, 0))
    prev = pl.BlockSpec((None, qb, LANES), lambda p, b, i: (p, b * nb + jnp.maximum(i - 1, 0), 0))
    return pl.pallas_call(
        _swa_prompt_body,
        grid=(A_WIDTH // LANES, bsz, nb),
        in_specs=[cur, prev, cur, prev, cur],
        out_specs=pl.BlockSpec((qb, LANES), lambda p, b, i: (b * nb + i, p)),
        out_shape=jax.ShapeDtypeStruct((bsz * seq, A_WIDTH), BF16),
        scratch_shapes=[pltpu.VMEM((qb, LANES), F32)] * 3 + [pltpu.VMEM((qb, LANES), BF16)] * 2,
        compiler_params=_cparams(("parallel", "parallel", "arbitrary")),
        name="swa_prompt",
    )(q, k, k, v, v)


def _swa_sample_body(q_ref, kn_ref, vn_ref, ck_ref, cv_ref, c1_ref, c2_ref,
                     o_ref, nk_ref, nv_ref):
    t_new = q_ref.shape[1]
    w_buf = ck_ref.shape[2]
    ck, cv = ck_ref[0], cv_ref[0]
    kn, vn = kn_ref[0], vn_ref[0]
    tail_lane = lax.broadcasted_iota(I32, (A_WIDTH, LANES), 1) >= LANES - t_new

    def shift_in(old, new, out_ref):
        moved = pltpu.roll(old, w_buf - t_new, 1)
        new_t = jnp.transpose(jnp.concatenate([jnp.zeros((LANES - t_new, A_WIDTH), F32), new], axis=0))
        out_ref[0, :, :w_buf - LANES] = moved[:, :w_buf - LANES]
        out_ref[0, :, w_buf - LANES:] = jnp.where(tail_lane, new_t, moved[:, w_buf - LANES:])

    shift_in(ck, kn, nk_ref)
    shift_in(cv, vn, nv_ref)

    q = q_ref[0]
    lane = lax.broadcasted_iota(I32, (t_new, LANES), 1)
    lo_mask = lane < A_HEAD_DIM
    c1, c2 = c1_ref[...], c2_ref[...]
    outs = []
    for p in range(A_WIDTH // LANES):
        sl = slice(p * LANES, (p + 1) * LANES)
        qp = q[:, sl]
        qblk = jnp.concatenate([jnp.where(lo_mask, qp, 0.0), jnp.where(lo_mask, 0.0, qp)],
                               axis=0).astype(BF16)
        s1 = jnp.where(c1 > 0, _dot(qblk, ck[sl, :].astype(BF16)), -jnp.inf)
        s2 = jnp.where(c2 > 0, _dot_nt(qblk, kn[:, sl].astype(BF16)), -jnp.inf)
        m = jnp.maximum(jnp.max(s1, axis=1, keepdims=True), jnp.max(s2, axis=1, keepdims=True))
        p1 = c1 * jnp.exp2(s1 - m)
        p2 = c2 * jnp.exp2(s2 - m)
        l = jnp.sum(p1, axis=1, keepdims=True) + jnp.sum(p2, axis=1, keepdims=True)
        o = (_dot_nt(p1.astype(BF16), cv[sl, :].astype(BF16))
             + _dot(p2.astype(BF16), vn[:, sl].astype(BF16))) / l
        outs.append(jnp.where(lo_mask, o[:t_new], o[t_new:]))
    o_ref[0] = jnp.concatenate(outs, axis=1).astype(o_ref.dtype)


def _sample_multiplicity(t_new, w_buf):
    t = np.arange(t_new)[:, None]
    e = np.arange(w_buf + t_new)[None, :]
    d = w_buf + t - e
    c = np.zeros(d.shape, np.float32)
    for window, dil in DILATED_GROUPS:
        c += ((d >= 0) & (d % dil == 0) & (d <= window)).astype(np.float32)
    c = np.concatenate([c, c], axis=0)
    return jnp.asarray(c[:, :w_buf]), jnp.asarray(c[:, w_buf:])


def _swa_sample(q, kn, vn, cache_k, cache_v):
    db, t_new, w = q.shape
    w_buf = cache_k.shape[2]
    assert w_buf >= max(win for win, _ in DILATED_GROUPS) and t_new % 8 == 0 and t_new <= LANES
    c1, c2 = _sample_multiplicity(t_new, w_buf)
    new = pl.BlockSpec((1, t_new, w), lambda b: (b, 0, 0))
    cache = pl.BlockSpec((1, w, w_buf), lambda b: (b, 0, 0))
    return pl.pallas_call(
        _swa_sample_body,
        grid=(db,),
        in_specs=[new, new, new, cache, cache, _full(c1.shape), _full(c2.shape)],
        out_specs=[new, cache, cache],
        out_shape=[jax.ShapeDtypeStruct((db, t_new, w), BF16),
                   jax.ShapeDtypeStruct(cache_k.shape, cache_k.dtype),
                   jax.ShapeDtypeStruct(cache_v.shape, cache_v.dtype)],
        compiler_params=_cparams(("parallel",)),
        name="swa_sample",
    )(q, kn, vn, cache_k, cache_v, c1, c2)


def _gla_body(chunk, sub, nch, q_ref, k_ref, g_ref, v_ref, gate_ref, s0_ref, tril_ref, dmask_ref,
              bones_ref, sbm_ref, bd_ref, gout_ref, o_ref, sfin_ref, sbd):
    j = pl.program_id(1)
    sbm = sbm_ref[...]
    nsub = chunk // sub
    pad = B_KEY_DIM - chunk

    @pl.when(j == 0)
    def _():
        s0 = s0_ref[0]
        sbd[...] = jnp.concatenate([s0] * B_HEADS, axis=1) * sbm

    row = lax.broadcasted_iota(I32, (chunk, 1), 0)
    sub_id = row // sub
    lane_w = lax.broadcasted_iota(I32, (chunk, LANES * max(nsub - 1, 1)), 1)
    lo_w = (lane_w % LANES) < B_KEY_DIM

    def one_chunk(c, carry):
        off = pl.multiple_of(c * chunk, chunk)
        rows = pl.ds(off, chunk)
        q, k, g = q_ref[rows, :], k_ref[rows, :], g_ref[rows, :]
        v = v_ref[rows, :].astype(F32)
        g1 = g.astype(BF16)
        r1 = g - g1.astype(F32)
        g2 = r1.astype(BF16)
        g3 = (r1 - g2.astype(F32)).astype(BF16)
        tril = tril_ref[...]
        b = _dot(tril, g1) + _dot(tril, g2) + _dot(tril, g3)
        b_last = b[chunk - 1:chunk, :]
        state = sbd[...]

        o = _dot((q * jnp.exp(b)).astype(BF16), state.astype(BF16))

        bones = bones_ref[...]
        att = _dot((q * k).astype(BF16), bones) * dmask_ref[0]
        gate = jnp.exp(g)
        decay = gate
        for d in range(1, sub):
            if d > 1:
                decay = decay * pltpu.roll(gate, d - 1, 0)
            w = q * pltpu.roll(k, d, 0) * decay
            att = att + _dot(w.astype(BF16), bones) * dmask_ref[d]

        if nsub > 1:
            qx, kx = [], []
            for i in range(1, nsub):
                r_i = b[sub * i - 1:sub * i, :]
                qx.append(jnp.where(sub_id == i, q * jnp.exp(jnp.minimum(b - r_i, 0.0)), 0.0))
                kx.append(jnp.where(sub_id < i, k * jnp.exp(jnp.minimum(r_i - b, 0.0)), 0.0))
            parts = []
            for p in range(B_QK_WIDTH // LANES):
                sl = slice(p * LANES, (p + 1) * LANES)
                qp = jnp.concatenate([x[:, sl] for x in qx], axis=1)
                kp = jnp.concatenate([x[:, sl] for x in kx], axis=1).astype(BF16)
                zero = jnp.zeros_like(kp)
                lhs = jnp.concatenate([jnp.where(lo_w, qp, 0.0), jnp.where(lo_w, 0.0, qp)],
                                      axis=1).astype(BF16)
                rhs = jnp.concatenate([jnp.concatenate([kp, zero], axis=1),
                                       jnp.concatenate([zero, kp], axis=1)], axis=0)
                parts.append(_dot_nt(lhs, rhs))
            att = att + jnp.concatenate(parts, axis=1)

        if pad:
            vrow = jnp.concatenate([v, jnp.zeros((pad, B_V_WIDTH), F32)], axis=0)
        else:
            vrow = v
        vbd = (jnp.concatenate([vrow] * B_HEADS, axis=0) * sbm).astype(BF16)
        o = o + _dot(att.astype(BF16), vbd)

        ke = (k * jnp.exp(b_last - b)).astype(BF16)
        upd = lax.dot_general(ke, v.astype(BF16), (((0,), (0,)), ((), ())),
                              preferred_element_type=F32)
        dec = jnp.transpose(jnp.broadcast_to(jnp.exp(b_last), (8, B_QK_WIDTH)))[:, 0:1]
        sbd[...] = (state * dec + upd) * sbm

        ms = _dot((o * o).astype(BF16), bd_ref[...])
        on = o * lax.rsqrt(ms + EPS) * gout_ref[...] * gate_ref[rows, :].astype(F32)
        o_ref[rows, :] = on.astype(o_ref.dtype)
        return carry

    lax.fori_loop(0, nch, one_chunk, 0, unroll=2 if nch % 2 == 0 else 1)

    @pl.when(j == pl.num_programs(1) - 1)
    def _():
        s = sbd[...]
        sfin_ref[0] = jnp.concatenate(
            [s[h * B_KEY_DIM:(h + 1) * B_KEY_DIM, h * B_VAL_DIM:(h + 1) * B_VAL_DIM]
             for h in range(B_HEADS)], axis=0)


def _gla_consts(chunk, sub):
    t = np.arange(chunk)
    tril = (t[:, None] >= t[None, :]).astype(np.float32)
    lane = np.arange(B_QK_WIDTH)
    dmask = np.zeros((sub, chunk, B_QK_WIDTH), np.float32)
    for d in range(sub):
        ok = (t % sub) >= d
        dmask[d] = ((lane[None, :] % B_KEY_DIM) == (t[:, None] - d)) & ok[:, None]
    r = np.arange(B_QK_WIDTH)[:, None] // B_KEY_DIM
    c = np.arange(B_V_WIDTH)[None, :] // B_VAL_DIM
    sbm = (r == c).astype(np.float32)
    return jnp.asarray(tril, BF16), jnp.asarray(dmask), jnp.asarray(sbm)


def _gla(q, k, g, v, gate, s0, wp, length, chunk, sub, step):
    n = q.shape[0]
    bsz = n // length
    assert chunk == sub or chunk == B_KEY_DIM
    tril, dmask, sbm = _gla_consts(chunk, sub)
    nstep = length // step
    row = lambda w: pl.BlockSpec((step, w), lambda b, j: (b * nstep + j, 0))
    st = pl.BlockSpec((1, B_QK_WIDTH, B_VAL_DIM), lambda b, j: (b, 0, 0))
    return pl.pallas_call(
        functools.partial(_gla_body, chunk, sub, step // chunk),
        grid=(bsz, nstep),
        in_specs=[row(256), row(256), row(256), row(512), row(512), st, _full(tril.shape),
                  _full(dmask.shape), _full((256, 256)), _full(sbm.shape), _full((512, 512)),
                  _full((1, 512))],
        out_specs=[row(512), st],
        out_shape=[jax.ShapeDtypeStruct((n, B_V_WIDTH), BF16),
                   jax.ShapeDtypeStruct(s0.shape, F32)],
        scratch_shapes=[pltpu.VMEM((B_QK_WIDTH, B_V_WIDTH), F32)],
        compiler_params=_cparams(("parallel", "arbitrary")),
        name="gla",
    )(q, k, g, v, gate, s0, tril, dmask, wp["bones64"], sbm, wp["bd128"], wp["gout"])


def _head_rms(z, g, scale=1.0):
    parts = []
    for h in range(MEM_HEADS):
        zh = z[:, h * LANES:(h + 1) * LANES]
        parts.append(zh * lax.rsqrt(jnp.mean(zh * zh, axis=-1, keepdims=True) + EPS))
    return jnp.concatenate(parts, axis=1) * (g * scale)


def _mem_kv_body(m_ref, gn_ref, wk_ref, wv_ref, gk_ref, mk_ref, mv_ref):
    mn = _rms(m_ref[...], gn_ref[...]).astype(BF16)
    mk_ref[...] = _head_rms(_dot(mn, wk_ref[...]), gk_ref[...])
    mv_ref[...] = _dot(mn, wv_ref[...])


def _mem_kv(mem, wp):
    n = mem.shape[0]
    tm = 256
    row = lambda w: pl.BlockSpec((tm, w), lambda i: (i, 0))
    return pl.pallas_call(
        _mem_kv_body,
        grid=(n // tm,),
        in_specs=[row(D_MODEL), _full((1, D_MODEL)), _full((D_MODEL, 512)), _full((D_MODEL, 512)),
                  _full((1, 512))],
        out_specs=[row(512), row(512)],
        out_shape=[jax.ShapeDtypeStruct((n, 512), F32)] * 2,
        compiler_params=_cparams(("parallel",)),
        name="mem_kv",
    )(mem, wp["gmem"], wp["wmk"], wp["wmv"], wp["gmk"])


def _post_body(nseq, x_ref, oa_ref, ob_ref, woa_ref, wob_ref, g2_ref, wq_ref, gmq_ref, mk_ref,
               mv_ref, wo_ref, g3_ref, rw_ref, rb_ref, cnt0_ref, tri_ref,
               h2_ref, xn_ref, meta_ref, cnt_ref, carry):
    tm = x_ref.shape[0]

    @pl.when(pl.program_id(0) == 0)
    def _():
        carry[...] = cnt0_ref[...]

    nk = nseq * N_MEM

    def mem_head(ref, hd):
        if ref.shape[-1] == MEM_HEAD_DIM:
            rows = ref[:, pl.ds(hd, N_MEM, stride=MEM_HEADS), :]
        else:
            rows = ref[:, :, hd * LANES:(hd + 1) * LANES]
        return rows.reshape(nk, LANES).astype(BF16)

    if nseq > 1:
        rt = lax.broadcasted_iota(I32, (tm, nk), 0) // (tm // nseq)
        ct = lax.broadcasted_iota(I32, (tm, nk), 1) // N_MEM
        same = rt == ct

    def router_logits():
        h = x_ref[...] + _dot(oa_ref[...], woa_ref[...]) + _dot(ob_ref[...], wob_ref[...])
        hn = _rms(h, g2_ref[...]).astype(BF16)
        qm = _head_rms(_dot(hn, wq_ref[...]), gmq_ref[...], MEM_HEAD_DIM ** -0.5).astype(BF16)
        outs = []
        for hd in range(MEM_HEADS):
            sl = slice(hd * LANES, (hd + 1) * LANES)
            s = _dot_nt(qm[:, sl], mem_head(mk_ref, hd))
            if nseq > 1:
                s = jnp.where(same, s, -jnp.inf)
            m = jnp.max(s, axis=1, keepdims=True)
            pr = jnp.exp(s - m)
            l = jnp.sum(pr, axis=1, keepdims=True)
            outs.append(_dot(pr.astype(BF16), mem_head(mv_ref, hd)) / l)
        h2 = h + _dot(jnp.concatenate(outs, axis=1).astype(BF16), wo_ref[...])
        h2_ref[...] = h2
        xn = _rms(h2, g3_ref[...])
        xn_ref[...] = xn
        x1 = xn.astype(BF16)
        x2 = (xn - x1.astype(F32)).astype(BF16)
        prod = _dot(jnp.concatenate([x1, x2], axis=0), rw_ref[...])
        return prod[:tm, :LANES] + prod[:tm, LANES:] + prod[tm:, :LANES] + prod[tm:, LANES:]

    logits = router_logits() + rb_ref[...]
    lane = lax.broadcasted_iota(I32, (tm, LANES), 1)
    vals, idxs, hots = [], [], []
    work = logits
    for _ in range(TOP_K):
        m = jnp.max(work, axis=1, keepdims=True)
        idx = jnp.min(jnp.where(work == m, lane, LANES), axis=1, keepdims=True)
        hot = lane == idx
        vals.append(m)
        idxs.append(idx)
        hots.append(hot)
        work = jnp.where(hot, -jnp.inf, work)
    exps = [jnp.exp(v - vals[0]) for v in vals]
    den = exps[0] + exps[1] + exps[2] + exps[3]

    sel = (hots[0] | hots[1] | hots[2] | hots[3]).astype(F32)
    before = _dot(tri_ref[...], sel.astype(BF16)) + carry[...]
    carry[...] = carry[...] + jnp.sum(sel, axis=0, keepdims=True)
    cnt_ref[...] = carry[...]

    meta = jnp.zeros((tm, LANES), F32)
    for kk in range(TOP_K):
        rank = jnp.sum(jnp.where(hots[kk], before, 0.0), axis=1, keepdims=True)
        meta = jnp.where(lane == kk, idxs[kk].astype(F32), meta)
        meta = jnp.where(lane == TOP_K + kk, rank, meta)
        meta = jnp.where(lane == 2 * TOP_K + kk, exps[kk] / den, meta)
    meta_ref[...] = meta


def _post(x, oa, ob, mk, mv, cnt0, wp, tm, nseq, tiles_per_mem):
    n = x.shape[0]
    row = lambda w: pl.BlockSpec((tm, w), lambda i: (i, 0))
    mem = pl.BlockSpec((nseq,) + mk.shape[1:], lambda i: (i // tiles_per_mem, 0, 0))
    tri = jnp.asarray(np.tril(np.ones((tm, tm), np.float32), -1), BF16)
    return pl.pallas_call(
        functools.partial(_post_body, nseq),
        grid=(n // tm,),
        in_specs=[row(D_MODEL), row(512), row(512), _full((512, D_MODEL)), _full((512, D_MODEL)),
                  _full((1, D_MODEL)), _full((D_MODEL, 512)), _full((1, 512)), mem, mem,
                  _full((512, D_MODEL)), _full((1, D_MODEL)), _full((D_MODEL, 2 * LANES)),
                  _full((1, LANES)), _full((1, LANES)), _full((tm, tm))],
        out_specs=[row(D_MODEL), row(D_MODEL), row(LANES), _full((1, LANES))],
        out_shape=[jax.ShapeDtypeStruct((n, D_MODEL), F32), jax.ShapeDtypeStruct((n, D_MODEL), F32),
                   jax.ShapeDtypeStruct((n, LANES), F32), jax.ShapeDtypeStruct((1, LANES), F32)],
        scratch_shapes=[pltpu.VMEM((1, LANES), F32)],
        compiler_params=_cparams(("arbitrary",)),
        name="post",
    )(x, oa, ob, wp["woa"], wp["wob"], wp["g2"], wp["wmq"], wp["gmq"], mk, mv, wp["wmo"],
      wp["g3"], wp["rw"], wp["rb"], cnt0, tri)


def _start_rows(src, src_row, dst, dst_row, n_rows, sem):
    s = pl.multiple_of(jnp.asarray(src_row, I32), GROUP_ROWS)
    d = pl.multiple_of(jnp.asarray(dst_row, I32), GROUP_ROWS)
    n_rows = pl.multiple_of(jnp.asarray(n_rows, I32), GROUP_ROWS)

    @pl.when(n_rows > 0)
    def _():
        pltpu.make_async_copy(src.at[pl.ds(s, n_rows)], dst.at[pl.ds(d, n_rows)], sem).start()

    return n_rows


def _wait_rows(like_src, like_dst, n_rows, sem):
    n_rows = pl.multiple_of(n_rows, GROUP_ROWS)

    @pl.when(n_rows > 0)
    def _():
        pltpu.make_async_copy(like_src.at[pl.ds(0, n_rows)], like_dst.at[pl.ds(0, n_rows)], sem).wait()


def _pad_rows_body(pstart_ref, prows_ref, nt_ref, xs_ref, zeros, sem):
    zeros[...] = jnp.zeros_like(zeros)
    n_tiles_max = xs_ref.shape[0] // EXPERT_TILE

    def expert_tail(e, total):
        return total + _start_rows(zeros, 0, xs_ref, pstart_ref[e], prows_ref[e], sem)

    def unused_tile(t, total):
        return total + _start_rows(zeros, 0, xs_ref, t * EXPERT_TILE, EXPERT_TILE, sem)

    total = lax.fori_loop(0, N_EXPERTS, expert_tail, jnp.int32(0))
    total = lax.fori_loop(nt_ref[0], n_tiles_max, unused_tile, total)
    _wait_rows(xs_ref, xs_ref, total, sem)


def _pad_rows(pad_start, pad_units, n_tiles, rows):
    return pl.pallas_call(
        _pad_rows_body,
        grid_spec=pltpu.PrefetchScalarGridSpec(
            num_scalar_prefetch=3, grid=(1,),
            in_specs=[],
            out_specs=pl.BlockSpec(memory_space=pltpu.HBM),
            scratch_shapes=[pltpu.VMEM((EXPERT_TILE, D_MODEL), F32), pltpu.SemaphoreType.DMA]),
        out_shape=jax.ShapeDtypeStruct((rows, D_MODEL), F32),
        compiler_params=_cparams(("arbitrary",)),
        name="moe_pad_rows",
    )(pad_start, pad_units, n_tiles)


def _dispatch_body(blk0, loff_ref, gstart_ref, nrows_ref, ldest_ref, x_ref, xs_in_ref, xs_ref,
                   xloc, sems):
    del xs_in_ref
    i = pl.program_id(0)
    b = i + blk0
    slot = i % 2
    tb = x_ref.shape[0]
    rows = lax.broadcasted_iota(I32, (LOCAL_ROWS, tb), 0)
    ld = ldest_ref[...]
    hot = rows == ld[0:1, :]
    for kk in range(1, TOP_K):
        hot = hot | (rows == ld[kk:kk + 1, :])
    xloc[slot] = _dot(jnp.where(hot, 1.0, 0.0).astype(BF16), x_ref[...].astype(BF16))

    def block_rows(bb):
        j = bb * N_EXPERTS + N_EXPERTS - 1
        return loff_ref[j] + nrows_ref[j]

    @pl.when(i > 0)
    def _():
        _wait_rows(xloc.at[1 - slot], xs_ref, block_rows(b - 1), sems.at[1 - slot])

    def group(e, carry):
        j = b * N_EXPERTS + e
        _start_rows(xloc.at[slot], loff_ref[j], xs_ref, gstart_ref[j], nrows_ref[j], sems.at[slot])
        return carry

    lax.fori_loop(0, N_EXPERTS, group, 0)

    @pl.when(i == pl.num_programs(0) - 1)
    def _():
        _wait_rows(xloc.at[slot], xs_ref, block_rows(b), sems.at[slot])


def _dispatch(tables, blk0, ldest_t, xn, xs):
    n = xn.shape[0]
    tb = DISPATCH_BLOCK
    idx = lambda i, *_: (i, 0)
    return pl.pallas_call(
        functools.partial(_dispatch_body, blk0),
        grid_spec=pltpu.PrefetchScalarGridSpec(
            num_scalar_prefetch=3, grid=(n // tb,),
            in_specs=[pl.BlockSpec((TOP_K, tb), lambda i, *_: (0, i)),
                      pl.BlockSpec((tb, D_MODEL), idx),
                      pl.BlockSpec(memory_space=pltpu.HBM)],
            out_specs=pl.BlockSpec(memory_space=pltpu.HBM),
            scratch_shapes=[pltpu.VMEM((2, LOCAL_ROWS, D_MODEL), F32), pltpu.SemaphoreType.DMA((2,))]),
        out_shape=jax.ShapeDtypeStruct(xs.shape, xs.dtype),
        input_output_aliases={5: 0},
        compiler_params=_cparams(("arbitrary",)),
        name="moe_dispatch",
    )(*tables, ldest_t, xn, xs)


def _expert_body(te_ref, first_ref, nt_ref, x_ref, w1_ref, sel_ref, b1g_ref, b1l_ref, w2_ref, b2_ref,
                 y_ref, w1g, w1l, w2):
    i = pl.program_id(0)
    live = i < nt_ref[0]

    @pl.when(live & (first_ref[i] == 1))
    def _():
        sel = sel_ref[...]
        for j in range(D_FF // LANES):
            z = _dot(w1_ref[0, :, 2 * LANES * j:2 * LANES * (j + 1)].astype(BF16), sel)
            w1g[:, LANES * j:LANES * (j + 1)] = z[:, :LANES].astype(BF16)
            w1l[:, LANES * j:LANES * (j + 1)] = z[:, LANES:].astype(BF16)
        w2[...] = w2_ref[0].astype(BF16)

    @pl.when(live)
    def _():
        x = x_ref[...].astype(BF16)
        glu = jnp.minimum(_dot(x, w1g[...]) + b1g_ref[0], SWIGLU_LIMIT)
        lin = jnp.clip(_dot(x, w1l[...]) + b1l_ref[0], -SWIGLU_LIMIT, SWIGLU_LIMIT)
        act = glu * jax.nn.sigmoid(SWIGLU_ALPHA * glu) * (lin + 1.0)
        y_ref[...] = _dot(act.astype(BF16), w2[...]) + b2_ref[0]

    @pl.when(jnp.logical_not(live))
    def _():
        y_ref[...] = jnp.zeros_like(y_ref)


def _experts(tile_expert, n_tiles, xs, wp):
    rows = xs.shape[0]
    first = jnp.concatenate([jnp.ones((1,), I32),
                             (tile_expert[1:] != tile_expert[:-1]).astype(I32)])
    c = np.arange(2 * LANES)
    sel = np.zeros((2 * LANES, 2 * LANES), np.float32)
    sel[c, (c % 2) * LANES + c // 2] = 1.0
    tile = lambda i, te, fi, nt: (jnp.minimum(i, nt[0] - 1), 0)
    out_tile = lambda i, te, fi, nt: (i, 0)
    wsel = lambda i, te, fi, nt: (te[jnp.minimum(i, nt[0] - 1)], 0, 0)
    wspec = lambda r, c: pl.BlockSpec((1, r, c), wsel)
    return pl.pallas_call(
        _expert_body,
        grid_spec=pltpu.PrefetchScalarGridSpec(
            num_scalar_prefetch=3, grid=(rows // EXPERT_TILE,),
            in_specs=[pl.BlockSpec((EXPERT_TILE, D_MODEL), tile),
                      wspec(D_MODEL, 2 * D_FF), pl.BlockSpec((2 * LANES, 2 * LANES), lambda *_: (0, 0)),
                      wspec(1, D_FF), wspec(1, D_FF), wspec(D_FF, D_MODEL), wspec(1, D_MODEL)],
            out_specs=pl.BlockSpec((EXPERT_TILE, D_MODEL), out_tile),
            scratch_shapes=[pltpu.VMEM((D_MODEL, D_FF), BF16)] * 2 + [pltpu.VMEM((D_FF, D_MODEL), BF16)]),
        out_shape=jax.ShapeDtypeStruct((rows, D_MODEL), F32),
        compiler_params=_cparams(("arbitrary",)),
        name="moe_experts",
    )(tile_expert, first, n_tiles, xs, wp["w1"], jnp.asarray(sel, BF16), wp["b1g"], wp["b1l"],
      wp["w2"], wp["b2"])


def _combine_body(blk0, loff_ref, gstart_ref, nrows_ref, ldest_ref, meta_ref, h_ref, ys_ref, o_ref,
                  yloc, sems):
    i = pl.program_id(0)
    b = i + blk0
    slot = i % 2
    tb = h_ref.shape[0]

    def gather(bb, s):
        def group(e, carry):
            j = bb * N_EXPERTS + e
            _start_rows(ys_ref, gstart_ref[j], yloc.at[s], loff_ref[j], nrows_ref[j], sems.at[s])
            return carry
        lax.fori_loop(0, N_EXPERTS, group, 0)

    @pl.when(i == 0)
    def _():
        yloc[...] = jnp.zeros_like(yloc)
        gather(b, slot)

    @pl.when(i + 1 < pl.num_programs(0))
    def _():
        gather(b + 1, 1 - slot)

    j_last = b * N_EXPERTS + N_EXPERTS - 1
    total = loff_ref[j_last] + nrows_ref[j_last]
    cols = lax.broadcasted_iota(I32, (tb, LOCAL_ROWS), 1)
    ld = ldest_ref[...]
    meta = meta_ref[...]
    gmat = jnp.zeros((tb, LOCAL_ROWS), F32)
    for kk in range(TOP_K):
        gate = meta[:, 2 * TOP_K + kk:2 * TOP_K + kk + 1]
        gmat = jnp.where(cols == ld[:, kk:kk + 1], gate, gmat)
    _wait_rows(ys_ref, yloc.at[slot], total, sems.at[slot])
    o_ref[...] = h_ref[...] + _dot(gmat.astype(BF16), yloc[slot].astype(BF16))


def _combine(tables, blk0, ldest, meta, h2, ys):
    n = h2.shape[0]
    tb = DISPATCH_BLOCK
    row = lambda w: pl.BlockSpec((tb, w), lambda i, *_: (i, 0))
    return pl.pallas_call(
        functools.partial(_combine_body, blk0),
        grid_spec=pltpu.PrefetchScalarGridSpec(
            num_scalar_prefetch=3, grid=(n // tb,),
            in_specs=[row(TOP_K), row(LANES), row(D_MODEL), pl.BlockSpec(memory_space=pltpu.HBM)],
            out_specs=row(D_MODEL),
            scratch_shapes=[pltpu.VMEM((2, LOCAL_ROWS, D_MODEL), F32), pltpu.SemaphoreType.DMA((2,))]),
        out_shape=jax.ShapeDtypeStruct((n, D_MODEL), F32),
        compiler_params=_cparams(("arbitrary",)),
        name="moe_combine",
    )(*tables, ldest, meta, h2, ys)


def _moe(groups, wp):
    tb = DISPATCH_BLOCK
    sizes = [g[0].shape[0] for g in groups]
    assert all(s % tb == 0 for s in sizes)
    n_tok = sum(sizes)
    nb = n_tok // tb
    rows = n_tok * TOP_K + nb * N_EXPERTS * GROUP_ROWS + N_EXPERTS * EXPERT_TILE
    n_tiles_max = rows // EXPERT_TILE

    eidx = jnp.concatenate([g[2][:, 0:TOP_K] for g in groups]).astype(I32)
    rank = jnp.concatenate([g[2][:, TOP_K:2 * TOP_K] for g in groups]).astype(I32)
    hot = eidx[:, :, None] == jnp.arange(N_EXPERTS, dtype=I32)
    cnt = jnp.sum(hot.reshape(nb, tb * TOP_K, N_EXPERTS), axis=1, dtype=I32)
    npad = (cnt + GROUP_ROWS - 1) // GROUP_ROWS * GROUP_ROWS
    loff = jnp.cumsum(npad, axis=1) - npad
    gsize = jnp.sum(npad, axis=0)
    gpad = (gsize + EXPERT_TILE - 1) // EXPERT_TILE * EXPERT_TILE
    ends = jnp.cumsum(gpad)
    gstart = (ends - gpad)[None, :] + jnp.cumsum(npad, axis=0) - npad
    before = jnp.cumsum(cnt, axis=0) - cnt
    n_tiles = (ends[-1] // EXPERT_TILE).reshape(1)
    tile_ids = jnp.arange(n_tiles_max, dtype=I32)
    tile_expert = jnp.minimum(
        jnp.sum((ends // EXPERT_TILE)[None, :] <= tile_ids[:, None], axis=1), N_EXPERTS - 1).astype(I32)

    base = jnp.repeat(loff - before, tb, axis=0)
    ldest = rank + jnp.sum(jnp.where(hot, base[:, None, :], 0), axis=2)
    ldest_t = ldest.T
    tables = (loff.reshape(-1), gstart.reshape(-1).astype(I32), npad.reshape(-1))

    xs = _pad_rows(((ends - gpad) + gsize).astype(I32), (gpad - gsize).astype(I32),
                   n_tiles, rows)
    starts = np.cumsum([0] + sizes[:-1])
    for (_, xn, _), t0, sz in zip(groups, starts, sizes):
        xs = _dispatch(tables, int(t0) // tb, ldest_t[:, t0:t0 + sz], xn, xs)
    ys = _experts(tile_expert, n_tiles, xs, wp)
    return [_combine(tables, int(t0) // tb, ldest[t0:t0 + sz], meta, h2, ys)
            for (h2, _, meta), t0, sz in zip(groups, starts, sizes)]


def _prep_weights(norm1_g, w_in, a_q_norm_g, a_k_norm_g, gla_w_alpha, gla_b_alpha, gla_out_norm_g,
                  w_out, norm2_g, mem_norm_g, mem_w_q, mem_w_k, mem_w_v, mem_q_norm_g,
                  mem_k_norm_g, mem_w_o, norm3_g, router_w, router_b, exp_w1, exp_b1, exp_w2,
                  exp_b2):
    main = 3 * A_WIDTH + 2 * B_QK_WIDTH + 2 * B_V_WIDTH
    w_lr = jnp.pad(w_in[:, main:], ((0, 0), (0, LANES - GATE_RANK)))
    rw = jnp.pad(router_w, ((0, 0), (0, LANES - N_EXPERTS)))
    rwh = rw.astype(BF16)
    return {
        "g1": norm1_g[None],
        "w_in": jnp.concatenate([w_in[:, :main], w_lr], axis=1).astype(BF16),
        "gq": jnp.tile(a_q_norm_g, A_HEADS)[None],
        "gk": jnp.tile(a_k_norm_g, A_HEADS)[None],
        "bd64": _block_diag(A_WIDTH, A_HEAD_DIM, 1.0 / A_HEAD_DIM, BF16),
        "wa": jnp.pad(gla_w_alpha, ((0, LANES - GATE_RANK), (0, 0))).astype(BF16),
        "ba": gla_b_alpha[None],
        "bones64": _block_diag(B_QK_WIDTH, B_KEY_DIM, 1.0, BF16),
        "bd128": _block_diag(B_V_WIDTH, B_VAL_DIM, 1.0 / B_VAL_DIM, BF16),
        "gout": jnp.tile(gla_out_norm_g, B_HEADS)[None],
        "woa": w_out[:A_WIDTH].astype(BF16),
        "wob": w_out[A_WIDTH:].astype(BF16),
        "g2": norm2_g[None],
        "gmem": mem_norm_g[None],
        "wmq": mem_w_q.astype(BF16),
        "wmk": mem_w_k.astype(BF16),
        "wmv": mem_w_v.astype(BF16),
        "gmq": jnp.tile(mem_q_norm_g, MEM_HEADS)[None],
        "gmk": jnp.tile(mem_k_norm_g, MEM_HEADS)[None],
        "wmo": mem_w_o.astype(BF16),
        "g3": norm3_g[None],
        "rw": jnp.concatenate([rwh, (rw - rwh.astype(F32)).astype(BF16)], axis=1),
        "rb": jnp.pad(router_b, (0, LANES - N_EXPERTS), constant_values=-1e30)[None],
        "w1": exp_w1,
        "b1g": exp_b1[:, None, 0::2],
        "b1l": exp_b1[:, None, 1::2],
        "w2": exp_w2,
        "b2": exp_b2[:, None, :],
    }


def _layer(xp, xs, mem_prompt, cache_k, cache_v, state_gla, cache_mk, cache_mv, wp):
    bsz, seq, _ = xp.shape
    db, t_new, _ = xs.shape
    w_p = min(max(w for w, _ in DILATED_GROUPS), seq)

    xpf = xp.reshape(bsz * seq, D_MODEL)
    q, k, v, k_last, v_last, bq, bk, bv, gate, la = _in_proj(xpf, wp, BF16, seq, w_p)

    def last_rows(a):
        return a.transpose(0, 2, 1) if w_p != seq else a.reshape(bsz, w_p, A_WIDTH)
    oa = _swa_prompt(q, k, v, bsz, seq)
    s0 = jnp.zeros((bsz, B_QK_WIDTH, B_VAL_DIM), F32)
    ob, s_p = _gla(bq, bk, la, bv, gate, s0, wp, seq, GLA_CHUNK, GLA_SUB, GLA_STEP)
    mk, mv = _mem_kv(mem_prompt.reshape(bsz * N_MEM, D_MODEL), wp)
    cnt0 = jnp.zeros((1, LANES), F32)
    h2_p, xn_p, meta_p, cnt = _post(xpf, oa, ob, mk.reshape(bsz, N_MEM, MEM_WIDTH),
                                    mv.reshape(bsz, N_MEM, MEM_WIDTH), cnt0, wp,
                                    POST_TILE, 1, seq // POST_TILE)

    xsf = xs.reshape(db * t_new, D_MODEL)
    q, _, _, ks, vs, bq, bk, bv, gate, la = _in_proj(xsf, wp, F32, t_new, t_new)
    new3 = lambda a: a.reshape(db, t_new, A_WIDTH)
    oa_s, nk, nv = _swa_sample(new3(q.transpose(1, 0, 2)), new3(ks), new3(vs), cache_k, cache_v)
    ob_s, s_s = _gla(bq, bk, la, bv, gate, state_gla, wp, t_new, t_new, t_new, t_new)
    h2_s, xn_s, meta_s, cnt = _post(xsf, oa_s.reshape(db * t_new, A_WIDTH), ob_s, cache_mk,
                                    cache_mv, cnt, wp, SAMPLE_SEQS * t_new, SAMPLE_SEQS, 1)

    y_p, y_s = _moe([(h2_p, xn_p, meta_p), (h2_s, xn_s, meta_s)], wp)
    return (y_p.reshape(bsz, seq, D_MODEL), y_s.reshape(db, t_new, D_MODEL),
            last_rows(k_last), last_rows(v_last),
            s_p, mk, mv, nk, nv, s_s)


def kernel(x_prompt, x_sample, mem_prompt, cache_swa_k, cache_swa_v, state_gla, cache_mem_k, cache_mem_v, norm1_g, w_in, a_q_norm_g, a_k_norm_g, gla_w_alpha, gla_b_alpha, gla_out_norm_g, w_out, norm2_g, mem_norm_g, mem_w_q, mem_w_k, mem_w_v, mem_q_norm_g, mem_k_norm_g, mem_w_o, norm3_g, router_w, router_b, exp_w1, exp_b1, exp_w2, exp_b2):
    depth = w_in.shape[0]
    bsz = x_prompt.shape[0]
    db, w_buf = cache_swa_k.shape[1], cache_swa_k.shape[2]
    xp, xs = x_prompt, x_sample
    per_layer = []
    for l in range(depth):
        wp = _prep_weights(
            norm1_g[l], w_in[l], a_q_norm_g[l], a_k_norm_g[l], gla_w_alpha[l], gla_b_alpha[l],
            gla_out_norm_g[l], w_out[l], norm2_g[l], mem_norm_g[l], mem_w_q[l], mem_w_k[l],
            mem_w_v[l], mem_q_norm_g[l], mem_k_norm_g[l], mem_w_o[l], norm3_g[l], router_w[l],
            router_b[l], exp_w1[l], exp_b1[l], exp_w2[l], exp_b2[l])
        xp, xs, kp, vp, s_p, mk, mv, nk, nv, s_s = _layer(
            xp, xs, mem_prompt,
            cache_swa_k[l].reshape(db, w_buf, A_WIDTH).transpose(0, 2, 1),
            cache_swa_v[l].reshape(db, w_buf, A_WIDTH).transpose(0, 2, 1),
            state_gla[l].reshape(db, B_QK_WIDTH, B_VAL_DIM),
            cache_mem_k[l].reshape(db, N_MEM * MEM_HEADS, MEM_HEAD_DIM),
            cache_mem_v[l].reshape(db, N_MEM * MEM_HEADS, MEM_HEAD_DIM),
            wp)
        w_p = kp.shape[1]
        per_layer.append((
            kp.reshape(bsz, w_p, A_HEADS, A_HEAD_DIM), vp.reshape(bsz, w_p, A_HEADS, A_HEAD_DIM),
            s_p.reshape(bsz, B_HEADS, B_KEY_DIM, B_VAL_DIM),
            mk.reshape(bsz, N_MEM, MEM_HEADS, MEM_HEAD_DIM), mv.reshape(bsz, N_MEM, MEM_HEADS, MEM_HEAD_DIM),
            nk.transpose(0, 2, 1).reshape(db, w_buf, A_HEADS, A_HEAD_DIM),
            nv.transpose(0, 2, 1).reshape(db, w_buf, A_HEADS, A_HEAD_DIM),
            s_s.reshape(db, B_HEADS, B_KEY_DIM, B_VAL_DIM)))
    stacked = [jnp.stack(t) for t in zip(*per_layer)]
    return (xp, xs, *stacked)
```

```python
import functools

import jax
import jax.numpy as jnp
import numpy as np
from jax import lax
from jax.experimental import pallas as pl
from jax.experimental.pallas import tpu as pltpu

F32 = jnp.float32
BF16 = jnp.bfloat16
I32 = jnp.int32

EPS = 1e-6
D_MODEL = 1024
A_HEADS, A_HEAD_DIM, A_WIDTH = 8, 64, 512
A_SCALE = A_HEAD_DIM ** -0.5
LOG2E = 1.4426950408889634
DILATED_GROUPS = ((128, 1), (512, 4), (2048, 16))
SPAN = 128
B_HEADS, B_KEY_DIM, B_VAL_DIM = 4, 64, 128
B_QK_WIDTH, B_V_WIDTH = 256, 512
GATE_RANK = 16
GATE_TEMP = 16.0
N_MEM, MEM_HEADS, MEM_HEAD_DIM, MEM_WIDTH = 256, 4, 128, 512
N_EXPERTS, TOP_K, D_FF = 32, 4, 1024
SWIGLU_ALPHA, SWIGLU_LIMIT = 1.702, 7.0

LANES = 128
IN_COLS = tuple(int(c) for c in np.cumsum(
    [0, A_WIDTH, A_WIDTH, A_WIDTH, B_QK_WIDTH, B_QK_WIDTH, B_V_WIDTH, B_V_WIDTH, LANES]))
VMEM_LIMIT = 56 * 1024 * 1024

PROJ_TILE = 512
SWA_BLOCK = 2048
SWA_UNROLL = 8
GLA_CHUNK, GLA_SUB, GLA_STEP = 64, 8, 1024
POST_TILE = 1024
SAMPLE_SEQS = 8
EXPERT_TILE = 512
DISPATCH_BLOCK = 512
GROUP_ROWS = 8
LOCAL_ROWS = DISPATCH_BLOCK * TOP_K + N_EXPERTS * GROUP_ROWS


def _cparams(sem, vmem=VMEM_LIMIT, **kw):
    return pltpu.CompilerParams(dimension_semantics=sem, vmem_limit_bytes=vmem, **kw)


def _full(shape):
    n = len(shape)
    return pl.BlockSpec(shape, lambda *_: (0,) * n)


def _rms(x, g):
    ms = jnp.mean(x * x, axis=-1, keepdims=True)
    return x * lax.rsqrt(ms + EPS) * g


def _dot(a, b):
    return jnp.dot(a, b, preferred_element_type=F32)


def _dot_nt(a, b):
    return lax.dot_general(a, b, (((1,), (1,)), ((), ())), preferred_element_type=F32)


def _block_diag(n, blk, val, dtype):
    i = np.arange(n)
    return jnp.asarray(np.where((i[:, None] // blk) == (i[None, :] // blk), val, 0.0), dtype)


def _in_proj_body(keep_transposed, x_ref, g1_ref, w_ref, gq_ref, gk_ref, bd_ref, wa_ref, ba_ref,
                  q_ref, k_ref, v_ref, kc_ref, vc_ref, bq_ref, bk_ref, bv_ref, gate_ref, la_ref):
    xn = _rms(x_ref[...], g1_ref[...]).astype(BF16)
    bd = bd_ref[...]

    def proj(lo, hi):
        return _dot(xn, w_ref[:, lo:hi])

    def headnorm(z, g):
        ms = _dot((z * z).astype(BF16), bd)
        return z * lax.rsqrt(ms + EPS) * g

    def put_groups(ref, z):
        for p in range(A_WIDTH // LANES):
            ref[p] = z[:, p * LANES:(p + 1) * LANES]

    c = IN_COLS
    put_groups(q_ref, headnorm(proj(c[0], c[1]), gq_ref[...]) * (A_SCALE * LOG2E))
    k = headnorm(proj(c[1], c[2]), gk_ref[...])
    v = proj(c[2], c[3])
    put_groups(k_ref, k)
    put_groups(v_ref, v)
    kc_ref[...] = jnp.transpose(k) if keep_transposed else k
    vc_ref[...] = jnp.transpose(v) if keep_transposed else v
    bq_ref[...] = proj(c[3], c[4]) * (B_KEY_DIM ** -0.5)
    bk_ref[...] = proj(c[4], c[5])
    bv_ref[...] = proj(c[5], c[6]).astype(bv_ref.dtype)
    br = proj(c[6], c[7])
    gate_ref[...] = (br * jax.nn.sigmoid(br)).astype(gate_ref.dtype)
    lr = proj(c[7], c[8]).astype(BF16)
    pre = _dot(lr, wa_ref[...]) + ba_ref[...]
    log_sig = jnp.minimum(pre, 0.0) - jnp.log1p(jnp.exp(-jnp.abs(pre)))
    la_ref[...] = log_sig * (1.0 / GATE_TEMP)


def _in_proj(x, wp, wide_dtype, seq, keep):
    n = x.shape[0]
    tm = min(PROJ_TILE, n)
    row = lambda w: pl.BlockSpec((tm, w), lambda i: (i, 0))
    ngrp = A_WIDTH // LANES
    grp = pl.BlockSpec((ngrp, tm, LANES), lambda i: (0, i, 0))
    if keep == seq:
        kept = row(A_WIDTH)
        kept_shape = (n, A_WIDTH)
    else:
        tps, kt = seq // tm, keep // tm
        assert tps * tm == seq and kt * tm == keep
        kept = pl.BlockSpec((None, A_WIDTH, tm),
                            lambda i: (i // tps, 0, jnp.maximum(i % tps - (tps - kt), 0)))
        kept_shape = (n // seq, A_WIDTH, keep)
    outs = [(B_QK_WIDTH, F32), (B_QK_WIDTH, F32), (B_V_WIDTH, wide_dtype), (B_V_WIDTH, wide_dtype),
            (B_QK_WIDTH, F32)]
    return pl.pallas_call(
        functools.partial(_in_proj_body, keep != seq),
        grid=(n // tm,),
        in_specs=[row(D_MODEL), _full((1, D_MODEL)), _full((D_MODEL, IN_COLS[-1])),
                  _full((1, A_WIDTH)), _full((1, A_WIDTH)), _full((A_WIDTH, A_WIDTH)),
                  _full((LANES, B_QK_WIDTH)), _full((1, B_QK_WIDTH))],
        out_specs=[grp] * 3 + [kept] * 2 + [row(w) for w, _ in outs],
        out_shape=[jax.ShapeDtypeStruct((ngrp, n, LANES), F32)] * 3
        + [jax.ShapeDtypeStruct(kept_shape, F32)] * 2
        + [jax.ShapeDtypeStruct((n, w), dt) for w, dt in outs],
        compiler_params=_cparams(("arbitrary",)),
        name="in_proj",
    )(x, wp["g1"], wp["w_in"], wp["gq"], wp["gk"], wp["bd64"], wp["wa"], wp["ba"])


def _unroll_for(trips):
    return max(u for u in range(1, SWA_UNROLL + 1) if trips % u == 0)


def _ds(start, size, stride):
    return pl.ds(start, size) if stride == 1 else pl.ds(start, size, stride=stride)


def _swa_prompt_body(q_ref, kp_ref, kc_ref, vp_ref, vc_ref, o_ref, m_s, l_s, acc_s, k_keep, v_keep):
    i = pl.program_id(2)
    qb = SWA_BLOCK

    @pl.when(i == 0)
    def _():
        k_keep[...] = jnp.zeros_like(k_keep)
        v_keep[...] = jnp.zeros_like(v_keep)

    lane = lax.broadcasted_iota(I32, (SPAN, LANES), 1)
    lo_mask = lane < A_HEAD_DIM
    jq = lax.broadcasted_iota(I32, (SPAN, 2 * SPAN), 0)
    jk = lax.broadcasted_iota(I32, (SPAN, 2 * SPAN), 1)
    dist = jq + SPAN - jk
    band = (dist >= 0) & (dist <= SPAN)
    cur_half = jk >= SPAN

    def attend(qp, kp, vp, valid, rows, first_group):
        vp1 = jnp.concatenate([vp, jnp.ones_like(vp)], axis=1)
        res = []
        for hh in range(2):
            msk = lo_mask if hh == 0 else jnp.logical_not(lo_mask)
            qm = jnp.where(msk, qp, 0.0).astype(BF16)
            s = jnp.where(valid, _dot_nt(qm, kp), -jnp.inf)
            m = jnp.max(s, axis=1, keepdims=True)
            res.append((m, _dot(jnp.exp2(s - m).astype(BF16), vp1)))
        m_new = jnp.where(lo_mask, res[0][0], res[1][0])
        l_new = jnp.where(lo_mask, res[0][1][:, LANES:], res[1][1][:, LANES:])
        o_new = jnp.where(lo_mask, res[0][1][:, :LANES], res[1][1][:, :LANES])
        if first_group:
            m_s[rows, :] = m_new
            l_s[rows, :] = l_new
            acc_s[rows, :] = o_new
        else:
            m_old = m_s[rows, :]
            m = jnp.maximum(m_old, m_new)
            a_old = jnp.exp2(m_old - m)
            a_new = jnp.exp2(m_new - m)
            m_s[rows, :] = m
            l_s[rows, :] = l_s[rows, :] * a_old + l_new * a_new
            acc_s[rows, :] = acc_s[rows, :] * a_old + o_new * a_new

    for gi, (_, dil) in enumerate(reversed(DILATED_GROUPS)):
        unit = dil * SPAN
        nblk = qb // unit
        first = gi == 0

        def head_block(r, carry, dil=dil, unit=unit, first=first, nblk=nblk):
            rows = _ds(r, SPAN, dil)
            valid = band & (cur_half | (i > 0))
            if nblk == 1:
                keep = pl.ds(pl.multiple_of(r * SPAN, SPAN), SPAN)
                k_cur = kc_ref[rows, :].astype(BF16)
                v_cur = vc_ref[rows, :].astype(BF16)
                ks = jnp.concatenate([k_keep[keep, :], k_cur], axis=0)
                vs = jnp.concatenate([v_keep[keep, :], v_cur], axis=0)
                k_keep[keep, :] = k_cur
                v_keep[keep, :] = v_cur
            else:
                prev = _ds(qb - unit + r, SPAN, dil)
                ks = jnp.concatenate([kp_ref[prev, :], kc_ref[rows, :]], axis=0).astype(BF16)
                vs = jnp.concatenate([vp_ref[prev, :], vc_ref[rows, :]], axis=0).astype(BF16)
            attend(q_ref[rows, :], ks, vs, valid, rows, first)
            return carry

        lax.fori_loop(0, dil, head_block, 0, unroll=_unroll_for(dil))

        if nblk > 1:
            def tail_block(idx, carry, dil=dil, unit=unit, first=first):
                n = idx // dil + 1
                r = idx % dil
                start = unit * n + r
                rows = _ds(start, SPAN, dil)
                keys = _ds(start - unit, 2 * SPAN, dil)
                attend(q_ref[rows, :], kc_ref[keys, :].astype(BF16), vc_ref[keys, :].astype(BF16),
                       band, rows, first)
                return carry

            lax.fori_loop(0, (nblk - 1) * dil, tail_block, 0, unroll=_unroll_for((nblk - 1) * dil))

    o_ref[...] = (acc_s[...] / l_s[...]).astype(o_ref.dtype)


def _swa_prompt(q, k, v, bsz, seq):
    qb = SWA_BLOCK
    nb = seq // qb
    cur = pl.BlockSpec((None, qb, LANES), lambda p, b, i: (p, b * nb + i, 0))
    prev = pl.BlockSpec((None, qb, LANES), lambda p, b, i: (p, b * nb + jnp.maximum(i - 1, 0), 0))
    return pl.pallas_call(
        _swa_prompt_body,
        grid=(A_WIDTH // LANES, bsz, nb),
        in_specs=[cur, prev, cur, prev, cur],
        out_specs=pl.BlockSpec((qb, LANES), lambda p, b, i: (b * nb + i, p)),
        out_shape=jax.ShapeDtypeStruct((bsz * seq, A_WIDTH), BF16),
        scratch_shapes=[pltpu.VMEM((qb, LANES), F32)] * 3 + [pltpu.VMEM((qb, LANES), BF16)] * 2,
        compiler_params=_cparams(("parallel", "parallel", "arbitrary")),
        name="swa_prompt",
    )(q, k, k, v, v)


def _swa_sample_body(q_ref, kn_ref, vn_ref, ck_ref, cv_ref, c1_ref, c2_ref,
                     o_ref, nk_ref, nv_ref):
    t_new = q_ref.shape[1]
    w_buf = ck_ref.shape[2]
    ck, cv = ck_ref[0], cv_ref[0]
    kn, vn = kn_ref[0], vn_ref[0]
    tail_lane = lax.broadcasted_iota(I32, (A_WIDTH, LANES), 1) >= LANES - t_new

    def shift_in(old, new, out_ref):
        moved = pltpu.roll(old, w_buf - t_new, 1)
        new_t = jnp.transpose(jnp.concatenate([jnp.zeros((LANES - t_new, A_WIDTH), F32), new], axis=0))
        out_ref[0, :, :w_buf - LANES] = moved[:, :w_buf - LANES]
        out_ref[0, :, w_buf - LANES:] = jnp.where(tail_lane, new_t, moved[:, w_buf - LANES:])

    shift_in(ck, kn, nk_ref)
    shift_in(cv, vn, nv_ref)

    q = q_ref[0]
    lane = lax.broadcasted_iota(I32, (t_new, LANES), 1)
    lo_mask = lane < A_HEAD_DIM
    c1, c2 = c1_ref[...], c2_ref[...]
    outs = []
    for p in range(A_WIDTH // LANES):
        sl = slice(p * LANES, (p + 1) * LANES)
        qp = q[:, sl]
        qblk = jnp.concatenate([jnp.where(lo_mask, qp, 0.0), jnp.where(lo_mask, 0.0, qp)],
                               axis=0).astype(BF16)
        s1 = jnp.where(c1 > 0, _dot(qblk, ck[sl, :].astype(BF16)), -jnp.inf)
        s2 = jnp.where(c2 > 0, _dot_nt(qblk, kn[:, sl].astype(BF16)), -jnp.inf)
        m = jnp.maximum(jnp.max(s1, axis=1, keepdims=True), jnp.max(s2, axis=1, keepdims=True))
        p1 = c1 * jnp.exp2(s1 - m)
        p2 = c2 * jnp.exp2(s2 - m)
        l = jnp.sum(p1, axis=1, keepdims=True) + jnp.sum(p2, axis=1, keepdims=True)
        o = (_dot_nt(p1.astype(BF16), cv[sl, :].astype(BF16))
             + _dot(p2.astype(BF16), vn[:, sl].astype(BF16))) / l
        outs.append(jnp.where(lo_mask, o[:t_new], o[t_new:]))
    o_ref[0] = jnp.concatenate(outs, axis=1).astype(o_ref.dtype)


def _sample_multiplicity(t_new, w_buf):
    t = np.arange(t_new)[:, None]
    e = np.arange(w_buf + t_new)[None, :]
    d = w_buf + t - e
    c = np.zeros(d.shape, np.float32)
    for window, dil in DILATED_GROUPS:
        c += ((d >= 0) & (d % dil == 0) & (d <= window)).astype(np.float32)
    c = np.concatenate([c, c], axis=0)
    return jnp.asarray(c[:, :w_buf]), jnp.asarray(c[:, w_buf:])


def _swa_sample(q, kn, vn, cache_k, cache_v):
    db, t_new, w = q.shape
    w_buf = cache_k.shape[2]
    assert w_buf >= max(win for win, _ in DILATED_GROUPS) and t_new % 8 == 0 and t_new <= LANES
    c1, c2 = _sample_multiplicity(t_new, w_buf)
    new = pl.BlockSpec((1, t_new, w), lambda b: (b, 0, 0))
    cache = pl.BlockSpec((1, w, w_buf), lambda b: (b, 0, 0))
    return pl.pallas_call(
        _swa_sample_body,
        grid=(db,),
        in_specs=[new, new, new, cache, cache, _full(c1.shape), _full(c2.shape)],
        out_specs=[new, cache, cache],
        out_shape=[jax.ShapeDtypeStruct((db, t_new, w), BF16),
                   jax.ShapeDtypeStruct(cache_k.shape, cache_k.dtype),
                   jax.ShapeDtypeStruct(cache_v.shape, cache_v.dtype)],
        compiler_params=_cparams(("parallel",)),
        name="swa_sample",
    )(q, kn, vn, cache_k, cache_v, c1, c2)


def _gla_body(chunk, sub, nch, q_ref, k_ref, g_ref, v_ref, gate_ref, s0_ref, tril_ref, dmask_ref,
              bones_ref, sbm_ref, bd_ref, gout_ref, o_ref, sfin_ref, sbd):
    j = pl.program_id(1)
    sbm = sbm_ref[...]
    nsub = chunk // sub
    pad = B_KEY_DIM - chunk

    @pl.when(j == 0)
    def _():
        s0 = s0_ref[0]
        sbd[...] = jnp.concatenate([s0] * B_HEADS, axis=1) * sbm

    row = lax.broadcasted_iota(I32, (chunk, 1), 0)
    sub_id = row // sub
    lane_w = lax.broadcasted_iota(I32, (chunk, LANES * max(nsub - 1, 1)), 1)
    lo_w = (lane_w % LANES) < B_KEY_DIM

    def one_chunk(c, carry):
        off = pl.multiple_of(c * chunk, chunk)
        rows = pl.ds(off, chunk)
        q, k, g = q_ref[rows, :], k_ref[rows, :], g_ref[rows, :]
        v = v_ref[rows, :].astype(F32)
        g1 = g.astype(BF16)
        r1 = g - g1.astype(F32)
        g2 = r1.astype(BF16)
        g3 = (r1 - g2.astype(F32)).astype(BF16)
        tril = tril_ref[...]
        b = _dot(tril, g1) + _dot(tril, g2) + _dot(tril, g3)
        b_last = b[chunk - 1:chunk, :]
        state = sbd[...]

        o = _dot((q * jnp.exp(b)).astype(BF16), state.astype(BF16))

        bones = bones_ref[...]
        att = _dot((q * k).astype(BF16), bones) * dmask_ref[0]
        gate = jnp.exp(g)
        decay = gate
        for d in range(1, sub):
            if d > 1:
                decay = decay * pltpu.roll(gate, d - 1, 0)
            w = q * pltpu.roll(k, d, 0) * decay
            att = att + _dot(w.astype(BF16), bones) * dmask_ref[d]

        if nsub > 1:
            qx, kx = [], []
            for i in range(1, nsub):
                r_i = b[sub * i - 1:sub * i, :]
                qx.append(jnp.where(sub_id == i, q * jnp.exp(jnp.minimum(b - r_i, 0.0)), 0.0))
                kx.append(jnp.where(sub_id < i, k * jnp.exp(jnp.minimum(r_i - b, 0.0)), 0.0))
            parts = []
            for p in range(B_QK_WIDTH // LANES):
                sl = slice(p * LANES, (p + 1) * LANES)
                qp = jnp.concatenate([x[:, sl] for x in qx], axis=1)
                kp = jnp.concatenate([x[:, sl] for x in kx], axis=1).astype(BF16)
                zero = jnp.zeros_like(kp)
                lhs = jnp.concatenate([jnp.where(lo_w, qp, 0.0), jnp.where(lo_w, 0.0, qp)],
                                      axis=1).astype(BF16)
                rhs = jnp.concatenate([jnp.concatenate([kp, zero], axis=1),
                                       jnp.concatenate([zero, kp], axis=1)], axis=0)
                parts.append(_dot_nt(lhs, rhs))
            att = att + jnp.concatenate(parts, axis=1)

        if pad:
            vrow = jnp.concatenate([v, jnp.zeros((pad, B_V_WIDTH), F32)], axis=0)
        else:
            vrow = v
        vbd = (jnp.concatenate([vrow] * B_HEADS, axis=0) * sbm).astype(BF16)
        o = o + _dot(att.astype(BF16), vbd)

        ke = (k * jnp.exp(b_last - b)).astype(BF16)
        upd = lax.dot_general(ke, v.astype(BF16), (((0,), (0,)), ((), ())),
                              preferred_element_type=F32)
        dec = jnp.transpose(jnp.broadcast_to(jnp.exp(b_last), (8, B_QK_WIDTH)))[:, 0:1]
        sbd[...] = (state * dec + upd) * sbm

        ms = _dot((o * o).astype(BF16), bd_ref[...])
        on = o * lax.rsqrt(ms + EPS) * gout_ref[...] * gate_ref[rows, :].astype(F32)
        o_ref[rows, :] = on.astype(o_ref.dtype)
        return carry

    lax.fori_loop(0, nch, one_chunk, 0, unroll=2 if nch % 2 == 0 else 1)

    @pl.when(j == pl.num_programs(1) - 1)
    def _():
        s = sbd[...]
        sfin_ref[0] = jnp.concatenate(
            [s[h * B_KEY_DIM:(h + 1) * B_KEY_DIM, h * B_VAL_DIM:(h + 1) * B_VAL_DIM]
             for h in range(B_HEADS)], axis=0)


def _gla_consts(chunk, sub):
    t = np.arange(chunk)
    tril = (t[:, None] >= t[None, :]).astype(np.float32)
    lane = np.arange(B_QK_WIDTH)
    dmask = np.zeros((sub, chunk, B_QK_WIDTH), np.float32)
    for d in range(sub):
        ok = (t % sub) >= d
        dmask[d] = ((lane[None, :] % B_KEY_DIM) == (t[:, None] - d)) & ok[:, None]
    r = np.arange(B_QK_WIDTH)[:, None] // B_KEY_DIM
    c = np.arange(B_V_WIDTH)[None, :] // B_VAL_DIM
    sbm = (r == c).astype(np.float32)
    return jnp.asarray(tril, BF16), jnp.asarray(dmask), jnp.asarray(sbm)


def _gla(q, k, g, v, gate, s0, wp, length, chunk, sub, step):
    n = q.shape[0]
    bsz = n // length
    assert chunk == sub or chunk == B_KEY_DIM
    tril, dmask, sbm = _gla_consts(chunk, sub)
    nstep = length // step
    row = lambda w: pl.BlockSpec((step, w), lambda b, j: (b * nstep + j, 0))
    st = pl.BlockSpec((1, B_QK_WIDTH, B_VAL_DIM), lambda b, j: (b, 0, 0))
    return pl.pallas_call(
        functools.partial(_gla_body, chunk, sub, step // chunk),
        grid=(bsz, nstep),
        in_specs=[row(256), row(256), row(256), row(512), row(512), st, _full(tril.shape),
                  _full(dmask.shape), _full((256, 256)), _full(sbm.shape), _full((512, 512)),
                  _full((1, 512))],
        out_specs=[row(512), st],
        out_shape=[jax.ShapeDtypeStruct((n, B_V_WIDTH), BF16),
                   jax.ShapeDtypeStruct(s0.shape, F32)],
        scratch_shapes=[pltpu.VMEM((B_QK_WIDTH, B_V_WIDTH), F32)],
        compiler_params=_cparams(("parallel", "arbitrary")),
        name="gla",
    )(q, k, g, v, gate, s0, tril, dmask, wp["bones64"], sbm, wp["bd128"], wp["gout"])


def _head_rms(z, g, scale=1.0):
    parts = []
    for h in range(MEM_HEADS):
        zh = z[:, h * LANES:(h + 1) * LANES]
        parts.append(zh * lax.rsqrt(jnp.mean(zh * zh, axis=-1, keepdims=True) + EPS))
    return jnp.concatenate(parts, axis=1) * (g * scale)


def _mem_kv_body(m_ref, gn_ref, wk_ref, wv_ref, gk_ref, mk_ref, mv_ref):
    mn = _rms(m_ref[...], gn_ref[...]).astype(BF16)
    mk_ref[...] = _head_rms(_dot(mn, wk_ref[...]), gk_ref[...])
    mv_ref[...] = _dot(mn, wv_ref[...])


def _mem_kv(mem, wp):
    n = mem.shape[0]
    tm = 256
    row = lambda w: pl.BlockSpec((tm, w), lambda i: (i, 0))
    return pl.pallas_call(
        _mem_kv_body,
        grid=(n // tm,),
        in_specs=[row(D_MODEL), _full((1, D_MODEL)), _full((D_MODEL, 512)), _full((D_MODEL, 512)),
                  _full((1, 512))],
        out_specs=[row(512), row(512)],
        out_shape=[jax.ShapeDtypeStruct((n, 512), F32)] * 2,
        compiler_params=_cparams(("parallel",)),
        name="mem_kv",
    )(mem, wp["gmem"], wp["wmk"], wp["wmv"], wp["gmk"])


def _post_body(nseq, x_ref, oa_ref, ob_ref, woa_ref, wob_ref, g2_ref, wq_ref, gmq_ref, mk_ref,
               mv_ref, wo_ref, g3_ref, rw_ref, rb_ref, cnt0_ref, tri_ref,
               h2_ref, xn_ref, meta_ref, cnt_ref, carry):
    tm = x_ref.shape[0]

    @pl.when(pl.program_id(0) == 0)
    def _():
        carry[...] = cnt0_ref[...]

    nk = nseq * N_MEM

    def mem_head(ref, hd):
        if ref.shape[-1] == MEM_HEAD_DIM:
            rows = ref[:, pl.ds(hd, N_MEM, stride=MEM_HEADS), :]
        else:
            rows = ref[:, :, hd * LANES:(hd + 1) * LANES]
        return rows.reshape(nk, LANES).astype(BF16)

    if nseq > 1:
        rt = lax.broadcasted_iota(I32, (tm, nk), 0) // (tm // nseq)
        ct = lax.broadcasted_iota(I32, (tm, nk), 1) // N_MEM
        same = rt == ct

    def router_logits():
        h = x_ref[...] + _dot(oa_ref[...], woa_ref[...]) + _dot(ob_ref[...], wob_ref[...])
        hn = _rms(h, g2_ref[...]).astype(BF16)
        qm = _head_rms(_dot(hn, wq_ref[...]), gmq_ref[...], MEM_HEAD_DIM ** -0.5).astype(BF16)
        outs = []
        for hd in range(MEM_HEADS):
            sl = slice(hd * LANES, (hd + 1) * LANES)
            s = _dot_nt(qm[:, sl], mem_head(mk_ref, hd))
            if nseq > 1:
                s = jnp.where(same, s, -jnp.inf)
            m = jnp.max(s, axis=1, keepdims=True)
            pr = jnp.exp(s - m)
            l = jnp.sum(pr, axis=1, keepdims=True)
            outs.append(_dot(pr.astype(BF16), mem_head(mv_ref, hd)) / l)
        h2 = h + _dot(jnp.concatenate(outs, axis=1).astype(BF16), wo_ref[...])
        h2_ref[...] = h2
        xn = _rms(h2, g3_ref[...])
        xn_ref[...] = xn
        x1 = xn.astype(BF16)
        x2 = (xn - x1.astype(F32)).astype(BF16)
        prod = _dot(jnp.concatenate([x1, x2], axis=0), rw_ref[...])
        return prod[:tm, :LANES] + prod[:tm, LANES:] + prod[tm:, :LANES] + prod[tm:, LANES:]

    logits = router_logits() + rb_ref[...]
    lane = lax.broadcasted_iota(I32, (tm, LANES), 1)
    vals, idxs, hots = [], [], []
    work = logits
    for _ in range(TOP_K):
        m = jnp.max(work, axis=1, keepdims=True)
        idx = jnp.min(jnp.where(work == m, lane, LANES), axis=1, keepdims=True)
        hot = lane == idx
        vals.append(m)
        idxs.append(idx)
        hots.append(hot)
        work = jnp.where(hot, -jnp.inf, work)
    exps = [jnp.exp(v - vals[0]) for v in vals]
    den = exps[0] + exps[1] + exps[2] + exps[3]

    sel = (hots[0] | hots[1] | hots[2] | hots[3]).astype(F32)
    before = _dot(tri_ref[...], sel.astype(BF16)) + carry[...]
    carry[...] = carry[...] + jnp.sum(sel, axis=0, keepdims=True)
    cnt_ref[...] = carry[...]

    meta = jnp.zeros((tm, LANES), F32)
    for kk in range(TOP_K):
        rank = jnp.sum(jnp.where(hots[kk], before, 0.0), axis=1, keepdims=True)
        meta = jnp.where(lane == kk, idxs[kk].astype(F32), meta)
        meta = jnp.where(lane == TOP_K + kk, rank, meta)
        meta = jnp.where(lane == 2 * TOP_K + kk, exps[kk] / den, meta)
    meta_ref[...] = meta


def _post(x, oa, ob, mk, mv, cnt0, wp, tm, nseq, tiles_per_mem):
    n = x.shape[0]
    row = lambda w: pl.BlockSpec((tm, w), lambda i: (i, 0))
    mem = pl.BlockSpec((nseq,) + mk.shape[1:], lambda i: (i // tiles_per_mem, 0, 0))
    tri = jnp.asarray(np.tril(np.ones((tm, tm), np.float32), -1), BF16)
    return pl.pallas_call(
        functools.partial(_post_body, nseq),
        grid=(n // tm,),
        in_specs=[row(D_MODEL), row(512), row(512), _full((512, D_MODEL)), _full((512, D_MODEL)),
                  _full((1, D_MODEL)), _full((D_MODEL, 512)), _full((1, 512)), mem, mem,
                  _full((512, D_MODEL)), _full((1, D_MODEL)), _full((D_MODEL, 2 * LANES)),
                  _full((1, LANES)), _full((1, LANES)), _full((tm, tm))],
        out_specs=[row(D_MODEL), row(D_MODEL), row(LANES), _full((1, LANES))],
        out_shape=[jax.ShapeDtypeStruct((n, D_MODEL), F32), jax.ShapeDtypeStruct((n, D_MODEL), F32),
                   jax.ShapeDtypeStruct((n, LANES), F32), jax.ShapeDtypeStruct((1, LANES), F32)],
        scratch_shapes=[pltpu.VMEM((1, LANES), F32)],
        compiler_params=_cparams(("arbitrary",)),
        name="post",
    )(x, oa, ob, wp["woa"], wp["wob"], wp["g2"], wp["wmq"], wp["gmq"], mk, mv, wp["wmo"],
      wp["g3"], wp["rw"], wp["rb"], cnt0, tri)


def _start_rows(src, src_row, dst, dst_row, n_rows, sem):
    s = pl.multiple_of(jnp.asarray(src_row, I32), GROUP_ROWS)
    d = pl.multiple_of(jnp.asarray(dst_row, I32), GROUP_ROWS)
    n_rows = pl.multiple_of(jnp.asarray(n_rows, I32), GROUP_ROWS)

    @pl.when(n_rows > 0)
    def _():
        pltpu.make_async_copy(src.at[pl.ds(s, n_rows)], dst.at[pl.ds(d, n_rows)], sem).start()

    return n_rows


def _wait_rows(like_src, like_dst, n_rows, sem):
    n_rows = pl.multiple_of(n_rows, GROUP_ROWS)

    @pl.when(n_rows > 0)
    def _():
        pltpu.make_async_copy(like_src.at[pl.ds(0, n_rows)], like_dst.at[pl.ds(0, n_rows)], sem).wait()


def _pad_rows_body(pstart_ref, prows_ref, nt_ref, xs_ref, zeros, sem):
    zeros[...] = jnp.zeros_like(zeros)
    n_tiles_max = xs_ref.shape[0] // EXPERT_TILE

    def expert_tail(e, total):
        return total + _start_rows(zeros, 0, xs_ref, pstart_ref[e], prows_ref[e], sem)

    def unused_tile(t, total):
        return total + _start_rows(zeros, 0, xs_ref, t * EXPERT_TILE, EXPERT_TILE, sem)

    total = lax.fori_loop(0, N_EXPERTS, expert_tail, jnp.int32(0))
    total = lax.fori_loop(nt_ref[0], n_tiles_max, unused_tile, total)
    _wait_rows(xs_ref, xs_ref, total, sem)


def _pad_rows(pad_start, pad_units, n_tiles, rows):
    return pl.pallas_call(
        _pad_rows_body,
        grid_spec=pltpu.PrefetchScalarGridSpec(
            num_scalar_prefetch=3, grid=(1,),
            in_specs=[],
            out_specs=pl.BlockSpec(memory_space=pltpu.HBM),
            scratch_shapes=[pltpu.VMEM((EXPERT_TILE, D_MODEL), F32), pltpu.SemaphoreType.DMA]),
        out_shape=jax.ShapeDtypeStruct((rows, D_MODEL), F32),
        compiler_params=_cparams(("arbitrary",)),
        name="moe_pad_rows",
    )(pad_start, pad_units, n_tiles)


def _dispatch_body(blk0, loff_ref, gstart_ref, nrows_ref, ldest_ref, x_ref, xs_in_ref, xs_ref,
                   xloc, sems):
    del xs_in_ref
    i = pl.program_id(0)
    b = i + blk0
    slot = i % 2
    tb = x_ref.shape[0]
    rows = lax.broadcasted_iota(I32, (LOCAL_ROWS, tb), 0)
    ld = ldest_ref[...]
    hot = rows == ld[0:1, :]
    for kk in range(1, TOP_K):
        hot = hot | (rows == ld[kk:kk + 1, :])
    xloc[slot] = _dot(jnp.where(hot, 1.0, 0.0).astype(BF16), x_ref[...].astype(BF16))

    def block_rows(bb):
        j = bb * N_EXPERTS + N_EXPERTS - 1
        return loff_ref[j] + nrows_ref[j]

    @pl.when(i > 0)
    def _():
        _wait_rows(xloc.at[1 - slot], xs_ref, block_rows(b - 1), sems.at[1 - slot])

    def group(e, carry):
        j = b * N_EXPERTS + e
        _start_rows(xloc.at[slot], loff_ref[j], xs_ref, gstart_ref[j], nrows_ref[j], sems.at[slot])
        return carry

    lax.fori_loop(0, N_EXPERTS, group, 0)

    @pl.when(i == pl.num_programs(0) - 1)
    def _():
        _wait_rows(xloc.at[slot], xs_ref, block_rows(b), sems.at[slot])


def _dispatch(tables, blk0, ldest_t, xn, xs):
    n = xn.shape[0]
    tb = DISPATCH_BLOCK
    idx = lambda i, *_: (i, 0)
    return pl.pallas_call(
        functools.partial(_dispatch_body, blk0),
        grid_spec=pltpu.PrefetchScalarGridSpec(
            num_scalar_prefetch=3, grid=(n // tb,),
            in_specs=[pl.BlockSpec((TOP_K, tb), lambda i, *_: (0, i)),
                      pl.BlockSpec((tb, D_MODEL), idx),
                      pl.BlockSpec(memory_space=pltpu.HBM)],
            out_specs=pl.BlockSpec(memory_space=pltpu.HBM),
            scratch_shapes=[pltpu.VMEM((2, LOCAL_ROWS, D_MODEL), F32), pltpu.SemaphoreType.DMA((2,))]),
        out_shape=jax.ShapeDtypeStruct(xs.shape, xs.dtype),
        input_output_aliases={5: 0},
        compiler_params=_cparams(("arbitrary",)),
        name="moe_dispatch",
    )(*tables, ldest_t, xn, xs)


def _expert_body(te_ref, first_ref, nt_ref, x_ref, w1_ref, sel_ref, b1g_ref, b1l_ref, w2_ref, b2_ref,
                 y_ref, w1g, w1l, w2):
    i = pl.program_id(0)
    live = i < nt_ref[0]

    @pl.when(live & (first_ref[i] == 1))
    def _():
        sel = sel_ref[...]
        for j in range(D_FF // LANES):
            z = _dot(w1_ref[0, :, 2 * LANES * j:2 * LANES * (j + 1)].astype(BF16), sel)
            w1g[:, LANES * j:LANES * (j + 1)] = z[:, :LANES].astype(BF16)
            w1l[:, LANES * j:LANES * (j + 1)] = z[:, LANES:].astype(BF16)
        w2[...] = w2_ref[0].astype(BF16)

    @pl.when(live)
    def _():
        x = x_ref[...].astype(BF16)
        glu = jnp.minimum(_dot(x, w1g[...]) + b1g_ref[0], SWIGLU_LIMIT)
        lin = jnp.clip(_dot(x, w1l[...]) + b1l_ref[0], -SWIGLU_LIMIT, SWIGLU_LIMIT)
        act = glu * jax.nn.sigmoid(SWIGLU_ALPHA * glu) * (lin + 1.0)
        y_ref[...] = _dot(act.astype(BF16), w2[...]) + b2_ref[0]

    @pl.when(jnp.logical_not(live))
    def _():
        y_ref[...] = jnp.zeros_like(y_ref)


def _experts(tile_expert, n_tiles, xs, wp):
    rows = xs.shape[0]
    first = jnp.concatenate([jnp.ones((1,), I32),
                             (tile_expert[1:] != tile_expert[:-1]).astype(I32)])
    c = np.arange(2 * LANES)
    sel = np.zeros((2 * LANES, 2 * LANES), np.float32)
    sel[c, (c % 2) * LANES + c // 2] = 1.0
    tile = lambda i, te, fi, nt: (jnp.minimum(i, nt[0] - 1), 0)
    out_tile = lambda i, te, fi, nt: (i, 0)
    wsel = lambda i, te, fi, nt: (te[jnp.minimum(i, nt[0] - 1)], 0, 0)
    wspec = lambda r, c: pl.BlockSpec((1, r, c), wsel)
    return pl.pallas_call(
        _expert_body,
        grid_spec=pltpu.PrefetchScalarGridSpec(
            num_scalar_prefetch=3, grid=(rows // EXPERT_TILE,),
            in_specs=[pl.BlockSpec((EXPERT_TILE, D_MODEL), tile),
                      wspec(D_MODEL, 2 * D_FF), pl.BlockSpec((2 * LANES, 2 * LANES), lambda *_: (0, 0)),
                      wspec(1, D_FF), wspec(1, D_FF), wspec(D_FF, D_MODEL), wspec(1, D_MODEL)],
            out_specs=pl.BlockSpec((EXPERT_TILE, D_MODEL), out_tile),
            scratch_shapes=[pltpu.VMEM((D_MODEL, D_FF), BF16)] * 2 + [pltpu.VMEM((D_FF, D_MODEL), BF16)]),
        out_shape=jax.ShapeDtypeStruct((rows, D_MODEL), F32),
        compiler_params=_cparams(("arbitrary",)),
        name="moe_experts",
    )(tile_expert, first, n_tiles, xs, wp["w1"], jnp.asarray(sel, BF16), wp["b1g"], wp["b1l"],
      wp["w2"], wp["b2"])


def _combine_body(blk0, loff_ref, gstart_ref, nrows_ref, ldest_ref, meta_ref, h_ref, ys_ref, o_ref,
                  yloc, sems):
    i = pl.program_id(0)
    b = i + blk0
    slot = i % 2
    tb = h_ref.shape[0]

    def gather(bb, s):
        def group(e, carry):
            j = bb * N_EXPERTS + e
            _start_rows(ys_ref, gstart_ref[j], yloc.at[s], loff_ref[j], nrows_ref[j], sems.at[s])
            return carry
        lax.fori_loop(0, N_EXPERTS, group, 0)

    @pl.when(i == 0)
    def _():
        yloc[...] = jnp.zeros_like(yloc)
        gather(b, slot)

    @pl.when(i + 1 < pl.num_programs(0))
    def _():
        gather(b + 1, 1 - slot)

    j_last = b * N_EXPERTS + N_EXPERTS - 1
    total = loff_ref[j_last] + nrows_ref[j_last]
    cols = lax.broadcasted_iota(I32, (tb, LOCAL_ROWS), 1)
    ld = ldest_ref[...]
    meta = meta_ref[...]
    gmat = jnp.zeros((tb, LOCAL_ROWS), F32)
    for kk in range(TOP_K):
        gate = meta[:, 2 * TOP_K + kk:2 * TOP_K + kk + 1]
        gmat = jnp.where(cols == ld[:, kk:kk + 1], gate, gmat)
    _wait_rows(ys_ref, yloc.at[slot], total, sems.at[slot])
    o_ref[...] = h_ref[...] + _dot(gmat.astype(BF16), yloc[slot].astype(BF16))


def _combine(tables, blk0, ldest, meta, h2, ys):
    n = h2.shape[0]
    tb = DISPATCH_BLOCK
    row = lambda w: pl.BlockSpec((tb, w), lambda i, *_: (i, 0))
    return pl.pallas_call(
        functools.partial(_combine_body, blk0),
        grid_spec=pltpu.PrefetchScalarGridSpec(
            num_scalar_prefetch=3, grid=(n // tb,),
            in_specs=[row(TOP_K), row(LANES), row(D_MODEL), pl.BlockSpec(memory_space=pltpu.HBM)],
            out_specs=row(D_MODEL),
            scratch_shapes=[pltpu.VMEM((2, LOCAL_ROWS, D_MODEL), F32), pltpu.SemaphoreType.DMA((2,))]),
        out_shape=jax.ShapeDtypeStruct((n, D_MODEL), F32),
        compiler_params=_cparams(("arbitrary",)),
        name="moe_combine",
    )(*tables, ldest, meta, h2, ys)


def _moe(groups, wp):
    tb = DISPATCH_BLOCK
    sizes = [g[0].shape[0] for g in groups]
    assert all(s % tb == 0 for s in sizes)
    n_tok = sum(sizes)
    nb = n_tok // tb
    rows = n_tok * TOP_K + nb * N_EXPERTS * GROUP_ROWS + N_EXPERTS * EXPERT_TILE
    n_tiles_max = rows // EXPERT_TILE

    eidx = jnp.concatenate([g[2][:, 0:TOP_K] for g in groups]).astype(I32)
    rank = jnp.concatenate([g[2][:, TOP_K:2 * TOP_K] for g in groups]).astype(I32)
    hot = eidx[:, :, None] == jnp.arange(N_EXPERTS, dtype=I32)
    cnt = jnp.sum(hot.reshape(nb, tb * TOP_K, N_EXPERTS), axis=1, dtype=I32)
    npad = (cnt + GROUP_ROWS - 1) // GROUP_ROWS * GROUP_ROWS
    loff = jnp.cumsum(npad, axis=1) - npad
    gsize = jnp.sum(npad, axis=0)
    gpad = (gsize + EXPERT_TILE - 1) // EXPERT_TILE * EXPERT_TILE
    ends = jnp.cumsum(gpad)
    gstart = (ends - gpad)[None, :] + jnp.cumsum(npad, axis=0) - npad
    before = jnp.cumsum(cnt, axis=0) - cnt
    n_tiles = (ends[-1] // EXPERT_TILE).reshape(1)
    tile_ids = jnp.arange(n_tiles_max, dtype=I32)
    tile_expert = jnp.minimum(
        jnp.sum((ends // EXPERT_TILE)[None, :] <= tile_ids[:, None], axis=1), N_EXPERTS - 1).astype(I32)

    base = jnp.repeat(loff - before, tb, axis=0)
    ldest = rank + jnp.sum(jnp.where(hot, base[:, None, :], 0), axis=2)
    ldest_t = ldest.T
    tables = (loff.reshape(-1), gstart.reshape(-1).astype(I32), npad.reshape(-1))

    xs = _pad_rows(((ends - gpad) + gsize).astype(I32), (gpad - gsize).astype(I32),
                   n_tiles, rows)
    starts = np.cumsum([0] + sizes[:-1])
    for (_, xn, _), t0, sz in zip(groups, starts, sizes):
        xs = _dispatch(tables, int(t0) // tb, ldest_t[:, t0:t0 + sz], xn, xs)
    ys = _experts(tile_expert, n_tiles, xs, wp)
    return [_combine(tables, int(t0) // tb, ldest[t0:t0 + sz], meta, h2, ys)
            for (h2, _, meta), t0, sz in zip(groups, starts, sizes)]


def _prep_weights(norm1_g, w_in, a_q_norm_g, a_k_norm_g, gla_w_alpha, gla_b_alpha, gla_out_norm_g,
                  w_out, norm2_g, mem_norm_g, mem_w_q, mem_w_k, mem_w_v, mem_q_norm_g,
                  mem_k_norm_g, mem_w_o, norm3_g, router_w, router_b, exp_w1, exp_b1, exp_w2,
                  exp_b2):
    main = 3 * A_WIDTH + 2 * B_QK_WIDTH + 2 * B_V_WIDTH
    w_lr = jnp.pad(w_in[:, main:], ((0, 0), (0, LANES - GATE_RANK)))
    rw = jnp.pad(router_w, ((0, 0), (0, LANES - N_EXPERTS)))
    rwh = rw.astype(BF16)
    return {
        "g1": norm1_g[None],
        "w_in": jnp.concatenate([w_in[:, :main], w_lr], axis=1).astype(BF16),
        "gq": jnp.tile(a_q_norm_g, A_HEADS)[None],
        "gk": jnp.tile(a_k_norm_g, A_HEADS)[None],
        "bd64": _block_diag(A_WIDTH, A_HEAD_DIM, 1.0 / A_HEAD_DIM, BF16),
        "wa": jnp.pad(gla_w_alpha, ((0, LANES - GATE_RANK), (0, 0))).astype(BF16),
        "ba": gla_b_alpha[None],
        "bones64": _block_diag(B_QK_WIDTH, B_KEY_DIM, 1.0, BF16),
        "bd128": _block_diag(B_V_WIDTH, B_VAL_DIM, 1.0 / B_VAL_DIM, BF16),
        "gout": jnp.tile(gla_out_norm_g, B_HEADS)[None],
        "woa": w_out[:A_WIDTH].astype(BF16),
        "wob": w_out[A_WIDTH:].astype(BF16),
        "g2": norm2_g[None],
        "gmem": mem_norm_g[None],
        "wmq": mem_w_q.astype(BF16),
        "wmk": mem_w_k.astype(BF16),
        "wmv": mem_w_v.astype(BF16),
        "gmq": jnp.tile(mem_q_norm_g, MEM_HEADS)[None],
        "gmk": jnp.tile(mem_k_norm_g, MEM_HEADS)[None],
        "wmo": mem_w_o.astype(BF16),
        "g3": norm3_g[None],
        "rw": jnp.concatenate([rwh, (rw - rwh.astype(F32)).astype(BF16)], axis=1),
        "rb": jnp.pad(router_b, (0, LANES - N_EXPERTS), constant_values=-1e30)[None],
        "w1": exp_w1,
        "b1g": exp_b1[:, None, 0::2],
        "b1l": exp_b1[:, None, 1::2],
        "w2": exp_w2,
        "b2": exp_b2[:, None, :],
    }


def _layer(xp, xs, mem_prompt, cache_k, cache_v, state_gla, cache_mk, cache_mv, wp):
    bsz, seq, _ = xp.shape
    db, t_new, _ = xs.shape
    w_p = min(max(w for w, _ in DILATED_GROUPS), seq)

    xpf = xp.reshape(bsz * seq, D_MODEL)
    q, k, v, k_last, v_last, bq, bk, bv, gate, la = _in_proj(xpf, wp, BF16, seq, w_p)

    def last_rows(a):
        return a.transpose(0, 2, 1) if w_p != seq else a.reshape(bsz, w_p, A_WIDTH)
    oa = _swa_prompt(q, k, v, bsz, seq)
    s0 = jnp.zeros((bsz, B_QK_WIDTH, B_VAL_DIM), F32)
    ob, s_p = _gla(bq, bk, la, bv, gate, s0, wp, seq, GLA_CHUNK, GLA_SUB, GLA_STEP)
    mk, mv = _mem_kv(mem_prompt.reshape(bsz * N_MEM, D_MODEL), wp)
    cnt0 = jnp.zeros((1, LANES), F32)
    h2_p, xn_p, meta_p, cnt = _post(xpf, oa, ob, mk.reshape(bsz, N_MEM, MEM_WIDTH),
                                    mv.reshape(bsz, N_MEM, MEM_WIDTH), cnt0, wp,
                                    POST_TILE, 1, seq // POST_TILE)

    xsf = xs.reshape(db * t_new, D_MODEL)
    q, _, _, ks, vs, bq, bk, bv, gate, la = _in_proj(xsf, wp, F32, t_new, t_new)
    new3 = lambda a: a.reshape(db, t_new, A_WIDTH)
    oa_s, nk, nv = _swa_sample(new3(q.transpose(1, 0, 2)), new3(ks), new3(vs), cache_k, cache_v)
    ob_s, s_s = _gla(bq, bk, la, bv, gate, state_gla, wp, t_new, t_new, t_new, t_new)
    h2_s, xn_s, meta_s, cnt = _post(xsf, oa_s.reshape(db * t_new, A_WIDTH), ob_s, cache_mk,
                                    cache_mv, cnt, wp, SAMPLE_SEQS * t_new, SAMPLE_SEQS, 1)

    y_p, y_s = _moe([(h2_p, xn_p, meta_p), (h2_s, xn_s, meta_s)], wp)
    return (y_p.reshape(bsz, seq, D_MODEL), y_s.reshape(db, t_new, D_MODEL),
            last_rows(k_last), last_rows(v_last),
            s_p, mk, mv, nk, nv, s_s)


def kernel(x_prompt, x_sample, mem_prompt, cache_swa_k, cache_swa_v, state_gla, cache_mem_k, cache_mem_v, norm1_g, w_in, a_q_norm_g, a_k_norm_g, gla_w_alpha, gla_b_alpha, gla_out_norm_g, w_out, norm2_g, mem_norm_g, mem_w_q, mem_w_k, mem_w_v, mem_q_norm_g, mem_k_norm_g, mem_w_o, norm3_g, router_w, router_b, exp_w1, exp_b1, exp_w2, exp_b2):
    depth = w_in.shape[0]
    bsz = x_prompt.shape[0]
    db, w_buf = cache_swa_k.shape[1], cache_swa_k.shape[2]
    xp, xs = x_prompt, x_sample
    per_layer = []
    for l in range(depth):
        wp = _prep_weights(
            norm1_g[l], w_in[l], a_q_norm_g[l], a_k_norm_g[l], gla_w_alpha[l], gla_b_alpha[l],
            gla_out_norm_g[l], w_out[l], norm2_g[l], mem_norm_g[l], mem_w_q[l], mem_w_k[l],
            mem_w_v[l], mem_q_norm_g[l], mem_k_norm_g[l], mem_w_o[l], norm3_g[l], router_w[l],
            router_b[l], exp_w1[l], exp_b1[l], exp_w2[l], exp_b2[l])
        xp, xs, kp, vp, s_p, mk, mv, nk, nv, s_s = _layer(
            xp, xs, mem_prompt,
            cache_swa_k[l].reshape(db, w_buf, A_WIDTH).transpose(0, 2, 1),
            cache_swa_v[l].reshape(db, w_buf, A_WIDTH).transpose(0, 2, 1),
            state_gla[l].reshape(db, B_QK_WIDTH, B_VAL_DIM),
            cache_mem_k[l].reshape(db, N_MEM * MEM_HEADS, MEM_HEAD_DIM),
            cache_mem_v[l].reshape(db, N_MEM * MEM_HEADS, MEM_HEAD_DIM),
            wp)
        w_p = kp.shape[1]
        per_layer.append((
            kp.reshape(bsz, w_p, A_HEADS, A_HEAD_DIM), vp.reshape(bsz, w_p, A_HEADS, A_HEAD_DIM),
            s_p.reshape(bsz, B_HEADS, B_KEY_DIM, B_VAL_DIM),
            mk.reshape(bsz, N_MEM, MEM_HEADS, MEM_HEAD_DIM), mv.reshape(bsz, N_MEM, MEM_HEADS, MEM_HEAD_DIM),
            nk.transpose(0, 2, 1).reshape(db, w_buf, A_HEADS, A_HEAD_DIM),
            nv.transpose(0, 2, 1).reshape(db, w_buf, A_HEADS, A_HEAD_DIM),
            s_s.reshape(db, B_HEADS, B_KEY_DIM, B_VAL_DIM)))
    stacked = [jnp.stack(t) for t in zip(*per_layer)]
    return (xp, xs, *stacked)
```

```python
import functools

import jax
import jax.numpy as jnp
import numpy as np
from jax import lax
from jax.experimental import pallas as pl
from jax.experimental.pallas import tpu as pltpu

F32 = jnp.float32
BF16 = jnp.bfloat16
I32 = jnp.int32

EPS = 1e-6
D_MODEL = 1024
A_HEADS, A_HEAD_DIM, A_WIDTH = 8, 64, 512
A_SCALE = A_HEAD_DIM ** -0.5
LOG2E = 1.4426950408889634
DILATED_GROUPS = ((128, 1), (512, 4), (2048, 16))
SPAN = 128
B_HEADS, B_KEY_DIM, B_VAL_DIM = 4, 64, 128
B_QK_WIDTH, B_V_WIDTH = 256, 512
GATE_RANK = 16
GATE_TEMP = 16.0
N_MEM, MEM_HEADS, MEM_HEAD_DIM, MEM_WIDTH = 256, 4, 128, 512
N_EXPERTS, TOP_K, D_FF = 32, 4, 1024
SWIGLU_ALPHA, SWIGLU_LIMIT = 1.702, 7.0

LANES = 128
IN_COLS = tuple(int(c) for c in np.cumsum(
    [0, A_WIDTH, A_WIDTH, A_WIDTH, B_QK_WIDTH, B_QK_WIDTH, B_V_WIDTH, B_V_WIDTH, LANES]))
VMEM_LIMIT = 56 * 1024 * 1024

PROJ_TILE = 512
SWA_BLOCK = 2048
SWA_UNROLL = 8
GLA_CHUNK, GLA_SUB, GLA_STEP = 64, 8, 1024
POST_TILE = 1024
SAMPLE_SEQS = 8
EXPERT_TILE = 512
DISPATCH_BLOCK = 512
GROUP_ROWS = 8
LOCAL_ROWS = DISPATCH_BLOCK * TOP_K + N_EXPERTS * GROUP_ROWS


def _cparams(sem, vmem=VMEM_LIMIT, **kw):
    return pltpu.CompilerParams(dimension_semantics=sem, vmem_limit_bytes=vmem, **kw)


def _full(shape):
    n = len(shape)
    return pl.BlockSpec(shape, lambda *_: (0,) * n)


def _rms(x, g):
    ms = jnp.mean(x * x, axis=-1, keepdims=True)
    return x * lax.rsqrt(ms + EPS) * g


def _dot(a, b):
    return jnp.dot(a, b, preferred_element_type=F32)


def _dot_nt(a, b):
    return lax.dot_general(a, b, (((1,), (1,)), ((), ())), preferred_element_type=F32)


def _block_diag(n, blk, val, dtype):
    i = np.arange(n)
    return jnp.asarray(np.where((i[:, None] // blk) == (i[None, :] // blk), val, 0.0), dtype)


def _in_proj_body(keep_transposed, x_ref, g1_ref, w_ref, gq_ref, gk_ref, bd_ref, wa_ref, ba_ref,
                  q_ref, k_ref, v_ref, kc_ref, vc_ref, bq_ref, bk_ref, bv_ref, gate_ref, la_ref):
    xn = _rms(x_ref[...], g1_ref[...]).astype(BF16)
    bd = bd_ref[...]

    def proj(lo, hi):
        return _dot(xn, w_ref[:, lo:hi])

    def headnorm(z, g):
        ms = _dot((z * z).astype(BF16), bd)
        return z * lax.rsqrt(ms + EPS) * g

    def put_groups(ref, z):
        for p in range(A_WIDTH // LANES):
            ref[p] = z[:, p * LANES:(p + 1) * LANES]

    c = IN_COLS
    put_groups(q_ref, headnorm(proj(c[0], c[1]), gq_ref[...]) * (A_SCALE * LOG2E))
    k = headnorm(proj(c[1], c[2]), gk_ref[...])
    v = proj(c[2], c[3])
    put_groups(k_ref, k)
    put_groups(v_ref, v)
    kc_ref[...] = jnp.transpose(k) if keep_transposed else k
    vc_ref[...] = jnp.transpose(v) if keep_transposed else v
    bq_ref[...] = proj(c[3], c[4]) * (B_KEY_DIM ** -0.5)
    bk_ref[...] = proj(c[4], c[5])
    bv_ref[...] = proj(c[5], c[6]).astype(bv_ref.dtype)
    br = proj(c[6], c[7])
    gate_ref[...] = (br * jax.nn.sigmoid(br)).astype(gate_ref.dtype)
    lr = proj(c[7], c[8]).astype(BF16)
    pre = _dot(lr, wa_ref[...]) + ba_ref[...]
    log_sig = jnp.minimum(pre, 0.0) - jnp.log1p(jnp.exp(-jnp.abs(pre)))
    la_ref[...] = log_sig * (1.0 / GATE_TEMP)


def _in_proj(x, wp, wide_dtype, seq, keep):
    n = x.shape[0]
    tm = min(PROJ_TILE, n)
    row = lambda w: pl.BlockSpec((tm, w), lambda i: (i, 0))
    ngrp = A_WIDTH // LANES
    grp = pl.BlockSpec((ngrp, tm, LANES), lambda i: (0, i, 0))
    if keep == seq:
        kept = row(A_WIDTH)
        kept_shape = (n, A_WIDTH)
    else:
        tps, kt = seq // tm, keep // tm
        assert tps * tm == seq and kt * tm == keep
        kept = pl.BlockSpec((None, A_WIDTH, tm),
                            lambda i: (i // tps, 0, jnp.maximum(i % tps - (tps - kt), 0)))
        kept_shape = (n // seq, A_WIDTH, keep)
    outs = [(B_QK_WIDTH, F32), (B_QK_WIDTH, F32), (B_V_WIDTH, wide_dtype), (B_V_WIDTH, wide_dtype),
            (B_QK_WIDTH, F32)]
    return pl.pallas_call(
        functools.partial(_in_proj_body, keep != seq),
        grid=(n // tm,),
        in_specs=[row(D_MODEL), _full((1, D_MODEL)), _full((D_MODEL, IN_COLS[-1])),
                  _full((1, A_WIDTH)), _full((1, A_WIDTH)), _full((A_WIDTH, A_WIDTH)),
                  _full((LANES, B_QK_WIDTH)), _full((1, B_QK_WIDTH))],
        out_specs=[grp] * 3 + [kept] * 2 + [row(w) for w, _ in outs],
        out_shape=[jax.ShapeDtypeStruct((ngrp, n, LANES), F32)] * 3
        + [jax.ShapeDtypeStruct(kept_shape, F32)] * 2
        + [jax.ShapeDtypeStruct((n, w), dt) for w, dt in outs],
        compiler_params=_cparams(("arbitrary",)),
        name="in_proj",
    )(x, wp["g1"], wp["w_in"], wp["gq"], wp["gk"], wp["bd64"], wp["wa"], wp["ba"])


def _unroll_for(trips):
    return max(u for u in range(1, SWA_UNROLL + 1) if trips % u == 0)


def _ds(start, size, stride):
    return pl.ds(start, size) if stride == 1 else pl.ds(start, size, stride=stride)


def _swa_prompt_body(q_ref, kp_ref, kc_ref, vp_ref, vc_ref, o_ref, m_s, l_s, acc_s, k_keep, v_keep):
    i = pl.program_id(2)
    qb = SWA_BLOCK

    @pl.when(i == 0)
    def _():
        k_keep[...] = jnp.zeros_like(k_keep)
        v_keep[...] = jnp.zeros_like(v_keep)

    lane = lax.broadcasted_iota(I32, (SPAN, LANES), 1)
    lo_mask = lane < A_HEAD_DIM
    jq = lax.broadcasted_iota(I32, (SPAN, 2 * SPAN), 0)
    jk = lax.broadcasted_iota(I32, (SPAN, 2 * SPAN), 1)
    dist = jq + SPAN - jk
    band = (dist >= 0) & (dist <= SPAN)
    cur_half = jk >= SPAN

    def attend(qp, kp, vp, valid, rows, first_group):
        vp1 = jnp.concatenate([vp, jnp.ones_like(vp)], axis=1)
        res = []
        for hh in range(2):
            msk = lo_mask if hh == 0 else jnp.logical_not(lo_mask)
            qm = jnp.where(msk, qp, 0.0).astype(BF16)
            s = jnp.where(valid, _dot_nt(qm, kp), -jnp.inf)
            m = jnp.max(s, axis=1, keepdims=True)
            res.append((m, _dot(jnp.exp2(s - m).astype(BF16), vp1)))
        m_new = jnp.where(lo_mask, res[0][0], res[1][0])
        l_new = jnp.where(lo_mask, res[0][1][:, LANES:], res[1][1][:, LANES:])
        o_new = jnp.where(lo_mask, res[0][1][:, :LANES], res[1][1][:, :LANES])
        if first_group:
            m_s[rows, :] = m_new
            l_s[rows, :] = l_new
            acc_s[rows, :] = o_new
        else:
            m_old = m_s[rows, :]
            m = jnp.maximum(m_old, m_new)
            a_old = jnp.exp2(m_old - m)
            a_new = jnp.exp2(m_new - m)
            m_s[rows, :] = m
            l_s[rows, :] = l_s[rows, :] * a_old + l_new * a_new
            acc_s[rows, :] = acc_s[rows, :] * a_old + o_new * a_new

    for gi, (_, dil) in enumerate(reversed(DILATED_GROUPS)):
        unit = dil * SPAN
        nblk = qb // unit
        first = gi == 0

        def head_block(r, carry, dil=dil, unit=unit, first=first, nblk=nblk):
            rows = _ds(r, SPAN, dil)
            valid = band & (cur_half | (i > 0))
            if nblk == 1:
                keep = pl.ds(pl.multiple_of(r * SPAN, SPAN), SPAN)
                k_cur = kc_ref[rows, :].astype(BF16)
                v_cur = vc_ref[rows, :].astype(BF16)
                ks = jnp.concatenate([k_keep[keep, :], k_cur], axis=0)
                vs = jnp.concatenate([v_keep[keep, :], v_cur], axis=0)
                k_keep[keep, :] = k_cur
                v_keep[keep, :] = v_cur
            else:
                prev = _ds(qb - unit + r, SPAN, dil)
                ks = jnp.concatenate([kp_ref[prev, :], kc_ref[rows, :]], axis=0).astype(BF16)
                vs = jnp.concatenate([vp_ref[prev, :], vc_ref[rows, :]], axis=0).astype(BF16)
            attend(q_ref[rows, :], ks, vs, valid, rows, first)
            return carry

        lax.fori_loop(0, dil, head_block, 0, unroll=_unroll_for(dil))

        if nblk > 1:
            def tail_block(idx, carry, dil=dil, unit=unit, first=first):
                n = idx // dil + 1
                r = idx % dil
                start = unit * n + r
                rows = _ds(start, SPAN, dil)
                keys = _ds(start - unit, 2 * SPAN, dil)
                attend(q_ref[rows, :], kc_ref[keys, :].astype(BF16), vc_ref[keys, :].astype(BF16),
                       band, rows, first)
                return carry

            lax.fori_loop(0, (nblk - 1) * dil, tail_block, 0, unroll=_unroll_for((nblk - 1) * dil))

    o_ref[...] = (acc_s[...] / l_s[...]).astype(o_ref.dtype)


def _swa_prompt(q, k, v, bsz, seq):
    qb = SWA_BLOCK
    nb = seq // qb
    cur = pl.BlockSpec((None, qb, LANES), lambda p, b, i: (p, b * nb + i, 0))
    prev = pl.BlockSpec((None, qb, LANES), lambda p, b, i: (p, b * nb + jnp.maximum(i - 1, 0), 0))
    return pl.pallas_call(
        _swa_prompt_body,
        grid=(A_WIDTH // LANES, bsz, nb),
        in_specs=[cur, prev, cur, prev, cur],
        out_specs=pl.BlockSpec((qb, LANES), lambda p, b, i: (b * nb + i, p)),
        out_shape=jax.ShapeDtypeStruct((bsz * seq, A_WIDTH), BF16),
        scratch_shapes=[pltpu.VMEM((qb, LANES), F32)] * 3 + [pltpu.VMEM((qb, LANES), BF16)] * 2,
        compiler_params=_cparams(("parallel", "parallel", "arbitrary")),
        name="swa_prompt",
    )(q, k, k, v, v)


def _swa_sample_body(q_ref, kn_ref, vn_ref, ck_ref, cv_ref, c1_ref, c2_ref,
                     o_ref, nk_ref, nv_ref):
    t_new = q_ref.shape[1]
    w_buf = ck_ref.shape[2]
    ck, cv = ck_ref[0], cv_ref[0]
    kn, vn = kn_ref[0], vn_ref[0]
    tail_lane = lax.broadcasted_iota(I32, (A_WIDTH, LANES), 1) >= LANES - t_new

    def shift_in(old, new, out_ref):
        moved = pltpu.roll(old, w_buf - t_new, 1)
        new_t = jnp.transpose(jnp.concatenate([jnp.zeros((LANES - t_new, A_WIDTH), F32), new], axis=0))
        out_ref[0, :, :w_buf - LANES] = moved[:, :w_buf - LANES]
        out_ref[0, :, w_buf - LANES:] = jnp.where(tail_lane, new_t, moved[:, w_buf - LANES:])

    shift_in(ck, kn, nk_ref)
    shift_in(cv, vn, nv_ref)

    q = q_ref[0]
    lane = lax.broadcasted_iota(I32, (t_new, LANES), 1)
    lo_mask = lane < A_HEAD_DIM
    c1, c2 = c1_ref[...], c2_ref[...]
    outs = []
    for p in range(A_WIDTH // LANES):
        sl = slice(p * LANES, (p + 1) * LANES)
        qp = q[:, sl]
        qblk = jnp.concatenate([jnp.where(lo_mask, qp, 0.0), jnp.where(lo_mask, 0.0, qp)],
                               axis=0).astype(BF16)
        s1 = jnp.where(c1 > 0, _dot(qblk, ck[sl, :].astype(BF16)), -jnp.inf)
        s2 = jnp.where(c2 > 0, _dot_nt(qblk, kn[:, sl].astype(BF16)), -jnp.inf)
        m = jnp.maximum(jnp.max(s1, axis=1, keepdims=True), jnp.max(s2, axis=1, keepdims=True))
        p1 = c1 * jnp.exp2(s1 - m)
        p2 = c2 * jnp.exp2(s2 - m)
        l = jnp.sum(p1, axis=1, keepdims=True) + jnp.sum(p2, axis=1, keepdims=True)
        o = (_dot_nt(p1.astype(BF16), cv[sl, :].astype(BF16))
             + _dot(p2.astype(BF16), vn[:, sl].astype(BF16))) / l
        outs.append(jnp.where(lo_mask, o[:t_new], o[t_new:]))
    o_ref[0] = jnp.concatenate(outs, axis=1).astype(o_ref.dtype)


def _sample_multiplicity(t_new, w_buf):
    t = np.arange(t_new)[:, None]
    e = np.arange(w_buf + t_new)[None, :]
    d = w_buf + t - e
    c = np.zeros(d.shape, np.float32)
    for window, dil in DILATED_GROUPS:
        c += ((d >= 0) & (d % dil == 0) & (d <= window)).astype(np.float32)
    c = np.concatenate([c, c], axis=0)
    return jnp.asarray(c[:, :w_buf]), jnp.asarray(c[:, w_buf:])


def _swa_sample(q, kn, vn, cache_k, cache_v):
    db, t_new, w = q.shape
    w_buf = cache_k.shape[2]
    assert w_buf >= max(win for win, _ in DILATED_GROUPS) and t_new % 8 == 0 and t_new <= LANES
    c1, c2 = _sample_multiplicity(t_new, w_buf)
    new = pl.BlockSpec((1, t_new, w), lambda b: (b, 0, 0))
    cache = pl.BlockSpec((1, w, w_buf), lambda b: (b, 0, 0))
    return pl.pallas_call(
        _swa_sample_body,
        grid=(db,),
        in_specs=[new, new, new, cache, cache, _full(c1.shape), _full(c2.shape)],
        out_specs=[new, cache, cache],
        out_shape=[jax.ShapeDtypeStruct((db, t_new, w), BF16),
                   jax.ShapeDtypeStruct(cache_k.shape, cache_k.dtype),
                   jax.ShapeDtypeStruct(cache_v.shape, cache_v.dtype)],
        compiler_params=_cparams(("parallel",)),
        name="swa_sample",
    )(q, kn, vn, cache_k, cache_v, c1, c2)


def _gla_body(chunk, sub, nch, q_ref, k_ref, g_ref, v_ref, gate_ref, s0_ref, tril_ref, dmask_ref,
              bones_ref, sbm_ref, bd_ref, gout_ref, o_ref, sfin_ref, sbd):
    j = pl.program_id(1)
    sbm = sbm_ref[...]
    nsub = chunk // sub
    pad = B_KEY_DIM - chunk

    @pl.when(j == 0)
    def _():
        s0 = s0_ref[0]
        sbd[...] = jnp.concatenate([s0] * B_HEADS, axis=1) * sbm

    row = lax.broadcasted_iota(I32, (chunk, 1), 0)
    sub_id = row // sub
    lane_w = lax.broadcasted_iota(I32, (chunk, LANES * max(nsub - 1, 1)), 1)
    lo_w = (lane_w % LANES) < B_KEY_DIM

    def one_chunk(c, carry):
        off = pl.multiple_of(c * chunk, chunk)
        rows = pl.ds(off, chunk)
        q, k, g = q_ref[rows, :], k_ref[rows, :], g_ref[rows, :]
        v = v_ref[rows, :].astype(F32)
        g1 = g.astype(BF16)
        r1 = g - g1.astype(F32)
        g2 = r1.astype(BF16)
        g3 = (r1 - g2.astype(F32)).astype(BF16)
        tril = tril_ref[...]
        b = _dot(tril, g1) + _dot(tril, g2) + _dot(tril, g3)
        b_last = b[chunk - 1:chunk, :]
        state = sbd[...]

        o = _dot((q * jnp.exp(b)).astype(BF16), state.astype(BF16))

        bones = bones_ref[...]
        att = _dot((q * k).astype(BF16), bones) * dmask_ref[0]
        gate = jnp.exp(g)
        decay = gate
        for d in range(1, sub):
            if d > 1:
                decay = decay * pltpu.roll(gate, d - 1, 0)
            w = q * pltpu.roll(k, d, 0) * decay
            att = att + _dot(w.astype(BF16), bones) * dmask_ref[d]

        if nsub > 1:
            qx, kx = [], []
            for i in range(1, nsub):
                r_i = b[sub * i - 1:sub * i, :]
                qx.append(jnp.where(sub_id == i, q * jnp.exp(jnp.minimum(b - r_i, 0.0)), 0.0))
                kx.append(jnp.where(sub_id < i, k * jnp.exp(jnp.minimum(r_i - b, 0.0)), 0.0))
            parts = []
            for p in range(B_QK_WIDTH // LANES):
                sl = slice(p * LANES, (p + 1) * LANES)
                qp = jnp.concatenate([x[:, sl] for x in qx], axis=1)
                kp = jnp.concatenate([x[:, sl] for x in kx], axis=1).astype(BF16)
                zero = jnp.zeros_like(kp)
                lhs = jnp.concatenate([jnp.where(lo_w, qp, 0.0), jnp.where(lo_w, 0.0, qp)],
                                      axis=1).astype(BF16)
                rhs = jnp.concatenate([jnp.concatenate([kp, zero], axis=1),
                                       jnp.concatenate([zero, kp], axis=1)], axis=0)
                parts.append(_dot_nt(lhs, rhs))
            att = att + jnp.concatenate(parts, axis=1)

        if pad:
            vrow = jnp.concatenate([v, jnp.zeros((pad, B_V_WIDTH), F32)], axis=0)
        else:
            vrow = v
        vbd = (jnp.concatenate([vrow] * B_HEADS, axis=0) * sbm).astype(BF16)
        o = o + _dot(att.astype(BF16), vbd)

        ke = (k * jnp.exp(b_last - b)).astype(BF16)
        upd = lax.dot_general(ke, v.astype(BF16), (((0,), (0,)), ((), ())),
                              preferred_element_type=F32)
        dec = jnp.transpose(jnp.broadcast_to(jnp.exp(b_last), (8, B_QK_WIDTH)))[:, 0:1]
        sbd[...] = (state * dec + upd) * sbm

        ms = _dot((o * o).astype(BF16), bd_ref[...])
        on = o * lax.rsqrt(ms + EPS) * gout_ref[...] * gate_ref[rows, :].astype(F32)
        o_ref[rows, :] = on.astype(o_ref.dtype)
        return carry

    lax.fori_loop(0, nch, one_chunk, 0, unroll=2 if nch % 2 == 0 else 1)

    @pl.when(j == pl.num_programs(1) - 1)
    def _():
        s = sbd[...]
        sfin_ref[0] = jnp.concatenate(
            [s[h * B_KEY_DIM:(h + 1) * B_KEY_DIM, h * B_VAL_DIM:(h + 1) * B_VAL_DIM]
             for h in range(B_HEADS)], axis=0)


def _gla_consts(chunk, sub):
    t = np.arange(chunk)
    tril = (t[:, None] >= t[None, :]).astype(np.float32)
    lane = np.arange(B_QK_WIDTH)
    dmask = np.zeros((sub, chunk, B_QK_WIDTH), np.float32)
    for d in range(sub):
        ok = (t % sub) >= d
        dmask[d] = ((lane[None, :] % B_KEY_DIM) == (t[:, None] - d)) & ok[:, None]
    r = np.arange(B_QK_WIDTH)[:, None] // B_KEY_DIM
    c = np.arange(B_V_WIDTH)[None, :] // B_VAL_DIM
    sbm = (r == c).astype(np.float32)
    return jnp.asarray(tril, BF16), jnp.asarray(dmask), jnp.asarray(sbm)


def _gla(q, k, g, v, gate, s0, wp, length, chunk, sub, step):
    n = q.shape[0]
    bsz = n // length
    assert chunk == sub or chunk == B_KEY_DIM
    tril, dmask, sbm = _gla_consts(chunk, sub)
    nstep = length // step
    row = lambda w: pl.BlockSpec((step, w), lambda b, j: (b * nstep + j, 0))
    st = pl.BlockSpec((1, B_QK_WIDTH, B_VAL_DIM), lambda b, j: (b, 0, 0))
    return pl.pallas_call(
        functools.partial(_gla_body, chunk, sub, step // chunk),
        grid=(bsz, nstep),
        in_specs=[row(256), row(256), row(256), row(512), row(512), st, _full(tril.shape),
                  _full(dmask.shape), _full((256, 256)), _full(sbm.shape), _full((512, 512)),
                  _full((1, 512))],
        out_specs=[row(512), st],
        out_shape=[jax.ShapeDtypeStruct((n, B_V_WIDTH), BF16),
                   jax.ShapeDtypeStruct(s0.shape, F32)],
        scratch_shapes=[pltpu.VMEM((B_QK_WIDTH, B_V_WIDTH), F32)],
        compiler_params=_cparams(("parallel", "arbitrary")),
        name="gla",
    )(q, k, g, v, gate, s0, tril, dmask, wp["bones64"], sbm, wp["bd128"], wp["gout"])


def _head_rms(z, g, scale=1.0):
    parts = []
    for h in range(MEM_HEADS):
        zh = z[:, h * LANES:(h + 1) * LANES]
        parts.append(zh * lax.rsqrt(jnp.mean(zh * zh, axis=-1, keepdims=True) + EPS))
    return jnp.concatenate(parts, axis=1) * (g * scale)


def _mem_kv_body(m_ref, gn_ref, wk_ref, wv_ref, gk_ref, mk_ref, mv_ref):
    mn = _rms(m_ref[...], gn_ref[...]).astype(BF16)
    mk_ref[...] = _head_rms(_dot(mn, wk_ref[...]), gk_ref[...])
    mv_ref[...] = _dot(mn, wv_ref[...])


def _mem_kv(mem, wp):
    n = mem.shape[0]
    tm = 256
    row = lambda w: pl.BlockSpec((tm, w), lambda i: (i, 0))
    return pl.pallas_call(
        _mem_kv_body,
        grid=(n // tm,),
        in_specs=[row(D_MODEL), _full((1, D_MODEL)), _full((D_MODEL, 512)), _full((D_MODEL, 512)),
                  _full((1, 512))],
        out_specs=[row(512), row(512)],
        out_shape=[jax.ShapeDtypeStruct((n, 512), F32)] * 2,
        compiler_params=_cparams(("parallel",)),
        name="mem_kv",
    )(mem, wp["gmem"], wp["wmk"], wp["wmv"], wp["gmk"])


def _post_body(nseq, x_ref, oa_ref, ob_ref, woa_ref, wob_ref, g2_ref, wq_ref, gmq_ref, mk_ref,
               mv_ref, wo_ref, g3_ref, rw_ref, rb_ref, cnt0_ref, tri_ref,
               h2_ref, xn_ref, meta_ref, cnt_ref, part_ref, carry):
    tm = x_ref.shape[0]

    @pl.when(pl.program_id(0) == 0)
    def _():
        carry[...] = cnt0_ref[...]

    nk = nseq * N_MEM

    def mem_head(ref, hd):
        if ref.shape[-1] == MEM_HEAD_DIM:
            rows = ref[:, pl.ds(hd, N_MEM, stride=MEM_HEADS), :]
        else:
            rows = ref[:, :, hd * LANES:(hd + 1) * LANES]
        return rows.reshape(nk, LANES).astype(BF16)

    if nseq > 1:
        rt = lax.broadcasted_iota(I32, (tm, nk), 0) // (tm // nseq)
        ct = lax.broadcasted_iota(I32, (tm, nk), 1) // N_MEM
        same = rt == ct

    def router_logits():
        h = x_ref[...] + _dot(oa_ref[...], woa_ref[...]) + _dot(ob_ref[...], wob_ref[...])
        hn = _rms(h, g2_ref[...]).astype(BF16)
        qm = _head_rms(_dot(hn, wq_ref[...]), gmq_ref[...], MEM_HEAD_DIM ** -0.5).astype(BF16)
        outs = []
        for hd in range(MEM_HEADS):
            sl = slice(hd * LANES, (hd + 1) * LANES)
            s = _dot_nt(qm[:, sl], mem_head(mk_ref, hd))
            if nseq > 1:
                s = jnp.where(same, s, -jnp.inf)
            m = jnp.max(s, axis=1, keepdims=True)
            pr = jnp.exp(s - m)
            l = jnp.sum(pr, axis=1, keepdims=True)
            outs.append(_dot(pr.astype(BF16), mem_head(mv_ref, hd)) / l)
        h2 = h + _dot(jnp.concatenate(outs, axis=1).astype(BF16), wo_ref[...])
        h2_ref[...] = h2
        xn = _rms(h2, g3_ref[...])
        xn_ref[...] = xn
        x1 = xn.astype(BF16)
        x2 = (xn - x1.astype(F32)).astype(BF16)
        prod = _dot(jnp.concatenate([x1, x2], axis=0), rw_ref[...])
        return prod[:tm, :LANES] + prod[:tm, LANES:] + prod[tm:, :LANES] + prod[tm:, LANES:]

    logits = router_logits() + rb_ref[...]
    lane = lax.broadcasted_iota(I32, (tm, LANES), 1)
    vals, idxs, hots = [], [], []
    work = logits
    for _ in range(TOP_K):
        m = jnp.max(work, axis=1, keepdims=True)
        idx = jnp.min(jnp.where(work == m, lane, LANES), axis=1, keepdims=True)
        hot = lane == idx
        vals.append(m)
        idxs.append(idx)
        hots.append(hot)
        work = jnp.where(hot, -jnp.inf, work)
    exps = [jnp.exp(v - vals[0]) for v in vals]
    den = exps[0] + exps[1] + exps[2] + exps[3]

    sel = (hots[0] | hots[1] | hots[2] | hots[3]).astype(F32)
    before = _dot(tri_ref[...], sel.astype(BF16)) + carry[...]
    carry[...] = carry[...] + jnp.sum(sel, axis=0, keepdims=True)
    cnt_ref[...] = carry[...]
    parts = part_ref.shape[1]
    for j in range(parts):
        part_ref[0, j:j + 1, :] = jnp.sum(sel[j * (tm // parts):(j + 1) * (tm // parts)], axis=0,
                                          keepdims=True)

    meta = jnp.zeros((tm, LANES), F32)
    for kk in range(TOP_K):
        rank = jnp.sum(jnp.where(hots[kk], before, 0.0), axis=1, keepdims=True)
        meta = jnp.where(lane == kk, idxs[kk].astype(F32), meta)
        meta = jnp.where(lane == TOP_K + kk, rank, meta)
        meta = jnp.where(lane == 2 * TOP_K + kk, exps[kk] / den, meta)
    meta_ref[...] = meta


def _post(x, oa, ob, mk, mv, cnt0, wp, tm, nseq, tiles_per_mem):
    n = x.shape[0]
    row = lambda w: pl.BlockSpec((tm, w), lambda i: (i, 0))
    mem = pl.BlockSpec((nseq,) + mk.shape[1:], lambda i: (i // tiles_per_mem, 0, 0))
    tri = jnp.asarray(np.tril(np.ones((tm, tm), np.float32), -1), BF16)
    parts = max(tm // DISPATCH_BLOCK, 1)
    return pl.pallas_call(
        functools.partial(_post_body, nseq),
        grid=(n // tm,),
        in_specs=[row(D_MODEL), row(512), row(512), _full((512, D_MODEL)), _full((512, D_MODEL)),
                  _full((1, D_MODEL)), _full((D_MODEL, 512)), _full((1, 512)), mem, mem,
                  _full((512, D_MODEL)), _full((1, D_MODEL)), _full((D_MODEL, 2 * LANES)),
                  _full((1, LANES)), _full((1, LANES)), _full((tm, tm))],
        out_specs=[row(D_MODEL), row(D_MODEL), row(LANES), _full((1, LANES)),
                   pl.BlockSpec((1, parts, LANES), lambda i: (i, 0, 0))],
        out_shape=[jax.ShapeDtypeStruct((n, D_MODEL), F32), jax.ShapeDtypeStruct((n, D_MODEL), F32),
                   jax.ShapeDtypeStruct((n, LANES), F32), jax.ShapeDtypeStruct((1, LANES), F32),
                   jax.ShapeDtypeStruct((n // tm, parts, LANES), F32)],
        scratch_shapes=[pltpu.VMEM((1, LANES), F32)],
        compiler_params=_cparams(("arbitrary",)),
        name="post",
    )(x, oa, ob, wp["woa"], wp["wob"], wp["g2"], wp["wmq"], wp["gmq"], mk, mv, wp["wmo"],
      wp["g3"], wp["rw"], wp["rb"], cnt0, tri)


def _start_rows(src, src_row, dst, dst_row, n_rows, sem):
    s = pl.multiple_of(jnp.asarray(src_row, I32), GROUP_ROWS)
    d = pl.multiple_of(jnp.asarray(dst_row, I32), GROUP_ROWS)
    n_rows = pl.multiple_of(jnp.asarray(n_rows, I32), GROUP_ROWS)

    @pl.when(n_rows > 0)
    def _():
        pltpu.make_async_copy(src.at[pl.ds(s, n_rows)], dst.at[pl.ds(d, n_rows)], sem).start()

    return n_rows


def _wait_rows(like_src, like_dst, n_rows, sem):
    n_rows = pl.multiple_of(n_rows, GROUP_ROWS)

    @pl.when(n_rows > 0)
    def _():
        pltpu.make_async_copy(like_src.at[pl.ds(0, n_rows)], like_dst.at[pl.ds(0, n_rows)], sem).wait()


def _pad_rows_body(pstart_ref, prows_ref, nt_ref, xs_ref, zeros, sem):
    zeros[...] = jnp.zeros_like(zeros)
    n_tiles_max = xs_ref.shape[0] // EXPERT_TILE

    def expert_tail(e, total):
        return total + _start_rows(zeros, 0, xs_ref, pstart_ref[e], prows_ref[e], sem)

    def unused_tile(t, total):
        return total + _start_rows(zeros, 0, xs_ref, t * EXPERT_TILE, EXPERT_TILE, sem)

    total = lax.fori_loop(0, N_EXPERTS, expert_tail, jnp.int32(0))
    total = lax.fori_loop(nt_ref[0], n_tiles_max, unused_tile, total)
    _wait_rows(xs_ref, xs_ref, total, sem)


def _pad_rows(pad_start, pad_units, n_tiles, rows):
    return pl.pallas_call(
        _pad_rows_body,
        grid_spec=pltpu.PrefetchScalarGridSpec(
            num_scalar_prefetch=3, grid=(1,),
            in_specs=[],
            out_specs=pl.BlockSpec(memory_space=pltpu.HBM),
            scratch_shapes=[pltpu.VMEM((EXPERT_TILE, D_MODEL), F32), pltpu.SemaphoreType.DMA]),
        out_shape=jax.ShapeDtypeStruct((rows, D_MODEL), F32),
        compiler_params=_cparams(("arbitrary",)),
        name="moe_pad_rows",
    )(pad_start, pad_units, n_tiles)


def _dispatch_body(blk0, loff_ref, gstart_ref, nrows_ref, ldest_ref, x_ref, xs_in_ref, xs_ref,
                   xloc, sems):
    del xs_in_ref
    i = pl.program_id(0)
    b = i + blk0
    slot = i % 2
    tb = x_ref.shape[0]
    rows = lax.broadcasted_iota(I32, (LOCAL_ROWS, tb), 0)
    ld = ldest_ref[...]
    hot = rows == ld[0:1, :]
    for kk in range(1, TOP_K):
        hot = hot | (rows == ld[kk:kk + 1, :])
    xloc[slot] = _dot(jnp.where(hot, 1.0, 0.0).astype(BF16), x_ref[...].astype(BF16))

    def block_rows(bb):
        j = bb * N_EXPERTS + N_EXPERTS - 1
        return loff_ref[j] + nrows_ref[j]

    @pl.when(i > 0)
    def _():
        _wait_rows(xloc.at[1 - slot], xs_ref, block_rows(b - 1), sems.at[1 - slot])

    def group(e, carry):
        j = b * N_EXPERTS + e
        _start_rows(xloc.at[slot], loff_ref[j], xs_ref, gstart_ref[j], nrows_ref[j], sems.at[slot])
        return carry

    lax.fori_loop(0, N_EXPERTS, group, 0)

    @pl.when(i == pl.num_programs(0) - 1)
    def _():
        _wait_rows(xloc.at[slot], xs_ref, block_rows(b), sems.at[slot])


def _dispatch(tables, blk0, ldest_t, xn, xs):
    n = xn.shape[0]
    tb = DISPATCH_BLOCK
    idx = lambda i, *_: (i, 0)
    return pl.pallas_call(
        functools.partial(_dispatch_body, blk0),
        grid_spec=pltpu.PrefetchScalarGridSpec(
            num_scalar_prefetch=3, grid=(n // tb,),
            in_specs=[pl.BlockSpec((TOP_K, tb), lambda i, *_: (0, i)),
                      pl.BlockSpec((tb, D_MODEL), idx),
                      pl.BlockSpec(memory_space=pltpu.HBM)],
            out_specs=pl.BlockSpec(memory_space=pltpu.HBM),
            scratch_shapes=[pltpu.VMEM((2, LOCAL_ROWS, D_MODEL), F32), pltpu.SemaphoreType.DMA((2,))]),
        out_shape=jax.ShapeDtypeStruct(xs.shape, xs.dtype),
        input_output_aliases={5: 0},
        compiler_params=_cparams(("arbitrary",)),
        name="moe_dispatch",
    )(*tables, ldest_t, xn, xs)


def _expert_body(te_ref, first_ref, nt_ref, x_ref, w1_ref, sel_ref, b1g_ref, b1l_ref, w2_ref, b2_ref,
                 y_ref, w1g, w1l, w2):
    i = pl.program_id(0)
    live = i < nt_ref[0]

    @pl.when(live & (first_ref[i] == 1))
    def _():
        sel = sel_ref[...]
        for j in range(D_FF // LANES):
            z = _dot(w1_ref[0, :, 2 * LANES * j:2 * LANES * (j + 1)].astype(BF16), sel)
            w1g[:, LANES * j:LANES * (j + 1)] = z[:, :LANES].astype(BF16)
            w1l[:, LANES * j:LANES * (j + 1)] = z[:, LANES:].astype(BF16)
        w2[...] = w2_ref[0].astype(BF16)

    @pl.when(live)
    def _():
        x = x_ref[...].astype(BF16)
        glu = jnp.minimum(_dot(x, w1g[...]) + b1g_ref[0], SWIGLU_LIMIT)
        lin = jnp.clip(_dot(x, w1l[...]) + b1l_ref[0], -SWIGLU_LIMIT, SWIGLU_LIMIT)
        act = glu * jax.nn.sigmoid(SWIGLU_ALPHA * glu) * (lin + 1.0)
        y_ref[...] = _dot(act.astype(BF16), w2[...]) + b2_ref[0]

    @pl.when(jnp.logical_not(live))
    def _():
        y_ref[...] = jnp.zeros_like(y_ref)


def _experts(tile_expert, n_tiles, xs, wp):
    rows = xs.shape[0]
    first = jnp.concatenate([jnp.ones((1,), I32),
                             (tile_expert[1:] != tile_expert[:-1]).astype(I32)])
    c = np.arange(2 * LANES)
    sel = np.zeros((2 * LANES, 2 * LANES), np.float32)
    sel[c, (c % 2) * LANES + c // 2] = 1.0
    tile = lambda i, te, fi, nt: (jnp.minimum(i, nt[0] - 1), 0)
    out_tile = lambda i, te, fi, nt: (i, 0)
    wsel = lambda i, te, fi, nt: (te[jnp.minimum(i, nt[0] - 1)], 0, 0)
    wspec = lambda r, c: pl.BlockSpec((1, r, c), wsel)
    return pl.pallas_call(
        _expert_body,
        grid_spec=pltpu.PrefetchScalarGridSpec(
            num_scalar_prefetch=3, grid=(rows // EXPERT_TILE,),
            in_specs=[pl.BlockSpec((EXPERT_TILE, D_MODEL), tile),
                      wspec(D_MODEL, 2 * D_FF), pl.BlockSpec((2 * LANES, 2 * LANES), lambda *_: (0, 0)),
                      wspec(1, D_FF), wspec(1, D_FF), wspec(D_FF, D_MODEL), wspec(1, D_MODEL)],
            out_specs=pl.BlockSpec((EXPERT_TILE, D_MODEL), out_tile),
            scratch_shapes=[pltpu.VMEM((D_MODEL, D_FF), BF16)] * 2 + [pltpu.VMEM((D_FF, D_MODEL), BF16)]),
        out_shape=jax.ShapeDtypeStruct((rows, D_MODEL), F32),
        compiler_params=_cparams(("arbitrary",)),
        name="moe_experts",
    )(tile_expert, first, n_tiles, xs, wp["w1"], jnp.asarray(sel, BF16), wp["b1g"], wp["b1l"],
      wp["w2"], wp["b2"])


def _combine_body(blk0, loff_ref, gstart_ref, nrows_ref, ldest_ref, meta_ref, h_ref, ys_ref, o_ref,
                  yloc, sems):
    i = pl.program_id(0)
    b = i + blk0
    slot = i % 2
    tb = h_ref.shape[0]

    def gather(bb, s):
        def group(e, carry):
            j = bb * N_EXPERTS + e
            _start_rows(ys_ref, gstart_ref[j], yloc.at[s], loff_ref[j], nrows_ref[j], sems.at[s])
            return carry
        lax.fori_loop(0, N_EXPERTS, group, 0)

    @pl.when(i == 0)
    def _():
        yloc[...] = jnp.zeros_like(yloc)
        gather(b, slot)

    @pl.when(i + 1 < pl.num_programs(0))
    def _():
        gather(b + 1, 1 - slot)

    j_last = b * N_EXPERTS + N_EXPERTS - 1
    total = loff_ref[j_last] + nrows_ref[j_last]
    cols = lax.broadcasted_iota(I32, (tb, LOCAL_ROWS), 1)
    ld = ldest_ref[...]
    meta = meta_ref[...]
    gmat = jnp.zeros((tb, LOCAL_ROWS), F32)
    for kk in range(TOP_K):
        gate = meta[:, 2 * TOP_K + kk:2 * TOP_K + kk + 1]
        gmat = jnp.where(cols == ld[:, kk:kk + 1], gate, gmat)
    _wait_rows(ys_ref, yloc.at[slot], total, sems.at[slot])
    o_ref[...] = h_ref[...] + _dot(gmat.astype(BF16), yloc[slot].astype(BF16))


def _combine(tables, blk0, ldest, meta, h2, ys):
    n = h2.shape[0]
    tb = DISPATCH_BLOCK
    row = lambda w: pl.BlockSpec((tb, w), lambda i, *_: (i, 0))
    return pl.pallas_call(
        functools.partial(_combine_body, blk0),
        grid_spec=pltpu.PrefetchScalarGridSpec(
            num_scalar_prefetch=3, grid=(n // tb,),
            in_specs=[row(TOP_K), row(LANES), row(D_MODEL), pl.BlockSpec(memory_space=pltpu.HBM)],
            out_specs=row(D_MODEL),
            scratch_shapes=[pltpu.VMEM((2, LOCAL_ROWS, D_MODEL), F32), pltpu.SemaphoreType.DMA((2,))]),
        out_shape=jax.ShapeDtypeStruct((n, D_MODEL), F32),
        compiler_params=_cparams(("arbitrary",)),
        name="moe_combine",
    )(*tables, ldest, meta, h2, ys)


def _moe(groups, wp):
    tb = DISPATCH_BLOCK
    sizes = [g[0].shape[0] for g in groups]
    assert all(s % tb == 0 for s in sizes)
    n_tok = sum(sizes)
    nb = n_tok // tb
    rows = n_tok * TOP_K + nb * N_EXPERTS * GROUP_ROWS + N_EXPERTS * EXPERT_TILE
    n_tiles_max = rows // EXPERT_TILE

    cnt = jnp.concatenate([g[3].reshape(sz // tb, -1, LANES).sum(axis=1)[:, :N_EXPERTS]
                           for g, sz in zip(groups, sizes)]).astype(I32)
    meta_t = jnp.concatenate([g[2][:, :2 * TOP_K] for g in groups]).T.astype(I32)
    eidx_t, rank_t = meta_t[:TOP_K], meta_t[TOP_K:]
    npad = (cnt + GROUP_ROWS - 1) // GROUP_ROWS * GROUP_ROWS
    loff = jnp.cumsum(npad, axis=1) - npad
    gsize = jnp.sum(npad, axis=0)
    gpad = (gsize + EXPERT_TILE - 1) // EXPERT_TILE * EXPERT_TILE
    ends = jnp.cumsum(gpad)
    gstart = (ends - gpad)[None, :] + jnp.cumsum(npad, axis=0) - npad
    before = jnp.cumsum(cnt, axis=0) - cnt
    n_tiles = (ends[-1] // EXPERT_TILE).reshape(1)
    tile_ids = jnp.arange(n_tiles_max, dtype=I32)
    tile_expert = jnp.minimum(
        jnp.sum((ends // EXPERT_TILE)[None, :] <= tile_ids[:, None], axis=1), N_EXPERTS - 1).astype(I32)

    base_t = jnp.repeat((loff - before).T, tb, axis=1)
    experts = jnp.arange(N_EXPERTS, dtype=I32)[:, None, None]
    ldest_t = rank_t + jnp.sum(jnp.where(eidx_t[None] == experts, base_t[:, None, :], 0), axis=0)
    ldest = ldest_t.T
    tables = (loff.reshape(-1), gstart.reshape(-1).astype(I32), npad.reshape(-1))

    xs = _pad_rows(((ends - gpad) + gsize).astype(I32), (gpad - gsize).astype(I32),
                   n_tiles, rows)
    starts = np.cumsum([0] + sizes[:-1])
    for g, t0, sz in zip(groups, starts, sizes):
        xs = _dispatch(tables, int(t0) // tb, ldest_t[:, t0:t0 + sz], g[1], xs)
    ys = _experts(tile_expert, n_tiles, xs, wp)
    return [_combine(tables, int(t0) // tb, ldest[t0:t0 + sz], g[2], g[0], ys)
            for g, t0, sz in zip(groups, starts, sizes)]


def _prep_weights(norm1_g, w_in, a_q_norm_g, a_k_norm_g, gla_w_alpha, gla_b_alpha, gla_out_norm_g,
                  w_out, norm2_g, mem_norm_g, mem_w_q, mem_w_k, mem_w_v, mem_q_norm_g,
                  mem_k_norm_g, mem_w_o, norm3_g, router_w, router_b, exp_w1, exp_b1, exp_w2,
                  exp_b2):
    main = 3 * A_WIDTH + 2 * B_QK_WIDTH + 2 * B_V_WIDTH
    w_lr = jnp.pad(w_in[:, main:], ((0, 0), (0, LANES - GATE_RANK)))
    rw = jnp.pad(router_w, ((0, 0), (0, LANES - N_EXPERTS)))
    rwh = rw.astype(BF16)
    return {
        "g1": norm1_g[None],
        "w_in": jnp.concatenate([w_in[:, :main], w_lr], axis=1).astype(BF16),
        "gq": jnp.tile(a_q_norm_g, A_HEADS)[None],
        "gk": jnp.tile(a_k_norm_g, A_HEADS)[None],
        "bd64": _block_diag(A_WIDTH, A_HEAD_DIM, 1.0 / A_HEAD_DIM, BF16),
        "wa": jnp.pad(gla_w_alpha, ((0, LANES - GATE_RANK), (0, 0))).astype(BF16),
        "ba": gla_b_alpha[None],
        "bones64": _block_diag(B_QK_WIDTH, B_KEY_DIM, 1.0, BF16),
        "bd128": _block_diag(B_V_WIDTH, B_VAL_DIM, 1.0 / B_VAL_DIM, BF16),
        "gout": jnp.tile(gla_out_norm_g, B_HEADS)[None],
        "woa": w_out[:A_WIDTH].astype(BF16),
        "wob": w_out[A_WIDTH:].astype(BF16),
        "g2": norm2_g[None],
        "gmem": mem_norm_g[None],
        "wmq": mem_w_q.astype(BF16),
        "wmk": mem_w_k.astype(BF16),
        "wmv": mem_w_v.astype(BF16),
        "gmq": jnp.tile(mem_q_norm_g, MEM_HEADS)[None],
        "gmk": jnp.tile(mem_k_norm_g, MEM_HEADS)[None],
        "wmo": mem_w_o.astype(BF16),
        "g3": norm3_g[None],
        "rw": jnp.concatenate([rwh, (rw - rwh.astype(F32)).astype(BF16)], axis=1),
        "rb": jnp.pad(router_b, (0, LANES - N_EXPERTS), constant_values=-1e30)[None],
        "w1": exp_w1,
        "b1g": exp_b1[:, None, 0::2],
        "b1l": exp_b1[:, None, 1::2],
        "w2": exp_w2,
        "b2": exp_b2[:, None, :],
    }


def _layer(xp, xs, mem_prompt, cache_k, cache_v, state_gla, cache_mk, cache_mv, wp):
    bsz, seq, _ = xp.shape
    db, t_new, _ = xs.shape
    w_p = min(max(w for w, _ in DILATED_GROUPS), seq)

    xpf = xp.reshape(bsz * seq, D_MODEL)
    q, k, v, k_last, v_last, bq, bk, bv, gate, la = _in_proj(xpf, wp, BF16, seq, w_p)

    def last_rows(a):
        return a.transpose(0, 2, 1) if w_p != seq else a.reshape(bsz, w_p, A_WIDTH)
    oa = _swa_prompt(q, k, v, bsz, seq)
    s0 = jnp.zeros((bsz, B_QK_WIDTH, B_VAL_DIM), F32)
    ob, s_p = _gla(bq, bk, la, bv, gate, s0, wp, seq, GLA_CHUNK, GLA_SUB, GLA_STEP)
    mk, mv = _mem_kv(mem_prompt.reshape(bsz * N_MEM, D_MODEL), wp)
    cnt0 = jnp.zeros((1, LANES), F32)
    h2_p, xn_p, meta_p, cnt, part_p = _post(xpf, oa, ob, mk.reshape(bsz, N_MEM, MEM_WIDTH),
                                            mv.reshape(bsz, N_MEM, MEM_WIDTH), cnt0, wp,
                                            POST_TILE, 1, seq // POST_TILE)

    xsf = xs.reshape(db * t_new, D_MODEL)
    q, _, _, ks, vs, bq, bk, bv, gate, la = _in_proj(xsf, wp, F32, t_new, t_new)
    new3 = lambda a: a.reshape(db, t_new, A_WIDTH)
    oa_s, nk, nv = _swa_sample(new3(q.transpose(1, 0, 2)), new3(ks), new3(vs), cache_k, cache_v)
    ob_s, s_s = _gla(bq, bk, la, bv, gate, state_gla, wp, t_new, t_new, t_new, t_new)
    h2_s, xn_s, meta_s, _, part_s = _post(xsf, oa_s.reshape(db * t_new, A_WIDTH), ob_s, cache_mk,
                                          cache_mv, cnt, wp, SAMPLE_SEQS * t_new, SAMPLE_SEQS, 1)

    y_p, y_s = _moe([(h2_p, xn_p, meta_p, part_p), (h2_s, xn_s, meta_s, part_s)], wp)
    return (y_p.reshape(bsz, seq, D_MODEL), y_s.reshape(db, t_new, D_MODEL),
            last_rows(k_last), last_rows(v_last),
            s_p, mk, mv, nk, nv, s_s)


def kernel(x_prompt, x_sample, mem_prompt, cache_swa_k, cache_swa_v, state_gla, cache_mem_k, cache_mem_v, norm1_g, w_in, a_q_norm_g, a_k_norm_g, gla_w_alpha, gla_b_alpha, gla_out_norm_g, w_out, norm2_g, mem_norm_g, mem_w_q, mem_w_k, mem_w_v, mem_q_norm_g, mem_k_norm_g, mem_w_o, norm3_g, router_w, router_b, exp_w1, exp_b1, exp_w2, exp_b2):
    depth = w_in.shape[0]
    bsz = x_prompt.shape[0]
    db, w_buf = cache_swa_k.shape[1], cache_swa_k.shape[2]
    xp, xs = x_prompt, x_sample
    per_layer = []
    for l in range(depth):
        wp = _prep_weights(
            norm1_g[l], w_in[l], a_q_norm_g[l], a_k_norm_g[l], gla_w_alpha[l], gla_b_alpha[l],
            gla_out_norm_g[l], w_out[l], norm2_g[l], mem_norm_g[l], mem_w_q[l], mem_w_k[l],
            mem_w_v[l], mem_q_norm_g[l], mem_k_norm_g[l], mem_w_o[l], norm3_g[l], router_w[l],
            router_b[l], exp_w1[l], exp_b1[l], exp_w2[l], exp_b2[l])
        xp, xs, kp, vp, s_p, mk, mv, nk, nv, s_s = _layer(
            xp, xs, mem_prompt,
            cache_swa_k[l].reshape(db, w_buf, A_WIDTH).transpose(0, 2, 1),
            cache_swa_v[l].reshape(db, w_buf, A_WIDTH).transpose(0, 2, 1),
            state_gla[l].reshape(db, B_QK_WIDTH, B_VAL_DIM),
            cache_mem_k[l].reshape(db, N_MEM * MEM_HEADS, MEM_HEAD_DIM),
            cache_mem_v[l].reshape(db, N_MEM * MEM_HEADS, MEM_HEAD_DIM),
            wp)
        w_p = kp.shape[1]
        per_layer.append((
            kp.reshape(bsz, w_p, A_HEADS, A_HEAD_DIM), vp.reshape(bsz, w_p, A_HEADS, A_HEAD_DIM),
            s_p.reshape(bsz, B_HEADS, B_KEY_DIM, B_VAL_DIM),
            mk.reshape(bsz, N_MEM, MEM_HEADS, MEM_HEAD_DIM), mv.reshape(bsz, N_MEM, MEM_HEADS, MEM_HEAD_DIM),
            nk.transpose(0, 2, 1).reshape(db, w_buf, A_HEADS, A_HEAD_DIM),
            nv.transpose(0, 2, 1).reshape(db, w_buf, A_HEADS, A_HEAD_DIM),
            s_s.reshape(db, B_HEADS, B_KEY_DIM, B_VAL_DIM)))
    stacked = [jnp.stack(t) for t in zip(*per_layer)]
    return (xp, xs, *stacked)
```

```python
import functools

import jax
import jax.numpy as jnp
import numpy as np
from jax import lax
from jax.experimental import pallas as pl
from jax.experimental.pallas import tpu as pltpu

F32 = jnp.float32
BF16 = jnp.bfloat16
I32 = jnp.int32

EPS = 1e-6
D_MODEL = 1024
A_HEADS, A_HEAD_DIM, A_WIDTH = 8, 64, 512
A_SCALE = A_HEAD_DIM ** -0.5
LOG2E = 1.4426950408889634
DILATED_GROUPS = ((128, 1), (512, 4), (2048, 16))
SPAN = 128
B_HEADS, B_KEY_DIM, B_VAL_DIM = 4, 64, 128
B_QK_WIDTH, B_V_WIDTH = 256, 512
GATE_RANK = 16
GATE_TEMP = 16.0
N_MEM, MEM_HEADS, MEM_HEAD_DIM, MEM_WIDTH = 256, 4, 128, 512
N_EXPERTS, TOP_K, D_FF = 32, 4, 1024
SWIGLU_ALPHA, SWIGLU_LIMIT = 1.702, 7.0

LANES = 128
IN_COLS = tuple(int(c) for c in np.cumsum(
    [0, A_WIDTH, A_WIDTH, A_WIDTH, B_QK_WIDTH, B_QK_WIDTH, B_V_WIDTH, B_V_WIDTH, LANES]))
VMEM_LIMIT = 56 * 1024 * 1024

PROJ_TILE = 512
SWA_BLOCK = 2048
SWA_UNROLL = 8
GLA_CHUNK, GLA_SUB, GLA_STEP = 64, 8, 1024
POST_TILE = 1024
SAMPLE_SEQS = 8
EXPERT_TILE = 512
DISPATCH_BLOCK = 512
GROUP_ROWS = 8
LOCAL_ROWS = DISPATCH_BLOCK * TOP_K + N_EXPERTS * GROUP_ROWS


def _cparams(sem, vmem=VMEM_LIMIT, **kw):
    return pltpu.CompilerParams(dimension_semantics=sem, vmem_limit_bytes=vmem, **kw)


def _full(shape):
    n = len(shape)
    return pl.BlockSpec(shape, lambda *_: (0,) * n)


def _rms(x, g):
    ms = jnp.mean(x * x, axis=-1, keepdims=True)
    return x * lax.rsqrt(ms + EPS) * g


def _dot(a, b):
    return jnp.dot(a, b, preferred_element_type=F32)


def _dot_nt(a, b):
    return lax.dot_general(a, b, (((1,), (1,)), ((), ())), preferred_element_type=F32)


def _block_diag(n, blk, val, dtype):
    i = np.arange(n)
    return jnp.asarray(np.where((i[:, None] // blk) == (i[None, :] // blk), val, 0.0), dtype)


def _in_proj_body(keep_transposed, x_ref, g1_ref, w_ref, gq_ref, gk_ref, bd_ref, wa_ref, ba_ref,
                  q_ref, k_ref, v_ref, kc_ref, vc_ref, bq_ref, bk_ref, bv_ref, gate_ref, la_ref):
    xn = _rms(x_ref[...], g1_ref[...]).astype(BF16)
    bd = bd_ref[...]

    def proj(lo, hi):
        return _dot(xn, w_ref[:, lo:hi])

    def headnorm(z, g):
        ms = _dot((z * z).astype(BF16), bd)
        return z * lax.rsqrt(ms + EPS) * g

    def put_groups(ref, z):
        for p in range(A_WIDTH // LANES):
            ref[p] = z[:, p * LANES:(p + 1) * LANES]

    c = IN_COLS
    put_groups(q_ref, headnorm(proj(c[0], c[1]), gq_ref[...]) * (A_SCALE * LOG2E))
    k = headnorm(proj(c[1], c[2]), gk_ref[...])
    v = proj(c[2], c[3])
    put_groups(k_ref, k)
    put_groups(v_ref, v)
    kc_ref[...] = jnp.transpose(k) if keep_transposed else k
    vc_ref[...] = jnp.transpose(v) if keep_transposed else v
    bq_ref[...] = proj(c[3], c[4]) * (B_KEY_DIM ** -0.5)
    bk_ref[...] = proj(c[4], c[5])
    bv_ref[...] = proj(c[5], c[6]).astype(bv_ref.dtype)
    br = proj(c[6], c[7])
    gate_ref[...] = (br * jax.nn.sigmoid(br)).astype(gate_ref.dtype)
    lr = proj(c[7], c[8]).astype(BF16)
    pre = _dot(lr, wa_ref[...]) + ba_ref[...]
    log_sig = jnp.minimum(pre, 0.0) - jnp.log1p(jnp.exp(-jnp.abs(pre)))
    la_ref[...] = log_sig * (1.0 / GATE_TEMP)


def _in_proj(x, wp, wide_dtype, seq, keep):
    n = x.shape[0]
    tm = min(PROJ_TILE, n)
    row = lambda w: pl.BlockSpec((tm, w), lambda i: (i, 0))
    ngrp = A_WIDTH // LANES
    grp = pl.BlockSpec((ngrp, tm, LANES), lambda i: (0, i, 0))
    if keep == seq:
        kept = row(A_WIDTH)
        kept_shape = (n, A_WIDTH)
    else:
        tps, kt = seq // tm, keep // tm
        assert tps * tm == seq and kt * tm == keep
        kept = pl.BlockSpec((None, A_WIDTH, tm),
                            lambda i: (i // tps, 0, jnp.maximum(i % tps - (tps - kt), 0)))
        kept_shape = (n // seq, A_WIDTH, keep)
    outs = [(B_QK_WIDTH, F32), (B_QK_WIDTH, F32), (B_V_WIDTH, wide_dtype), (B_V_WIDTH, wide_dtype),
            (B_QK_WIDTH, F32)]
    return pl.pallas_call(
        functools.partial(_in_proj_body, keep != seq),
        grid=(n // tm,),
        in_specs=[row(D_MODEL), _full((1, D_MODEL)), _full((D_MODEL, IN_COLS[-1])),
                  _full((1, A_WIDTH)), _full((1, A_WIDTH)), _full((A_WIDTH, A_WIDTH)),
                  _full((LANES, B_QK_WIDTH)), _full((1, B_QK_WIDTH))],
        out_specs=[grp] * 3 + [kept] * 2 + [row(w) for w, _ in outs],
        out_shape=[jax.ShapeDtypeStruct((ngrp, n, LANES), F32)] * 3
        + [jax.ShapeDtypeStruct(kept_shape, F32)] * 2
        + [jax.ShapeDtypeStruct((n, w), dt) for w, dt in outs],
        compiler_params=_cparams(("arbitrary",)),
        name="in_proj",
    )(x, wp["g1"], wp["w_in"], wp["gq"], wp["gk"], wp["bd64"], wp["wa"], wp["ba"])


def _unroll_for(trips):
    return max(u for u in range(1, SWA_UNROLL + 1) if trips % u == 0)


def _ds(start, size, stride):
    return pl.ds(start, size) if stride == 1 else pl.ds(start, size, stride=stride)


def _swa_prompt_body(q_ref, kp_ref, kc_ref, vp_ref, vc_ref, o_ref, m_s, l_s, acc_s, k_keep, v_keep):
    i = pl.program_id(2)
    qb = SWA_BLOCK

    @pl.when(i == 0)
    def _():
        k_keep[...] = jnp.zeros_like(k_keep)
        v_keep[...] = jnp.zeros_like(v_keep)

    lane = lax.broadcasted_iota(I32, (SPAN, LANES), 1)
    lo_mask = lane < A_HEAD_DIM
    jq = lax.broadcasted_iota(I32, (SPAN, 2 * SPAN), 0)
    jk = lax.broadcasted_iota(I32, (SPAN, 2 * SPAN), 1)
    dist = jq + SPAN - jk
    band = (dist >= 0) & (dist <= SPAN)
    cur_half = jk >= SPAN

    def attend(qp, kp, vp, valid, rows, first_group):
        vp1 = jnp.concatenate([vp, jnp.ones_like(vp)], axis=1)
        res = []
        for hh in range(2):
            msk = lo_mask if hh == 0 else jnp.logical_not(lo_mask)
            qm = jnp.where(msk, qp, 0.0).astype(BF16)
            s = jnp.where(valid, _dot_nt(qm, kp), -jnp.inf)
            m = jnp.max(s, axis=1, keepdims=True)
            res.append((m, _dot(jnp.exp2(s - m).astype(BF16), vp1)))
        m_new = jnp.where(lo_mask, res[0][0], res[1][0])
        l_new = jnp.where(lo_mask, res[0][1][:, LANES:], res[1][1][:, LANES:])
        o_new = jnp.where(lo_mask, res[0][1][:, :LANES], res[1][1][:, :LANES])
        if first_group:
            m_s[rows, :] = m_new
            l_s[rows, :] = l_new
            acc_s[rows, :] = o_new
        else:
            m_old = m_s[rows, :]
            m = jnp.maximum(m_old, m_new)
            a_old = jnp.exp2(m_old - m)
            a_new = jnp.exp2(m_new - m)
            m_s[rows, :] = m
            l_s[rows, :] = l_s[rows, :] * a_old + l_new * a_new
            acc_s[rows, :] = acc_s[rows, :] * a_old + o_new * a_new

    for gi, (_, dil) in enumerate(reversed(DILATED_GROUPS)):
        unit = dil * SPAN
        nblk = qb // unit
        first = gi == 0

        def head_block(r, carry, dil=dil, unit=unit, first=first, nblk=nblk):
            rows = _ds(r, SPAN, dil)
            valid = band & (cur_half | (i > 0))
            if nblk == 1:
                keep = pl.ds(pl.multiple_of(r * SPAN, SPAN), SPAN)
                k_cur = kc_ref[rows, :].astype(BF16)
                v_cur = vc_ref[rows, :].astype(BF16)
                ks = jnp.concatenate([k_keep[keep, :], k_cur], axis=0)
                vs = jnp.concatenate([v_keep[keep, :], v_cur], axis=0)
                k_keep[keep, :] = k_cur
                v_keep[keep, :] = v_cur
            else:
                prev = _ds(qb - unit + r, SPAN, dil)
                ks = jnp.concatenate([kp_ref[prev, :], kc_ref[rows, :]], axis=0).astype(BF16)
                vs = jnp.concatenate([vp_ref[prev, :], vc_ref[rows, :]], axis=0).astype(BF16)
            attend(q_ref[rows, :], ks, vs, valid, rows, first)
            return carry

        lax.fori_loop(0, dil, head_block, 0, unroll=_unroll_for(dil))

        if nblk > 1:
            def tail_block(idx, carry, dil=dil, unit=unit, first=first):
                n = idx // dil + 1
                r = idx % dil
                start = unit * n + r
                rows = _ds(start, SPAN, dil)
                keys = _ds(start - unit, 2 * SPAN, dil)
                attend(q_ref[rows, :], kc_ref[keys, :].astype(BF16), vc_ref[keys, :].astype(BF16),
                       band, rows, first)
                return carry

            lax.fori_loop(0, (nblk - 1) * dil, tail_block, 0, unroll=_unroll_for((nblk - 1) * dil))

    o_ref[...] = (acc_s[...] / l_s[...]).astype(o_ref.dtype)


def _swa_prompt(q, k, v, bsz, seq):
    qb = SWA_BLOCK
    nb = seq // qb
    cur = pl.BlockSpec((None, qb, LANES), lambda p, b, i: (p, b * nb + i, 0))
    prev = pl.BlockSpec((None, qb, LANES), lambda p, b, i: (p, b * nb + jnp.maximum(i - 1, 0), 0))
    return pl.pallas_call(
        _swa_prompt_body,
        grid=(A_WIDTH // LANES, bsz, nb),
        in_specs=[cur, prev, cur, prev, cur],
        out_specs=pl.BlockSpec((qb, LANES), lambda p, b, i: (b * nb + i, p)),
        out_shape=jax.ShapeDtypeStruct((bsz * seq, A_WIDTH), BF16),
        scratch_shapes=[pltpu.VMEM((qb, LANES), F32)] * 3 + [pltpu.VMEM((qb, LANES), BF16)] * 2,
        compiler_params=_cparams(("parallel", "parallel", "arbitrary")),
        name="swa_prompt",
    )(q, k, k, v, v)


def _swa_sample_body(q_ref, kn_ref, vn_ref, ck_ref, cv_ref, c1_ref, c2_ref,
                     o_ref, nk_ref, nv_ref):
    t_new = q_ref.shape[1]
    w_buf = ck_ref.shape[2]
    ck, cv = ck_ref[0], cv_ref[0]
    kn, vn = kn_ref[0], vn_ref[0]
    tail_lane = lax.broadcasted_iota(I32, (A_WIDTH, LANES), 1) >= LANES - t_new

    def shift_in(old, new, out_ref):
        moved = pltpu.roll(old, w_buf - t_new, 1)
        new_t = jnp.transpose(jnp.concatenate([jnp.zeros((LANES - t_new, A_WIDTH), F32), new], axis=0))
        out_ref[0, :, :w_buf - LANES] = moved[:, :w_buf - LANES]
        out_ref[0, :, w_buf - LANES:] = jnp.where(tail_lane, new_t, moved[:, w_buf - LANES:])

    shift_in(ck, kn, nk_ref)
    shift_in(cv, vn, nv_ref)

    q = q_ref[0]
    lane = lax.broadcasted_iota(I32, (t_new, LANES), 1)
    lo_mask = lane < A_HEAD_DIM
    c1, c2 = c1_ref[...], c2_ref[...]
    outs = []
    for p in range(A_WIDTH // LANES):
        sl = slice(p * LANES, (p + 1) * LANES)
        qp = q[:, sl]
        qblk = jnp.concatenate([jnp.where(lo_mask, qp, 0.0), jnp.where(lo_mask, 0.0, qp)],
                               axis=0).astype(BF16)
        s1 = jnp.where(c1 > 0, _dot(qblk, ck[sl, :].astype(BF16)), -jnp.inf)
        s2 = jnp.where(c2 > 0, _dot_nt(qblk, kn[:, sl].astype(BF16)), -jnp.inf)
        m = jnp.maximum(jnp.max(s1, axis=1, keepdims=True), jnp.max(s2, axis=1, keepdims=True))
        p1 = c1 * jnp.exp2(s1 - m)
        p2 = c2 * jnp.exp2(s2 - m)
        l = jnp.sum(p1, axis=1, keepdims=True) + jnp.sum(p2, axis=1, keepdims=True)
        o = (_dot_nt(p1.astype(BF16), cv[sl, :].astype(BF16))
             + _dot(p2.astype(BF16), vn[:, sl].astype(BF16))) / l
        outs.append(jnp.where(lo_mask, o[:t_new], o[t_new:]))
    o_ref[0] = jnp.concatenate(outs, axis=1).astype(o_ref.dtype)


def _sample_multiplicity(t_new, w_buf):
    t = np.arange(t_new)[:, None]
    e = np.arange(w_buf + t_new)[None, :]
    d = w_buf + t - e
    c = np.zeros(d.shape, np.float32)
    for window, dil in DILATED_GROUPS:
        c += ((d >= 0) & (d % dil == 0) & (d <= window)).astype(np.float32)
    c = np.concatenate([c, c], axis=0)
    return jnp.asarray(c[:, :w_buf]), jnp.asarray(c[:, w_buf:])


def _swa_sample(q, kn, vn, cache_k, cache_v):
    db, t_new, w = q.shape
    w_buf = cache_k.shape[2]
    assert w_buf >= max(win for win, _ in DILATED_GROUPS) and t_new % 8 == 0 and t_new <= LANES
    c1, c2 = _sample_multiplicity(t_new, w_buf)
    new = pl.BlockSpec((1, t_new, w), lambda b: (b, 0, 0))
    cache = pl.BlockSpec((1, w, w_buf), lambda b: (b, 0, 0))
    return pl.pallas_call(
        _swa_sample_body,
        grid=(db,),
        in_specs=[new, new, new, cache, cache, _full(c1.shape), _full(c2.shape)],
        out_specs=[new, cache, cache],
        out_shape=[jax.ShapeDtypeStruct((db, t_new, w), BF16),
                   jax.ShapeDtypeStruct(cache_k.shape, cache_k.dtype),
                   jax.ShapeDtypeStruct(cache_v.shape, cache_v.dtype)],
        compiler_params=_cparams(("parallel",)),
        name="swa_sample",
    )(q, kn, vn, cache_k, cache_v, c1, c2)


def _gla_body(chunk, sub, nch, seqs, q_ref, k_ref, g_ref, v_ref, gate_ref, s0_ref, tril_ref, dmask_ref,
              bones_ref, sbm_ref, bd_ref, gout_ref, o_ref, sfin_ref, sbd):
    j = pl.program_id(1)
    sbm = sbm_ref[...]
    nsub = chunk // sub
    pad = B_KEY_DIM - chunk

    def load_state(s):
        sbd[...] = jnp.concatenate([s0_ref[s]] * B_HEADS, axis=1) * sbm

    def store_state(s):
        st = sbd[...]
        sfin_ref[s] = jnp.concatenate(
            [st[h * B_KEY_DIM:(h + 1) * B_KEY_DIM, h * B_VAL_DIM:(h + 1) * B_VAL_DIM]
             for h in range(B_HEADS)], axis=0)

    row = lax.broadcasted_iota(I32, (chunk, 1), 0)
    sub_id = row // sub
    lane_w = lax.broadcasted_iota(I32, (chunk, LANES * max(nsub - 1, 1)), 1)
    lo_w = (lane_w % LANES) < B_KEY_DIM

    def one_chunk(c, first_row):
        off = pl.multiple_of(first_row + c * chunk, chunk)
        rows = pl.ds(off, chunk)
        q, k, g = q_ref[rows, :], k_ref[rows, :], g_ref[rows, :]
        v = v_ref[rows, :].astype(F32)
        g1 = g.astype(BF16)
        r1 = g - g1.astype(F32)
        g2 = r1.astype(BF16)
        g3 = (r1 - g2.astype(F32)).astype(BF16)
        tril = tril_ref[...]
        b = _dot(tril, g1) + _dot(tril, g2) + _dot(tril, g3)
        b_last = b[chunk - 1:chunk, :]
        state = sbd[...]

        o = _dot((q * jnp.exp(b)).astype(BF16), state.astype(BF16))

        bones = bones_ref[...]
        att = _dot((q * k).astype(BF16), bones) * dmask_ref[0]
        gate = jnp.exp(g)
        decay = gate
        for d in range(1, sub):
            if d > 1:
                decay = decay * pltpu.roll(gate, d - 1, 0)
            w = q * pltpu.roll(k, d, 0) * decay
            att = att + _dot(w.astype(BF16), bones) * dmask_ref[d]

        if nsub > 1:
            qx, kx = [], []
            for i in range(1, nsub):
                r_i = b[sub * i - 1:sub * i, :]
                qx.append(jnp.where(sub_id == i, q * jnp.exp(jnp.minimum(b - r_i, 0.0)), 0.0))
                kx.append(jnp.where(sub_id < i, k * jnp.exp(jnp.minimum(r_i - b, 0.0)), 0.0))
            parts = []
            for p in range(B_QK_WIDTH // LANES):
                sl = slice(p * LANES, (p + 1) * LANES)
                qp = jnp.concatenate([x[:, sl] for x in qx], axis=1)
                kp = jnp.concatenate([x[:, sl] for x in kx], axis=1).astype(BF16)
                zero = jnp.zeros_like(kp)
                lhs = jnp.concatenate([jnp.where(lo_w, qp, 0.0), jnp.where(lo_w, 0.0, qp)],
                                      axis=1).astype(BF16)
                rhs = jnp.concatenate([jnp.concatenate([kp, zero], axis=1),
                                       jnp.concatenate([zero, kp], axis=1)], axis=0)
                parts.append(_dot_nt(lhs, rhs))
            att = att + jnp.concatenate(parts, axis=1)

        if pad:
            vrow = jnp.concatenate([v, jnp.zeros((pad, B_V_WIDTH), F32)], axis=0)
        else:
            vrow = v
        vbd = (jnp.concatenate([vrow] * B_HEADS, axis=0) * sbm).astype(BF16)
        o = o + _dot(att.astype(BF16), vbd)

        ke = (k * jnp.exp(b_last - b)).astype(BF16)
        upd = lax.dot_general(ke, v.astype(BF16), (((0,), (0,)), ((), ())),
                              preferred_element_type=F32)
        dec = jnp.transpose(jnp.broadcast_to(jnp.exp(b_last), (8, B_QK_WIDTH)))[:, 0:1]
        sbd[...] = (state * dec + upd) * sbm

        ms = _dot((o * o).astype(BF16), bd_ref[...])
        on = o * lax.rsqrt(ms + EPS) * gout_ref[...] * gate_ref[rows, :].astype(F32)
        o_ref[rows, :] = on.astype(o_ref.dtype)
        return first_row

    def chunks(first_row):
        lax.fori_loop(0, nch, one_chunk, first_row, unroll=2 if nch % 2 == 0 else 1)

    if seqs == 1:
        pl.when(j == 0)(lambda: load_state(0))
        chunks(jnp.int32(0))
        pl.when(j == pl.num_programs(1) - 1)(lambda: store_state(0))
    else:
        def one_sequence(s, carry):
            load_state(s)
            chunks(s * (nch * chunk))
            store_state(s)
            return carry

        lax.fori_loop(0, seqs, one_sequence, 0)


def _gla_consts(chunk, sub):
    t = np.arange(chunk)
    tril = (t[:, None] >= t[None, :]).astype(np.float32)
    lane = np.arange(B_QK_WIDTH)
    dmask = np.zeros((sub, chunk, B_QK_WIDTH), np.float32)
    for d in range(sub):
        ok = (t % sub) >= d
        dmask[d] = ((lane[None, :] % B_KEY_DIM) == (t[:, None] - d)) & ok[:, None]
    r = np.arange(B_QK_WIDTH)[:, None] // B_KEY_DIM
    c = np.arange(B_V_WIDTH)[None, :] // B_VAL_DIM
    sbm = (r == c).astype(np.float32)
    return jnp.asarray(tril, BF16), jnp.asarray(dmask), jnp.asarray(sbm)


def _gla(q, k, g, v, gate, s0, wp, length, chunk, sub, step, seqs=1, out_dtype=BF16):
    n = q.shape[0]
    bsz = n // length
    assert chunk == sub or chunk == B_KEY_DIM
    assert seqs == 1 or (step == length and bsz % seqs == 0)
    tril, dmask, sbm = _gla_consts(chunk, sub)
    nstep = length // step
    row = lambda w: pl.BlockSpec((step * seqs, w), lambda b, j: (b * nstep + j, 0))
    st = pl.BlockSpec((seqs, B_QK_WIDTH, B_VAL_DIM), lambda b, j: (b, 0, 0))
    return pl.pallas_call(
        functools.partial(_gla_body, chunk, sub, step // chunk, seqs),
        grid=(bsz // seqs, nstep),
        in_specs=[row(256), row(256), row(256), row(512), row(512), st, _full(tril.shape),
                  _full(dmask.shape), _full((256, 256)), _full(sbm.shape), _full((512, 512)),
                  _full((1, 512))],
        out_specs=[row(512), st],
        out_shape=[jax.ShapeDtypeStruct((n, B_V_WIDTH), out_dtype),
                   jax.ShapeDtypeStruct(s0.shape, F32)],
        scratch_shapes=[pltpu.VMEM((B_QK_WIDTH, B_V_WIDTH), F32)],
        compiler_params=_cparams(("parallel", "arbitrary")),
        name="gla",
    )(q, k, g, v, gate, s0, tril, dmask, wp["bones64"], sbm, wp["bd128"], wp["gout"])


def _head_rms(z, g, scale=1.0):
    parts = []
    for h in range(MEM_HEADS):
        zh = z[:, h * LANES:(h + 1) * LANES]
        parts.append(zh * lax.rsqrt(jnp.mean(zh * zh, axis=-1, keepdims=True) + EPS))
    return jnp.concatenate(parts, axis=1) * (g * scale)


def _mem_kv_body(m_ref, gn_ref, wk_ref, wv_ref, gk_ref, mk_ref, mv_ref):
    mn = _rms(m_ref[...], gn_ref[...]).astype(BF16)
    mk_ref[...] = _head_rms(_dot(mn, wk_ref[...]), gk_ref[...])
    mv_ref[...] = _dot(mn, wv_ref[...])


def _mem_kv(mem, wp):
    n = mem.shape[0]
    tm = 256
    row = lambda w: pl.BlockSpec((tm, w), lambda i: (i, 0))
    return pl.pallas_call(
        _mem_kv_body,
        grid=(n // tm,),
        in_specs=[row(D_MODEL), _full((1, D_MODEL)), _full((D_MODEL, 512)), _full((D_MODEL, 512)),
                  _full((1, 512))],
        out_specs=[row(512), row(512)],
        out_shape=[jax.ShapeDtypeStruct((n, 512), F32)] * 2,
        compiler_params=_cparams(("parallel",)),
        name="mem_kv",
    )(mem, wp["gmem"], wp["wmk"], wp["wmv"], wp["gmk"])


def _post_body(nseq, x_ref, oa_ref, ob_ref, woa_ref, wob_ref, g2_ref, wq_ref, gmq_ref, mk_ref,
               mv_ref, wo_ref, g3_ref, rw_ref, rb_ref, cnt0_ref, tri_ref,
               h2_ref, xn_ref, meta_ref, cnt_ref, part_ref, carry):
    tm = x_ref.shape[0]

    @pl.when(pl.program_id(0) == 0)
    def _():
        carry[...] = cnt0_ref[...]

    nk = nseq * N_MEM

    def mem_head(ref, hd):
        if ref.shape[-1] == MEM_HEAD_DIM:
            rows = ref[:, pl.ds(hd, N_MEM, stride=MEM_HEADS), :]
        else:
            rows = ref[:, :, hd * LANES:(hd + 1) * LANES]
        return rows.reshape(nk, LANES).astype(BF16)

    if nseq > 1:
        rt = lax.broadcasted_iota(I32, (tm, nk), 0) // (tm // nseq)
        ct = lax.broadcasted_iota(I32, (tm, nk), 1) // N_MEM
        same = rt == ct

    def router_logits():
        h = (x_ref[...] + _dot(oa_ref[...], woa_ref[...])
             + _dot(ob_ref[...].astype(BF16), wob_ref[...]))
        hn = _rms(h, g2_ref[...]).astype(BF16)
        qm = _head_rms(_dot(hn, wq_ref[...]), gmq_ref[...], MEM_HEAD_DIM ** -0.5).astype(BF16)
        outs = []
        for hd in range(MEM_HEADS):
            sl = slice(hd * LANES, (hd + 1) * LANES)
            s = _dot_nt(qm[:, sl], mem_head(mk_ref, hd))
            if nseq > 1:
                s = jnp.where(same, s, -jnp.inf)
            m = jnp.max(s, axis=1, keepdims=True)
            pr = jnp.exp(s - m)
            l = jnp.sum(pr, axis=1, keepdims=True)
            outs.append(_dot(pr.astype(BF16), mem_head(mv_ref, hd)) / l)
        h2 = h + _dot(jnp.concatenate(outs, axis=1).astype(BF16), wo_ref[...])
        h2_ref[...] = h2
        xn = _rms(h2, g3_ref[...])
        xn_ref[...] = xn
        x1 = xn.astype(BF16)
        x2 = (xn - x1.astype(F32)).astype(BF16)
        prod = _dot(jnp.concatenate([x1, x2], axis=0), rw_ref[...])
        return prod[:tm, :LANES] + prod[:tm, LANES:] + prod[tm:, :LANES] + prod[tm:, LANES:]

    logits = router_logits() + rb_ref[...]
    lane = lax.broadcasted_iota(I32, (tm, LANES), 1)
    vals, idxs, hots = [], [], []
    work = logits
    for _ in range(TOP_K):
        m = jnp.max(work, axis=1, keepdims=True)
        idx = jnp.min(jnp.where(work == m, lane, LANES), axis=1, keepdims=True)
        hot = lane == idx
        vals.append(m)
        idxs.append(idx)
        hots.append(hot)
        work = jnp.where(hot, -jnp.inf, work)
    exps = [jnp.exp(v - vals[0]) for v in vals]
    den = exps[0] + exps[1] + exps[2] + exps[3]

    sel = (hots[0] | hots[1] | hots[2] | hots[3]).astype(F32)
    before = _dot(tri_ref[...], sel.astype(BF16)) + carry[...]
    carry[...] = carry[...] + jnp.sum(sel, axis=0, keepdims=True)
    cnt_ref[...] = carry[...]
    parts = part_ref.shape[1]
    for j in range(parts):
        part_ref[0, j:j + 1, :] = jnp.sum(sel[j * (tm // parts):(j + 1) * (tm // parts)], axis=0,
                                          keepdims=True)

    meta = jnp.zeros((tm, LANES), F32)
    for kk in range(TOP_K):
        rank = jnp.sum(jnp.where(hots[kk], before, 0.0), axis=1, keepdims=True)
        meta = jnp.where(lane == kk, idxs[kk].astype(F32), meta)
        meta = jnp.where(lane == TOP_K + kk, rank, meta)
        meta = jnp.where(lane == 2 * TOP_K + kk, exps[kk] / den, meta)
    meta_ref[...] = meta


def _post(x, oa, ob, mk, mv, cnt0, wp, tm, nseq, tiles_per_mem):
    n = x.shape[0]
    row = lambda w: pl.BlockSpec((tm, w), lambda i: (i, 0))
    mem = pl.BlockSpec((nseq,) + mk.shape[1:], lambda i: (i // tiles_per_mem, 0, 0))
    tri = jnp.asarray(np.tril(np.ones((tm, tm), np.float32), -1), BF16)
    parts = max(tm // DISPATCH_BLOCK, 1)
    return pl.pallas_call(
        functools.partial(_post_body, nseq),
        grid=(n // tm,),
        in_specs=[row(D_MODEL), row(512), row(512), _full((512, D_MODEL)), _full((512, D_MODEL)),
                  _full((1, D_MODEL)), _full((D_MODEL, 512)), _full((1, 512)), mem, mem,
                  _full((512, D_MODEL)), _full((1, D_MODEL)), _full((D_MODEL, 2 * LANES)),
                  _full((1, LANES)), _full((1, LANES)), _full((tm, tm))],
        out_specs=[row(D_MODEL), row(D_MODEL), row(LANES), _full((1, LANES)),
                   pl.BlockSpec((1, parts, LANES), lambda i: (i, 0, 0))],
        out_shape=[jax.ShapeDtypeStruct((n, D_MODEL), F32), jax.ShapeDtypeStruct((n, D_MODEL), F32),
                   jax.ShapeDtypeStruct((n, LANES), F32), jax.ShapeDtypeStruct((1, LANES), F32),
                   jax.ShapeDtypeStruct((n // tm, parts, LANES), F32)],
        scratch_shapes=[pltpu.VMEM((1, LANES), F32)],
        compiler_params=_cparams(("arbitrary",)),
        name="post",
    )(x, oa, ob, wp["woa"], wp["wob"], wp["g2"], wp["wmq"], wp["gmq"], mk, mv, wp["wmo"],
      wp["g3"], wp["rw"], wp["rb"], cnt0, tri)


def _start_rows(src, src_row, dst, dst_row, n_rows, sem):
    s = pl.multiple_of(jnp.asarray(src_row, I32), GROUP_ROWS)
    d = pl.multiple_of(jnp.asarray(dst_row, I32), GROUP_ROWS)
    n_rows = pl.multiple_of(jnp.asarray(n_rows, I32), GROUP_ROWS)

    @pl.when(n_rows > 0)
    def _():
        pltpu.make_async_copy(src.at[pl.ds(s, n_rows)], dst.at[pl.ds(d, n_rows)], sem).start()

    return n_rows


def _wait_rows(like_src, like_dst, n_rows, sem):
    n_rows = pl.multiple_of(n_rows, GROUP_ROWS)

    @pl.when(n_rows > 0)
    def _():
        pltpu.make_async_copy(like_src.at[pl.ds(0, n_rows)], like_dst.at[pl.ds(0, n_rows)], sem).wait()


def _pad_rows_body(pstart_ref, prows_ref, nt_ref, xs_ref, zeros, sem):
    zeros[...] = jnp.zeros_like(zeros)
    n_tiles_max = xs_ref.shape[0] // EXPERT_TILE

    def expert_tail(e, total):
        return total + _start_rows(zeros, 0, xs_ref, pstart_ref[e], prows_ref[e], sem)

    def unused_tile(t, total):
        return total + _start_rows(zeros, 0, xs_ref, t * EXPERT_TILE, EXPERT_TILE, sem)

    total = lax.fori_loop(0, N_EXPERTS, expert_tail, jnp.int32(0))
    total = lax.fori_loop(nt_ref[0], n_tiles_max, unused_tile, total)
    _wait_rows(xs_ref, xs_ref, total, sem)


def _pad_rows(pad_start, pad_units, n_tiles, rows):
    return pl.pallas_call(
        _pad_rows_body,
        grid_spec=pltpu.PrefetchScalarGridSpec(
            num_scalar_prefetch=3, grid=(1,),
            in_specs=[],
            out_specs=pl.BlockSpec(memory_space=pltpu.HBM),
            scratch_shapes=[pltpu.VMEM((EXPERT_TILE, D_MODEL), F32), pltpu.SemaphoreType.DMA]),
        out_shape=jax.ShapeDtypeStruct((rows, D_MODEL), F32),
        compiler_params=_cparams(("arbitrary",)),
        name="moe_pad_rows",
    )(pad_start, pad_units, n_tiles)


def _dispatch_body(blk0, loff_ref, gstart_ref, nrows_ref, ldest_ref, x_ref, xs_in_ref, xs_ref,
                   xloc, sems):
    del xs_in_ref
    i = pl.program_id(0)
    b = i + blk0
    slot = i % 2
    tb = x_ref.shape[0]
    rows = lax.broadcasted_iota(I32, (LOCAL_ROWS, tb), 0)
    ld = ldest_ref[...]
    hot = rows == ld[0:1, :]
    for kk in range(1, TOP_K):
        hot = hot | (rows == ld[kk:kk + 1, :])
    xloc[slot] = _dot(jnp.where(hot, 1.0, 0.0).astype(BF16), x_ref[...].astype(BF16))

    def block_rows(bb):
        j = bb * N_EXPERTS + N_EXPERTS - 1
        return loff_ref[j] + nrows_ref[j]

    @pl.when(i > 0)
    def _():
        _wait_rows(xloc.at[1 - slot], xs_ref, block_rows(b - 1), sems.at[1 - slot])

    def group(e, carry):
        j = b * N_EXPERTS + e
        _start_rows(xloc.at[slot], loff_ref[j], xs_ref, gstart_ref[j], nrows_ref[j], sems.at[slot])
        return carry

    lax.fori_loop(0, N_EXPERTS, group, 0)

    @pl.when(i == pl.num_programs(0) - 1)
    def _():
        _wait_rows(xloc.at[slot], xs_ref, block_rows(b), sems.at[slot])


def _dispatch(tables, blk0, ldest_t, xn, xs):
    n = xn.shape[0]
    tb = DISPATCH_BLOCK
    idx = lambda i, *_: (i, 0)
    return pl.pallas_call(
        functools.partial(_dispatch_body, blk0),
        grid_spec=pltpu.PrefetchScalarGridSpec(
            num_scalar_prefetch=3, grid=(n // tb,),
            in_specs=[pl.BlockSpec((TOP_K, tb), lambda i, *_: (0, i)),
                      pl.BlockSpec((tb, D_MODEL), idx),
                      pl.BlockSpec(memory_space=pltpu.HBM)],
            out_specs=pl.BlockSpec(memory_space=pltpu.HBM),
            scratch_shapes=[pltpu.VMEM((2, LOCAL_ROWS, D_MODEL), F32), pltpu.SemaphoreType.DMA((2,))]),
        out_shape=jax.ShapeDtypeStruct(xs.shape, xs.dtype),
        input_output_aliases={5: 0},
        compiler_params=_cparams(("arbitrary",)),
        name="moe_dispatch",
    )(*tables, ldest_t, xn, xs)


def _expert_body(te_ref, first_ref, nt_ref, x_ref, w1_ref, sel_ref, b1g_ref, b1l_ref, w2_ref, b2_ref,
                 y_ref, w1g, w1l, w2):
    i = pl.program_id(0)
    live = i < nt_ref[0]

    @pl.when(live & (first_ref[i] == 1))
    def _():
        sel = sel_ref[...]
        for j in range(D_FF // LANES):
            z = _dot(w1_ref[0, :, 2 * LANES * j:2 * LANES * (j + 1)].astype(BF16), sel)
            w1g[:, LANES * j:LANES * (j + 1)] = z[:, :LANES].astype(BF16)
            w1l[:, LANES * j:LANES * (j + 1)] = z[:, LANES:].astype(BF16)
        w2[...] = w2_ref[0].astype(BF16)

    @pl.when(live)
    def _():
        x = x_ref[...].astype(BF16)
        glu = jnp.minimum(_dot(x, w1g[...]) + b1g_ref[0], SWIGLU_LIMIT)
        lin = jnp.clip(_dot(x, w1l[...]) + b1l_ref[0], -SWIGLU_LIMIT, SWIGLU_LIMIT)
        act = glu * jax.nn.sigmoid(SWIGLU_ALPHA * glu) * (lin + 1.0)
        y_ref[...] = _dot(act.astype(BF16), w2[...]) + b2_ref[0]

    @pl.when(jnp.logical_not(live))
    def _():
        y_ref[...] = jnp.zeros_like(y_ref)


def _experts(tile_expert, n_tiles, xs, wp):
    rows = xs.shape[0]
    first = jnp.concatenate([jnp.ones((1,), I32),
                             (tile_expert[1:] != tile_expert[:-1]).astype(I32)])
    c = np.arange(2 * LANES)
    sel = np.zeros((2 * LANES, 2 * LANES), np.float32)
    sel[c, (c % 2) * LANES + c // 2] = 1.0
    tile = lambda i, te, fi, nt: (jnp.minimum(i, nt[0] - 1), 0)
    out_tile = lambda i, te, fi, nt: (i, 0)
    wsel = lambda i, te, fi, nt: (te[jnp.minimum(i, nt[0] - 1)], 0, 0)
    wspec = lambda r, c: pl.BlockSpec((1, r, c), wsel)
    return pl.pallas_call(
        _expert_body,
        grid_spec=pltpu.PrefetchScalarGridSpec(
            num_scalar_prefetch=3, grid=(rows // EXPERT_TILE,),
            in_specs=[pl.BlockSpec((EXPERT_TILE, D_MODEL), tile),
                      wspec(D_MODEL, 2 * D_FF), pl.BlockSpec((2 * LANES, 2 * LANES), lambda *_: (0, 0)),
                      wspec(1, D_FF), wspec(1, D_FF), wspec(D_FF, D_MODEL), wspec(1, D_MODEL)],
            out_specs=pl.BlockSpec((EXPERT_TILE, D_MODEL), out_tile),
            scratch_shapes=[pltpu.VMEM((D_MODEL, D_FF), BF16)] * 2 + [pltpu.VMEM((D_FF, D_MODEL), BF16)]),
        out_shape=jax.ShapeDtypeStruct((rows, D_MODEL), F32),
        compiler_params=_cparams(("arbitrary",)),
        name="moe_experts",
    )(tile_expert, first, n_tiles, xs, wp["w1"], jnp.asarray(sel, BF16), wp["b1g"], wp["b1l"],
      wp["w2"], wp["b2"])


def _combine_body(blk0, loff_ref, gstart_ref, nrows_ref, ldest_ref, meta_ref, h_ref, ys_ref, o_ref,
                  yloc, sems):
    i = pl.program_id(0)
    b = i + blk0
    slot = i % 2
    tb = h_ref.shape[0]

    def gather(bb, s):
        def group(e, carry):
            j = bb * N_EXPERTS + e
            _start_rows(ys_ref, gstart_ref[j], yloc.at[s], loff_ref[j], nrows_ref[j], sems.at[s])
            return carry
        lax.fori_loop(0, N_EXPERTS, group, 0)

    @pl.when(i == 0)
    def _():
        yloc[...] = jnp.zeros_like(yloc)
        gather(b, slot)

    @pl.when(i + 1 < pl.num_programs(0))
    def _():
        gather(b + 1, 1 - slot)

    j_last = b * N_EXPERTS + N_EXPERTS - 1
    total = loff_ref[j_last] + nrows_ref[j_last]
    cols = lax.broadcasted_iota(I32, (tb, LOCAL_ROWS), 1)
    ld = ldest_ref[...]
    meta = meta_ref[...]
    gmat = jnp.zeros((tb, LOCAL_ROWS), F32)
    for kk in range(TOP_K):
        gate = meta[:, 2 * TOP_K + kk:2 * TOP_K + kk + 1]
        gmat = jnp.where(cols == ld[:, kk:kk + 1], gate, gmat)
    _wait_rows(ys_ref, yloc.at[slot], total, sems.at[slot])
    o_ref[...] = h_ref[...] + _dot(gmat.astype(BF16), yloc[slot].astype(BF16))


def _combine(tables, blk0, ldest, meta, h2, ys):
    n = h2.shape[0]
    tb = DISPATCH_BLOCK
    row = lambda w: pl.BlockSpec((tb, w), lambda i, *_: (i, 0))
    return pl.pallas_call(
        functools.partial(_combine_body, blk0),
        grid_spec=pltpu.PrefetchScalarGridSpec(
            num_scalar_prefetch=3, grid=(n // tb,),
            in_specs=[row(TOP_K), row(LANES), row(D_MODEL), pl.BlockSpec(memory_space=pltpu.HBM)],
            out_specs=row(D_MODEL),
            scratch_shapes=[pltpu.VMEM((2, LOCAL_ROWS, D_MODEL), F32), pltpu.SemaphoreType.DMA((2,))]),
        out_shape=jax.ShapeDtypeStruct((n, D_MODEL), F32),
        compiler_params=_cparams(("arbitrary",)),
        name="moe_combine",
    )(*tables, ldest, meta, h2, ys)


def _moe(groups, wp):
    tb = DISPATCH_BLOCK
    sizes = [g[0].shape[0] for g in groups]
    assert all(s % tb == 0 for s in sizes)
    n_tok = sum(sizes)
    nb = n_tok // tb
    rows = n_tok * TOP_K + nb * N_EXPERTS * GROUP_ROWS + N_EXPERTS * EXPERT_TILE
    n_tiles_max = rows // EXPERT_TILE

    cnt = jnp.concatenate([g[3].reshape(sz // tb, -1, LANES).sum(axis=1)[:, :N_EXPERTS]
                           for g, sz in zip(groups, sizes)]).astype(I32)
    meta_t = jnp.concatenate([g[2][:, :2 * TOP_K] for g in groups]).T.astype(I32)
    eidx_t, rank_t = meta_t[:TOP_K], meta_t[TOP_K:]
    npad = (cnt + GROUP_ROWS - 1) // GROUP_ROWS * GROUP_ROWS
    loff = jnp.cumsum(npad, axis=1) - npad
    gsize = jnp.sum(npad, axis=0)
    gpad = (gsize + EXPERT_TILE - 1) // EXPERT_TILE * EXPERT_TILE
    ends = jnp.cumsum(gpad)
    gstart = (ends - gpad)[None, :] + jnp.cumsum(npad, axis=0) - npad
    before = jnp.cumsum(cnt, axis=0) - cnt
    n_tiles = (ends[-1] // EXPERT_TILE).reshape(1)
    tile_ids = jnp.arange(n_tiles_max, dtype=I32)
    tile_expert = jnp.minimum(
        jnp.sum((ends // EXPERT_TILE)[None, :] <= tile_ids[:, None], axis=1), N_EXPERTS - 1).astype(I32)

    base_t = jnp.repeat((loff - before).T, tb, axis=1)
    experts = jnp.arange(N_EXPERTS, dtype=I32)[:, None, None]
    ldest_t = rank_t + jnp.sum(jnp.where(eidx_t[None] == experts, base_t[:, None, :], 0), axis=0)
    ldest = ldest_t.T
    tables = (loff.reshape(-1), gstart.reshape(-1).astype(I32), npad.reshape(-1))

    xs = _pad_rows(((ends - gpad) + gsize).astype(I32), (gpad - gsize).astype(I32),
                   n_tiles, rows)
    starts = np.cumsum([0] + sizes[:-1])
    for g, t0, sz in zip(groups, starts, sizes):
        xs = _dispatch(tables, int(t0) // tb, ldest_t[:, t0:t0 + sz], g[1], xs)
    ys = _experts(tile_expert, n_tiles, xs, wp)
    return [_combine(tables, int(t0) // tb, ldest[t0:t0 + sz], g[2], g[0], ys)
            for g, t0, sz in zip(groups, starts, sizes)]


def _prep_weights(norm1_g, w_in, a_q_norm_g, a_k_norm_g, gla_w_alpha, gla_b_alpha, gla_out_norm_g,
                  w_out, norm2_g, mem_norm_g, mem_w_q, mem_w_k, mem_w_v, mem_q_norm_g,
                  mem_k_norm_g, mem_w_o, norm3_g, router_w, router_b, exp_w1, exp_b1, exp_w2,
                  exp_b2):
    main = 3 * A_WIDTH + 2 * B_QK_WIDTH + 2 * B_V_WIDTH
    w_lr = jnp.pad(w_in[:, main:], ((0, 0), (0, LANES - GATE_RANK)))
    rw = jnp.pad(router_w, ((0, 0), (0, LANES - N_EXPERTS)))
    rwh = rw.astype(BF16)
    return {
        "g1": norm1_g[None],
        "w_in": jnp.concatenate([w_in[:, :main], w_lr], axis=1).astype(BF16),
        "gq": jnp.tile(a_q_norm_g, A_HEADS)[None],
        "gk": jnp.tile(a_k_norm_g, A_HEADS)[None],
        "bd64": _block_diag(A_WIDTH, A_HEAD_DIM, 1.0 / A_HEAD_DIM, BF16),
        "wa": jnp.pad(gla_w_alpha, ((0, LANES - GATE_RANK), (0, 0))).astype(BF16),
        "ba": gla_b_alpha[None],
        "bones64": _block_diag(B_QK_WIDTH, B_KEY_DIM, 1.0, BF16),
        "bd128": _block_diag(B_V_WIDTH, B_VAL_DIM, 1.0 / B_VAL_DIM, BF16),
        "gout": jnp.tile(gla_out_norm_g, B_HEADS)[None],
        "woa": w_out[:A_WIDTH].astype(BF16),
        "wob": w_out[A_WIDTH:].astype(BF16),
        "g2": norm2_g[None],
        "gmem": mem_norm_g[None],
        "wmq": mem_w_q.astype(BF16),
        "wmk": mem_w_k.astype(BF16),
        "wmv": mem_w_v.astype(BF16),
        "gmq": jnp.tile(mem_q_norm_g, MEM_HEADS)[None],
        "gmk": jnp.tile(mem_k_norm_g, MEM_HEADS)[None],
        "wmo": mem_w_o.astype(BF16),
        "g3": norm3_g[None],
        "rw": jnp.concatenate([rwh, (rw - rwh.astype(F32)).astype(BF16)], axis=1),
        "rb": jnp.pad(router_b, (0, LANES - N_EXPERTS), constant_values=-1e30)[None],
        "w1": exp_w1,
        "b1g": exp_b1[:, None, 0::2],
        "b1l": exp_b1[:, None, 1::2],
        "w2": exp_w2,
        "b2": exp_b2[:, None, :],
    }


def _layer(xp, xs, mem_prompt, cache_k, cache_v, state_gla, cache_mk, cache_mv, wp):
    bsz, seq, _ = xp.shape
    db, t_new, _ = xs.shape
    w_p = min(max(w for w, _ in DILATED_GROUPS), seq)

    xpf = xp.reshape(bsz * seq, D_MODEL)
    q, k, v, k_last, v_last, bq, bk, bv, gate, la = _in_proj(xpf, wp, BF16, seq, w_p)

    def last_rows(a):
        return a.transpose(0, 2, 1) if w_p != seq else a.reshape(bsz, w_p, A_WIDTH)
    oa = _swa_prompt(q, k, v, bsz, seq)
    s0 = jnp.zeros((bsz, B_QK_WIDTH, B_VAL_DIM), F32)
    ob, s_p = _gla(bq, bk, la, bv, gate, s0, wp, seq, GLA_CHUNK, GLA_SUB, GLA_STEP)
    mk, mv = _mem_kv(mem_prompt.reshape(bsz * N_MEM, D_MODEL), wp)
    cnt0 = jnp.zeros((1, LANES), F32)
    h2_p, xn_p, meta_p, cnt, part_p = _post(xpf, oa, ob, mk.reshape(bsz, N_MEM, MEM_WIDTH),
                                            mv.reshape(bsz, N_MEM, MEM_WIDTH), cnt0, wp,
                                            POST_TILE, 1, seq // POST_TILE)

    xsf = xs.reshape(db * t_new, D_MODEL)
    q, _, _, ks, vs, bq, bk, bv, gate, la = _in_proj(xsf, wp, F32, t_new, t_new)
    new3 = lambda a: a.reshape(db, t_new, A_WIDTH)
    oa_s, nk, nv = _swa_sample(new3(q.transpose(1, 0, 2)), new3(ks), new3(vs), cache_k, cache_v)
    ob_s, s_s = _gla(bq, bk, la, bv, gate, state_gla, wp, t_new, t_new, t_new, t_new,
                     seqs=SAMPLE_SEQS, out_dtype=F32)
    h2_s, xn_s, meta_s, _, part_s = _post(xsf, oa_s.reshape(db * t_new, A_WIDTH), ob_s, cache_mk,
                                          cache_mv, cnt, wp, SAMPLE_SEQS * t_new, SAMPLE_SEQS, 1)

    y_p, y_s = _moe([(h2_p, xn_p, meta_p, part_p), (h2_s, xn_s, meta_s, part_s)], wp)
    return (y_p.reshape(bsz, seq, D_MODEL), y_s.reshape(db, t_new, D_MODEL),
            last_rows(k_last), last_rows(v_last),
            s_p, mk, mv, nk, nv, s_s)


def kernel(x_prompt, x_sample, mem_prompt, cache_swa_k, cache_swa_v, state_gla, cache_mem_k, cache_mem_v, norm1_g, w_in, a_q_norm_g, a_k_norm_g, gla_w_alpha, gla_b_alpha, gla_out_norm_g, w_out, norm2_g, mem_norm_g, mem_w_q, mem_w_k, mem_w_v, mem_q_norm_g, mem_k_norm_g, mem_w_o, norm3_g, router_w, router_b, exp_w1, exp_b1, exp_w2, exp_b2):
    depth = w_in.shape[0]
    bsz = x_prompt.shape[0]
    db, w_buf = cache_swa_k.shape[1], cache_swa_k.shape[2]
    xp, xs = x_prompt, x_sample
    per_layer = []
    for l in range(depth):
        wp = _prep_weights(
            norm1_g[l], w_in[l], a_q_norm_g[l], a_k_norm_g[l], gla_w_alpha[l], gla_b_alpha[l],
            gla_out_norm_g[l], w_out[l], norm2_g[l], mem_norm_g[l], mem_w_q[l], mem_w_k[l],
            mem_w_v[l], mem_q_norm_g[l], mem_k_norm_g[l], mem_w_o[l], norm3_g[l], router_w[l],
            router_b[l], exp_w1[l], exp_b1[l], exp_w2[l], exp_b2[l])
        xp, xs, kp, vp, s_p, mk, mv, nk, nv, s_s = _layer(
            xp, xs, mem_prompt,
            cache_swa_k[l].reshape(db, w_buf, A_WIDTH).transpose(0, 2, 1),
            cache_swa_v[l].reshape(db, w_buf, A_WIDTH).transpose(0, 2, 1),
            state_gla[l].reshape(db, B_QK_WIDTH, B_VAL_DIM),
            cache_mem_k[l].reshape(db, N_MEM * MEM_HEADS, MEM_HEAD_DIM),
            cache_mem_v[l].reshape(db, N_MEM * MEM_HEADS, MEM_HEAD_DIM),
            wp)
        w_p = kp.shape[1]
        per_layer.append((
            kp.reshape(bsz, w_p, A_HEADS, A_HEAD_DIM), vp.reshape(bsz, w_p, A_HEADS, A_HEAD_DIM),
            s_p.reshape(bsz, B_HEADS, B_KEY_DIM, B_VAL_DIM),
            mk.reshape(bsz, N_MEM, MEM_HEADS, MEM_HEAD_DIM), mv.reshape(bsz, N_MEM, MEM_HEADS, MEM_HEAD_DIM),
            nk.transpose(0, 2, 1).reshape(db, w_buf, A_HEADS, A_HEAD_DIM),
            nv.transpose(0, 2, 1).reshape(db, w_buf, A_HEADS, A_HEAD_DIM),
            s_s.reshape(db, B_HEADS, B_KEY_DIM, B_VAL_DIM)))
    stacked = [jnp.stack(t) for t in zip(*per_layer)]
    return (xp, xs, *stacked)
```

```python
import functools

import jax
import jax.numpy as jnp
import numpy as np
from jax import lax
from jax.experimental import pallas as pl
from jax.experimental.pallas import tpu as pltpu

F32 = jnp.float32
BF16 = jnp.bfloat16
I32 = jnp.int32

EPS = 1e-6
D_MODEL = 1024
A_HEADS, A_HEAD_DIM, A_WIDTH = 8, 64, 512
A_SCALE = A_HEAD_DIM ** -0.5
LOG2E = 1.4426950408889634
DILATED_GROUPS = ((128, 1), (512, 4), (2048, 16))
SPAN = 128
B_HEADS, B_KEY_DIM, B_VAL_DIM = 4, 64, 128
B_QK_WIDTH, B_V_WIDTH = 256, 512
GATE_RANK = 16
GATE_TEMP = 16.0
N_MEM, MEM_HEADS, MEM_HEAD_DIM, MEM_WIDTH = 256, 4, 128, 512
N_EXPERTS, TOP_K, D_FF = 32, 4, 1024
SWIGLU_ALPHA, SWIGLU_LIMIT = 1.702, 7.0

LANES = 128
IN_COLS = tuple(int(c) for c in np.cumsum(
    [0, A_WIDTH, A_WIDTH, A_WIDTH, B_QK_WIDTH, B_QK_WIDTH, B_V_WIDTH, B_V_WIDTH, LANES]))
VMEM_LIMIT = 56 * 1024 * 1024

PROJ_TILE = 512
SWA_BLOCK = 2048
SWA_UNROLL = 8
GLA_CHUNK, GLA_SUB, GLA_STEP = 64, 8, 1024
POST_TILE = 1024
SAMPLE_SEQS = 8
EXPERT_TILE = 512
DISPATCH_BLOCK = 256
GROUP_ROWS = 8
LOCAL_ROWS = DISPATCH_BLOCK * TOP_K + N_EXPERTS * GROUP_ROWS


def _cparams(sem, vmem=VMEM_LIMIT, **kw):
    return pltpu.CompilerParams(dimension_semantics=sem, vmem_limit_bytes=vmem, **kw)


def _full(shape):
    n = len(shape)
    return pl.BlockSpec(shape, lambda *_: (0,) * n)


def _rms(x, g):
    ms = jnp.mean(x * x, axis=-1, keepdims=True)
    return x * lax.rsqrt(ms + EPS) * g


def _dot(a, b):
    return jnp.dot(a, b, preferred_element_type=F32)


def _dot_nt(a, b):
    return lax.dot_general(a, b, (((1,), (1,)), ((), ())), preferred_element_type=F32)


def _block_diag(n, blk, val, dtype):
    i = np.arange(n)
    return jnp.asarray(np.where((i[:, None] // blk) == (i[None, :] // blk), val, 0.0), dtype)


def _in_proj_body(keep_transposed, x_ref, g1_ref, w_ref, gq_ref, gk_ref, bd_ref, wa_ref, ba_ref,
                  q_ref, k_ref, v_ref, kc_ref, vc_ref, bq_ref, bk_ref, bv_ref, gate_ref, la_ref):
    xn = _rms(x_ref[...], g1_ref[...]).astype(BF16)
    bd = bd_ref[...]

    def proj(lo, hi):
        return _dot(xn, w_ref[:, lo:hi])

    def headnorm(z, g):
        ms = _dot((z * z).astype(BF16), bd)
        return z * lax.rsqrt(ms + EPS) * g

    def put_groups(ref, z):
        for p in range(A_WIDTH // LANES):
            ref[p] = z[:, p * LANES:(p + 1) * LANES]

    c = IN_COLS
    put_groups(q_ref, headnorm(proj(c[0], c[1]), gq_ref[...]) * (A_SCALE * LOG2E))
    k = headnorm(proj(c[1], c[2]), gk_ref[...])
    v = proj(c[2], c[3])
    put_groups(k_ref, k)
    put_groups(v_ref, v)
    kc_ref[...] = jnp.transpose(k) if keep_transposed else k
    vc_ref[...] = jnp.transpose(v) if keep_transposed else v
    bq_ref[...] = proj(c[3], c[4]) * (B_KEY_DIM ** -0.5)
    bk_ref[...] = proj(c[4], c[5])
    bv_ref[...] = proj(c[5], c[6]).astype(bv_ref.dtype)
    br = proj(c[6], c[7])
    gate_ref[...] = (br * jax.nn.sigmoid(br)).astype(gate_ref.dtype)
    lr = proj(c[7], c[8]).astype(BF16)
    pre = _dot(lr, wa_ref[...]) + ba_ref[...]
    log_sig = jnp.minimum(pre, 0.0) - jnp.log1p(jnp.exp(-jnp.abs(pre)))
    la_ref[...] = log_sig * (1.0 / GATE_TEMP)


def _in_proj(x, wp, wide_dtype, seq, keep):
    n = x.shape[0]
    tm = min(PROJ_TILE, n)
    row = lambda w: pl.BlockSpec((tm, w), lambda i: (i, 0))
    ngrp = A_WIDTH // LANES
    grp = pl.BlockSpec((ngrp, tm, LANES), lambda i: (0, i, 0))
    if keep == seq:
        kept = row(A_WIDTH)
        kept_shape = (n, A_WIDTH)
    else:
        tps, kt = seq // tm, keep // tm
        assert tps * tm == seq and kt * tm == keep
        kept = pl.BlockSpec((None, A_WIDTH, tm),
                            lambda i: (i // tps, 0, jnp.maximum(i % tps - (tps - kt), 0)))
        kept_shape = (n // seq, A_WIDTH, keep)
    outs = [(B_QK_WIDTH, F32), (B_QK_WIDTH, F32), (B_V_WIDTH, wide_dtype), (B_V_WIDTH, wide_dtype),
            (B_QK_WIDTH, F32)]
    return pl.pallas_call(
        functools.partial(_in_proj_body, keep != seq),
        grid=(n // tm,),
        in_specs=[row(D_MODEL), _full((1, D_MODEL)), _full((D_MODEL, IN_COLS[-1])),
                  _full((1, A_WIDTH)), _full((1, A_WIDTH)), _full((A_WIDTH, A_WIDTH)),
                  _full((LANES, B_QK_WIDTH)), _full((1, B_QK_WIDTH))],
        out_specs=[grp] * 3 + [kept] * 2 + [row(w) for w, _ in outs],
        out_shape=[jax.ShapeDtypeStruct((ngrp, n, LANES), F32)] * 3
        + [jax.ShapeDtypeStruct(kept_shape, F32)] * 2
        + [jax.ShapeDtypeStruct((n, w), dt) for w, dt in outs],
        compiler_params=_cparams(("arbitrary",)),
        name="in_proj",
    )(x, wp["g1"], wp["w_in"], wp["gq"], wp["gk"], wp["bd64"], wp["wa"], wp["ba"])


def _unroll_for(trips):
    return max(u for u in range(1, SWA_UNROLL + 1) if trips % u == 0)


def _ds(start, size, stride):
    return pl.ds(start, size) if stride == 1 else pl.ds(start, size, stride=stride)


def _swa_prompt_body(q_ref, kp_ref, kc_ref, vp_ref, vc_ref, o_ref, m_s, l_s, acc_s, k_keep, v_keep):
    i = pl.program_id(2)
    qb = SWA_BLOCK

    @pl.when(i == 0)
    def _():
        k_keep[...] = jnp.zeros_like(k_keep)
        v_keep[...] = jnp.zeros_like(v_keep)

    lane = lax.broadcasted_iota(I32, (SPAN, LANES), 1)
    lo_mask = lane < A_HEAD_DIM
    jq = lax.broadcasted_iota(I32, (SPAN, 2 * SPAN), 0)
    jk = lax.broadcasted_iota(I32, (SPAN, 2 * SPAN), 1)
    dist = jq + SPAN - jk
    band = (dist >= 0) & (dist <= SPAN)
    cur_half = jk >= SPAN

    def attend(qp, kp, vp, valid, rows, first_group):
        vp1 = jnp.concatenate([vp, jnp.ones_like(vp)], axis=1)
        res = []
        for hh in range(2):
            msk = lo_mask if hh == 0 else jnp.logical_not(lo_mask)
            qm = jnp.where(msk, qp, 0.0).astype(BF16)
            s = jnp.where(valid, _dot_nt(qm, kp), -jnp.inf)
            m = jnp.max(s, axis=1, keepdims=True)
            res.append((m, _dot(jnp.exp2(s - m).astype(BF16), vp1)))
        m_new = jnp.where(lo_mask, res[0][0], res[1][0])
        l_new = jnp.where(lo_mask, res[0][1][:, LANES:], res[1][1][:, LANES:])
        o_new = jnp.where(lo_mask, res[0][1][:, :LANES], res[1][1][:, :LANES])
        if first_group:
            m_s[rows, :] = m_new
            l_s[rows, :] = l_new
            acc_s[rows, :] = o_new
        else:
            m_old = m_s[rows, :]
            m = jnp.maximum(m_old, m_new)
            a_old = jnp.exp2(m_old - m)
            a_new = jnp.exp2(m_new - m)
            m_s[rows, :] = m
            l_s[rows, :] = l_s[rows, :] * a_old + l_new * a_new
            acc_s[rows, :] = acc_s[rows, :] * a_old + o_new * a_new

    for gi, (_, dil) in enumerate(reversed(DILATED_GROUPS)):
        unit = dil * SPAN
        nblk = qb // unit
        first = gi == 0

        def head_block(r, carry, dil=dil, unit=unit, first=first, nblk=nblk):
            rows = _ds(r, SPAN, dil)
            valid = band & (cur_half | (i > 0))
            if nblk == 1:
                keep = pl.ds(pl.multiple_of(r * SPAN, SPAN), SPAN)
                k_cur = kc_ref[rows, :].astype(BF16)
                v_cur = vc_ref[rows, :].astype(BF16)
                ks = jnp.concatenate([k_keep[keep, :], k_cur], axis=0)
                vs = jnp.concatenate([v_keep[keep, :], v_cur], axis=0)
                k_keep[keep, :] = k_cur
                v_keep[keep, :] = v_cur
            else:
                prev = _ds(qb - unit + r, SPAN, dil)
                ks = jnp.concatenate([kp_ref[prev, :], kc_ref[rows, :]], axis=0).astype(BF16)
                vs = jnp.concatenate([vp_ref[prev, :], vc_ref[rows, :]], axis=0).astype(BF16)
            attend(q_ref[rows, :], ks, vs, valid, rows, first)
            return carry

        lax.fori_loop(0, dil, head_block, 0, unroll=_unroll_for(dil))

        if nblk > 1:
            def tail_block(idx, carry, dil=dil, unit=unit, first=first):
                n = idx // dil + 1
                r = idx % dil
                start = unit * n + r
                rows = _ds(start, SPAN, dil)
                keys = _ds(start - unit, 2 * SPAN, dil)
                attend(q_ref[rows, :], kc_ref[keys, :].astype(BF16), vc_ref[keys, :].astype(BF16),
                       band, rows, first)
                return carry

            lax.fori_loop(0, (nblk - 1) * dil, tail_block, 0, unroll=_unroll_for((nblk - 1) * dil))

    o_ref[...] = (acc_s[...] / l_s[...]).astype(o_ref.dtype)


def _swa_prompt(q, k, v, bsz, seq):
    qb = SWA_BLOCK
    nb = seq // qb
    cur = pl.BlockSpec((None, qb, LANES), lambda p, b, i: (p, b * nb + i, 0))
    prev = pl.BlockSpec((None, qb, LANES), lambda p, b, i: (p, b * nb + jnp.maximum(i - 1, 0), 0))
    return pl.pallas_call(
        _swa_prompt_body,
        grid=(A_WIDTH // LANES, bsz, nb),
        in_specs=[cur, prev, cur, prev, cur],
        out_specs=pl.BlockSpec((qb, LANES), lambda p, b, i: (b * nb + i, p)),
        out_shape=jax.ShapeDtypeStruct((bsz * seq, A_WIDTH), BF16),
        scratch_shapes=[pltpu.VMEM((qb, LANES), F32)] * 3 + [pltpu.VMEM((qb, LANES), BF16)] * 2,
        compiler_params=_cparams(("parallel", "parallel", "arbitrary")),
        name="swa_prompt",
    )(q, k, k, v, v)


def _swa_sample_body(q_ref, kn_ref, vn_ref, ck_ref, cv_ref, c1_ref, c2_ref,
                     o_ref, nk_ref, nv_ref):
    t_new = q_ref.shape[1]
    w_buf = ck_ref.shape[2]
    ck, cv = ck_ref[0], cv_ref[0]
    kn, vn = kn_ref[0], vn_ref[0]
    tail_lane = lax.broadcasted_iota(I32, (A_WIDTH, LANES), 1) >= LANES - t_new

    def shift_in(old, new, out_ref):
        moved = pltpu.roll(old, w_buf - t_new, 1)
        new_t = jnp.transpose(jnp.concatenate([jnp.zeros((LANES - t_new, A_WIDTH), F32), new], axis=0))
        out_ref[0, :, :w_buf - LANES] = moved[:, :w_buf - LANES]
        out_ref[0, :, w_buf - LANES:] = jnp.where(tail_lane, new_t, moved[:, w_buf - LANES:])

    shift_in(ck, kn, nk_ref)
    shift_in(cv, vn, nv_ref)

    q = q_ref[0]
    lane = lax.broadcasted_iota(I32, (t_new, LANES), 1)
    lo_mask = lane < A_HEAD_DIM
    c1, c2 = c1_ref[...], c2_ref[...]
    outs = []
    for p in range(A_WIDTH // LANES):
        sl = slice(p * LANES, (p + 1) * LANES)
        qp = q[:, sl]
        qblk = jnp.concatenate([jnp.where(lo_mask, qp, 0.0), jnp.where(lo_mask, 0.0, qp)],
                               axis=0).astype(BF16)
        s1 = jnp.where(c1 > 0, _dot(qblk, ck[sl, :].astype(BF16)), -jnp.inf)
        s2 = jnp.where(c2 > 0, _dot_nt(qblk, kn[:, sl].astype(BF16)), -jnp.inf)
        m = jnp.maximum(jnp.max(s1, axis=1, keepdims=True), jnp.max(s2, axis=1, keepdims=True))
        p1 = c1 * jnp.exp2(s1 - m)
        p2 = c2 * jnp.exp2(s2 - m)
        l = jnp.sum(p1, axis=1, keepdims=True) + jnp.sum(p2, axis=1, keepdims=True)
        o = (_dot_nt(p1.astype(BF16), cv[sl, :].astype(BF16))
             + _dot(p2.astype(BF16), vn[:, sl].astype(BF16))) / l
        outs.append(jnp.where(lo_mask, o[:t_new], o[t_new:]))
    o_ref[0] = jnp.concatenate(outs, axis=1).astype(o_ref.dtype)


def _sample_multiplicity(t_new, w_buf):
    t = np.arange(t_new)[:, None]
    e = np.arange(w_buf + t_new)[None, :]
    d = w_buf + t - e
    c = np.zeros(d.shape, np.float32)
    for window, dil in DILATED_GROUPS:
        c += ((d >= 0) & (d % dil == 0) & (d <= window)).astype(np.float32)
    c = np.concatenate([c, c], axis=0)
    return jnp.asarray(c[:, :w_buf]), jnp.asarray(c[:, w_buf:])


def _swa_sample(q, kn, vn, cache_k, cache_v):
    db, t_new, w = q.shape
    w_buf = cache_k.shape[2]
    assert w_buf >= max(win for win, _ in DILATED_GROUPS) and t_new % 8 == 0 and t_new <= LANES
    c1, c2 = _sample_multiplicity(t_new, w_buf)
    new = pl.BlockSpec((1, t_new, w), lambda b: (b, 0, 0))
    cache = pl.BlockSpec((1, w, w_buf), lambda b: (b, 0, 0))
    return pl.pallas_call(
        _swa_sample_body,
        grid=(db,),
        in_specs=[new, new, new, cache, cache, _full(c1.shape), _full(c2.shape)],
        out_specs=[new, cache, cache],
        out_shape=[jax.ShapeDtypeStruct((db, t_new, w), BF16),
                   jax.ShapeDtypeStruct(cache_k.shape, cache_k.dtype),
                   jax.ShapeDtypeStruct(cache_v.shape, cache_v.dtype)],
        compiler_params=_cparams(("parallel",)),
        name="swa_sample",
    )(q, kn, vn, cache_k, cache_v, c1, c2)


def _gla_body(chunk, sub, nch, seqs, q_ref, k_ref, g_ref, v_ref, gate_ref, s0_ref, tril_ref, dmask_ref,
              bones_ref, sbm_ref, bd_ref, gout_ref, o_ref, sfin_ref, sbd):
    j = pl.program_id(1)
    sbm = sbm_ref[...]
    nsub = chunk // sub
    pad = B_KEY_DIM - chunk

    def load_state(s):
        sbd[...] = jnp.concatenate([s0_ref[s]] * B_HEADS, axis=1) * sbm

    def store_state(s):
        st = sbd[...]
        sfin_ref[s] = jnp.concatenate(
            [st[h * B_KEY_DIM:(h + 1) * B_KEY_DIM, h * B_VAL_DIM:(h + 1) * B_VAL_DIM]
             for h in range(B_HEADS)], axis=0)

    row = lax.broadcasted_iota(I32, (chunk, 1), 0)
    sub_id = row // sub
    lane_w = lax.broadcasted_iota(I32, (chunk, LANES * max(nsub - 1, 1)), 1)
    lo_w = (lane_w % LANES) < B_KEY_DIM

    def one_chunk(c, first_row):
        off = pl.multiple_of(first_row + c * chunk, chunk)
        rows = pl.ds(off, chunk)
        q, k, g = q_ref[rows, :], k_ref[rows, :], g_ref[rows, :]
        v = v_ref[rows, :].astype(F32)
        g1 = g.astype(BF16)
        r1 = g - g1.astype(F32)
        g2 = r1.astype(BF16)
        g3 = (r1 - g2.astype(F32)).astype(BF16)
        tril = tril_ref[...]
        b = _dot(tril, g1) + _dot(tril, g2) + _dot(tril, g3)
        b_last = b[chunk - 1:chunk, :]
        state = sbd[...]

        o = _dot((q * jnp.exp(b)).astype(BF16), state.astype(BF16))

        bones = bones_ref[...]
        att = _dot((q * k).astype(BF16), bones) * dmask_ref[0]
        gate = jnp.exp(g)
        decay = gate
        for d in range(1, sub):
            if d > 1:
                decay = decay * pltpu.roll(gate, d - 1, 0)
            w = q * pltpu.roll(k, d, 0) * decay
            att = att + _dot(w.astype(BF16), bones) * dmask_ref[d]

        if nsub > 1:
            qx, kx = [], []
            for i in range(1, nsub):
                r_i = b[sub * i - 1:sub * i, :]
                qx.append(jnp.where(sub_id == i, q * jnp.exp(jnp.minimum(b - r_i, 0.0)), 0.0))
                kx.append(jnp.where(sub_id < i, k * jnp.exp(jnp.minimum(r_i - b, 0.0)), 0.0))
            parts = []
            for p in range(B_QK_WIDTH // LANES):
                sl = slice(p * LANES, (p + 1) * LANES)
                qp = jnp.concatenate([x[:, sl] for x in qx], axis=1)
                kp = jnp.concatenate([x[:, sl] for x in kx], axis=1).astype(BF16)
                zero = jnp.zeros_like(kp)
                lhs = jnp.concatenate([jnp.where(lo_w, qp, 0.0), jnp.where(lo_w, 0.0, qp)],
                                      axis=1).astype(BF16)
                rhs = jnp.concatenate([jnp.concatenate([kp, zero], axis=1),
                                       jnp.concatenate([zero, kp], axis=1)], axis=0)
                parts.append(_dot_nt(lhs, rhs))
            att = att + jnp.concatenate(parts, axis=1)

        if pad:
            vrow = jnp.concatenate([v, jnp.zeros((pad, B_V_WIDTH), F32)], axis=0)
        else:
            vrow = v
        vbd = (jnp.concatenate([vrow] * B_HEADS, axis=0) * sbm).astype(BF16)
        o = o + _dot(att.astype(BF16), vbd)

        ke = (k * jnp.exp(b_last - b)).astype(BF16)
        upd = lax.dot_general(ke, v.astype(BF16), (((0,), (0,)), ((), ())),
                              preferred_element_type=F32)
        dec = jnp.transpose(jnp.broadcast_to(jnp.exp(b_last), (8, B_QK_WIDTH)))[:, 0:1]
        sbd[...] = (state * dec + upd) * sbm

        ms = _dot((o * o).astype(BF16), bd_ref[...])
        on = o * lax.rsqrt(ms + EPS) * gout_ref[...] * gate_ref[rows, :].astype(F32)
        o_ref[rows, :] = on.astype(o_ref.dtype)
        return first_row

    def chunks(first_row):
        lax.fori_loop(0, nch, one_chunk, first_row, unroll=2 if nch % 2 == 0 else 1)

    if seqs == 1:
        pl.when(j == 0)(lambda: load_state(0))
        chunks(jnp.int32(0))
        pl.when(j == pl.num_programs(1) - 1)(lambda: store_state(0))
    else:
        def one_sequence(s, carry):
            load_state(s)
            chunks(s * (nch * chunk))
            store_state(s)
            return carry

        lax.fori_loop(0, seqs, one_sequence, 0)


def _gla_consts(chunk, sub):
    t = np.arange(chunk)
    tril = (t[:, None] >= t[None, :]).astype(np.float32)
    lane = np.arange(B_QK_WIDTH)
    dmask = np.zeros((sub, chunk, B_QK_WIDTH), np.float32)
    for d in range(sub):
        ok = (t % sub) >= d
        dmask[d] = ((lane[None, :] % B_KEY_DIM) == (t[:, None] - d)) & ok[:, None]
    r = np.arange(B_QK_WIDTH)[:, None] // B_KEY_DIM
    c = np.arange(B_V_WIDTH)[None, :] // B_VAL_DIM
    sbm = (r == c).astype(np.float32)
    return jnp.asarray(tril, BF16), jnp.asarray(dmask), jnp.asarray(sbm)


def _gla(q, k, g, v, gate, s0, wp, length, chunk, sub, step, seqs=1, out_dtype=BF16):
    n = q.shape[0]
    bsz = n // length
    assert chunk == sub or chunk == B_KEY_DIM
    assert seqs == 1 or (step == length and bsz % seqs == 0)
    tril, dmask, sbm = _gla_consts(chunk, sub)
    nstep = length // step
    row = lambda w: pl.BlockSpec((step * seqs, w), lambda b, j: (b * nstep + j, 0))
    st = pl.BlockSpec((seqs, B_QK_WIDTH, B_VAL_DIM), lambda b, j: (b, 0, 0))
    return pl.pallas_call(
        functools.partial(_gla_body, chunk, sub, step // chunk, seqs),
        grid=(bsz // seqs, nstep),
        in_specs=[row(256), row(256), row(256), row(512), row(512), st, _full(tril.shape),
                  _full(dmask.shape), _full((256, 256)), _full(sbm.shape), _full((512, 512)),
                  _full((1, 512))],
        out_specs=[row(512), st],
        out_shape=[jax.ShapeDtypeStruct((n, B_V_WIDTH), out_dtype),
                   jax.ShapeDtypeStruct(s0.shape, F32)],
        scratch_shapes=[pltpu.VMEM((B_QK_WIDTH, B_V_WIDTH), F32)],
        compiler_params=_cparams(("parallel", "arbitrary")),
        name="gla",
    )(q, k, g, v, gate, s0, tril, dmask, wp["bones64"], sbm, wp["bd128"], wp["gout"])


def _head_rms(z, g, scale=1.0):
    parts = []
    for h in range(MEM_HEADS):
        zh = z[:, h * LANES:(h + 1) * LANES]
        parts.append(zh * lax.rsqrt(jnp.mean(zh * zh, axis=-1, keepdims=True) + EPS))
    return jnp.concatenate(parts, axis=1) * (g * scale)


def _mem_kv_body(m_ref, gn_ref, wk_ref, wv_ref, gk_ref, mk_ref, mv_ref):
    mn = _rms(m_ref[...], gn_ref[...]).astype(BF16)
    mk_ref[...] = _head_rms(_dot(mn, wk_ref[...]), gk_ref[...])
    mv_ref[...] = _dot(mn, wv_ref[...])


def _mem_kv(mem, wp):
    n = mem.shape[0]
    tm = 256
    row = lambda w: pl.BlockSpec((tm, w), lambda i: (i, 0))
    return pl.pallas_call(
        _mem_kv_body,
        grid=(n // tm,),
        in_specs=[row(D_MODEL), _full((1, D_MODEL)), _full((D_MODEL, 512)), _full((D_MODEL, 512)),
                  _full((1, 512))],
        out_specs=[row(512), row(512)],
        out_shape=[jax.ShapeDtypeStruct((n, 512), F32)] * 2,
        compiler_params=_cparams(("parallel",)),
        name="mem_kv",
    )(mem, wp["gmem"], wp["wmk"], wp["wmv"], wp["gmk"])


def _post_body(nseq, x_ref, oa_ref, ob_ref, woa_ref, wob_ref, g2_ref, wq_ref, gmq_ref, mk_ref,
               mv_ref, wo_ref, g3_ref, rw_ref, rb_ref, cnt0_ref, tri_ref,
               h2_ref, xn_ref, meta_ref, cnt_ref, part_ref, carry):
    tm = x_ref.shape[0]

    @pl.when(pl.program_id(0) == 0)
    def _():
        carry[...] = cnt0_ref[...]

    nk = nseq * N_MEM

    def mem_head(ref, hd):
        if ref.shape[-1] == MEM_HEAD_DIM:
            rows = ref[:, pl.ds(hd, N_MEM, stride=MEM_HEADS), :]
        else:
            rows = ref[:, :, hd * LANES:(hd + 1) * LANES]
        return rows.reshape(nk, LANES).astype(BF16)

    if nseq > 1:
        rt = lax.broadcasted_iota(I32, (tm, nk), 0) // (tm // nseq)
        ct = lax.broadcasted_iota(I32, (tm, nk), 1) // N_MEM
        same = rt == ct

    def router_logits():
        h = (x_ref[...] + _dot(oa_ref[...], woa_ref[...])
             + _dot(ob_ref[...].astype(BF16), wob_ref[...]))
        hn = _rms(h, g2_ref[...]).astype(BF16)
        qm = _head_rms(_dot(hn, wq_ref[...]), gmq_ref[...], MEM_HEAD_DIM ** -0.5).astype(BF16)
        outs = []
        for hd in range(MEM_HEADS):
            sl = slice(hd * LANES, (hd + 1) * LANES)
            s = _dot_nt(qm[:, sl], mem_head(mk_ref, hd))
            if nseq > 1:
                s = jnp.where(same, s, -jnp.inf)
            m = jnp.max(s, axis=1, keepdims=True)
            pr = jnp.exp(s - m)
            l = jnp.sum(pr, axis=1, keepdims=True)
            outs.append(_dot(pr.astype(BF16), mem_head(mv_ref, hd)) / l)
        h2 = h + _dot(jnp.concatenate(outs, axis=1).astype(BF16), wo_ref[...])
        h2_ref[...] = h2
        xn = _rms(h2, g3_ref[...])
        xn_ref[...] = xn
        x1 = xn.astype(BF16)
        x2 = (xn - x1.astype(F32)).astype(BF16)
        prod = _dot(jnp.concatenate([x1, x2], axis=0), rw_ref[...])
        return prod[:tm, :LANES] + prod[:tm, LANES:] + prod[tm:, :LANES] + prod[tm:, LANES:]

    logits = router_logits() + rb_ref[...]
    lane = lax.broadcasted_iota(I32, (tm, LANES), 1)
    vals, idxs, hots = [], [], []
    work = logits
    for _ in range(TOP_K):
        m = jnp.max(work, axis=1, keepdims=True)
        idx = jnp.min(jnp.where(work == m, lane, LANES), axis=1, keepdims=True)
        hot = lane == idx
        vals.append(m)
        idxs.append(idx)
        hots.append(hot)
        work = jnp.where(hot, -jnp.inf, work)
    exps = [jnp.exp(v - vals[0]) for v in vals]
    den = exps[0] + exps[1] + exps[2] + exps[3]

    sel = (hots[0] | hots[1] | hots[2] | hots[3]).astype(F32)
    before = _dot(tri_ref[...], sel.astype(BF16)) + carry[...]
    carry[...] = carry[...] + jnp.sum(sel, axis=0, keepdims=True)
    cnt_ref[...] = carry[...]
    parts = part_ref.shape[1]
    for j in range(parts):
        part_ref[0, j:j + 1, :] = jnp.sum(sel[j * (tm // parts):(j + 1) * (tm // parts)], axis=0,
                                          keepdims=True)

    meta = jnp.zeros((tm, LANES), F32)
    for kk in range(TOP_K):
        rank = jnp.sum(jnp.where(hots[kk], before, 0.0), axis=1, keepdims=True)
        meta = jnp.where(lane == kk, idxs[kk].astype(F32), meta)
        meta = jnp.where(lane == TOP_K + kk, rank, meta)
        meta = jnp.where(lane == 2 * TOP_K + kk, exps[kk] / den, meta)
    meta_ref[...] = meta


def _post(x, oa, ob, mk, mv, cnt0, wp, tm, nseq, tiles_per_mem):
    n = x.shape[0]
    row = lambda w: pl.BlockSpec((tm, w), lambda i: (i, 0))
    mem = pl.BlockSpec((nseq,) + mk.shape[1:], lambda i: (i // tiles_per_mem, 0, 0))
    tri = jnp.asarray(np.tril(np.ones((tm, tm), np.float32), -1), BF16)
    parts = max(tm // DISPATCH_BLOCK, 1)
    return pl.pallas_call(
        functools.partial(_post_body, nseq),
        grid=(n // tm,),
        in_specs=[row(D_MODEL), row(512), row(512), _full((512, D_MODEL)), _full((512, D_MODEL)),
                  _full((1, D_MODEL)), _full((D_MODEL, 512)), _full((1, 512)), mem, mem,
                  _full((512, D_MODEL)), _full((1, D_MODEL)), _full((D_MODEL, 2 * LANES)),
                  _full((1, LANES)), _full((1, LANES)), _full((tm, tm))],
        out_specs=[row(D_MODEL), row(D_MODEL), row(LANES), _full((1, LANES)),
                   pl.BlockSpec((1, parts, LANES), lambda i: (i, 0, 0))],
        out_shape=[jax.ShapeDtypeStruct((n, D_MODEL), F32), jax.ShapeDtypeStruct((n, D_MODEL), F32),
                   jax.ShapeDtypeStruct((n, LANES), F32), jax.ShapeDtypeStruct((1, LANES), F32),
                   jax.ShapeDtypeStruct((n // tm, parts, LANES), F32)],
        scratch_shapes=[pltpu.VMEM((1, LANES), F32)],
        compiler_params=_cparams(("arbitrary",)),
        name="post",
    )(x, oa, ob, wp["woa"], wp["wob"], wp["g2"], wp["wmq"], wp["gmq"], mk, mv, wp["wmo"],
      wp["g3"], wp["rw"], wp["rb"], cnt0, tri)


def _start_rows(src, src_row, dst, dst_row, n_rows, sem):
    s = pl.multiple_of(jnp.asarray(src_row, I32), GROUP_ROWS)
    d = pl.multiple_of(jnp.asarray(dst_row, I32), GROUP_ROWS)
    n_rows = pl.multiple_of(jnp.asarray(n_rows, I32), GROUP_ROWS)

    @pl.when(n_rows > 0)
    def _():
        pltpu.make_async_copy(src.at[pl.ds(s, n_rows)], dst.at[pl.ds(d, n_rows)], sem).start()

    return n_rows


def _wait_rows(like_src, like_dst, n_rows, sem):
    n_rows = pl.multiple_of(n_rows, GROUP_ROWS)

    @pl.when(n_rows > 0)
    def _():
        pltpu.make_async_copy(like_src.at[pl.ds(0, n_rows)], like_dst.at[pl.ds(0, n_rows)], sem).wait()


def _pad_rows_body(pstart_ref, prows_ref, nt_ref, xs_ref, zeros, sem):
    zeros[...] = jnp.zeros_like(zeros)
    n_tiles_max = xs_ref.shape[0] // EXPERT_TILE

    def expert_tail(e, total):
        return total + _start_rows(zeros, 0, xs_ref, pstart_ref[e], prows_ref[e], sem)

    def unused_tile(t, total):
        return total + _start_rows(zeros, 0, xs_ref, t * EXPERT_TILE, EXPERT_TILE, sem)

    total = lax.fori_loop(0, N_EXPERTS, expert_tail, jnp.int32(0))
    total = lax.fori_loop(nt_ref[0], n_tiles_max, unused_tile, total)
    _wait_rows(xs_ref, xs_ref, total, sem)


def _pad_rows(pad_start, pad_units, n_tiles, rows):
    return pl.pallas_call(
        _pad_rows_body,
        grid_spec=pltpu.PrefetchScalarGridSpec(
            num_scalar_prefetch=3, grid=(1,),
            in_specs=[],
            out_specs=pl.BlockSpec(memory_space=pltpu.HBM),
            scratch_shapes=[pltpu.VMEM((EXPERT_TILE, D_MODEL), F32), pltpu.SemaphoreType.DMA]),
        out_shape=jax.ShapeDtypeStruct((rows, D_MODEL), F32),
        compiler_params=_cparams(("arbitrary",)),
        name="moe_pad_rows",
    )(pad_start, pad_units, n_tiles)


def _dispatch_body(blk0, loff_ref, gstart_ref, nrows_ref, ldest_ref, x_ref, xs_in_ref, xs_ref,
                   xloc, sems):
    del xs_in_ref
    i = pl.program_id(0)
    b = i + blk0
    slot = i % 2
    tb = x_ref.shape[0]
    rows = lax.broadcasted_iota(I32, (LOCAL_ROWS, tb), 0)
    ld = ldest_ref[...]
    hot = rows == ld[0:1, :]
    for kk in range(1, TOP_K):
        hot = hot | (rows == ld[kk:kk + 1, :])
    xloc[slot] = _dot(jnp.where(hot, 1.0, 0.0).astype(BF16), x_ref[...].astype(BF16))

    def block_rows(bb):
        j = bb * N_EXPERTS + N_EXPERTS - 1
        return loff_ref[j] + nrows_ref[j]

    @pl.when(i > 0)
    def _():
        _wait_rows(xloc.at[1 - slot], xs_ref, block_rows(b - 1), sems.at[1 - slot])

    def group(e, carry):
        j = b * N_EXPERTS + e
        _start_rows(xloc.at[slot], loff_ref[j], xs_ref, gstart_ref[j], nrows_ref[j], sems.at[slot])
        return carry

    lax.fori_loop(0, N_EXPERTS, group, 0)

    @pl.when(i == pl.num_programs(0) - 1)
    def _():
        _wait_rows(xloc.at[slot], xs_ref, block_rows(b), sems.at[slot])


def _dispatch(tables, blk0, ldest_t, xn, xs):
    n = xn.shape[0]
    tb = DISPATCH_BLOCK
    idx = lambda i, *_: (i, 0)
    return pl.pallas_call(
        functools.partial(_dispatch_body, blk0),
        grid_spec=pltpu.PrefetchScalarGridSpec(
            num_scalar_prefetch=3, grid=(n // tb,),
            in_specs=[pl.BlockSpec((TOP_K, tb), lambda i, *_: (0, i)),
                      pl.BlockSpec((tb, D_MODEL), idx),
                      pl.BlockSpec(memory_space=pltpu.HBM)],
            out_specs=pl.BlockSpec(memory_space=pltpu.HBM),
            scratch_shapes=[pltpu.VMEM((2, LOCAL_ROWS, D_MODEL), F32), pltpu.SemaphoreType.DMA((2,))]),
        out_shape=jax.ShapeDtypeStruct(xs.shape, xs.dtype),
        input_output_aliases={5: 0},
        compiler_params=_cparams(("arbitrary",)),
        name="moe_dispatch",
    )(*tables, ldest_t, xn, xs)


def _expert_body(te_ref, first_ref, nt_ref, x_ref, w1_ref, sel_ref, b1g_ref, b1l_ref, w2_ref, b2_ref,
                 y_ref, w1g, w1l, w2):
    i = pl.program_id(0)
    live = i < nt_ref[0]

    @pl.when(live & (first_ref[i] == 1))
    def _():
        sel = sel_ref[...]
        for j in range(D_FF // LANES):
            z = _dot(w1_ref[0, :, 2 * LANES * j:2 * LANES * (j + 1)].astype(BF16), sel)
            w1g[:, LANES * j:LANES * (j + 1)] = z[:, :LANES].astype(BF16)
            w1l[:, LANES * j:LANES * (j + 1)] = z[:, LANES:].astype(BF16)
        w2[...] = w2_ref[0].astype(BF16)

    @pl.when(live)
    def _():
        x = x_ref[...].astype(BF16)
        glu = jnp.minimum(_dot(x, w1g[...]) + b1g_ref[0], SWIGLU_LIMIT)
        lin = jnp.clip(_dot(x, w1l[...]) + b1l_ref[0], -SWIGLU_LIMIT, SWIGLU_LIMIT)
        act = glu * jax.nn.sigmoid(SWIGLU_ALPHA * glu) * (lin + 1.0)
        y_ref[...] = _dot(act.astype(BF16), w2[...]) + b2_ref[0]

    @pl.when(jnp.logical_not(live))
    def _():
        y_ref[...] = jnp.zeros_like(y_ref)


def _experts(tile_expert, n_tiles, xs, wp):
    rows = xs.shape[0]
    first = jnp.concatenate([jnp.ones((1,), I32),
                             (tile_expert[1:] != tile_expert[:-1]).astype(I32)])
    c = np.arange(2 * LANES)
    sel = np.zeros((2 * LANES, 2 * LANES), np.float32)
    sel[c, (c % 2) * LANES + c // 2] = 1.0
    tile = lambda i, te, fi, nt: (jnp.minimum(i, nt[0] - 1), 0)
    out_tile = lambda i, te, fi, nt: (i, 0)
    wsel = lambda i, te, fi, nt: (te[jnp.minimum(i, nt[0] - 1)], 0, 0)
    wspec = lambda r, c: pl.BlockSpec((1, r, c), wsel)
    return pl.pallas_call(
        _expert_body,
        grid_spec=pltpu.PrefetchScalarGridSpec(
            num_scalar_prefetch=3, grid=(rows // EXPERT_TILE,),
            in_specs=[pl.BlockSpec((EXPERT_TILE, D_MODEL), tile),
                      wspec(D_MODEL, 2 * D_FF), pl.BlockSpec((2 * LANES, 2 * LANES), lambda *_: (0, 0)),
                      wspec(1, D_FF), wspec(1, D_FF), wspec(D_FF, D_MODEL), wspec(1, D_MODEL)],
            out_specs=pl.BlockSpec((EXPERT_TILE, D_MODEL), out_tile),
            scratch_shapes=[pltpu.VMEM((D_MODEL, D_FF), BF16)] * 2 + [pltpu.VMEM((D_FF, D_MODEL), BF16)]),
        out_shape=jax.ShapeDtypeStruct((rows, D_MODEL), F32),
        compiler_params=_cparams(("arbitrary",)),
        name="moe_experts",
    )(tile_expert, first, n_tiles, xs, wp["w1"], jnp.asarray(sel, BF16), wp["b1g"], wp["b1l"],
      wp["w2"], wp["b2"])


def _combine_body(blk0, loff_ref, gstart_ref, nrows_ref, ldest_ref, meta_ref, h_ref, ys_ref, o_ref,
                  yloc, sems):
    i = pl.program_id(0)
    b = i + blk0
    slot = i % 2
    tb = h_ref.shape[0]

    def gather(bb, s):
        def group(e, carry):
            j = bb * N_EXPERTS + e
            _start_rows(ys_ref, gstart_ref[j], yloc.at[s], loff_ref[j], nrows_ref[j], sems.at[s])
            return carry
        lax.fori_loop(0, N_EXPERTS, group, 0)

    @pl.when(i == 0)
    def _():
        yloc[...] = jnp.zeros_like(yloc)
        gather(b, slot)

    @pl.when(i + 1 < pl.num_programs(0))
    def _():
        gather(b + 1, 1 - slot)

    j_last = b * N_EXPERTS + N_EXPERTS - 1
    total = loff_ref[j_last] + nrows_ref[j_last]
    cols = lax.broadcasted_iota(I32, (tb, LOCAL_ROWS), 1)
    ld = ldest_ref[...]
    meta = meta_ref[...]
    gmat = jnp.zeros((tb, LOCAL_ROWS), F32)
    for kk in range(TOP_K):
        gate = meta[:, 2 * TOP_K + kk:2 * TOP_K + kk + 1]
        gmat = jnp.where(cols == ld[:, kk:kk + 1], gate, gmat)
    _wait_rows(ys_ref, yloc.at[slot], total, sems.at[slot])
    o_ref[...] = h_ref[...] + _dot(gmat.astype(BF16), yloc[slot].astype(BF16))


def _combine(tables, blk0, ldest, meta, h2, ys):
    n = h2.shape[0]
    tb = DISPATCH_BLOCK
    row = lambda w: pl.BlockSpec((tb, w), lambda i, *_: (i, 0))
    return pl.pallas_call(
        functools.partial(_combine_body, blk0),
        grid_spec=pltpu.PrefetchScalarGridSpec(
            num_scalar_prefetch=3, grid=(n // tb,),
            in_specs=[row(TOP_K), row(LANES), row(D_MODEL), pl.BlockSpec(memory_space=pltpu.HBM)],
            out_specs=row(D_MODEL),
            scratch_shapes=[pltpu.VMEM((2, LOCAL_ROWS, D_MODEL), F32), pltpu.SemaphoreType.DMA((2,))]),
        out_shape=jax.ShapeDtypeStruct((n, D_MODEL), F32),
        compiler_params=_cparams(("arbitrary",)),
        name="moe_combine",
    )(*tables, ldest, meta, h2, ys)


def _moe(groups, wp):
    tb = DISPATCH_BLOCK
    sizes = [g[0].shape[0] for g in groups]
    assert all(s % tb == 0 for s in sizes)
    n_tok = sum(sizes)
    nb = n_tok // tb
    rows = n_tok * TOP_K + nb * N_EXPERTS * GROUP_ROWS + N_EXPERTS * EXPERT_TILE
    n_tiles_max = rows // EXPERT_TILE

    cnt = jnp.concatenate([g[3].reshape(sz // tb, -1, LANES).sum(axis=1)[:, :N_EXPERTS]
                           for g, sz in zip(groups, sizes)]).astype(I32)
    meta_t = jnp.concatenate([g[2][:, :2 * TOP_K] for g in groups]).T.astype(I32)
    eidx_t, rank_t = meta_t[:TOP_K], meta_t[TOP_K:]
    npad = (cnt + GROUP_ROWS - 1) // GROUP_ROWS * GROUP_ROWS
    loff = jnp.cumsum(npad, axis=1) - npad
    gsize = jnp.sum(npad, axis=0)
    gpad = (gsize + EXPERT_TILE - 1) // EXPERT_TILE * EXPERT_TILE
    ends = jnp.cumsum(gpad)
    gstart = (ends - gpad)[None, :] + jnp.cumsum(npad, axis=0) - npad
    before = jnp.cumsum(cnt, axis=0) - cnt
    n_tiles = (ends[-1] // EXPERT_TILE).reshape(1)
    tile_ids = jnp.arange(n_tiles_max, dtype=I32)
    tile_expert = jnp.minimum(
        jnp.sum((ends // EXPERT_TILE)[None, :] <= tile_ids[:, None], axis=1), N_EXPERTS - 1).astype(I32)

    base_t = jnp.repeat((loff - before).T, tb, axis=1)
    experts = jnp.arange(N_EXPERTS, dtype=I32)[:, None, None]
    ldest_t = rank_t + jnp.sum(jnp.where(eidx_t[None] == experts, base_t[:, None, :], 0), axis=0)
    ldest = ldest_t.T
    tables = (loff.reshape(-1), gstart.reshape(-1).astype(I32), npad.reshape(-1))

    xs = _pad_rows(((ends - gpad) + gsize).astype(I32), (gpad - gsize).astype(I32),
                   n_tiles, rows)
    starts = np.cumsum([0] + sizes[:-1])
    for g, t0, sz in zip(groups, starts, sizes):
        xs = _dispatch(tables, int(t0) // tb, ldest_t[:, t0:t0 + sz], g[1], xs)
    ys = _experts(tile_expert, n_tiles, xs, wp)
    return [_combine(tables, int(t0) // tb, ldest[t0:t0 + sz], g[2], g[0], ys)
            for g, t0, sz in zip(groups, starts, sizes)]


def _prep_weights(norm1_g, w_in, a_q_norm_g, a_k_norm_g, gla_w_alpha, gla_b_alpha, gla_out_norm_g,
                  w_out, norm2_g, mem_norm_g, mem_w_q, mem_w_k, mem_w_v, mem_q_norm_g,
                  mem_k_norm_g, mem_w_o, norm3_g, router_w, router_b, exp_w1, exp_b1, exp_w2,
                  exp_b2):
    main = 3 * A_WIDTH + 2 * B_QK_WIDTH + 2 * B_V_WIDTH
    w_lr = jnp.pad(w_in[:, main:], ((0, 0), (0, LANES - GATE_RANK)))
    rw = jnp.pad(router_w, ((0, 0), (0, LANES - N_EXPERTS)))
    rwh = rw.astype(BF16)
    return {
        "g1": norm1_g[None],
        "w_in": jnp.concatenate([w_in[:, :main], w_lr], axis=1).astype(BF16),
        "gq": jnp.tile(a_q_norm_g, A_HEADS)[None],
        "gk": jnp.tile(a_k_norm_g, A_HEADS)[None],
        "bd64": _block_diag(A_WIDTH, A_HEAD_DIM, 1.0 / A_HEAD_DIM, BF16),
        "wa": jnp.pad(gla_w_alpha, ((0, LANES - GATE_RANK), (0, 0))).astype(BF16),
        "ba": gla_b_alpha[None],
        "bones64": _block_diag(B_QK_WIDTH, B_KEY_DIM, 1.0, BF16),
        "bd128": _block_diag(B_V_WIDTH, B_VAL_DIM, 1.0 / B_VAL_DIM, BF16),
        "gout": jnp.tile(gla_out_norm_g, B_HEADS)[None],
        "woa": w_out[:A_WIDTH].astype(BF16),
        "wob": w_out[A_WIDTH:].astype(BF16),
        "g2": norm2_g[None],
        "gmem": mem_norm_g[None],
        "wmq": mem_w_q.astype(BF16),
        "wmk": mem_w_k.astype(BF16),
        "wmv": mem_w_v.astype(BF16),
        "gmq": jnp.tile(mem_q_norm_g, MEM_HEADS)[None],
        "gmk": jnp.tile(mem_k_norm_g, MEM_HEADS)[None],
        "wmo": mem_w_o.astype(BF16),
        "g3": norm3_g[None],
        "rw": jnp.concatenate([rwh, (rw - rwh.astype(F32)).astype(BF16)], axis=1),
        "rb": jnp.pad(router_b, (0, LANES - N_EXPERTS), constant_values=-1e30)[None],
        "w1": exp_w1,
        "b1g": exp_b1[:, None, 0::2],
        "b1l": exp_b1[:, None, 1::2],
        "w2": exp_w2,
        "b2": exp_b2[:, None, :],
    }


def _layer(xp, xs, mem_prompt, cache_k, cache_v, state_gla, cache_mk, cache_mv, wp):
    bsz, seq, _ = xp.shape
    db, t_new, _ = xs.shape
    w_p = min(max(w for w, _ in DILATED_GROUPS), seq)

    xpf = xp.reshape(bsz * seq, D_MODEL)
    q, k, v, k_last, v_last, bq, bk, bv, gate, la = _in_proj(xpf, wp, BF16, seq, w_p)

    def last_rows(a):
        return a.transpose(0, 2, 1) if w_p != seq else a.reshape(bsz, w_p, A_WIDTH)
    oa = _swa_prompt(q, k, v, bsz, seq)
    s0 = jnp.zeros((bsz, B_QK_WIDTH, B_VAL_DIM), F32)
    ob, s_p = _gla(bq, bk, la, bv, gate, s0, wp, seq, GLA_CHUNK, GLA_SUB, GLA_STEP)
    mk, mv = _mem_kv(mem_prompt.reshape(bsz * N_MEM, D_MODEL), wp)
    cnt0 = jnp.zeros((1, LANES), F32)
    h2_p, xn_p, meta_p, cnt, part_p = _post(xpf, oa, ob, mk.reshape(bsz, N_MEM, MEM_WIDTH),
                                            mv.reshape(bsz, N_MEM, MEM_WIDTH), cnt0, wp,
                                            POST_TILE, 1, seq // POST_TILE)

    xsf = xs.reshape(db * t_new, D_MODEL)
    q, _, _, ks, vs, bq, bk, bv, gate, la = _in_proj(xsf, wp, F32, t_new, t_new)
    new3 = lambda a: a.reshape(db, t_new, A_WIDTH)
    oa_s, nk, nv = _swa_sample(new3(q.transpose(1, 0, 2)), new3(ks), new3(vs), cache_k, cache_v)
    ob_s, s_s = _gla(bq, bk, la, bv, gate, state_gla, wp, t_new, t_new, t_new, t_new,
                     seqs=SAMPLE_SEQS, out_dtype=F32)
    h2_s, xn_s, meta_s, _, part_s = _post(xsf, oa_s.reshape(db * t_new, A_WIDTH), ob_s, cache_mk,
                                          cache_mv, cnt, wp, SAMPLE_SEQS * t_new, SAMPLE_SEQS, 1)

    y_p, y_s = _moe([(h2_p, xn_p, meta_p, part_p), (h2_s, xn_s, meta_s, part_s)], wp)
    return (y_p.reshape(bsz, seq, D_MODEL), y_s.reshape(db, t_new, D_MODEL),
            last_rows(k_last), last_rows(v_last),
            s_p, mk, mv, nk, nv, s_s)


def kernel(x_prompt, x_sample, mem_prompt, cache_swa_k, cache_swa_v, state_gla, cache_mem_k, cache_mem_v, norm1_g, w_in, a_q_norm_g, a_k_norm_g, gla_w_alpha, gla_b_alpha, gla_out_norm_g, w_out, norm2_g, mem_norm_g, mem_w_q, mem_w_k, mem_w_v, mem_q_norm_g, mem_k_norm_g, mem_w_o, norm3_g, router_w, router_b, exp_w1, exp_b1, exp_w2, exp_b2):
    depth = w_in.shape[0]
    bsz = x_prompt.shape[0]
    db, w_buf = cache_swa_k.shape[1], cache_swa_k.shape[2]
    xp, xs = x_prompt, x_sample
    per_layer = []
    for l in range(depth):
        wp = _prep_weights(
            norm1_g[l], w_in[l], a_q_norm_g[l], a_k_norm_g[l], gla_w_alpha[l], gla_b_alpha[l],
            gla_out_norm_g[l], w_out[l], norm2_g[l], mem_norm_g[l], mem_w_q[l], mem_w_k[l],
            mem_w_v[l], mem_q_norm_g[l], mem_k_norm_g[l], mem_w_o[l], norm3_g[l], router_w[l],
            router_b[l], exp_w1[l], exp_b1[l], exp_w2[l], exp_b2[l])
        xp, xs, kp, vp, s_p, mk, mv, nk, nv, s_s = _layer(
            xp, xs, mem_prompt,
            cache_swa_k[l].reshape(db, w_buf, A_WIDTH).transpose(0, 2, 1),
            cache_swa_v[l].reshape(db, w_buf, A_WIDTH).transpose(0, 2, 1),
            state_gla[l].reshape(db, B_QK_WIDTH, B_VAL_DIM),
            cache_mem_k[l].reshape(db, N_MEM * MEM_HEADS, MEM_HEAD_DIM),
            cache_mem_v[l].reshape(db, N_MEM * MEM_HEADS, MEM_HEAD_DIM),
            wp)
        w_p = kp.shape[1]
        per_layer.append((
            kp.reshape(bsz, w_p, A_HEADS, A_HEAD_DIM), vp.reshape(bsz, w_p, A_HEADS, A_HEAD_DIM),
            s_p.reshape(bsz, B_HEADS, B_KEY_DIM, B_VAL_DIM),
            mk.reshape(bsz, N_MEM, MEM_HEADS, MEM_HEAD_DIM), mv.reshape(bsz, N_MEM, MEM_HEADS, MEM_HEAD_DIM),
            nk.transpose(0, 2, 1).reshape(db, w_buf, A_HEADS, A_HEAD_DIM),
            nv.transpose(0, 2, 1).reshape(db, w_buf, A_HEADS, A_HEAD_DIM),
            s_s.reshape(db, B_HEADS, B_KEY_DIM, B_VAL_DIM)))
    stacked = [jnp.stack(t) for t in zip(*per_layer)]
    return (xp, xs, *stacked)
```
